```python
import jax, jax.numpy as jnp
from jax import lax
import numpy as np

D_MODEL = 1024
BATCH = 8
SEQ = 2048
DEPTH = 2

CHUNK = 64
HEAD_DIM = 64
SWA_HEADS = 8
SWA_KV_HEADS = 2
SWA_GROUP = SWA_HEADS // SWA_KV_HEADS
SWA_WINDOW = 128
SWA_WIN_CHUNKS = (SWA_WINDOW + CHUNK - 1) // CHUNK
RWKV_HEADS = 4
RWKV_W_RANK = 64
RWKV_A_RANK = 64
RWKV_G_RANK = 128
RWKV_GN_EPS = 64e-5
FOX_HEADS = 4
FOX_BLOCK = 128
DSA_HEADS = 4
IDX_HEADS = 4
IDX_DIM = 64
DSA_TOPK_MAX = 256
DSA_BLOCK = 128
D_FF = 2816
N_EXPERTS = 8
TOP_K = 2
D_FF_EXPERT = 1408
N_DENSE = (DEPTH + 1) // 2
N_MOE = DEPTH // 2
N_BRANCHES = 4
DN_ALPHA = (2 * DEPTH) ** 0.25
DN_BETA = (8 * DEPTH) ** -0.25
LN_EPS = 1e-5
ATTN_SCALE = HEAD_DIM ** -0.5
IDX_SCALE = IDX_DIM ** -0.5
IDX_W_SCALE = IDX_HEADS ** -0.5

SWA_Q = SWA_HEADS * HEAD_DIM
SWA_KV = SWA_KV_HEADS * HEAD_DIM
RWKV_W = RWKV_HEADS * HEAD_DIM
FOX_W = FOX_HEADS * HEAD_DIM
DSA_W = DSA_HEADS * HEAD_DIM
MIX_WIDTHS = (SWA_Q, RWKV_W, FOX_W, DSA_W)
MIX_WIDTH = SWA_Q + RWKV_W + FOX_W + DSA_W
SWA_SPLITS = (SWA_Q, SWA_KV, SWA_KV)
RWKV_SPLITS = (RWKV_W, RWKV_W, RWKV_W, RWKV_W_RANK, RWKV_A_RANK, RWKV_G_RANK)
FOX_SPLITS = (FOX_W, FOX_W, FOX_W, FOX_HEADS)
DSA_SPLITS = (DSA_W, HEAD_DIM, HEAD_DIM, IDX_HEADS * IDX_DIM, IDX_DIM, IDX_HEADS)
RWKV_IN = sum(RWKV_SPLITS)
GATE_COLS = N_BRANCHES * D_MODEL
IN_GROUPS = (sum(SWA_SPLITS), RWKV_IN, sum(FOX_SPLITS), sum(DSA_SPLITS), GATE_COLS)
D_IN = sum(IN_GROUPS)

kernel_name = 'hybrid_gated_4mixer_deepnorm_moe'


def split_cols(t, sizes, axis=-1):
    idx = [int(i) for i in np.cumsum(sizes)[:-1]]
    return jnp.split(t, idx, axis=axis)


def layer_norm(x, g, b):
    xf = x.astype(jnp.float32)
    mu = jnp.mean(xf, axis=-1, keepdims=True)
    var = jnp.mean(jnp.square(xf - mu), axis=-1, keepdims=True)
    return ((xf - mu) * lax.rsqrt(var + LN_EPS) * g + b).astype(x.dtype)


def swa_branch(q, k, v, sinks):
    B, S, _ = q.shape
    nc = S // CHUNK
    W = SWA_WIN_CHUNKS
    q = q.reshape(B, nc, CHUNK, SWA_KV_HEADS, SWA_GROUP, HEAD_DIM)
    pad = ((0, 0), (W, 0), (0, 0), (0, 0), (0, 0))
    kp = jnp.pad(k.reshape(B, nc, CHUNK, SWA_KV_HEADS, HEAD_DIM), pad)
    vp = jnp.pad(v.reshape(B, nc, CHUNK, SWA_KV_HEADS, HEAD_DIM), pad)
    kb = jnp.concatenate([kp[:, j:j + nc] for j in range(W + 1)], axis=2)
    vb = jnp.concatenate([vp[:, j:j + nc] for j in range(W + 1)], axis=2)
    key_chunk = jnp.arange(nc)[:, None] - W + jnp.arange(W + 1)[None, :]
    valid = jnp.repeat(key_chunk >= 0, CHUNK, axis=1)
    logits = jnp.einsum('bcqhgd,bckhd->bchgqk', q, kb).astype(jnp.float32) * ATTN_SCALE
    logits = jnp.where(valid[None, :, None, None, None, :], logits, -jnp.inf)
    sink = sinks.astype(jnp.float32).reshape(1, 1, SWA_KV_HEADS, SWA_GROUP, 1, 1)
    m = jnp.maximum(jnp.max(logits, axis=-1, keepdims=True), sink)
    e = jnp.exp(logits - m)
    p = e / (jnp.sum(e, axis=-1, keepdims=True) + jnp.exp(sink - m))
    o = jnp.einsum('bchgqk,bckhd->bcqhgd', p.astype(v.dtype), vb)
    return o.reshape(B, S, SWA_Q)


def rwkv7_branch(p, mu, w0, w_up, a0, a_up, g_up, k_k, k_a, r_k, gn_g, gn_b):
    B, S, _ = p.shape
    p_prev = jnp.pad(p[:, :-1], ((0, 0), (1, 0), (0, 0)))
    p = p + mu * (p_prev - p)
    r, k, v, w_lo, a_lo, g_lo = split_cols(p, RWKV_SPLITS)
    log_w = -jnp.exp(-jax.nn.softplus(-(w0 + jnp.tanh(w_lo) @ w_up).astype(jnp.float32)) - 0.5)
    a = jax.nn.sigmoid(a0 + a_lo @ a_up)
    g = jax.nn.sigmoid(g_lo) @ g_up
    hd = lambda t: t.reshape(B, S, RWKV_HEADS, HEAD_DIM).astype(jnp.float32)
    kk = hd(k * k_k)
    kk = kk * lax.rsqrt(jnp.sum(kk * kk, axis=-1, keepdims=True) + 1e-12)
    k = k * (1.0 + (a - 1.0) * k_a)
    r_h, k_h, v_h, a_h, w_h = hd(r), hd(k), hd(v), hd(a), hd(jnp.exp(log_w))
    seq = tuple(jnp.moveaxis(t, 1, 0) for t in (r_h, w_h, k_h, v_h, kk, a_h))

    def step(state, inp):
        r_t, w_t, k_t, v_t, kk_t, a_t = inp
        sa = jnp.einsum('bhvk,bhk->bhv', state, -kk_t)
        state = (state * w_t[:, :, None, :]
                 + sa[..., None] * (kk_t * a_t)[:, :, None, :]
                 + v_t[..., None] * k_t[:, :, None, :])
        return state, jnp.einsum('bhvk,bhk->bhv', state, r_t)

    s0 = jnp.zeros((B, RWKV_HEADS, HEAD_DIM, HEAD_DIM), jnp.float32)
    _, y = lax.scan(step, s0, seq)
    y = jnp.moveaxis(y, 0, 1)
    ym = jnp.mean(y, axis=-1, keepdims=True)
    yv = jnp.mean(jnp.square(y - ym), axis=-1, keepdims=True)
    y = ((y - ym) * lax.rsqrt(yv + RWKV_GN_EPS)).reshape(B, S, RWKV_W) * gn_g + gn_b
    bonus = (jnp.sum(r_h * k_h * r_k, axis=-1, keepdims=True) * v_h).reshape(B, S, RWKV_W)
    return ((y + bonus) * g).astype(p.dtype)


def fox_branch(q, k, v, f_logit, f_bias):
    B, S, _ = q.shape
    q = q.reshape(B, S, FOX_HEADS, HEAD_DIM)
    k = k.reshape(B, S, FOX_HEADS, HEAD_DIM)
    v = v.reshape(B, S, FOX_HEADS, HEAD_DIM)
    log_f = jax.nn.log_sigmoid(f_logit.astype(jnp.float32) + f_bias)
    c = jnp.cumsum(log_f, axis=1).transpose(0, 2, 1)
    outs = []
    for blk in range(S // FOX_BLOCK):
        lo, hi = blk * FOX_BLOCK, (blk + 1) * FOX_BLOCK
        logits = jnp.einsum('bqhd,bkhd->bhqk', q[:, lo:hi], k[:, :hi]).astype(jnp.float32) * ATTN_SCALE
        logits = logits + c[:, :, lo:hi, None] - c[:, :, None, :hi]
        causal = jnp.arange(lo, hi)[:, None] >= jnp.arange(hi)[None, :]
        logits = jnp.where(causal, logits, -jnp.inf)
        p = jax.nn.softmax(logits, axis=-1)
        outs.append(jnp.einsum('bhqk,bkhd->bqhd', p.astype(v.dtype), v[:, :hi]))
    return jnp.concatenate(outs, axis=1).reshape(B, S, FOX_W)


def dsa_branch(q, k, v, q_idx, k_idx, w_idx):
    B, S, _ = q.shape
    topk = min(DSA_TOPK_MAX, S // 4)
    nb = S // DSA_BLOCK
    qT = jnp.moveaxis(q.reshape(B, nb, DSA_BLOCK, DSA_HEADS, HEAD_DIM), 1, 0)
    qiT = jnp.moveaxis(q_idx.reshape(B, nb, DSA_BLOCK, IDX_HEADS, IDX_DIM), 1, 0)
    wiT = jnp.moveaxis(w_idx.reshape(B, nb, DSA_BLOCK, IDX_HEADS), 1, 0)
    key_chunk = jnp.arange(S) // CHUNK

    def block(args):
        blk, qb, qib, wib = args
        t = blk * DSA_BLOCK + jnp.arange(DSA_BLOCK)
        admissible = key_chunk[None, :] <= (t // CHUNK)[:, None]
        rel = jax.nn.relu(jnp.einsum('bqhd,bsd->bqhs', qib, k_idx) * IDX_SCALE)
        score = jnp.einsum('bqhs,bqh->bqs', rel, wib * IDX_W_SCALE).astype(jnp.float32)
        score = jnp.where(admissible[None], score, -jnp.inf)
        top_val, top_idx = lax.top_k(score, topk)
        valid = jnp.isfinite(top_val)
        k_sel = jax.vmap(lambda kk, ii: kk[ii])(k, top_idx)
        v_sel = jax.vmap(lambda vv, ii: vv[ii])(v, top_idx)
        logits = jnp.einsum('bqhd,bqkd->bqhk', qb, k_sel).astype(jnp.float32) * ATTN_SCALE
        logits = jnp.where(valid[:, :, None, :], logits, -jnp.inf)
        p = jax.nn.softmax(logits, axis=-1)
        return jnp.einsum('bqhk,bqkd->bqhd', p.astype(v.dtype), v_sel)

    outs = lax.map(block, (jnp.arange(nb), qT, qiT, wiT))
    return jnp.moveaxis(outs, 0, 1).reshape(B, S, DSA_W)


def mixer_block(x, w_in, sinks, mu, w0, w_up, a0, a_up, g_up, k_k, k_a, r_k, gn_g, gn_b,
                f_bias, gate_bias, w_branch, w_out):
    B, S, _ = x.shape
    p_swa, p_rwkv, p_fox, p_dsa, p_gate = split_cols(x @ w_in, IN_GROUPS)
    o_swa = swa_branch(*split_cols(p_swa, SWA_SPLITS), sinks)
    o_rwkv = rwkv7_branch(p_rwkv, mu, w0, w_up, a0, a_up, g_up, k_k, k_a, r_k, gn_g, gn_b)
    o_fox = fox_branch(*split_cols(p_fox, FOX_SPLITS), f_bias)
    o_dsa = dsa_branch(*split_cols(p_dsa, DSA_SPLITS))
    gates = jax.nn.sigmoid(p_gate.reshape(B, S, N_BRANCHES, D_MODEL) + gate_bias)
    w_rows = split_cols(w_branch, MIX_WIDTHS, axis=0)
    merged = jnp.zeros_like(x)
    for i, (o, w) in enumerate(zip((o_swa, o_rwkv, o_fox, o_dsa), w_rows)):
        merged = merged + gates[:, :, i] * (o @ w)
    return merged @ w_out


def swiglu(x, w1, w3, w2):
    return (jax.nn.silu(x @ w1) * (x @ w3)) @ w2


def moe_ffn(x, router_w, router_b, w1, w3, w2):
    logits = (x @ router_w).astype(jnp.float32) + router_b
    top_val, top_idx = lax.top_k(logits, TOP_K)
    weights = jax.nn.softmax(top_val, axis=-1)
    combine = jnp.sum(jax.nn.one_hot(top_idx, N_EXPERTS, dtype=jnp.float32) * weights[..., None], axis=-2)
    y = jnp.zeros_like(x)
    for e in range(N_EXPERTS):
        y = y + combine[..., e:e + 1].astype(x.dtype) * swiglu(x, w1[e], w3[e], w2[e])
    return y


def setup_inputs(seed: int = 0) -> dict:
    key = jax.random.key(seed)
    ks = iter(jax.random.split(key, 32))
    nrm = lambda shape, scale: scale * jax.random.normal(next(ks), shape, jnp.float32)
    unif = lambda shape, lo, hi: jax.random.uniform(next(ks), shape, jnp.float32, lo, hi)
    L = DEPTH
    return {
        'x': nrm((BATCH, SEQ, D_MODEL), 1.0),
        'w_in': nrm((L, D_MODEL, D_IN), D_MODEL ** -0.5),
        'swa_sinks': nrm((L, SWA_HEADS), 0.5),
        'rwkv_mu': unif((L, RWKV_IN), 0.0, 1.0),
        'rwkv_w0': unif((L, RWKV_W), -6.0, 1.0),
        'rwkv_w_up': nrm((L, RWKV_W_RANK, RWKV_W), 0.1),
        'rwkv_a0': nrm((L, RWKV_W), 0.1),
        'rwkv_a_up': nrm((L, RWKV_A_RANK, RWKV_W), RWKV_A_RANK ** -0.5),
        'rwkv_g_up': nrm((L, RWKV_G_RANK, RWKV_W), RWKV_G_RANK ** -0.5),
        'rwkv_k_k': 0.85 + nrm((L, RWKV_W), 0.05),
        'rwkv_k_a': 1.0 + nrm((L, RWKV_W), 0.05),
        'rwkv_r_k': nrm((L, RWKV_HEADS, HEAD_DIM), 0.1),
        'rwkv_gn_g': 1.0 + nrm((L, RWKV_W), 0.05),
        'rwkv_gn_b': nrm((L, RWKV_W), 0.02),
        'fox_f_bias': 2.0 + nrm((L, FOX_HEADS), 0.5),
        'gate_bias': nrm((L, N_BRANCHES, D_MODEL), 0.1),
        'w_branch': nrm((L, MIX_WIDTH, D_MODEL), (4 * HEAD_DIM) ** -0.5),
        'w_out': nrm((L, D_MODEL, D_MODEL), DN_BETA * D_MODEL ** -0.5),
        'ln_g': 1.0 + nrm((L, 2, D_MODEL), 0.05),
        'ln_b': nrm((L, 2, D_MODEL), 0.02),
        'ffn_w1': nrm((N_DENSE, D_MODEL, D_FF), D_MODEL ** -0.5),
        'ffn_w3': nrm((N_DENSE, D_MODEL, D_FF), D_MODEL ** -0.5),
        'ffn_w2': nrm((N_DENSE, D_FF, D_MODEL), DN_BETA * D_FF ** -0.5),
        'router_w': nrm((N_MOE, D_MODEL, N_EXPERTS), D_MODEL ** -0.5),
        'router_b': nrm((N_MOE, N_EXPERTS), 0.01),
        'exp_w1': nrm((N_MOE, N_EXPERTS, D_MODEL, D_FF_EXPERT), D_MODEL ** -0.5),
        'exp_w3': nrm((N_MOE, N_EXPERTS, D_MODEL, D_FF_EXPERT), D_MODEL ** -0.5),
        'exp_w2': nrm((N_MOE, N_EXPERTS, D_FF_EXPERT, D_MODEL), DN_BETA * D_FF_EXPERT ** -0.5),
    }


def reference(x, w_in, swa_sinks, rwkv_mu, rwkv_w0, rwkv_w_up, rwkv_a0, rwkv_a_up, rwkv_g_up,
              rwkv_k_k, rwkv_k_a, rwkv_r_k, rwkv_gn_g, rwkv_gn_b, fox_f_bias, gate_bias, w_branch,
              w_out, ln_g, ln_b, ffn_w1, ffn_w3, ffn_w2, router_w, router_b, exp_w1, exp_w3, exp_w2):
    for layer in range(DEPTH):
        y = mixer_block(x, w_in[layer], swa_sinks[layer], rwkv_mu[layer], rwkv_w0[layer],
                        rwkv_w_up[layer], rwkv_a0[layer], rwkv_a_up[layer], rwkv_g_up[layer],
                        rwkv_k_k[layer], rwkv_k_a[layer], rwkv_r_k[layer], rwkv_gn_g[layer],
                        rwkv_gn_b[layer], fox_f_bias[layer], gate_bias[layer], w_branch[layer],
                        w_out[layer])
        x = layer_norm(DN_ALPHA * x + y, ln_g[layer, 0], ln_b[layer, 0])
        j = layer // 2
        if layer % 2 == 0:
            y = swiglu(x, ffn_w1[j], ffn_w3[j], ffn_w2[j])
        else:
            y = moe_ffn(x, router_w[j], router_b[j], exp_w1[j], exp_w3[j], exp_w2[j])
        x = layer_norm(DN_ALPHA * x + y, ln_g[layer, 1], ln_b[layer, 1])
    return x
```

```python
import functools

import jax
import jax.numpy as jnp
from jax import lax
from jax.experimental import pallas as pl
from jax.experimental.pallas import tpu as pltpu

F32 = jnp.float32
BF16 = jnp.bfloat16
HIGHEST = lax.Precision.HIGHEST

D_MODEL = 1024
DEPTH = 2
CHUNK = 64
HEAD_DIM = 64
CHUNK_SHIFT = 6
HEAD_SHIFT = 6
SWA_HEADS = 8
SWA_KV_HEADS = 2
SWA_GROUP = SWA_HEADS // SWA_KV_HEADS
SWA_WINDOW = 128
SWA_WIN_CHUNKS = SWA_WINDOW // CHUNK
RWKV_HEADS = 4
RWKV_W = RWKV_HEADS * HEAD_DIM
RWKV_GN_EPS = 64e-5
FOX_HEADS = 4
DSA_HEADS = 4
IDX_HEADS = 4
IDX_DIM = 64
DSA_TOPK_MAX = 256
D_FF = 2816
N_EXPERTS = 8
D_FF_EXPERT = 1408
N_BRANCHES = 4
DN_ALPHA = (2 * DEPTH) ** 0.25
LN_EPS = 1e-5
ATTN_SCALE = HEAD_DIM ** -0.5
IDX_SCALE = IDX_DIM ** -0.5
IDX_W_SCALE = IDX_HEADS ** -0.5
MIX_WIDTHS = (SWA_HEADS * HEAD_DIM, RWKV_W, FOX_HEADS * HEAD_DIM, DSA_HEADS * HEAD_DIM)

COL_GATE = 0
COL_RWKV = 4096
COL_SWA_Q = 5120
COL_FOX_KV = 5632
COL_DSA_QQ = 6144
COL_FOX_Q = 6656
COL_SWA_KV = 6912
COL_DSA_KV = 7168
COL_MISC = 7424
P_WIDTH = 7680

NEG_BIG = -1e30
INT_MIN = -(2 ** 31)
VMEM_LIMIT = 48 * 1024 * 1024


def _layout_w_in(w):
    d = w.shape[0]
    z = lambda n: jnp.zeros((d, n), w.dtype)
    swa, rwkv, fox, dsa, gate = 0, 768, 1792, 2564, 3272
    parts = [
        w[:, gate:gate + 4096],
        w[:, rwkv:rwkv + 1024],
        w[:, swa:swa + 512],
        w[:, fox + 256:fox + 768],
        w[:, dsa:dsa + 256], w[:, dsa + 384:dsa + 640],
        w[:, fox:fox + 256],
        w[:, swa + 512:swa + 768],
        w[:, dsa + 256:dsa + 384], w[:, dsa + 640:dsa + 704], z(64),
        w[:, fox + 768:fox + 772], w[:, dsa + 704:dsa + 708], z(120),
        z(128),
    ]
    return jnp.concatenate(parts, axis=1)


def _cparams(sem):
    return pltpu.CompilerParams(dimension_semantics=sem, vmem_limit_bytes=VMEM_LIMIT)


def _nt(a, b, precision=None):
    return lax.dot_general(a, b, (((1,), (1,)), ((), ())), precision=precision,
                           preferred_element_type=F32)


def _tn(a, b, precision=None):
    return lax.dot_general(a, b, (((0,), (0,)), ((), ())), precision=precision,
                           preferred_element_type=F32)


def _dot(a, b, precision=None):
    return jnp.dot(a, b, precision=precision, preferred_element_type=F32)


def _sigmoid(x):
    return 1.0 / (1.0 + jnp.exp(-x))


def _layer_norm(z, g, b):
    mu = jnp.mean(z, axis=-1, keepdims=True)
    zc = z - mu
    var = jnp.mean(zc * zc, axis=-1, keepdims=True)
    return zc * lax.rsqrt(var + LN_EPS) * g + b


def _head_block_ones(n, scale):
    r = lax.broadcasted_iota(jnp.int32, (n, n), 0) >> HEAD_SHIFT
    c = lax.broadcasted_iota(jnp.int32, (n, n), 1) >> HEAD_SHIFT
    return jnp.where(r == c, scale, 0.0).astype(F32)


def _mm_kernel(a_ref, b_ref, o_ref):
    o_ref[...] = _dot(a_ref[...], b_ref[...])


def _matmul(a, b, tm, tn):
    m, k = a.shape
    n = b.shape[1]
    tm = min(tm, m)
    return pl.pallas_call(
        _mm_kernel,
        grid=(m // tm, n // tn),
        in_specs=[pl.BlockSpec((tm, k), lambda i, j: (i, 0)),
                  pl.BlockSpec((k, tn), lambda i, j: (0, j))],
        out_specs=pl.BlockSpec((tm, tn), lambda i, j: (i, j)),
        out_shape=jax.ShapeDtypeStruct((m, n), F32),
        compiler_params=_cparams(("parallel", "parallel")),
        name="in_proj",
    )(a, b)


def _swa_kernel(sink_ref, q_ref, kv_ref, o_ref, *, tq):
    i = pl.program_id(1)
    win = tq + SWA_WINDOW
    s0 = pl.multiple_of(jnp.maximum(i * tq - SWA_WINDOW, 0), SWA_WINDOW)
    q_chunk = (i * tq + lax.broadcasted_iota(jnp.int32, (tq, win), 0)) >> CHUNK_SHIFT
    k_chunk = (s0 + lax.broadcasted_iota(jnp.int32, (tq, win), 1)) >> CHUNK_SHIFT
    bias = jnp.where(k_chunk <= q_chunk,
                     jnp.where(k_chunk >= q_chunk - SWA_WIN_CHUNKS, 0.0, NEG_BIG), NEG_BIG)
    kv_w = SWA_KV_HEADS * HEAD_DIM
    for hk in range(SWA_KV_HEADS):
        k = kv_ref[0, pl.ds(s0, win), hk * HEAD_DIM:(hk + 1) * HEAD_DIM].astype(BF16)
        v = kv_ref[0, pl.ds(s0, win), kv_w + hk * HEAD_DIM:kv_w + (hk + 1) * HEAD_DIM].astype(BF16)
        for g in range(SWA_GROUP):
            h = hk * SWA_GROUP + g
            q = (q_ref[0, :, h * HEAD_DIM:(h + 1) * HEAD_DIM] * ATTN_SCALE).astype(BF16)
            s = _nt(q, k) + bias
            sink = sink_ref[h]
            m = jnp.maximum(jnp.max(s, axis=-1, keepdims=True), sink)
            e = jnp.exp(s - m)
            denom = jnp.sum(e, axis=-1, keepdims=True) + jnp.exp(sink - m)
            o = _dot(e.astype(BF16), v) / denom
            o_ref[0, :, h * HEAD_DIM:(h + 1) * HEAD_DIM] = o


def _swa(p3, sinks, tq=256):
    b, s, _ = p3.shape
    return pl.pallas_call(
        functools.partial(_swa_kernel, tq=tq),
        grid=(b, s // tq),
        in_specs=[pl.BlockSpec(memory_space=pltpu.SMEM),
                  pl.BlockSpec((1, tq, 512), lambda bi, i: (bi, i, COL_SWA_Q // 512)),
                  pl.BlockSpec((1, s, 256), lambda bi, i: (bi, 0, COL_SWA_KV // 256))],
        out_specs=pl.BlockSpec((1, tq, 512), lambda bi, i: (bi, i, 0)),
        out_shape=jax.ShapeDtypeStruct((b, s, 512), F32),
        compiler_params=_cparams(("parallel", "parallel")),
        name="swa",
    )(sinks, p3, p3)


def _fox_c_kernel(f_ref, b_ref, c_ref, *, blk):
    s = f_ref.shape[2]
    r = lax.broadcasted_iota(jnp.int32, (blk, blk), 0)
    c = lax.broadcasted_iota(jnp.int32, (blk, blk), 1)
    tri = jnp.where(r <= c, 1.0, 0.0).astype(F32)
    carry = jnp.zeros((8, 1), F32)
    for j in range(s // blk):
        x = f_ref[0, :, j * blk:(j + 1) * blk] + b_ref[...]
        log_f = jnp.minimum(x, 0.0) - jnp.log1p(jnp.exp(-jnp.abs(x)))
        cs = _dot(log_f, tri, HIGHEST) + carry
        c_ref[0, :, j * blk:(j + 1) * blk] = cs
        carry = cs[:, blk - 1:blk]


def _fox_c(f_t, bias8):
    b, _, s = f_t.shape
    return pl.pallas_call(
        functools.partial(_fox_c_kernel, blk=256),
        grid=(b,),
        in_specs=[pl.BlockSpec((1, 8, s), lambda bi: (bi, 0, 0)),
                  pl.BlockSpec((8, 1), lambda bi: (0, 0))],
        out_specs=pl.BlockSpec((1, 8, s), lambda bi: (bi, 0, 0)),
        out_shape=jax.ShapeDtypeStruct((b, 8, s), F32),
        compiler_params=_cparams(("parallel",)),
        name="fox_cumsum",
    )(f_t, bias8)


def _fox_kernel(q_ref, kv_ref, c_ref, o_ref, *, tq):
    i = pl.program_id(1)
    tk = tq
    row = lax.broadcasted_iota(jnp.int32, (tq, tk), 0)
    col = lax.broadcasted_iota(jnp.int32, (tq, tk), 1)
    diag_bias = jnp.where(col <= row, 0.0, NEG_BIG)
    w = FOX_HEADS * HEAD_DIM
    for h in range(FOX_HEADS):
        q = (q_ref[0, :, h * HEAD_DIM:(h + 1) * HEAD_DIM] * ATTN_SCALE).astype(BF16)

        def block(j, carry, masked, h=h, q=q):
            m, l, acc = carry
            ks = pl.multiple_of(j * tk, tk)
            k = kv_ref[0, pl.ds(ks, tk), h * HEAD_DIM:(h + 1) * HEAD_DIM].astype(BF16)
            v = kv_ref[0, pl.ds(ks, tk), w + h * HEAD_DIM:w + (h + 1) * HEAD_DIM].astype(BF16)
            s = _nt(q, k) - c_ref[0, h:h + 1, pl.ds(ks, tk)]
            if masked:
                s = s + diag_bias
            m_new = jnp.maximum(m, jnp.max(s, axis=-1, keepdims=True))
            alpha = jnp.exp(m - m_new)
            p = jnp.exp(s - m_new)
            l = alpha * l + jnp.sum(p, axis=-1, keepdims=True)
            acc = alpha * acc + _dot(p.astype(BF16), v)
            return m_new, l, acc

        init = (jnp.full((tq, 1), NEG_BIG, F32), jnp.zeros((tq, 1), F32), jnp.zeros((tq, HEAD_DIM), F32))
        carry = lax.fori_loop(0, i, lambda j, cr: block(j, cr, False), init)
        _, l, acc = block(i, carry, True)
        o_ref[0, :, h * HEAD_DIM:(h + 1) * HEAD_DIM] = acc / l


def _fox(p3, c, tq=256):
    b, s, _ = p3.shape
    return pl.pallas_call(
        functools.partial(_fox_kernel, tq=tq),
        grid=(b, s // tq),
        in_specs=[pl.BlockSpec((1, tq, 256), lambda bi, i: (bi, i, COL_FOX_Q // 256)),
                  pl.BlockSpec((1, s, 512), lambda bi, i: (bi, 0, COL_FOX_KV // 512)),
                  pl.BlockSpec((1, 8, s), lambda bi, i: (bi, 0, 0))],
        out_specs=pl.BlockSpec((1, tq, 256), lambda bi, i: (bi, i, 0)),
        out_shape=jax.ShapeDtypeStruct((b, s, 256), F32),
        compiler_params=_cparams(("parallel", "parallel")),
        name="fox",
    )(p3, p3, c)


def _dsa_kernel(qq_ref, misc_ref, kv_ref, vt_ref, o_ref, keys_ref, *, tq, kb, topk):
    i = pl.program_id(1)
    sub = 64
    n_kb = lax.div((i + 1) * tq + kb - 1, kb)
    lane8 = lax.broadcasted_iota(jnp.int32, (8, 128), 1)
    row8 = lax.broadcasted_iota(jnp.int32, (8, 128), 0)
    pick = jnp.where((lane8 == row8 + 4) & (row8 < IDX_HEADS), 1.0, 0.0).astype(F32)
    w_t = _nt(pick, misc_ref[0], HIGHEST) * IDX_W_SCALE
    q_chunk = (i * tq + lax.broadcasted_iota(jnp.int32, (1, tq), 1)) >> CHUNK_SHIFT
    qw = DSA_HEADS * HEAD_DIM
    q_idx = [qq_ref[0, :, qw + h * IDX_DIM:qw + (h + 1) * IDX_DIM].astype(BF16) for h in range(IDX_HEADS)]

    def score_block(j, carry):
        ks = pl.multiple_of(j * kb, kb)
        k_idx = kv_ref[0, pl.ds(ks, kb), 2 * HEAD_DIM:2 * HEAD_DIM + IDX_DIM].astype(BF16)
        score = jnp.zeros((kb, tq), F32)
        for h in range(IDX_HEADS):
            rel = jnp.maximum(_nt(k_idx, q_idx[h]) * IDX_SCALE, 0.0)
            score = score + rel * w_t[h:h + 1, :]
        score = jnp.where(score == 0.0, 0.0, score)
        bits = pltpu.bitcast(score, jnp.int32)
        key = bits ^ ((bits >> 31) & 0x7FFFFFFF)
        k_chunk = (ks + lax.broadcasted_iota(jnp.int32, (kb, tq), 0)) >> CHUNK_SHIFT
        keys_ref[pl.ds(ks, kb), :] = jnp.where(k_chunk <= q_chunk, key, INT_MIN)
        return carry

    lax.fori_loop(0, n_kb, score_block, 0)

    n_sub = n_kb * (kb // sub)

    def count(pred):
        def body(r, acc):
            blk = keys_ref[pl.ds(pl.multiple_of(r * sub, sub), sub), :]
            hit = jnp.where(pred(blk), 1, 0).astype(jnp.int32)
            return acc + jnp.sum(hit.reshape(sub // 8, 8, tq), axis=0)
        acc = lax.fori_loop(0, n_sub, body, jnp.zeros((8, tq), jnp.int32))
        return jnp.sum(acc.astype(F32), axis=0, keepdims=True)

    zero = jnp.zeros((1, tq), jnp.int32)
    thr = jnp.where(count(lambda blk: blk >= zero) >= topk, 0, INT_MIN).astype(jnp.int32)

    def bit_step(bi, thr):
        cand = thr | lax.shift_left(jnp.int32(1), 30 - bi)
        return jnp.where(count(lambda blk: blk >= cand) >= topk, cand, thr)

    thr = lax.fori_loop(0, 31, bit_step, thr)
    need = topk - count(lambda blk: blk > thr)

    r = lax.broadcasted_iota(jnp.int32, (kb, kb), 0)
    c = lax.broadcasted_iota(jnp.int32, (kb, kb), 1)
    earlier = jnp.where(c < r, 1.0, 0.0).astype(BF16)
    q_att = [(qq_ref[0, :, h * HEAD_DIM:(h + 1) * HEAD_DIM] * ATTN_SCALE).astype(BF16)
             for h in range(DSA_HEADS)]

    def attend_block(j, carry):
        ties_before, stats = carry
        ks = pl.multiple_of(j * kb, kb)
        key = keys_ref[pl.ds(ks, kb), :]
        tie = jnp.where(key == thr, jnp.where(key != INT_MIN, 1.0, 0.0), 0.0)
        rank = _dot(earlier, tie.astype(BF16)) + ties_before
        tie_taken = jnp.where(rank < need, tie, 0.0)
        bias = jnp.where(key > thr, 0.0, jnp.where(tie_taken > 0.5, 0.0, NEG_BIG))
        ties_before = ties_before + jnp.sum(tie, axis=0, keepdims=True)
        k = kv_ref[0, pl.ds(ks, kb), 0:HEAD_DIM].astype(BF16)
        v_t = vt_ref[0, :, pl.ds(ks, kb)].astype(BF16)
        new_stats = []
        for h in range(DSA_HEADS):
            m, l, acc = stats[h]
            s = _nt(k, q_att[h]) + bias
            m_new = jnp.maximum(m, jnp.max(s, axis=0, keepdims=True))
            alpha = jnp.exp(m - m_new)
            p = jnp.exp(s - m_new)
            l = alpha * l + jnp.sum(p, axis=0, keepdims=True)
            acc = alpha * acc + _dot(v_t, p.astype(BF16))
            new_stats.append((m_new, l, acc))
        return ties_before, tuple(new_stats)

    init = (jnp.zeros((1, tq), F32),
            tuple((jnp.full((1, tq), NEG_BIG, F32), jnp.zeros((1, tq), F32), jnp.zeros((HEAD_DIM, tq), F32))
                  for _ in range(DSA_HEADS)))
    _, stats = lax.fori_loop(0, n_kb, attend_block, init)
    out_t = jnp.concatenate([acc / l for (_, l, acc) in stats], axis=0)
    o_ref[0] = out_t.T


def _dsa(p3, v_t, tq=128, kb=256):
    b, s, _ = p3.shape
    topk = min(DSA_TOPK_MAX, s // 4)
    return pl.pallas_call(
        functools.partial(_dsa_kernel, tq=tq, kb=kb, topk=topk),
        grid=(b, s // tq),
        in_specs=[pl.BlockSpec((1, tq, 512), lambda bi, i: (bi, i, COL_DSA_QQ // 512)),
                  pl.BlockSpec((1, tq, 128), lambda bi, i: (bi, i, COL_MISC // 128)),
                  pl.BlockSpec((1, s, 256), lambda bi, i: (bi, 0, COL_DSA_KV // 256)),
                  pl.BlockSpec((1, HEAD_DIM, s), lambda bi, i: (bi, 0, 0))],
        out_specs=pl.BlockSpec((1, tq, 256), lambda bi, i: (bi, i, 0)),
        out_shape=jax.ShapeDtypeStruct((b, s, 256), F32),
        scratch_shapes=[pltpu.VMEM((s, tq), jnp.int32)],
        compiler_params=_cparams(("parallel", "parallel")),
        name="dsa",
    )(p3, p3, p3, v_t)


def _rwkv_pre_kernel(p_ref, mu_ref, w0_ref, wup_ref, a0_ref, aup_ref, gup_ref, kk_ref, ka_ref, rk_ref,
                     r_o, lw_o, k_o, v_o, kn_o, b_o, g_o, bonus_o, last_ref, *, tt):
    t = pl.program_id(1)

    @pl.when(t == 0)
    def _():
        last_ref[...] = jnp.zeros_like(last_ref)

    p = p_ref[0]
    row = lax.broadcasted_iota(jnp.int32, p.shape, 0)
    p_prev = jnp.where(row == 0, last_ref[...], pltpu.roll(p, 1, axis=0))
    last_ref[...] = p[tt - 1:tt, :]
    ps = p + mu_ref[...] * (p_prev - p)
    w = RWKV_W
    r, k, v = ps[:, 0:w], ps[:, w:2 * w], ps[:, 2 * w:3 * w]
    w_lo, a_lo, g_lo = ps[:, 3 * w:3 * w + 64], ps[:, 3 * w + 64:3 * w + 128], ps[:, 3 * w + 128:3 * w + 256]
    ww = w0_ref[...] + _dot(jnp.tanh(w_lo), wup_ref[...], HIGHEST)
    softplus_neg = jnp.maximum(-ww, 0.0) + jnp.log1p(jnp.exp(-jnp.abs(ww)))
    log_w = -jnp.exp(-softplus_neg - 0.5)
    a = _sigmoid(a0_ref[...] + _dot(a_lo, aup_ref[...], HIGHEST))
    g = _dot(_sigmoid(g_lo), gup_ref[...], HIGHEST)
    head_sum = _head_block_ones(w, 1.0)
    kn = k * kk_ref[...]
    kn = kn * lax.rsqrt(_dot(kn * kn, head_sum, HIGHEST) + 1e-12)
    k2 = k * (1.0 + (a - 1.0) * ka_ref[...])
    bonus = _dot(r * k2 * rk_ref[...], head_sum, HIGHEST) * v
    r_o[0] = r
    lw_o[0] = log_w
    k_o[0] = k2
    v_o[0] = v
    kn_o[0] = kn
    b_o[0] = kn * a
    g_o[0] = g
    bonus_o[0] = bonus


def _rwkv_pre(p3, mu, w0, w_up, a0, a_up, g_up, k_k, k_a, r_k, tt=256):
    b, s, _ = p3.shape
    w = RWKV_W
    row = lambda x: x.reshape(1, -1)
    full = lambda shape: pl.BlockSpec(shape, lambda bi, t: (0,) * len(shape))
    out = jax.ShapeDtypeStruct((b, s, w), F32)
    return pl.pallas_call(
        functools.partial(_rwkv_pre_kernel, tt=tt),
        grid=(b, s // tt),
        in_specs=[pl.BlockSpec((1, tt, 1024), lambda bi, t: (bi, t, COL_RWKV // 1024)),
                  full((1, 1024)), full((1, w)), full((64, w)), full((1, w)), full((64, w)),
                  full((128, w)), full((1, w)), full((1, w)), full((1, w))],
        out_specs=[pl.BlockSpec((1, tt, w), lambda bi, t: (bi, t, 0))] * 8,
        out_shape=[out] * 8,
        scratch_shapes=[pltpu.VMEM((1, 1024), F32)],
        compiler_params=_cparams(("parallel", "arbitrary")),
        name="rwkv_pre",
    )(p3, row(mu), row(w0), w_up, row(a0), a_up, g_up, row(k_k), row(k_a), row(r_k))


def _rwkv_chunk_kernel(r_ref, lw_ref, k_ref, v_ref, kn_ref, b_ref, y_ref, s_ref, *, cs):
    c = pl.program_id(1)

    @pl.when(c == 0)
    def _():
        s_ref[...] = jnp.zeros_like(s_ref)

    row = lax.broadcasted_iota(jnp.int32, (cs, cs), 0)
    col = lax.broadcasted_iota(jnp.int32, (cs, cs), 1)
    incl = col <= row
    strict = col < row
    tri = jnp.where(incl, 1.0, 0.0).astype(F32)
    eye = jnp.where(col == row, 1.0, 0.0).astype(F32)
    lw = lw_ref[0]
    cum = _dot(tri, lw, HIGHEST)
    e_incl = jnp.exp(cum)
    e_neg = jnp.exp(-cum)
    abar_all = -kn_ref[0] * jnp.exp(cum - lw)
    rbar_all = r_ref[0] * e_incl
    bt_all = b_ref[0] * e_neg
    kt_all = k_ref[0] * e_neg
    p_last_all = e_incl[cs - 1:cs, :]
    bh_all = bt_all * p_last_all
    kh_all = kt_all * p_last_all
    v_all = v_ref[0]
    n_double = max(cs.bit_length() - 2, 0)
    for h in range(RWKV_HEADS):
        sl = slice(h * HEAD_DIM, (h + 1) * HEAD_DIM)
        abar, rbar, bt, kt, v = abar_all[:, sl], rbar_all[:, sl], bt_all[:, sl], kt_all[:, sl], v_all[:, sl]
        a_ab = jnp.where(strict, _nt(abar, bt, HIGHEST), 0.0)
        a_ak = jnp.where(strict, _nt(abar, kt, HIGHEST), 0.0)
        a_rb = jnp.where(incl, _nt(rbar, bt, HIGHEST), 0.0)
        a_rk = jnp.where(incl, _nt(rbar, kt, HIGHEST), 0.0)
        inv = eye + a_ab
        power = a_ab
        for _ in range(n_double):
            power = _dot(power, power, HIGHEST)
            inv = inv + _dot(inv, power, HIGHEST)
        a_hat = _dot(inv, abar, HIGHEST)
        u_hat = _dot(inv, _dot(a_ak, v, HIGHEST), HIGHEST)
        r_hat = rbar + _dot(a_rb, a_hat, HIGHEST)
        y_hat = _dot(a_rb, u_hat, HIGHEST) + _dot(a_rk, v, HIGHEST)
        g_mat = eye * p_last_all[:, sl] + _tn(a_hat, bh_all[:, sl], HIGHEST)
        h_mat = _tn(u_hat, bh_all[:, sl], HIGHEST) + _tn(v, kh_all[:, sl], HIGHEST)
        state = s_ref[h]
        y_ref[0, :, sl] = _nt(r_hat, state, HIGHEST) + y_hat
        s_ref[h] = _dot(state, g_mat, HIGHEST) + h_mat


def _rwkv_chunk(r, lw, k, v, kn, b_, cs=CHUNK):
    b, s, w = r.shape
    spec = pl.BlockSpec((1, cs, w), lambda bi, c: (bi, c, 0))
    return pl.pallas_call(
        functools.partial(_rwkv_chunk_kernel, cs=cs),
        grid=(b, s // cs),
        in_specs=[spec] * 6,
        out_specs=spec,
        out_shape=jax.ShapeDtypeStruct((b, s, w), F32),
        scratch_shapes=[pltpu.VMEM((RWKV_HEADS, HEAD_DIM, HEAD_DIM), F32)],
        compiler_params=_cparams(("parallel", "arbitrary")),
        name="rwkv_chunk",
    )(r, lw, k, v, kn, b_)


def _merge_kernel(x_ref, gate_ref, swa_ref, y_ref, g_ref, bonus_ref, fox_ref, dsa_ref, gng_ref, gnb_ref,
                  gbias_ref, wb_ref, wo_ref, lng_ref, lnb_ref, o_ref, ob_ref):
    y = y_ref[...]
    head_mean = _head_block_ones(RWKV_W, 1.0 / HEAD_DIM)
    yc = y - _dot(y, head_mean, HIGHEST)
    yv = _dot(yc * yc, head_mean, HIGHEST)
    o_rwkv = (yc * lax.rsqrt(yv + RWKV_GN_EPS) * gng_ref[...] + gnb_ref[...] + bonus_ref[...]) * g_ref[...]
    branches = (swa_ref[...], o_rwkv, fox_ref[...], dsa_ref[...])
    merged = jnp.zeros(o_ref.shape, F32)
    off = 0
    for i, o in enumerate(branches):
        width = MIX_WIDTHS[i]
        proj = _dot(o.astype(BF16), wb_ref[off:off + width, :])
        gate = _sigmoid(gate_ref[:, i * D_MODEL:(i + 1) * D_MODEL] + gbias_ref[i:i + 1, :])
        merged = merged + gate * proj
        off += width
    y_out = _dot(merged.astype(BF16), wo_ref[...])
    xn = _layer_norm(DN_ALPHA * x_ref[...] + y_out, lng_ref[...], lnb_ref[...])
    o_ref[...] = xn
    ob_ref[...] = xn.astype(BF16)


def _merge(xf, p2, o_swa, y, g, bonus, o_fox, o_dsa, gn_g, gn_b, gate_bias, w_branch, w_out, ln_g, ln_b, tm=256):
    n, d = xf.shape
    row = lambda x: x.reshape(1, -1)
    tok = lambda wdt: pl.BlockSpec((tm, wdt), lambda i: (i, 0))
    full = lambda shape: pl.BlockSpec(shape, lambda i: (0,) * len(shape))
    return pl.pallas_call(
        _merge_kernel,
        grid=(n // tm,),
        in_specs=[tok(d), pl.BlockSpec((tm, N_BRANCHES * d), lambda i: (i, COL_GATE)),
                  tok(512), tok(256), tok(256), tok(256), tok(256), tok(256),
                  full((1, 256)), full((1, 256)), full((N_BRANCHES, d)),
                  full((sum(MIX_WIDTHS), d)), full((d, d)), full((1, d)), full((1, d))],
        out_specs=[tok(d), tok(d)],
        out_shape=[jax.ShapeDtypeStruct((n, d), F32), jax.ShapeDtypeStruct((n, d), BF16)],
        compiler_params=_cparams(("parallel",)),
        name="merge",
    )(xf, p2, o_swa, y, g, bonus, o_fox, o_dsa, row(gn_g), row(gn_b), gate_bias, w_branch, w_out,
      row(ln_g), row(ln_b))


def _silu(x):
    return x * _sigmoid(x)


def _ffn_kernel(xb_ref, x_ref, w1_ref, w3_ref, w2_ref, lng_ref, lnb_ref, o_ref, ob_ref, acc_ref):
    f = pl.program_id(1)

    @pl.when(f == 0)
    def _():
        acc_ref[...] = jnp.zeros_like(acc_ref)

    xb = xb_ref[...]
    hidden = _silu(_dot(xb, w1_ref[...])) * _dot(xb, w3_ref[...])
    acc_ref[...] += _dot(hidden.astype(BF16), w2_ref[...])

    @pl.when(f == pl.num_programs(1) - 1)
    def _():
        xn = _layer_norm(DN_ALPHA * x_ref[...] + acc_ref[...], lng_ref[...], lnb_ref[...])
        o_ref[...] = xn
        ob_ref[...] = xn.astype(BF16)


def _ffn(xb, xf, w1, w3, w2, ln_g, ln_b, tm=512, tf=1408):
    n, d = xf.shape
    ff = w1.shape[1]
    row = lambda x: x.reshape(1, -1)
    return pl.pallas_call(
        _ffn_kernel,
        grid=(n // tm, ff // tf),
        in_specs=[pl.BlockSpec((tm, d), lambda i, f: (i, 0)),
                  pl.BlockSpec((tm, d), lambda i, f: (i, 0)),
                  pl.BlockSpec((d, tf), lambda i, f: (0, f)),
                  pl.BlockSpec((d, tf), lambda i, f: (0, f)),
                  pl.BlockSpec((tf, d), lambda i, f: (f, 0)),
                  pl.BlockSpec((1, d), lambda i, f: (0, 0)),
                  pl.BlockSpec((1, d), lambda i, f: (0, 0))],
        out_specs=[pl.BlockSpec((tm, d), lambda i, f: (i, 0))] * 2,
        out_shape=[jax.ShapeDtypeStruct((n, d), F32), jax.ShapeDtypeStruct((n, d), BF16)],
        scratch_shapes=[pltpu.VMEM((tm, d), F32)],
        compiler_params=_cparams(("parallel", "arbitrary")),
        name="ffn",
    )(xb, xf, w1, w3, w2, row(ln_g), row(ln_b))


def _moe_kernel(xb_ref, x_ref, rw_ref, rb_ref, w1_ref, w3_ref, w2_ref, lng_ref, lnb_ref, o_ref, ob_ref,
                acc_ref, comb_ref):
    e = pl.program_id(1)
    lane = lax.broadcasted_iota(jnp.int32, comb_ref.shape, 1).astype(F32)

    @pl.when(e == 0)
    def _():
        acc_ref[...] = jnp.zeros_like(acc_ref)
        logits = _dot(x_ref[...], rw_ref[...], HIGHEST) + rb_ref[...]
        logits = jnp.where(lane < N_EXPERTS, logits, NEG_BIG)
        m1 = jnp.max(logits, axis=-1, keepdims=True)
        i1 = jnp.min(jnp.where(logits == m1, lane, 128.0), axis=-1, keepdims=True)
        rest = jnp.where(lane == i1, NEG_BIG, logits)
        m2 = jnp.max(rest, axis=-1, keepdims=True)
        i2 = jnp.min(jnp.where(rest == m2, lane, 128.0), axis=-1, keepdims=True)
        e2 = jnp.exp(m2 - m1)
        comb_ref[...] = jnp.where(lane == i1, 1.0 / (1.0 + e2), 0.0) + jnp.where(lane == i2, e2 / (1.0 + e2), 0.0)

    weight = jnp.sum(jnp.where(lane == e.astype(F32), comb_ref[...], 0.0), axis=-1, keepdims=True)
    xb = xb_ref[...]
    hidden = _silu(_dot(xb, w1_ref[0])) * _dot(xb, w3_ref[0]) * weight
    acc_ref[...] += _dot(hidden.astype(BF16), w2_ref[0])

    @pl.when(e == pl.num_programs(1) - 1)
    def _():
        xn = _layer_norm(DN_ALPHA * x_ref[...] + acc_ref[...], lng_ref[...], lnb_ref[...])
        o_ref[...] = xn
        ob_ref[...] = xn.astype(BF16)


def _moe(xb, xf, router_w, router_b, w1, w3, w2, ln_g, ln_b, tm=512):
    n, d = xf.shape
    ne, _, fe = w1.shape
    rw = jnp.pad(router_w, ((0, 0), (0, 128 - ne)))
    rb = jnp.pad(router_b, (0, 128 - ne)).reshape(1, 128)
    row = lambda x: x.reshape(1, -1)
    return pl.pallas_call(
        _moe_kernel,
        grid=(n // tm, ne),
        in_specs=[pl.BlockSpec((tm, d), lambda i, e: (i, 0)),
                  pl.BlockSpec((tm, d), lambda i, e: (i, 0)),
                  pl.BlockSpec((d, 128), lambda i, e: (0, 0)),
                  pl.BlockSpec((1, 128), lambda i, e: (0, 0)),
                  pl.BlockSpec((1, d, fe), lambda i, e: (e, 0, 0)),
                  pl.BlockSpec((1, d, fe), lambda i, e: (e, 0, 0)),
                  pl.BlockSpec((1, fe, d), lambda i, e: (e, 0, 0)),
                  pl.BlockSpec((1, d), lambda i, e: (0, 0)),
                  pl.BlockSpec((1, d), lambda i, e: (0, 0))],
        out_specs=[pl.BlockSpec((tm, d), lambda i, e: (i, 0))] * 2,
        out_shape=[jax.ShapeDtypeStruct((n, d), F32), jax.ShapeDtypeStruct((n, d), BF16)],
        scratch_shapes=[pltpu.VMEM((tm, d), F32), pltpu.VMEM((tm, 128), F32)],
        compiler_params=_cparams(("parallel", "arbitrary")),
        name="moe",
    )(xb, xf, rw, rb, w1, w3, w2, row(ln_g), row(ln_b))


def _mixer_layer(xf, xb, bsz, seq, w_in, sinks, mu, w0, w_up, a0, a_up, g_up, k_k, k_a, r_k, gn_g, gn_b,
                 f_bias, gate_bias, w_branch, w_out, ln_g, ln_b):
    p2 = _matmul(xb, _layout_w_in(w_in).astype(BF16), tm=1024, tn=512)
    p3 = p2.reshape(bsz, seq, P_WIDTH)
    o_swa = _swa(p3, sinks)
    misc_t = jnp.swapaxes(p3[:, :, COL_MISC:COL_MISC + 8], 1, 2)
    bias8 = jnp.pad(f_bias, (0, 8 - FOX_HEADS)).reshape(8, 1)
    o_fox = _fox(p3, _fox_c(misc_t, bias8))
    v_t = jnp.swapaxes(p3[:, :, COL_DSA_KV + HEAD_DIM:COL_DSA_KV + 2 * HEAD_DIM], 1, 2)
    o_dsa = _dsa(p3, v_t)
    r, lw, k2, v, kn, b_, g, bonus = _rwkv_pre(p3, mu, w0, w_up, a0, a_up, g_up, k_k, k_a, r_k)
    y = _rwkv_chunk(r, lw, k2, v, kn, b_)
    n = bsz * seq
    flat = lambda t: t.reshape(n, t.shape[-1])
    return _merge(xf, p2, flat(o_swa), flat(y), flat(g), flat(bonus), flat(o_fox), flat(o_dsa), gn_g, gn_b,
                  gate_bias, w_branch.astype(BF16), w_out.astype(BF16), ln_g, ln_b)


def kernel(x, w_in, swa_sinks, rwkv_mu, rwkv_w0, rwkv_w_up, rwkv_a0, rwkv_a_up, rwkv_g_up, rwkv_k_k, rwkv_k_a,
           rwkv_r_k, rwkv_gn_g, rwkv_gn_b, fox_f_bias, gate_bias, w_branch, w_out, ln_g, ln_b, ffn_w1, ffn_w3,
           ffn_w2, router_w, router_b, exp_w1, exp_w3, exp_w2):
    bsz, seq, d = x.shape
    xf = x.reshape(bsz * seq, d)
    xb = xf.astype(BF16)
    for layer in range(DEPTH):
        xf, xb = _mixer_layer(xf, xb, bsz, seq, w_in[layer], swa_sinks[layer], rwkv_mu[layer], rwkv_w0[layer],
                              rwkv_w_up[layer], rwkv_a0[layer], rwkv_a_up[layer], rwkv_g_up[layer],
                              rwkv_k_k[layer], rwkv_k_a[layer], rwkv_r_k[layer], rwkv_gn_g[layer],
                              rwkv_gn_b[layer], fox_f_bias[layer], gate_bias[layer], w_branch[layer],
                              w_out[layer], ln_g[layer, 0], ln_b[layer, 0])
        j = layer // 2
        if layer % 2 == 0:
            xf, xb = _ffn(xb, xf, ffn_w1[j].astype(BF16), ffn_w3[j].astype(BF16), ffn_w2[j].astype(BF16),
                          ln_g[layer, 1], ln_b[layer, 1])
        else:
            xf, xb = _moe(xb, xf, router_w[j], router_b[j], exp_w1[j].astype(BF16), exp_w3[j].astype(BF16),
                          exp_w2[j].astype(BF16), ln_g[layer, 1], ln_b[layer, 1])
    return xf.reshape(bsz, seq, d)
```

```python
import functools

import jax
import jax.numpy as jnp
from jax import lax
from jax.experimental import pallas as pl
from jax.experimental.pallas import tpu as pltpu

F32 = jnp.float32
BF16 = jnp.bfloat16
HIGHEST = lax.Precision.HIGHEST

D_MODEL = 1024
DEPTH = 2
CHUNK = 64
HEAD_DIM = 64
CHUNK_SHIFT = 6
HEAD_SHIFT = 6
SWA_HEADS = 8
SWA_KV_HEADS = 2
SWA_GROUP = SWA_HEADS // SWA_KV_HEADS
SWA_WINDOW = 128
SWA_WIN_CHUNKS = SWA_WINDOW // CHUNK
RWKV_HEADS = 4
RWKV_W = RWKV_HEADS * HEAD_DIM
RWKV_GN_EPS = 64e-5
FOX_HEADS = 4
DSA_HEADS = 4
IDX_HEADS = 4
IDX_DIM = 64
DSA_TOPK_MAX = 256
D_FF = 2816
N_EXPERTS = 8
D_FF_EXPERT = 1408
N_BRANCHES = 4
DN_ALPHA = (2 * DEPTH) ** 0.25
LN_EPS = 1e-5
ATTN_SCALE = HEAD_DIM ** -0.5
IDX_SCALE = IDX_DIM ** -0.5
IDX_W_SCALE = IDX_HEADS ** -0.5
MIX_WIDTHS = (SWA_HEADS * HEAD_DIM, RWKV_W, FOX_HEADS * HEAD_DIM, DSA_HEADS * HEAD_DIM)

COL_GATE = 0
COL_RWKV = 4096
COL_SWA_Q = 5120
COL_FOX_KV = 5632
COL_DSA_QQ = 6144
COL_FOX_Q = 6656
COL_SWA_KV = 6912
COL_DSA_KV = 7168
COL_MISC = 7424
P_WIDTH = 7680

NEG_BIG = -1e30
INT_MIN = -(2 ** 31)
VMEM_LIMIT = 48 * 1024 * 1024


def _layout_w_in(w):
    d = w.shape[0]
    z = lambda n: jnp.zeros((d, n), w.dtype)
    swa, rwkv, fox, dsa, gate = 0, 768, 1792, 2564, 3272
    parts = [
        w[:, gate:gate + 4096],
        w[:, rwkv:rwkv + 1024],
        w[:, swa:swa + 512],
        w[:, fox + 256:fox + 768],
        w[:, dsa:dsa + 256], w[:, dsa + 384:dsa + 640],
        w[:, fox:fox + 256],
        w[:, swa + 512:swa + 768],
        w[:, dsa + 256:dsa + 384], w[:, dsa + 640:dsa + 704], z(64),
        w[:, fox + 768:fox + 772], w[:, dsa + 704:dsa + 708], z(120),
        z(128),
    ]
    return jnp.concatenate(parts, axis=1)


def _cparams(sem):
    return pltpu.CompilerParams(dimension_semantics=sem, vmem_limit_bytes=VMEM_LIMIT)


def _nt(a, b, precision=None):
    return lax.dot_general(a, b, (((1,), (1,)), ((), ())), precision=precision,
                           preferred_element_type=F32)


def _tn(a, b, precision=None):
    return lax.dot_general(a, b, (((0,), (0,)), ((), ())), precision=precision,
                           preferred_element_type=F32)


def _dot(a, b, precision=None):
    return jnp.dot(a, b, precision=precision, preferred_element_type=F32)


def _sigmoid(x):
    return 1.0 / (1.0 + jnp.exp(-x))


def _layer_norm(z, g, b):
    mu = jnp.mean(z, axis=-1, keepdims=True)
    zc = z - mu
    var = jnp.mean(zc * zc, axis=-1, keepdims=True)
    return zc * lax.rsqrt(var + LN_EPS) * g + b


def _head_block_ones(n, scale):
    r = lax.broadcasted_iota(jnp.int32, (n, n), 0) >> HEAD_SHIFT
    c = lax.broadcasted_iota(jnp.int32, (n, n), 1) >> HEAD_SHIFT
    return jnp.where(r == c, scale, 0.0).astype(F32)


def _mm_kernel(a_ref, b_ref, o_ref):
    o_ref[...] = _dot(a_ref[...], b_ref[...])


def _matmul(a, b, tm, tn):
    m, k = a.shape
    n = b.shape[1]
    tm = min(tm, m)
    return pl.pallas_call(
        _mm_kernel,
        grid=(m // tm, n // tn),
        in_specs=[pl.BlockSpec((tm, k), lambda i, j: (i, 0)),
                  pl.BlockSpec((k, tn), lambda i, j: (0, j))],
        out_specs=pl.BlockSpec((tm, tn), lambda i, j: (i, j)),
        out_shape=jax.ShapeDtypeStruct((m, n), F32),
        compiler_params=_cparams(("parallel", "parallel")),
        name="in_proj",
    )(a, b)


def _swa_kernel(sink_ref, q_ref, kv_ref, o_ref, *, tq):
    i = pl.program_id(1)
    win = tq + SWA_WINDOW
    s0 = pl.multiple_of(jnp.maximum(i * tq - SWA_WINDOW, 0), SWA_WINDOW)
    q_chunk = (i * tq + lax.broadcasted_iota(jnp.int32, (tq, win), 0)) >> CHUNK_SHIFT
    k_chunk = (s0 + lax.broadcasted_iota(jnp.int32, (tq, win), 1)) >> CHUNK_SHIFT
    bias = jnp.where(k_chunk <= q_chunk,
                     jnp.where(k_chunk >= q_chunk - SWA_WIN_CHUNKS, 0.0, NEG_BIG), NEG_BIG)
    kv_w = SWA_KV_HEADS * HEAD_DIM
    for hk in range(SWA_KV_HEADS):
        k = kv_ref[0, pl.ds(s0, win), hk * HEAD_DIM:(hk + 1) * HEAD_DIM].astype(BF16)
        v = kv_ref[0, pl.ds(s0, win), kv_w + hk * HEAD_DIM:kv_w + (hk + 1) * HEAD_DIM].astype(BF16)
        for g in range(SWA_GROUP):
            h = hk * SWA_GROUP + g
            q = (q_ref[0, :, h * HEAD_DIM:(h + 1) * HEAD_DIM] * ATTN_SCALE).astype(BF16)
            s = _nt(q, k) + bias
            sink = sink_ref[h]
            m = jnp.maximum(jnp.max(s, axis=-1, keepdims=True), sink)
            e = jnp.exp(s - m)
            denom = jnp.sum(e, axis=-1, keepdims=True) + jnp.exp(sink - m)
            o = _dot(e.astype(BF16), v) / denom
            o_ref[0, :, h * HEAD_DIM:(h + 1) * HEAD_DIM] = o


def _swa(p3, sinks, tq=256):
    b, s, _ = p3.shape
    return pl.pallas_call(
        functools.partial(_swa_kernel, tq=tq),
        grid=(b, s // tq),
        in_specs=[pl.BlockSpec(memory_space=pltpu.SMEM),
                  pl.BlockSpec((1, tq, 512), lambda bi, i: (bi, i, COL_SWA_Q // 512)),
                  pl.BlockSpec((1, s, 256), lambda bi, i: (bi, 0, COL_SWA_KV // 256))],
        out_specs=pl.BlockSpec((1, tq, 512), lambda bi, i: (bi, i, 0)),
        out_shape=jax.ShapeDtypeStruct((b, s, 512), F32),
        compiler_params=_cparams(("parallel", "parallel")),
        name="swa",
    )(sinks, p3, p3)


def _fox_c_kernel(f_ref, b_ref, c_ref, *, blk):
    s = f_ref.shape[2]
    r = lax.broadcasted_iota(jnp.int32, (blk, blk), 0)
    c = lax.broadcasted_iota(jnp.int32, (blk, blk), 1)
    tri = jnp.where(r <= c, 1.0, 0.0).astype(F32)
    carry = jnp.zeros((8, 1), F32)
    for j in range(s // blk):
        x = f_ref[0, :, j * blk:(j + 1) * blk] + b_ref[...]
        log_f = jnp.minimum(x, 0.0) - jnp.log1p(jnp.exp(-jnp.abs(x)))
        cs = _dot(log_f, tri, HIGHEST) + carry
        c_ref[0, :, j * blk:(j + 1) * blk] = cs
        carry = cs[:, blk - 1:blk]


def _fox_c(f_t, bias8):
    b, _, s = f_t.shape
    return pl.pallas_call(
        functools.partial(_fox_c_kernel, blk=256),
        grid=(b,),
        in_specs=[pl.BlockSpec((1, 8, s), lambda bi: (bi, 0, 0)),
                  pl.BlockSpec((8, 1), lambda bi: (0, 0))],
        out_specs=pl.BlockSpec((1, 8, s), lambda bi: (bi, 0, 0)),
        out_shape=jax.ShapeDtypeStruct((b, 8, s), F32),
        compiler_params=_cparams(("parallel",)),
        name="fox_cumsum",
    )(f_t, bias8)


def _fox_kernel(q_ref, kv_ref, c_ref, o_ref, *, tq):
    i = pl.program_id(1)
    tk = tq
    row = lax.broadcasted_iota(jnp.int32, (tq, tk), 0)
    col = lax.broadcasted_iota(jnp.int32, (tq, tk), 1)
    diag_bias = jnp.where(col <= row, 0.0, NEG_BIG)
    w = FOX_HEADS * HEAD_DIM
    for h in range(FOX_HEADS):
        q = (q_ref[0, :, h * HEAD_DIM:(h + 1) * HEAD_DIM] * ATTN_SCALE).astype(BF16)

        def block(j, carry, masked, h=h, q=q):
            m, l, acc = carry
            ks = pl.multiple_of(j * tk, tk)
            k = kv_ref[0, pl.ds(ks, tk), h * HEAD_DIM:(h + 1) * HEAD_DIM].astype(BF16)
            v = kv_ref[0, pl.ds(ks, tk), w + h * HEAD_DIM:w + (h + 1) * HEAD_DIM].astype(BF16)
            s = _nt(q, k) - c_ref[0, h:h + 1, pl.ds(ks, tk)]
            if masked:
                s = s + diag_bias
            m_new = jnp.maximum(m, jnp.max(s, axis=-1, keepdims=True))
            alpha = jnp.exp(m - m_new)
            p = jnp.exp(s - m_new)
            l = alpha * l + jnp.sum(p, axis=-1, keepdims=True)
            acc = alpha * acc + _dot(p.astype(BF16), v)
            return m_new, l, acc

        init = (jnp.full((tq, 1), NEG_BIG, F32), jnp.zeros((tq, 1), F32), jnp.zeros((tq, HEAD_DIM), F32))
        carry = lax.fori_loop(0, i, lambda j, cr: block(j, cr, False), init)
        _, l, acc = block(i, carry, True)
        o_ref[0, :, h * HEAD_DIM:(h + 1) * HEAD_DIM] = acc / l


def _fox(p3, c, tq=256):
    b, s, _ = p3.shape
    return pl.pallas_call(
        functools.partial(_fox_kernel, tq=tq),
        grid=(b, s // tq),
        in_specs=[pl.BlockSpec((1, tq, 256), lambda bi, i: (bi, i, COL_FOX_Q // 256)),
                  pl.BlockSpec((1, s, 512), lambda bi, i: (bi, 0, COL_FOX_KV // 512)),
                  pl.BlockSpec((1, 8, s), lambda bi, i: (bi, 0, 0))],
        out_specs=pl.BlockSpec((1, tq, 256), lambda bi, i: (bi, i, 0)),
        out_shape=jax.ShapeDtypeStruct((b, s, 256), F32),
        compiler_params=_cparams(("parallel", "parallel")),
        name="fox",
    )(p3, p3, c)


def _key_to_score(key):
    return pltpu.bitcast(key ^ ((key >> 31) & 0x7FFFFFFF), F32)


def _dsa_kernel(qq_ref, misc_ref, kv_ref, vt_ref, o_ref, sc_ref, *, tq, kb, topk):
    i = pl.program_id(1)
    n_kb = lax.div((i + 1) * tq + kb - 1, kb)
    lane8 = lax.broadcasted_iota(jnp.int32, (8, 128), 1)
    row8 = lax.broadcasted_iota(jnp.int32, (8, 128), 0)
    pick = jnp.where((lane8 == row8 + 4) & (row8 < IDX_HEADS), 1.0, 0.0).astype(F32)
    w_t = _nt(pick, misc_ref[0], HIGHEST) * IDX_W_SCALE
    q_chunk = (i * tq + lax.broadcasted_iota(jnp.int32, (1, tq), 1)) >> CHUNK_SHIFT
    qw = DSA_HEADS * HEAD_DIM
    q_idx = [qq_ref[0, :, qw + h * IDX_DIM:qw + (h + 1) * IDX_DIM].astype(BF16) for h in range(IDX_HEADS)]

    def score_block(j, carry):
        ks = pl.multiple_of(j * kb, kb)
        k_idx = kv_ref[0, pl.ds(ks, kb), 2 * HEAD_DIM:2 * HEAD_DIM + IDX_DIM].astype(BF16)
        score = jnp.zeros((kb, tq), F32)
        for h in range(IDX_HEADS):
            rel = jnp.maximum(_nt(k_idx, q_idx[h]) * IDX_SCALE, 0.0)
            score = score + rel * w_t[h:h + 1, :]
        k_chunk = (ks + lax.broadcasted_iota(jnp.int32, (kb, tq), 0)) >> CHUNK_SHIFT
        sc_ref[pl.ds(ks, kb), :] = jnp.where(k_chunk <= q_chunk, score, -jnp.inf)
        return carry

    lax.fori_loop(0, n_kb, score_block, 0)

    def count(pred):
        def body(j, acc):
            blk = sc_ref[pl.ds(pl.multiple_of(j * kb, kb), kb), :]
            hit = jnp.where(pred(blk), 1, 0).astype(jnp.int32)
            return acc + jnp.sum(hit.reshape(kb // 8, 8, tq), axis=0)
        acc = lax.fori_loop(0, n_kb, body, jnp.zeros((8, tq), jnp.int32))
        return jnp.sum(acc.astype(F32), axis=0, keepdims=True)

    zero = jnp.zeros((1, tq), F32)
    thr_key = jnp.where(count(lambda blk: blk >= zero) >= topk, 0, INT_MIN).astype(jnp.int32)

    def bit_step(bi, thr_key):
        cand = thr_key | lax.shift_left(jnp.int32(1), 30 - bi)
        cand_score = _key_to_score(cand)
        return jnp.where(count(lambda blk: blk >= cand_score) >= topk, cand, thr_key)

    thr_key = lax.fori_loop(0, 31, bit_step, thr_key)
    thr = jnp.where(thr_key == INT_MIN, -jnp.inf, _key_to_score(thr_key))
    need = topk - count(lambda blk: blk > thr)

    r = lax.broadcasted_iota(jnp.int32, (kb, kb), 0)
    c = lax.broadcasted_iota(jnp.int32, (kb, kb), 1)
    earlier = jnp.where(c < r, 1.0, 0.0).astype(BF16)
    q_att = [(qq_ref[0, :, h * HEAD_DIM:(h + 1) * HEAD_DIM] * ATTN_SCALE).astype(BF16)
             for h in range(DSA_HEADS)]

    def attend_block(j, carry):
        ties_before, stats = carry
        ks = pl.multiple_of(j * kb, kb)
        score = sc_ref[pl.ds(ks, kb), :]
        k_chunk = (ks + lax.broadcasted_iota(jnp.int32, (kb, tq), 0)) >> CHUNK_SHIFT
        tie = jnp.where(score == thr, jnp.where(k_chunk <= q_chunk, 1.0, 0.0), 0.0)
        rank = _dot(earlier, tie.astype(BF16)) + ties_before
        tie_taken = jnp.where(rank < need, tie, 0.0)
        bias = jnp.where(score > thr, 0.0, jnp.where(tie_taken > 0.5, 0.0, NEG_BIG))
        ties_before = ties_before + jnp.sum(tie, axis=0, keepdims=True)
        k = kv_ref[0, pl.ds(ks, kb), 0:HEAD_DIM].astype(BF16)
        v_t = vt_ref[0, :, pl.ds(ks, kb)].astype(BF16)
        new_stats = []
        for h in range(DSA_HEADS):
            m, l, acc = stats[h]
            s = _nt(k, q_att[h]) + bias
            m_new = jnp.maximum(m, jnp.max(s, axis=0, keepdims=True))
            alpha = jnp.exp(m - m_new)
            p = jnp.exp(s - m_new)
            l = alpha * l + jnp.sum(p, axis=0, keepdims=True)
            acc = alpha * acc + _dot(v_t, p.astype(BF16))
            new_stats.append((m_new, l, acc))
        return ties_before, tuple(new_stats)

    init = (jnp.zeros((1, tq), F32),
            tuple((jnp.full((1, tq), NEG_BIG, F32), jnp.zeros((1, tq), F32), jnp.zeros((HEAD_DIM, tq), F32))
                  for _ in range(DSA_HEADS)))
    _, stats = lax.fori_loop(0, n_kb, attend_block, init)
    out_t = jnp.concatenate([acc / l for (_, l, acc) in stats], axis=0)
    o_ref[0] = out_t.T


def _dsa(p3, v_t, tq=128, kb=256):
    b, s, _ = p3.shape
    topk = min(DSA_TOPK_MAX, s // 4)
    return pl.pallas_call(
        functools.partial(_dsa_kernel, tq=tq, kb=kb, topk=topk),
        grid=(b, s // tq),
        in_specs=[pl.BlockSpec((1, tq, 512), lambda bi, i: (bi, i, COL_DSA_QQ // 512)),
                  pl.BlockSpec((1, tq, 128), lambda bi, i: (bi, i, COL_MISC // 128)),
                  pl.BlockSpec((1, s, 256), lambda bi, i: (bi, 0, COL_DSA_KV // 256)),
                  pl.BlockSpec((1, HEAD_DIM, s), lambda bi, i: (bi, 0, 0))],
        out_specs=pl.BlockSpec((1, tq, 256), lambda bi, i: (bi, i, 0)),
        out_shape=jax.ShapeDtypeStruct((b, s, 256), F32),
        scratch_shapes=[pltpu.VMEM((s, tq), F32)],
        compiler_params=_cparams(("parallel", "parallel")),
        name="dsa",
    )(p3, p3, p3, v_t)


def _rwkv_pre_kernel(p_ref, mu_ref, w0_ref, wup_ref, a0_ref, aup_ref, gup_ref, kk_ref, ka_ref, rk_ref,
                     r_o, lw_o, k_o, v_o, kn_o, b_o, g_o, bonus_o, last_ref, *, tt):
    t = pl.program_id(1)

    @pl.when(t == 0)
    def _():
        last_ref[...] = jnp.zeros_like(last_ref)

    p = p_ref[0]
    row = lax.broadcasted_iota(jnp.int32, p.shape, 0)
    p_prev = jnp.where(row == 0, last_ref[...], pltpu.roll(p, 1, axis=0))
    last_ref[...] = p[tt - 1:tt, :]
    ps = p + mu_ref[...] * (p_prev - p)
    w = RWKV_W
    r, k, v = ps[:, 0:w], ps[:, w:2 * w], ps[:, 2 * w:3 * w]
    w_lo, a_lo, g_lo = ps[:, 3 * w:3 * w + 64], ps[:, 3 * w + 64:3 * w + 128], ps[:, 3 * w + 128:3 * w + 256]
    ww = w0_ref[...] + _dot(jnp.tanh(w_lo), wup_ref[...], HIGHEST)
    softplus_neg = jnp.maximum(-ww, 0.0) + jnp.log1p(jnp.exp(-jnp.abs(ww)))
    log_w = -jnp.exp(-softplus_neg - 0.5)
    a = _sigmoid(a0_ref[...] + _dot(a_lo, aup_ref[...], HIGHEST))
    g = _dot(_sigmoid(g_lo), gup_ref[...], HIGHEST)
    head_sum = _head_block_ones(w, 1.0)
    kn = k * kk_ref[...]
    kn = kn * lax.rsqrt(_dot(kn * kn, head_sum, HIGHEST) + 1e-12)
    k2 = k * (1.0 + (a - 1.0) * ka_ref[...])
    bonus = _dot(r * k2 * rk_ref[...], head_sum, HIGHEST) * v
    r_o[0] = r
    lw_o[0] = log_w
    k_o[0] = k2
    v_o[0] = v
    kn_o[0] = kn
    b_o[0] = kn * a
    g_o[0] = g
    bonus_o[0] = bonus


def _rwkv_pre(p3, mu, w0, w_up, a0, a_up, g_up, k_k, k_a, r_k, tt=256):
    b, s, _ = p3.shape
    w = RWKV_W
    row = lambda x: x.reshape(1, -1)
    full = lambda shape: pl.BlockSpec(shape, lambda bi, t: (0,) * len(shape))
    out = jax.ShapeDtypeStruct((b, s, w), F32)
    return pl.pallas_call(
        functools.partial(_rwkv_pre_kernel, tt=tt),
        grid=(b, s // tt),
        in_specs=[pl.BlockSpec((1, tt, 1024), lambda bi, t: (bi, t, COL_RWKV // 1024)),
                  full((1, 1024)), full((1, w)), full((64, w)), full((1, w)), full((64, w)),
                  full((128, w)), full((1, w)), full((1, w)), full((1, w))],
        out_specs=[pl.BlockSpec((1, tt, w), lambda bi, t: (bi, t, 0))] * 8,
        out_shape=[out] * 8,
        scratch_shapes=[pltpu.VMEM((1, 1024), F32)],
        compiler_params=_cparams(("parallel", "arbitrary")),
        name="rwkv_pre",
    )(p3, row(mu), row(w0), w_up, row(a0), a_up, g_up, row(k_k), row(k_a), row(r_k))


def _bdot(a, b):
    return _dot(a.astype(BF16), b.astype(BF16))


def _bnt(a, b):
    return _nt(a.astype(BF16), b.astype(BF16))


def _btn(a, b):
    return _tn(a.astype(BF16), b.astype(BF16))


def _rwkv_chunk_kernel(r_ref, lw_ref, k_ref, v_ref, kn_ref, b_ref, y_ref, s_ref, *, cs, nch):
    c = pl.program_id(1)

    @pl.when(c == 0)
    def _():
        s_ref[...] = jnp.zeros_like(s_ref)

    rows = cs * nch
    big_row = lax.broadcasted_iota(jnp.int32, (rows, rows), 0)
    big_col = lax.broadcasted_iota(jnp.int32, (rows, rows), 1)
    same_chunk = (big_row >> CHUNK_SHIFT) == (big_col >> CHUNK_SHIFT)
    tri = jnp.where(same_chunk, jnp.where(big_col <= big_row, 1.0, 0.0), 0.0).astype(F32)
    row = lax.broadcasted_iota(jnp.int32, (cs, cs), 0)
    col = lax.broadcasted_iota(jnp.int32, (cs, cs), 1)
    incl = col <= row
    strict = col < row
    eye = jnp.where(col == row, 1.0, 0.0).astype(F32)
    lw = lw_ref[0]
    cum = _dot(tri, lw, HIGHEST)
    e_incl = jnp.exp(cum)
    e_neg = jnp.exp(-cum)
    abar_all = -kn_ref[0] * jnp.exp(cum - lw)
    rbar_all = r_ref[0] * e_incl
    bt_all = b_ref[0] * e_neg
    kt_all = k_ref[0] * e_neg
    v_all = v_ref[0]
    n_double = max(cs.bit_length() - 2, 0)
    probs = [(j, h) for j in range(nch) for h in range(RWKV_HEADS)]
    cut = lambda t, jh: t[jh[0] * cs:(jh[0] + 1) * cs, jh[1] * HEAD_DIM:(jh[1] + 1) * HEAD_DIM]
    each = lambda fn: {jh: fn(jh) for jh in probs}
    abar, rbar = each(lambda jh: cut(abar_all, jh)), each(lambda jh: cut(rbar_all, jh))
    bt, kt, v = each(lambda jh: cut(bt_all, jh)), each(lambda jh: cut(kt_all, jh)), each(lambda jh: cut(v_all, jh))
    p_last = each(lambda jh: e_incl[(jh[0] + 1) * cs - 1:(jh[0] + 1) * cs, jh[1] * HEAD_DIM:(jh[1] + 1) * HEAD_DIM])
    ar = each(lambda jh: jnp.concatenate([abar[jh], rbar[jh]], axis=0))
    ar_b = each(lambda jh: _bnt(ar[jh], bt[jh]))
    ar_k = each(lambda jh: _bnt(ar[jh], kt[jh]))
    a_ab = each(lambda jh: jnp.where(strict, ar_b[jh][:cs], 0.0))
    a_ak = each(lambda jh: jnp.where(strict, ar_k[jh][:cs], 0.0))
    a_rb = each(lambda jh: jnp.where(incl, ar_b[jh][cs:], 0.0))
    a_rk = each(lambda jh: jnp.where(incl, ar_k[jh][cs:], 0.0))
    ak_v = each(lambda jh: _bdot(a_ak[jh], v[jh]))
    inv = each(lambda jh: eye + a_ab[jh])
    power = a_ab
    for _ in range(n_double):
        power = each(lambda jh: _bdot(power[jh], power[jh]))
        inv = each(lambda jh: inv[jh] + _bdot(inv[jh], power[jh]))
    a_hat = each(lambda jh: _bdot(inv[jh], abar[jh]))
    u_hat = each(lambda jh: _bdot(inv[jh], ak_v[jh]))
    r_hat = each(lambda jh: rbar[jh] + _bdot(a_rb[jh], a_hat[jh]))
    y_hat = each(lambda jh: _bdot(a_rb[jh], u_hat[jh]) + _bdot(a_rk[jh], v[jh]))
    g_mat = each(lambda jh: eye * p_last[jh] + _btn(a_hat[jh], bt[jh] * p_last[jh]))
    h_mat = each(lambda jh: _btn(jnp.concatenate([u_hat[jh], v[jh]], axis=0),
                                 jnp.concatenate([bt[jh], kt[jh]], axis=0) * p_last[jh]))
    states = [s_ref[h] for h in range(RWKV_HEADS)]
    for j in range(nch):
        for h in range(RWKV_HEADS):
            y_ref[0, j * cs:(j + 1) * cs, h * HEAD_DIM:(h + 1) * HEAD_DIM] = (
                _bnt(r_hat[j, h], states[h]) + y_hat[j, h])
        states = [_dot(states[h], g_mat[j, h], HIGHEST) + h_mat[j, h] for h in range(RWKV_HEADS)]
    for h in range(RWKV_HEADS):
        s_ref[h] = states[h]


def _rwkv_chunk(r, lw, k, v, kn, b_, cs=CHUNK, nch=4):
    b, s, w = r.shape
    spec = pl.BlockSpec((1, cs * nch, w), lambda bi, c: (bi, c, 0))
    return pl.pallas_call(
        functools.partial(_rwkv_chunk_kernel, cs=cs, nch=nch),
        grid=(b, s // (cs * nch)),
        in_specs=[spec] * 6,
        out_specs=spec,
        out_shape=jax.ShapeDtypeStruct((b, s, w), F32),
        scratch_shapes=[pltpu.VMEM((RWKV_HEADS, HEAD_DIM, HEAD_DIM), F32)],
        compiler_params=_cparams(("parallel", "arbitrary")),
        name="rwkv_chunk",
    )(r, lw, k, v, kn, b_)


def _merge_kernel(x_ref, gate_ref, swa_ref, y_ref, g_ref, bonus_ref, fox_ref, dsa_ref, gng_ref, gnb_ref,
                  gbias_ref, wb_ref, wo_ref, lng_ref, lnb_ref, o_ref, ob_ref):
    y = y_ref[...]
    head_mean = _head_block_ones(RWKV_W, 1.0 / HEAD_DIM)
    yc = y - _dot(y, head_mean, HIGHEST)
    yv = _dot(yc * yc, head_mean, HIGHEST)
    o_rwkv = (yc * lax.rsqrt(yv + RWKV_GN_EPS) * gng_ref[...] + gnb_ref[...] + bonus_ref[...]) * g_ref[...]
    branches = (swa_ref[...], o_rwkv, fox_ref[...], dsa_ref[...])
    merged = jnp.zeros(o_ref.shape, F32)
    off = 0
    for i, o in enumerate(branches):
        width = MIX_WIDTHS[i]
        proj = _dot(o.astype(BF16), wb_ref[off:off + width, :])
        gate = _sigmoid(gate_ref[:, i * D_MODEL:(i + 1) * D_MODEL] + gbias_ref[i:i + 1, :])
        merged = merged + gate * proj
        off += width
    y_out = _dot(merged.astype(BF16), wo_ref[...])
    xn = _layer_norm(DN_ALPHA * x_ref[...] + y_out, lng_ref[...], lnb_ref[...])
    o_ref[...] = xn
    ob_ref[...] = xn.astype(BF16)


def _merge(xf, p2, o_swa, y, g, bonus, o_fox, o_dsa, gn_g, gn_b, gate_bias, w_branch, w_out, ln_g, ln_b, tm=256):
    n, d = xf.shape
    row = lambda x: x.reshape(1, -1)
    tok = lambda wdt: pl.BlockSpec((tm, wdt), lambda i: (i, 0))
    full = lambda shape: pl.BlockSpec(shape, lambda i: (0,) * len(shape))
    return pl.pallas_call(
        _merge_kernel,
        grid=(n // tm,),
        in_specs=[tok(d), pl.BlockSpec((tm, N_BRANCHES * d), lambda i: (i, COL_GATE)),
                  tok(512), tok(256), tok(256), tok(256), tok(256), tok(256),
                  full((1, 256)), full((1, 256)), full((N_BRANCHES, d)),
                  full((sum(MIX_WIDTHS), d)), full((d, d)), full((1, d)), full((1, d))],
        out_specs=[tok(d), tok(d)],
        out_shape=[jax.ShapeDtypeStruct((n, d), F32), jax.ShapeDtypeStruct((n, d), BF16)],
        compiler_params=_cparams(("parallel",)),
        name="merge",
    )(xf, p2, o_swa, y, g, bonus, o_fox, o_dsa, row(gn_g), row(gn_b), gate_bias, w_branch, w_out,
      row(ln_g), row(ln_b))


def _silu(x):
    return x * _sigmoid(x)


def _ffn_kernel(xb_ref, x_ref, w1_ref, w3_ref, w2_ref, lng_ref, lnb_ref, o_ref, ob_ref, acc_ref):
    f = pl.program_id(1)

    @pl.when(f == 0)
    def _():
        acc_ref[...] = jnp.zeros_like(acc_ref)

    xb = xb_ref[...]
    hidden = _silu(_dot(xb, w1_ref[...])) * _dot(xb, w3_ref[...])
    acc_ref[...] += _dot(hidden.astype(BF16), w2_ref[...])

    @pl.when(f == pl.num_programs(1) - 1)
    def _():
        xn = _layer_norm(DN_ALPHA * x_ref[...] + acc_ref[...], lng_ref[...], lnb_ref[...])
        o_ref[...] = xn
        ob_ref[...] = xn.astype(BF16)


def _ffn(xb, xf, w1, w3, w2, ln_g, ln_b, tm=512, tf=1408):
    n, d = xf.shape
    ff = w1.shape[1]
    row = lambda x: x.reshape(1, -1)
    return pl.pallas_call(
        _ffn_kernel,
        grid=(n // tm, ff // tf),
        in_specs=[pl.BlockSpec((tm, d), lambda i, f: (i, 0)),
                  pl.BlockSpec((tm, d), lambda i, f: (i, 0)),
                  pl.BlockSpec((d, tf), lambda i, f: (0, f)),
                  pl.BlockSpec((d, tf), lambda i, f: (0, f)),
                  pl.BlockSpec((tf, d), lambda i, f: (f, 0)),
                  pl.BlockSpec((1, d), lambda i, f: (0, 0)),
                  pl.BlockSpec((1, d), lambda i, f: (0, 0))],
        out_specs=[pl.BlockSpec((tm, d), lambda i, f: (i, 0))] * 2,
        out_shape=[jax.ShapeDtypeStruct((n, d), F32), jax.ShapeDtypeStruct((n, d), BF16)],
        scratch_shapes=[pltpu.VMEM((tm, d), F32)],
        compiler_params=_cparams(("parallel", "arbitrary")),
        name="ffn",
    )(xb, xf, w1, w3, w2, row(ln_g), row(ln_b))


def _moe_kernel(xb_ref, x_ref, rw_ref, rb_ref, w1_ref, w3_ref, w2_ref, lng_ref, lnb_ref, o_ref, ob_ref,
                acc_ref, comb_ref):
    e = pl.program_id(1)
    lane = lax.broadcasted_iota(jnp.int32, comb_ref.shape, 1).astype(F32)

    @pl.when(e == 0)
    def _():
        acc_ref[...] = jnp.zeros_like(acc_ref)
        logits = _dot(x_ref[...], rw_ref[...], HIGHEST) + rb_ref[...]
        logits = jnp.where(lane < N_EXPERTS, logits, NEG_BIG)
        m1 = jnp.max(logits, axis=-1, keepdims=True)
        i1 = jnp.min(jnp.where(logits == m1, lane, 128.0), axis=-1, keepdims=True)
        rest = jnp.where(lane == i1, NEG_BIG, logits)
        m2 = jnp.max(rest, axis=-1, keepdims=True)
        i2 = jnp.min(jnp.where(rest == m2, lane, 128.0), axis=-1, keepdims=True)
        e2 = jnp.exp(m2 - m1)
        comb_ref[...] = jnp.where(lane == i1, 1.0 / (1.0 + e2), 0.0) + jnp.where(lane == i2, e2 / (1.0 + e2), 0.0)

    weight = jnp.sum(jnp.where(lane == e.astype(F32), comb_ref[...], 0.0), axis=-1, keepdims=True)
    xb = xb_ref[...]
    hidden = _silu(_dot(xb, w1_ref[0])) * _dot(xb, w3_ref[0]) * weight
    acc_ref[...] += _dot(hidden.astype(BF16), w2_ref[0])

    @pl.when(e == pl.num_programs(1) - 1)
    def _():
        xn = _layer_norm(DN_ALPHA * x_ref[...] + acc_ref[...], lng_ref[...], lnb_ref[...])
        o_ref[...] = xn
        ob_ref[...] = xn.astype(BF16)


def _moe(xb, xf, router_w, router_b, w1, w3, w2, ln_g, ln_b, tm=512):
    n, d = xf.shape
    ne, _, fe = w1.shape
    rw = jnp.pad(router_w, ((0, 0), (0, 128 - ne)))
    rb = jnp.pad(router_b, (0, 128 - ne)).reshape(1, 128)
    row = lambda x: x.reshape(1, -1)
    return pl.pallas_call(
        _moe_kernel,
        grid=(n // tm, ne),
        in_specs=[pl.BlockSpec((tm, d), lambda i, e: (i, 0)),
                  pl.BlockSpec((tm, d), lambda i, e: (i, 0)),
                  pl.BlockSpec((d, 128), lambda i, e: (0, 0)),
                  pl.BlockSpec((1, 128), lambda i, e: (0, 0)),
                  pl.BlockSpec((1, d, fe), lambda i, e: (e, 0, 0)),
                  pl.BlockSpec((1, d, fe), lambda i, e: (e, 0, 0)),
                  pl.BlockSpec((1, fe, d), lambda i, e: (e, 0, 0)),
                  pl.BlockSpec((1, d), lambda i, e: (0, 0)),
                  pl.BlockSpec((1, d), lambda i, e: (0, 0))],
        out_specs=[pl.BlockSpec((tm, d), lambda i, e: (i, 0))] * 2,
        out_shape=[jax.ShapeDtypeStruct((n, d), F32), jax.ShapeDtypeStruct((n, d), BF16)],
        scratch_shapes=[pltpu.VMEM((tm, d), F32), pltpu.VMEM((tm, 128), F32)],
        compiler_params=_cparams(("parallel", "arbitrary")),
        name="moe",
    )(xb, xf, rw, rb, w1, w3, w2, row(ln_g), row(ln_b))


def _mixer_layer(xf, xb, bsz, seq, w_in, sinks, mu, w0, w_up, a0, a_up, g_up, k_k, k_a, r_k, gn_g, gn_b,
                 f_bias, gate_bias, w_branch, w_out, ln_g, ln_b):
    p2 = _matmul(xb, _layout_w_in(w_in).astype(BF16), tm=1024, tn=512)
    p3 = p2.reshape(bsz, seq, P_WIDTH)
    o_swa = _swa(p3, sinks)
    misc_t = jnp.swapaxes(p3[:, :, COL_MISC:COL_MISC + 8], 1, 2)
    bias8 = jnp.pad(f_bias, (0, 8 - FOX_HEADS)).reshape(8, 1)
    o_fox = _fox(p3, _fox_c(misc_t, bias8))
    v_t = jnp.swapaxes(p3[:, :, COL_DSA_KV + HEAD_DIM:COL_DSA_KV + 2 * HEAD_DIM], 1, 2)
    o_dsa = _dsa(p3, v_t)
    r, lw, k2, v, kn, b_, g, bonus = _rwkv_pre(p3, mu, w0, w_up, a0, a_up, g_up, k_k, k_a, r_k)
    y = _rwkv_chunk(r, lw, k2, v, kn, b_)
    n = bsz * seq
    flat = lambda t: t.reshape(n, t.shape[-1])
    return _merge(xf, p2, flat(o_swa), flat(y), flat(g), flat(bonus), flat(o_fox), flat(o_dsa), gn_g, gn_b,
                  gate_bias, w_branch.astype(BF16), w_out.astype(BF16), ln_g, ln_b)


def kernel(x, w_in, swa_sinks, rwkv_mu, rwkv_w0, rwkv_w_up, rwkv_a0, rwkv_a_up, rwkv_g_up, rwkv_k_k, rwkv_k_a,
           rwkv_r_k, rwkv_gn_g, rwkv_gn_b, fox_f_bias, gate_bias, w_branch, w_out, ln_g, ln_b, ffn_w1, ffn_w3,
           ffn_w2, router_w, router_b, exp_w1, exp_w3, exp_w2):
    bsz, seq, d = x.shape
    xf = x.reshape(bsz * seq, d)
    xb = xf.astype(BF16)
    for layer in range(DEPTH):
        xf, xb = _mixer_layer(xf, xb, bsz, seq, w_in[layer], swa_sinks[layer], rwkv_mu[layer], rwkv_w0[layer],
                              rwkv_w_up[layer], rwkv_a0[layer], rwkv_a_up[layer], rwkv_g_up[layer],
                              rwkv_k_k[layer], rwkv_k_a[layer], rwkv_r_k[layer], rwkv_gn_g[layer],
                              rwkv_gn_b[layer], fox_f_bias[layer], gate_bias[layer], w_branch[layer],
                              w_out[layer], ln_g[layer, 0], ln_b[layer, 0])
        j = layer // 2
        if layer % 2 == 0:
            xf, xb = _ffn(xb, xf, ffn_w1[j].astype(BF16), ffn_w3[j].astype(BF16), ffn_w2[j].astype(BF16),
                          ln_g[layer, 1], ln_b[layer, 1])
        else:
            xf, xb = _moe(xb, xf, router_w[j], router_b[j], exp_w1[j].astype(BF16), exp_w3[j].astype(BF16),
                          exp_w2[j].astype(BF16), ln_g[layer, 1], ln_b[layer, 1])
    return xf.reshape(bsz, seq, d)
```

```python
import functools

import jax
import jax.numpy as jnp
from jax import lax
from jax.experimental import pallas as pl
from jax.experimental.pallas import tpu as pltpu

F32 = jnp.float32
BF16 = jnp.bfloat16
HIGHEST = lax.Precision.HIGHEST

D_MODEL = 1024
DEPTH = 2
CHUNK = 64
HEAD_DIM = 64
CHUNK_SHIFT = 6
HEAD_SHIFT = 6
SWA_HEADS = 8
SWA_KV_HEADS = 2
SWA_GROUP = SWA_HEADS // SWA_KV_HEADS
SWA_WINDOW = 128
SWA_WIN_CHUNKS = SWA_WINDOW // CHUNK
RWKV_HEADS = 4
RWKV_W = RWKV_HEADS * HEAD_DIM
RWKV_GN_EPS = 64e-5
FOX_HEADS = 4
DSA_HEADS = 4
IDX_HEADS = 4
IDX_DIM = 64
DSA_TOPK_MAX = 256
D_FF = 2816
N_EXPERTS = 8
D_FF_EXPERT = 1408
N_BRANCHES = 4
DN_ALPHA = (2 * DEPTH) ** 0.25
LN_EPS = 1e-5
ATTN_SCALE = HEAD_DIM ** -0.5
IDX_SCALE = IDX_DIM ** -0.5
IDX_W_SCALE = IDX_HEADS ** -0.5
MIX_WIDTHS = (SWA_HEADS * HEAD_DIM, RWKV_W, FOX_HEADS * HEAD_DIM, DSA_HEADS * HEAD_DIM)

COL_GATE = 0
COL_RWKV = 4096
COL_SWA_Q = 5120
COL_FOX_KV = 5632
COL_DSA_QQ = 6144
COL_FOX_Q = 6656
COL_SWA_KV = 6912
COL_DSA_KV = 7168
COL_MISC = 7424
P_WIDTH = 7680

NEG_BIG = -1e30
INT_MIN = -(2 ** 31)
VMEM_LIMIT = 48 * 1024 * 1024


def _layout_w_in(w):
    d = w.shape[0]
    z = lambda n: jnp.zeros((d, n), w.dtype)
    swa, rwkv, fox, dsa, gate = 0, 768, 1792, 2564, 3272
    parts = [
        w[:, gate:gate + 4096],
        w[:, rwkv:rwkv + 1024],
        w[:, swa:swa + 512],
        w[:, fox + 256:fox + 768],
        w[:, dsa:dsa + 256], w[:, dsa + 384:dsa + 640],
        w[:, fox:fox + 256],
        w[:, swa + 512:swa + 768],
        w[:, dsa + 256:dsa + 384], w[:, dsa + 640:dsa + 704], z(64),
        w[:, fox + 768:fox + 772], w[:, dsa + 704:dsa + 708], z(120),
        z(128),
    ]
    return jnp.concatenate(parts, axis=1)


def _cparams(sem):
    return pltpu.CompilerParams(dimension_semantics=sem, vmem_limit_bytes=VMEM_LIMIT)


def _nt(a, b, precision=None):
    return lax.dot_general(a, b, (((1,), (1,)), ((), ())), precision=precision,
                           preferred_element_type=F32)


def _tn(a, b, precision=None):
    return lax.dot_general(a, b, (((0,), (0,)), ((), ())), precision=precision,
                           preferred_element_type=F32)


def _dot(a, b, precision=None):
    return jnp.dot(a, b, precision=precision, preferred_element_type=F32)


def _sigmoid(x):
    return 1.0 / (1.0 + jnp.exp(-x))


def _layer_norm(z, g, b):
    mu = jnp.mean(z, axis=-1, keepdims=True)
    zc = z - mu
    var = jnp.mean(zc * zc, axis=-1, keepdims=True)
    return zc * lax.rsqrt(var + LN_EPS) * g + b


def _skewed(n, first, second):
    out = []
    staged = first(0)
    for h in range(n):
        nxt = first(h + 1) if h + 1 < n else None
        out.append(second(h, staged))
        staged = nxt
    return out


def _head_block_ones(n, scale):
    r = lax.broadcasted_iota(jnp.int32, (n, n), 0) >> HEAD_SHIFT
    c = lax.broadcasted_iota(jnp.int32, (n, n), 1) >> HEAD_SHIFT
    return jnp.where(r == c, scale, 0.0).astype(F32)


def _mm_kernel(a_ref, b_ref, o_ref):
    o_ref[...] = _dot(a_ref[...], b_ref[...])


def _matmul(a, b, tm, tn):
    m, k = a.shape
    n = b.shape[1]
    tm = min(tm, m)
    return pl.pallas_call(
        _mm_kernel,
        grid=(m // tm, n // tn),
        in_specs=[pl.BlockSpec((tm, k), lambda i, j: (i, 0)),
                  pl.BlockSpec((k, tn), lambda i, j: (0, j))],
        out_specs=pl.BlockSpec((tm, tn), lambda i, j: (i, j)),
        out_shape=jax.ShapeDtypeStruct((m, n), F32),
        compiler_params=_cparams(("parallel", "parallel")),
        name="in_proj",
    )(a, b)


def _swa_kernel(sink_ref, q_ref, kv_ref, o_ref, *, tq):
    i = pl.program_id(1)
    win = tq + SWA_WINDOW
    s0 = pl.multiple_of(jnp.maximum(i * tq - SWA_WINDOW, 0), SWA_WINDOW)
    q_chunk = (i * tq + lax.broadcasted_iota(jnp.int32, (tq, win), 0)) >> CHUNK_SHIFT
    k_chunk = (s0 + lax.broadcasted_iota(jnp.int32, (tq, win), 1)) >> CHUNK_SHIFT
    bias = jnp.where(k_chunk <= q_chunk,
                     jnp.where(k_chunk >= q_chunk - SWA_WIN_CHUNKS, 0.0, NEG_BIG), NEG_BIG)
    kv_w = SWA_KV_HEADS * HEAD_DIM
    ks = [kv_ref[0, pl.ds(s0, win), hk * HEAD_DIM:(hk + 1) * HEAD_DIM].astype(BF16)
          for hk in range(SWA_KV_HEADS)]
    vs = [kv_ref[0, pl.ds(s0, win), kv_w + hk * HEAD_DIM:kv_w + (hk + 1) * HEAD_DIM].astype(BF16)
          for hk in range(SWA_KV_HEADS)]

    def logits(h):
        q = (q_ref[0, :, h * HEAD_DIM:(h + 1) * HEAD_DIM] * ATTN_SCALE).astype(BF16)
        return _nt(q, ks[h // SWA_GROUP]) + bias

    def finish(h, s):
        sink = sink_ref[h]
        m = jnp.maximum(jnp.max(s, axis=-1, keepdims=True), sink)
        e = jnp.exp(s - m)
        denom = jnp.sum(e, axis=-1, keepdims=True) + jnp.exp(sink - m)
        o_ref[0, :, h * HEAD_DIM:(h + 1) * HEAD_DIM] = _dot(e.astype(BF16), vs[h // SWA_GROUP]) / denom

    _skewed(SWA_HEADS, logits, finish)


def _swa(p3, sinks, tq=256):
    b, s, _ = p3.shape
    return pl.pallas_call(
        functools.partial(_swa_kernel, tq=tq),
        grid=(b, s // tq),
        in_specs=[pl.BlockSpec(memory_space=pltpu.SMEM),
                  pl.BlockSpec((1, tq, 512), lambda bi, i: (bi, i, COL_SWA_Q // 512)),
                  pl.BlockSpec((1, s, 256), lambda bi, i: (bi, 0, COL_SWA_KV // 256))],
        out_specs=pl.BlockSpec((1, tq, 512), lambda bi, i: (bi, i, 0)),
        out_shape=jax.ShapeDtypeStruct((b, s, 512), F32),
        compiler_params=_cparams(("parallel", "parallel")),
        name="swa",
    )(sinks, p3, p3)


def _fox_c_kernel(f_ref, bias_ref, c_ref, *, blk):
    s = f_ref.shape[1]
    r = lax.broadcasted_iota(jnp.int32, (blk, blk), 0)
    c = lax.broadcasted_iota(jnp.int32, (blk, blk), 1)
    tri = jnp.where(c <= r, 1.0, 0.0).astype(F32)
    carry = jnp.zeros((1, 128), F32)
    for j in range(s // blk):
        x = f_ref[0, j * blk:(j + 1) * blk, :] + bias_ref[...]
        log_f = jnp.minimum(x, 0.0) - jnp.log1p(jnp.exp(-jnp.abs(x)))
        cs = _dot(tri, log_f, HIGHEST) + carry
        carry = cs[blk - 1:blk, :]
        for h in range(FOX_HEADS):
            c_ref[0, h, j * blk:(j + 1) * blk, :] = jnp.broadcast_to(cs[:, h:h + 1], (blk, 128))


def _fox_c(p3, f_bias):
    b, s, _ = p3.shape
    bias_row = jnp.pad(f_bias, (0, 128 - FOX_HEADS)).reshape(1, 128)
    return pl.pallas_call(
        functools.partial(_fox_c_kernel, blk=256),
        grid=(b,),
        in_specs=[pl.BlockSpec((1, s, 128), lambda bi: (bi, 0, COL_MISC // 128)),
                  pl.BlockSpec((1, 128), lambda bi: (0, 0))],
        out_specs=pl.BlockSpec((1, FOX_HEADS, s, 128), lambda bi: (bi, 0, 0, 0)),
        out_shape=jax.ShapeDtypeStruct((b, FOX_HEADS, s, 128), F32),
        compiler_params=_cparams(("parallel",)),
        name="fox_cumsum",
    )(p3, bias_row)


def _fox_kernel(q_ref, k_ref, vt_ref, c_ref, o_ref, lg_ref, acc_ref, *, tq):
    i = pl.program_id(1)
    tk = tq
    key_i = lax.broadcasted_iota(jnp.int32, (tk, tq), 0)
    qry_i = lax.broadcasted_iota(jnp.int32, (tk, tq), 1)
    diag_bias = jnp.where(key_i <= qry_i, 0.0, NEG_BIG)
    q = [(q_ref[0, :, h * HEAD_DIM:(h + 1) * HEAD_DIM] * ATTN_SCALE).astype(BF16) for h in range(FOX_HEADS)]

    def logits_block(j, maxes, masked):
        ks = pl.multiple_of(j * tk, tk)
        dots = [_nt(k_ref[0, pl.ds(ks, tk), h * HEAD_DIM:(h + 1) * HEAD_DIM].astype(BF16), q[h])
                for h in range(FOX_HEADS)]
        new_maxes = []
        for h in range(FOX_HEADS):
            c_k = c_ref[0, h, pl.ds(ks, tk), :]
            s = dots[h] - jnp.concatenate([c_k] * (tq // 128), axis=1)
            if masked:
                s = s + diag_bias
            lg_ref[h, pl.ds(ks, tk), :] = s
            new_maxes.append(jnp.maximum(maxes[h], jnp.max(s, axis=0, keepdims=True)))
        return tuple(new_maxes)

    maxes = tuple(jnp.full((1, tq), NEG_BIG, F32) for _ in range(FOX_HEADS))
    maxes = lax.fori_loop(0, i, lambda j, mx: logits_block(j, mx, False), maxes)
    maxes = logits_block(i, maxes, True)
    acc_ref[...] = jnp.zeros_like(acc_ref)

    def attend_block(j, sums):
        ks = pl.multiple_of(j * tk, tk)
        new_sums = []
        for h in range(FOX_HEADS):
            p = jnp.exp(lg_ref[h, pl.ds(ks, tk), :] - maxes[h])
            v_t = vt_ref[0, h * HEAD_DIM:(h + 1) * HEAD_DIM, pl.ds(ks, tk)].astype(BF16)
            acc_ref[h] += _dot(v_t, p.astype(BF16))
            new_sums.append(sums[h] + jnp.sum(p, axis=0, keepdims=True))
        return tuple(new_sums)

    sums = lax.fori_loop(0, i + 1, attend_block, tuple(jnp.zeros((1, tq), F32) for _ in range(FOX_HEADS)))
    out_t = jnp.concatenate([acc_ref[h] / sums[h] for h in range(FOX_HEADS)], axis=0)
    o_ref[0] = out_t.T


def _fox(p3, v_t, c, tq=256):
    b, s, _ = p3.shape
    w = FOX_HEADS * HEAD_DIM
    return pl.pallas_call(
        functools.partial(_fox_kernel, tq=tq),
        grid=(b, s // tq),
        in_specs=[pl.BlockSpec((1, tq, w), lambda bi, i: (bi, i, COL_FOX_Q // w)),
                  pl.BlockSpec((1, s, w), lambda bi, i: (bi, 0, COL_FOX_KV // w)),
                  pl.BlockSpec((1, w, s), lambda bi, i: (bi, 0, 0)),
                  pl.BlockSpec((1, FOX_HEADS, s, 128), lambda bi, i: (bi, 0, 0, 0))],
        out_specs=pl.BlockSpec((1, tq, w), lambda bi, i: (bi, i, 0)),
        out_shape=jax.ShapeDtypeStruct((b, s, w), F32),
        scratch_shapes=[pltpu.VMEM((FOX_HEADS, s, tq), F32), pltpu.VMEM((FOX_HEADS, HEAD_DIM, tq), F32)],
        compiler_params=_cparams(("parallel", "parallel")),
        name="fox",
    )(p3, p3, v_t, c)


def _key_to_score(key):
    return pltpu.bitcast(key ^ ((key >> 31) & 0x7FFFFFFF), F32)


def _dsa_kernel(qq_ref, misc_ref, kv_ref, vt_ref, o_ref, sc_ref, lg_ref, acc_ref, *, tq, kb, topk):
    i = pl.program_id(1)
    n_kb = lax.div((i + 1) * tq + kb - 1, kb)
    lane8 = lax.broadcasted_iota(jnp.int32, (8, 128), 1)
    row8 = lax.broadcasted_iota(jnp.int32, (8, 128), 0)
    pick = jnp.where((lane8 == row8 + 4) & (row8 < IDX_HEADS), 1.0, 0.0).astype(F32)
    w_t = _nt(pick, misc_ref[0], HIGHEST) * (IDX_W_SCALE * IDX_SCALE)
    q_chunk = (i * tq + lax.broadcasted_iota(jnp.int32, (1, tq), 1)) >> CHUNK_SHIFT
    qw = DSA_HEADS * HEAD_DIM
    q_idx = [qq_ref[0, :, qw + h * IDX_DIM:qw + (h + 1) * IDX_DIM].astype(BF16) for h in range(IDX_HEADS)]

    def score_block(j, carry):
        ks = pl.multiple_of(j * kb, kb)
        k_idx = kv_ref[0, pl.ds(ks, kb), 2 * HEAD_DIM:2 * HEAD_DIM + IDX_DIM].astype(BF16)
        dots = [_nt(k_idx, q_idx[h]) for h in range(IDX_HEADS)]
        score = jnp.maximum(dots[0], 0.0) * w_t[0:1, :]
        for h in range(1, IDX_HEADS):
            score = score + jnp.maximum(dots[h], 0.0) * w_t[h:h + 1, :]
        k_chunk = (ks + lax.broadcasted_iota(jnp.int32, (kb, tq), 0)) >> CHUNK_SHIFT
        sc_ref[pl.ds(ks, kb), :] = jnp.where(k_chunk <= q_chunk, score, -jnp.inf)
        return carry

    lax.fori_loop(0, n_kb, score_block, 0)

    def count(pred):
        def body(j, acc):
            blk = sc_ref[pl.ds(pl.multiple_of(j * kb, kb), kb), :]
            hit = jnp.where(pred(blk), 1, 0).astype(jnp.int32)
            return acc + jnp.sum(hit.reshape(kb // 8, 8, tq), axis=0)
        acc = lax.fori_loop(0, n_kb, body, jnp.zeros((8, tq), jnp.int32))
        return jnp.sum(acc.astype(F32), axis=0, keepdims=True)

    zero = jnp.zeros((1, tq), F32)
    thr_key = jnp.where(count(lambda blk: blk >= zero) >= topk, 0, INT_MIN).astype(jnp.int32)

    def bit_step(bi, thr_key):
        cand = thr_key | lax.shift_left(jnp.int32(1), 30 - bi)
        cand_score = _key_to_score(cand)
        return jnp.where(count(lambda blk: blk >= cand_score) >= topk, cand, thr_key)

    thr_key = lax.fori_loop(0, 31, bit_step, thr_key)
    thr = jnp.where(thr_key == INT_MIN, -jnp.inf, _key_to_score(thr_key))
    need = topk - count(lambda blk: blk > thr)

    r = lax.broadcasted_iota(jnp.int32, (kb, kb), 0)
    c = lax.broadcasted_iota(jnp.int32, (kb, kb), 1)
    earlier = jnp.where(c < r, 1.0, 0.0).astype(BF16)
    q_att = [(qq_ref[0, :, h * HEAD_DIM:(h + 1) * HEAD_DIM] * ATTN_SCALE).astype(BF16)
             for h in range(DSA_HEADS)]

    def logits_block(j, carry):
        ties_before, maxes = carry
        ks = pl.multiple_of(j * kb, kb)
        k = kv_ref[0, pl.ds(ks, kb), 0:HEAD_DIM].astype(BF16)
        dots = [_nt(k, q_att[h]) for h in range(DSA_HEADS)]
        score = sc_ref[pl.ds(ks, kb), :]
        k_chunk = (ks + lax.broadcasted_iota(jnp.int32, (kb, tq), 0)) >> CHUNK_SHIFT
        tie = jnp.where(score == thr, jnp.where(k_chunk <= q_chunk, 1.0, 0.0), 0.0)
        rank = _dot(earlier, tie.astype(BF16)) + ties_before
        tie_taken = jnp.where(rank < need, tie, 0.0)
        bias = jnp.where(score > thr, 0.0, jnp.where(tie_taken > 0.5, 0.0, NEG_BIG))
        new_maxes = []
        for h in range(DSA_HEADS):
            s = dots[h] + bias
            lg_ref[h, pl.ds(ks, kb), :] = s
            new_maxes.append(jnp.maximum(maxes[h], jnp.max(s, axis=0, keepdims=True)))
        return ties_before + jnp.sum(tie, axis=0, keepdims=True), tuple(new_maxes)

    init = (jnp.zeros((1, tq), F32), tuple(jnp.full((1, tq), NEG_BIG, F32) for _ in range(DSA_HEADS)))
    _, maxes = lax.fori_loop(0, n_kb, logits_block, init)

    acc_ref[...] = jnp.zeros_like(acc_ref)

    def attend_block(j, sums):
        ks = pl.multiple_of(j * kb, kb)
        v_t = vt_ref[0, :, pl.ds(ks, kb)].astype(BF16)
        new_sums = []
        for h in range(DSA_HEADS):
            p = jnp.exp(lg_ref[h, pl.ds(ks, kb), :] - maxes[h])
            acc_ref[h] += _dot(v_t, p.astype(BF16))
            new_sums.append(sums[h] + jnp.sum(p, axis=0, keepdims=True))
        return tuple(new_sums)

    sums = lax.fori_loop(0, n_kb, attend_block, tuple(jnp.zeros((1, tq), F32) for _ in range(DSA_HEADS)))
    out_t = jnp.concatenate([acc_ref[h] / sums[h] for h in range(DSA_HEADS)], axis=0)
    o_ref[0] = out_t.T


def _dsa(p3, v_t, tq=256, kb=256):
    b, s, _ = p3.shape
    topk = min(DSA_TOPK_MAX, s // 4)
    return pl.pallas_call(
        functools.partial(_dsa_kernel, tq=tq, kb=kb, topk=topk),
        grid=(b, s // tq),
        in_specs=[pl.BlockSpec((1, tq, 512), lambda bi, i: (bi, i, COL_DSA_QQ // 512)),
                  pl.BlockSpec((1, tq, 128), lambda bi, i: (bi, i, COL_MISC // 128)),
                  pl.BlockSpec((1, s, 256), lambda bi, i: (bi, 0, COL_DSA_KV // 256)),
                  pl.BlockSpec((1, HEAD_DIM, s), lambda bi, i: (bi, 0, 0))],
        out_specs=pl.BlockSpec((1, tq, 256), lambda bi, i: (bi, i, 0)),
        out_shape=jax.ShapeDtypeStruct((b, s, 256), F32),
        scratch_shapes=[pltpu.VMEM((s, tq), F32), pltpu.VMEM((DSA_HEADS, s, tq), F32),
                        pltpu.VMEM((DSA_HEADS, HEAD_DIM, tq), F32)],
        compiler_params=_cparams(("parallel", "parallel")),
        name="dsa",
    )(p3, p3, p3, v_t)


def _rwkv_pre_kernel(p_ref, mu_ref, w0_ref, wup_ref, a0_ref, aup_ref, gup_ref, kk_ref, ka_ref, rk_ref,
                     r_o, lw_o, k_o, v_o, kn_o, b_o, g_o, bonus_o, last_ref, *, tt):
    t = pl.program_id(1)

    @pl.when(t == 0)
    def _():
        last_ref[...] = jnp.zeros_like(last_ref)

    p = p_ref[0]
    row = lax.broadcasted_iota(jnp.int32, p.shape, 0)
    p_prev = jnp.where(row == 0, last_ref[...], pltpu.roll(p, 1, axis=0))
    last_ref[...] = p[tt - 1:tt, :]
    ps = p + mu_ref[...] * (p_prev - p)
    w = RWKV_W
    r, k, v = ps[:, 0:w], ps[:, w:2 * w], ps[:, 2 * w:3 * w]
    w_lo, a_lo, g_lo = ps[:, 3 * w:3 * w + 64], ps[:, 3 * w + 64:3 * w + 128], ps[:, 3 * w + 128:3 * w + 256]
    ww = w0_ref[...] + _dot(jnp.tanh(w_lo), wup_ref[...], HIGHEST)
    softplus_neg = jnp.maximum(-ww, 0.0) + jnp.log1p(jnp.exp(-jnp.abs(ww)))
    log_w = -jnp.exp(-softplus_neg - 0.5)
    a = _sigmoid(a0_ref[...] + _dot(a_lo, aup_ref[...], HIGHEST))
    g = _dot(_sigmoid(g_lo), gup_ref[...], HIGHEST)
    head_sum = _head_block_ones(w, 1.0)
    kn = k * kk_ref[...]
    kn = kn * lax.rsqrt(_dot(kn * kn, head_sum, HIGHEST) + 1e-12)
    k2 = k * (1.0 + (a - 1.0) * ka_ref[...])
    bonus = _dot(r * k2 * rk_ref[...], head_sum, HIGHEST) * v
    r_o[0] = r
    lw_o[0] = log_w
    k_o[0] = k2
    v_o[0] = v
    kn_o[0] = kn
    b_o[0] = kn * a
    g_o[0] = g
    bonus_o[0] = bonus


def _rwkv_pre(p3, mu, w0, w_up, a0, a_up, g_up, k_k, k_a, r_k, tt=256):
    b, s, _ = p3.shape
    w = RWKV_W
    row = lambda x: x.reshape(1, -1)
    full = lambda shape: pl.BlockSpec(shape, lambda bi, t: (0,) * len(shape))
    out = jax.ShapeDtypeStruct((b, s, w), F32)
    return pl.pallas_call(
        functools.partial(_rwkv_pre_kernel, tt=tt),
        grid=(b, s // tt),
        in_specs=[pl.BlockSpec((1, tt, 1024), lambda bi, t: (bi, t, COL_RWKV // 1024)),
                  full((1, 1024)), full((1, w)), full((64, w)), full((1, w)), full((64, w)),
                  full((128, w)), full((1, w)), full((1, w)), full((1, w))],
        out_specs=[pl.BlockSpec((1, tt, w), lambda bi, t: (bi, t, 0))] * 8,
        out_shape=[out] * 8,
        scratch_shapes=[pltpu.VMEM((1, 1024), F32)],
        compiler_params=_cparams(("parallel", "arbitrary")),
        name="rwkv_pre",
    )(p3, row(mu), row(w0), w_up, row(a0), a_up, g_up, row(k_k), row(k_a), row(r_k))


def _bdot(a, b):
    return _dot(a.astype(BF16), b.astype(BF16))


def _bnt(a, b):
    return _nt(a.astype(BF16), b.astype(BF16))


def _btn(a, b):
    return _tn(a.astype(BF16), b.astype(BF16))


def _rwkv_chunk_kernel(r_ref, lw_ref, k_ref, v_ref, kn_ref, b_ref, y_ref, s_ref, *, cs, nch):
    c = pl.program_id(1)

    @pl.when(c == 0)
    def _():
        s_ref[...] = jnp.zeros_like(s_ref)

    rows = cs * nch
    big_row = lax.broadcasted_iota(jnp.int32, (rows, rows), 0)
    big_col = lax.broadcasted_iota(jnp.int32, (rows, rows), 1)
    same_chunk = (big_row >> CHUNK_SHIFT) == (big_col >> CHUNK_SHIFT)
    tri = jnp.where(same_chunk, jnp.where(big_col <= big_row, 1.0, 0.0), 0.0).astype(F32)
    row = lax.broadcasted_iota(jnp.int32, (cs, cs), 0)
    col = lax.broadcasted_iota(jnp.int32, (cs, cs), 1)
    incl = col <= row
    strict = col < row
    eye = jnp.where(col == row, 1.0, 0.0).astype(F32)
    lw = lw_ref[0]
    cum = _dot(tri, lw, HIGHEST)
    e_incl = jnp.exp(cum)
    e_neg = jnp.exp(-cum)
    abar_all = -kn_ref[0] * jnp.exp(cum - lw)
    rbar_all = r_ref[0] * e_incl
    bt_all = b_ref[0] * e_neg
    kt_all = k_ref[0] * e_neg
    v_all = v_ref[0]
    n_double = max(cs.bit_length() - 2, 0)
    probs = [(j, h) for j in range(nch) for h in range(RWKV_HEADS)]
    cut = lambda t, jh: t[jh[0] * cs:(jh[0] + 1) * cs, jh[1] * HEAD_DIM:(jh[1] + 1) * HEAD_DIM]
    each = lambda fn: {jh: fn(jh) for jh in probs}
    abar, rbar = each(lambda jh: cut(abar_all, jh)), each(lambda jh: cut(rbar_all, jh))
    bt, kt, v = each(lambda jh: cut(bt_all, jh)), each(lambda jh: cut(kt_all, jh)), each(lambda jh: cut(v_all, jh))
    p_last = each(lambda jh: e_incl[(jh[0] + 1) * cs - 1:(jh[0] + 1) * cs, jh[1] * HEAD_DIM:(jh[1] + 1) * HEAD_DIM])
    ar = each(lambda jh: jnp.concatenate([abar[jh], rbar[jh]], axis=0))
    ar_b = each(lambda jh: _bnt(ar[jh], bt[jh]))
    ar_k = each(lambda jh: _bnt(ar[jh], kt[jh]))
    a_ab = each(lambda jh: jnp.where(strict, ar_b[jh][:cs], 0.0))
    a_ak = each(lambda jh: jnp.where(strict, ar_k[jh][:cs], 0.0))
    a_rb = each(lambda jh: jnp.where(incl, ar_b[jh][cs:], 0.0))
    a_rk = each(lambda jh: jnp.where(incl, ar_k[jh][cs:], 0.0))
    ak_v = each(lambda jh: _bdot(a_ak[jh], v[jh]))
    inv = each(lambda jh: eye + a_ab[jh])
    power = a_ab
    for _ in range(n_double):
        power = each(lambda jh: _bdot(power[jh], power[jh]))
        inv = each(lambda jh: inv[jh] + _bdot(inv[jh], power[jh]))
    a_hat = each(lambda jh: _bdot(inv[jh], abar[jh]))
    u_hat = each(lambda jh: _bdot(inv[jh], ak_v[jh]))
    r_hat = each(lambda jh: rbar[jh] + _bdot(a_rb[jh], a_hat[jh]))
    y_hat = each(lambda jh: _bdot(a_rb[jh], u_hat[jh]) + _bdot(a_rk[jh], v[jh]))
    g_mat = each(lambda jh: eye * p_last[jh] + _btn(a_hat[jh], bt[jh] * p_last[jh]))
    h_mat = each(lambda jh: _btn(jnp.concatenate([u_hat[jh], v[jh]], axis=0),
                                 jnp.concatenate([bt[jh], kt[jh]], axis=0) * p_last[jh]))
    states = [s_ref[h] for h in range(RWKV_HEADS)]
    for j in range(nch):
        for h in range(RWKV_HEADS):
            y_ref[0, j * cs:(j + 1) * cs, h * HEAD_DIM:(h + 1) * HEAD_DIM] = (
                _bnt(r_hat[j, h], states[h]) + y_hat[j, h])
        states = [_dot(states[h], g_mat[j, h], HIGHEST) + h_mat[j, h] for h in range(RWKV_HEADS)]
    for h in range(RWKV_HEADS):
        s_ref[h] = states[h]


def _rwkv_chunk(r, lw, k, v, kn, b_, cs=CHUNK, nch=4):
    b, s, w = r.shape
    spec = pl.BlockSpec((1, cs * nch, w), lambda bi, c: (bi, c, 0))
    return pl.pallas_call(
        functools.partial(_rwkv_chunk_kernel, cs=cs, nch=nch),
        grid=(b, s // (cs * nch)),
        in_specs=[spec] * 6,
        out_specs=spec,
        out_shape=jax.ShapeDtypeStruct((b, s, w), F32),
        scratch_shapes=[pltpu.VMEM((RWKV_HEADS, HEAD_DIM, HEAD_DIM), F32)],
        compiler_params=_cparams(("parallel", "arbitrary")),
        name="rwkv_chunk",
    )(r, lw, k, v, kn, b_)


def _merge_kernel(x_ref, gate_ref, swa_ref, y_ref, g_ref, bonus_ref, fox_ref, dsa_ref, gng_ref, gnb_ref,
                  gbias_ref, wb_ref, wo_ref, lng_ref, lnb_ref, o_ref, ob_ref):
    y = y_ref[...]
    head_mean = _head_block_ones(RWKV_W, 1.0 / HEAD_DIM)
    yc = y - _dot(y, head_mean, HIGHEST)
    yv = _dot(yc * yc, head_mean, HIGHEST)
    o_rwkv = (yc * lax.rsqrt(yv + RWKV_GN_EPS) * gng_ref[...] + gnb_ref[...] + bonus_ref[...]) * g_ref[...]
    branches = (swa_ref[...], o_rwkv, fox_ref[...], dsa_ref[...])
    merged = jnp.zeros(o_ref.shape, F32)
    off = 0
    for i, o in enumerate(branches):
        width = MIX_WIDTHS[i]
        proj = _dot(o.astype(BF16), wb_ref[off:off + width, :])
        gate = _sigmoid(gate_ref[:, i * D_MODEL:(i + 1) * D_MODEL] + gbias_ref[i:i + 1, :])
        merged = merged + gate * proj
        off += width
    y_out = _dot(merged.astype(BF16), wo_ref[...])
    xn = _layer_norm(DN_ALPHA * x_ref[...] + y_out, lng_ref[...], lnb_ref[...])
    o_ref[...] = xn
    ob_ref[...] = xn.astype(BF16)


def _merge(xf, p2, o_swa, y, g, bonus, o_fox, o_dsa, gn_g, gn_b, gate_bias, w_branch, w_out, ln_g, ln_b, tm=256):
    n, d = xf.shape
    row = lambda x: x.reshape(1, -1)
    tok = lambda wdt: pl.BlockSpec((tm, wdt), lambda i: (i, 0))
    full = lambda shape: pl.BlockSpec(shape, lambda i: (0,) * len(shape))
    return pl.pallas_call(
        _merge_kernel,
        grid=(n // tm,),
        in_specs=[tok(d), pl.BlockSpec((tm, N_BRANCHES * d), lambda i: (i, COL_GATE)),
                  tok(512), tok(256), tok(256), tok(256), tok(256), tok(256),
                  full((1, 256)), full((1, 256)), full((N_BRANCHES, d)),
                  full((sum(MIX_WIDTHS), d)), full((d, d)), full((1, d)), full((1, d))],
        out_specs=[tok(d), tok(d)],
        out_shape=[jax.ShapeDtypeStruct((n, d), F32), jax.ShapeDtypeStruct((n, d), BF16)],
        compiler_params=_cparams(("parallel",)),
        name="merge",
    )(xf, p2, o_swa, y, g, bonus, o_fox, o_dsa, row(gn_g), row(gn_b), gate_bias, w_branch, w_out,
      row(ln_g), row(ln_b))


def _silu(x):
    return x * _sigmoid(x)


def _ffn_kernel(xb_ref, x_ref, w1_ref, w3_ref, w2_ref, lng_ref, lnb_ref, o_ref, ob_ref, acc_ref):
    f = pl.program_id(1)

    @pl.when(f == 0)
    def _():
        acc_ref[...] = jnp.zeros_like(acc_ref)

    xb = xb_ref[...]
    hidden = _silu(_dot(xb, w1_ref[...])) * _dot(xb, w3_ref[...])
    acc_ref[...] += _dot(hidden.astype(BF16), w2_ref[...])

    @pl.when(f == pl.num_programs(1) - 1)
    def _():
        xn = _layer_norm(DN_ALPHA * x_ref[...] + acc_ref[...], lng_ref[...], lnb_ref[...])
        o_ref[...] = xn
        ob_ref[...] = xn.astype(BF16)


def _ffn(xb, xf, w1, w3, w2, ln_g, ln_b, tm=512, tf=1408):
    n, d = xf.shape
    ff = w1.shape[1]
    row = lambda x: x.reshape(1, -1)
    return pl.pallas_call(
        _ffn_kernel,
        grid=(n // tm, ff // tf),
        in_specs=[pl.BlockSpec((tm, d), lambda i, f: (i, 0)),
                  pl.BlockSpec((tm, d), lambda i, f: (i, 0)),
                  pl.BlockSpec((d, tf), lambda i, f: (0, f)),
                  pl.BlockSpec((d, tf), lambda i, f: (0, f)),
                  pl.BlockSpec((tf, d), lambda i, f: (f, 0)),
                  pl.BlockSpec((1, d), lambda i, f: (0, 0)),
                  pl.BlockSpec((1, d), lambda i, f: (0, 0))],
        out_specs=[pl.BlockSpec((tm, d), lambda i, f: (i, 0))] * 2,
        out_shape=[jax.ShapeDtypeStruct((n, d), F32), jax.ShapeDtypeStruct((n, d), BF16)],
        scratch_shapes=[pltpu.VMEM((tm, d), F32)],
        compiler_params=_cparams(("parallel", "arbitrary")),
        name="ffn",
    )(xb, xf, w1, w3, w2, row(ln_g), row(ln_b))


def _moe_kernel(xb_ref, x_ref, rw_ref, rb_ref, w1_ref, w3_ref, w2_ref, lng_ref, lnb_ref, o_ref, ob_ref,
                acc_ref, comb_ref):
    e = pl.program_id(1)
    lane = lax.broadcasted_iota(jnp.int32, comb_ref.shape, 1).astype(F32)

    @pl.when(e == 0)
    def _():
        acc_ref[...] = jnp.zeros_like(acc_ref)
        logits = _dot(x_ref[...], rw_ref[...], HIGHEST) + rb_ref[...]
        logits = jnp.where(lane < N_EXPERTS, logits, NEG_BIG)
        m1 = jnp.max(logits, axis=-1, keepdims=True)
        i1 = jnp.min(jnp.where(logits == m1, lane, 128.0), axis=-1, keepdims=True)
        rest = jnp.where(lane == i1, NEG_BIG, logits)
        m2 = jnp.max(rest, axis=-1, keepdims=True)
        i2 = jnp.min(jnp.where(rest == m2, lane, 128.0), axis=-1, keepdims=True)
        e2 = jnp.exp(m2 - m1)
        comb_ref[...] = jnp.where(lane == i1, 1.0 / (1.0 + e2), 0.0) + jnp.where(lane == i2, e2 / (1.0 + e2), 0.0)

    weight = jnp.sum(jnp.where(lane == e.astype(F32), comb_ref[...], 0.0), axis=-1, keepdims=True)
    xb = xb_ref[...]
    hidden = _silu(_dot(xb, w1_ref[0])) * _dot(xb, w3_ref[0]) * weight
    acc_ref[...] += _dot(hidden.astype(BF16), w2_ref[0])

    @pl.when(e == pl.num_programs(1) - 1)
    def _():
        xn = _layer_norm(DN_ALPHA * x_ref[...] + acc_ref[...], lng_ref[...], lnb_ref[...])
        o_ref[...] = xn
        ob_ref[...] = xn.astype(BF16)


def _moe(xb, xf, router_w, router_b, w1, w3, w2, ln_g, ln_b, tm=512):
    n, d = xf.shape
    ne, _, fe = w1.shape
    rw = jnp.pad(router_w, ((0, 0), (0, 128 - ne)))
    rb = jnp.pad(router_b, (0, 128 - ne)).reshape(1, 128)
    row = lambda x: x.reshape(1, -1)
    return pl.pallas_call(
        _moe_kernel,
        grid=(n // tm, ne),
        in_specs=[pl.BlockSpec((tm, d), lambda i, e: (i, 0)),
                  pl.BlockSpec((tm, d), lambda i, e: (i, 0)),
                  pl.BlockSpec((d, 128), lambda i, e: (0, 0)),
                  pl.BlockSpec((1, 128), lambda i, e: (0, 0)),
                  pl.BlockSpec((1, d, fe), lambda i, e: (e, 0, 0)),
                  pl.BlockSpec((1, d, fe), lambda i, e: (e, 0, 0)),
                  pl.BlockSpec((1, fe, d), lambda i, e: (e, 0, 0)),
                  pl.BlockSpec((1, d), lambda i, e: (0, 0)),
                  pl.BlockSpec((1, d), lambda i, e: (0, 0))],
        out_specs=[pl.BlockSpec((tm, d), lambda i, e: (i, 0))] * 2,
        out_shape=[jax.ShapeDtypeStruct((n, d), F32), jax.ShapeDtypeStruct((n, d), BF16)],
        scratch_shapes=[pltpu.VMEM((tm, d), F32), pltpu.VMEM((tm, 128), F32)],
        compiler_params=_cparams(("parallel", "arbitrary")),
        name="moe",
    )(xb, xf, rw, rb, w1, w3, w2, row(ln_g), row(ln_b))


def _mixer_layer(xf, xb, bsz, seq, w_in, sinks, mu, w0, w_up, a0, a_up, g_up, k_k, k_a, r_k, gn_g, gn_b,
                 f_bias, gate_bias, w_branch, w_out, ln_g, ln_b):
    p2 = _matmul(xb, _layout_w_in(w_in).astype(BF16), tm=1024, tn=512)
    p3 = p2.reshape(bsz, seq, P_WIDTH)
    o_swa = _swa(p3, sinks)
    fox_w = FOX_HEADS * HEAD_DIM
    fox_v_t = jnp.swapaxes(p3[:, :, COL_FOX_KV + fox_w:COL_FOX_KV + 2 * fox_w], 1, 2)
    o_fox = _fox(p3, fox_v_t, _fox_c(p3, f_bias))
    v_t = jnp.swapaxes(p3[:, :, COL_DSA_KV + HEAD_DIM:COL_DSA_KV + 2 * HEAD_DIM], 1, 2)
    o_dsa = _dsa(p3, v_t)
    r, lw, k2, v, kn, b_, g, bonus = _rwkv_pre(p3, mu, w0, w_up, a0, a_up, g_up, k_k, k_a, r_k)
    y = _rwkv_chunk(r, lw, k2, v, kn, b_)
    n = bsz * seq
    flat = lambda t: t.reshape(n, t.shape[-1])
    return _merge(xf, p2, flat(o_swa), flat(y), flat(g), flat(bonus), flat(o_fox), flat(o_dsa), gn_g, gn_b,
                  gate_bias, w_branch.astype(BF16), w_out.astype(BF16), ln_g, ln_b)


def kernel(x, w_in, swa_sinks, rwkv_mu, rwkv_w0, rwkv_w_up, rwkv_a0, rwkv_a_up, rwkv_g_up, rwkv_k_k, rwkv_k_a,
           rwkv_r_k, rwkv_gn_g, rwkv_gn_b, fox_f_bias, gate_bias, w_branch, w_out, ln_g, ln_b, ffn_w1, ffn_w3,
           ffn_w2, router_w, router_b, exp_w1, exp_w3, exp_w2):
    bsz, seq, d = x.shape
    xf = x.reshape(bsz * seq, d)
    xb = xf.astype(BF16)
    for layer in range(DEPTH):
        xf, xb = _mixer_layer(xf, xb, bsz, seq, w_in[layer], swa_sinks[layer], rwkv_mu[layer], rwkv_w0[layer],
                              rwkv_w_up[layer], rwkv_a0[layer], rwkv_a_up[layer], rwkv_g_up[layer],
                              rwkv_k_k[layer], rwkv_k_a[layer], rwkv_r_k[layer], rwkv_gn_g[layer],
                              rwkv_gn_b[layer], fox_f_bias[layer], gate_bias[layer], w_branch[layer],
                              w_out[layer], ln_g[layer, 0], ln_b[layer, 0])
        j = layer // 2
        if layer % 2 == 0:
            xf, xb = _ffn(xb, xf, ffn_w1[j].astype(BF16), ffn_w3[j].astype(BF16), ffn_w2[j].astype(BF16),
                          ln_g[layer, 1], ln_b[layer, 1])
        else:
            xf, xb = _moe(xb, xf, router_w[j], router_b[j], exp_w1[j].astype(BF16), exp_w3[j].astype(BF16),
                          exp_w2[j].astype(BF16), ln_g[layer, 1], ln_b[layer, 1])
    return xf.reshape(bsz, seq, d)
```

```python
import functools

import jax
import jax.numpy as jnp
from jax import lax
from jax.experimental import pallas as pl
from jax.experimental.pallas import tpu as pltpu

F32 = jnp.float32
BF16 = jnp.bfloat16
HIGHEST = lax.Precision.HIGHEST

D_MODEL = 1024
DEPTH = 2
CHUNK = 64
HEAD_DIM = 64
CHUNK_SHIFT = 6
HEAD_SHIFT = 6
SWA_HEADS = 8
SWA_KV_HEADS = 2
SWA_GROUP = SWA_HEADS // SWA_KV_HEADS
SWA_WINDOW = 128
SWA_WIN_CHUNKS = SWA_WINDOW // CHUNK
RWKV_HEADS = 4
RWKV_W = RWKV_HEADS * HEAD_DIM
RWKV_GN_EPS = 64e-5
FOX_HEADS = 4
DSA_HEADS = 4
IDX_HEADS = 4
IDX_DIM = 64
DSA_TOPK_MAX = 256
D_FF = 2816
N_EXPERTS = 8
D_FF_EXPERT = 1408
N_BRANCHES = 4
DN_ALPHA = (2 * DEPTH) ** 0.25
LN_EPS = 1e-5
ATTN_SCALE = HEAD_DIM ** -0.5
IDX_SCALE = IDX_DIM ** -0.5
IDX_W_SCALE = IDX_HEADS ** -0.5
MIX_WIDTHS = (SWA_HEADS * HEAD_DIM, RWKV_W, FOX_HEADS * HEAD_DIM, DSA_HEADS * HEAD_DIM)

COL_RWKV = 0
COL_SWA_Q = 1024
COL_FOX_KV = 1536
COL_DSA_QQ = 2048
COL_FOX_Q = 2560
COL_SWA_KV = 2816
COL_DSA_KV = 3072
COL_MISC = 3328
P_WIDTH = 3584

NEG_BIG = -1e30
INT_MIN = -(2 ** 31)
VMEM_LIMIT = 48 * 1024 * 1024
MOE_VMEM_LIMIT = 58 * 1024 * 1024


def _layout_w_in(w):
    d = w.shape[0]
    z = lambda n: jnp.zeros((d, n), w.dtype)
    swa, rwkv, fox, dsa, gate = 0, 768, 1792, 2564, 3272
    parts = [
        w[:, rwkv:rwkv + 1024],
        w[:, swa:swa + 512],
        w[:, fox + 256:fox + 768],
        w[:, dsa:dsa + 256], w[:, dsa + 384:dsa + 640],
        w[:, fox:fox + 256],
        w[:, swa + 512:swa + 768],
        w[:, dsa + 256:dsa + 384], w[:, dsa + 640:dsa + 704], z(64),
        w[:, fox + 768:fox + 772], w[:, dsa + 704:dsa + 708], z(120),
        z(128),
    ]
    return jnp.concatenate(parts, axis=1), w[:, gate:gate + N_BRANCHES * D_MODEL]


def _cparams(sem):
    return pltpu.CompilerParams(dimension_semantics=sem, vmem_limit_bytes=VMEM_LIMIT)


def _nt(a, b, precision=None):
    return lax.dot_general(a, b, (((1,), (1,)), ((), ())), precision=precision,
                           preferred_element_type=F32)


def _tn(a, b, precision=None):
    return lax.dot_general(a, b, (((0,), (0,)), ((), ())), precision=precision,
                           preferred_element_type=F32)


def _dot(a, b, precision=None):
    return jnp.dot(a, b, precision=precision, preferred_element_type=F32)


def _sigmoid(x):
    return 1.0 / (1.0 + jnp.exp(-x))


def _layer_norm(z, g, b):
    mu = jnp.mean(z, axis=-1, keepdims=True)
    zc = z - mu
    var = jnp.mean(zc * zc, axis=-1, keepdims=True)
    return zc * lax.rsqrt(var + LN_EPS) * g + b


def _skewed(n, first, second):
    out = []
    staged = first(0)
    for h in range(n):
        nxt = first(h + 1) if h + 1 < n else None
        out.append(second(h, staged))
        staged = nxt
    return out


def _head_block_ones(n, scale):
    r = lax.broadcasted_iota(jnp.int32, (n, n), 0) >> HEAD_SHIFT
    c = lax.broadcasted_iota(jnp.int32, (n, n), 1) >> HEAD_SHIFT
    return jnp.where(r == c, scale, 0.0).astype(F32)


def _mm_kernel(a_ref, b_ref, o_ref):
    o_ref[...] = _dot(a_ref[...], b_ref[...])


def _matmul(a, b, tm, tn):
    m, k = a.shape
    n = b.shape[1]
    tm = min(tm, m)
    return pl.pallas_call(
        _mm_kernel,
        grid=(m // tm, n // tn),
        in_specs=[pl.BlockSpec((tm, k), lambda i, j: (i, 0)),
                  pl.BlockSpec((k, tn), lambda i, j: (0, j))],
        out_specs=pl.BlockSpec((tm, tn), lambda i, j: (i, j)),
        out_shape=jax.ShapeDtypeStruct((m, n), F32),
        compiler_params=_cparams(("parallel", "parallel")),
        name="in_proj",
    )(a, b)


def _swa_kernel(sink_ref, q_ref, kv_ref, o_ref, *, tq):
    i = pl.program_id(1)
    win = tq + SWA_WINDOW
    s0 = pl.multiple_of(jnp.maximum(i * tq - SWA_WINDOW, 0), SWA_WINDOW)
    q_chunk = (i * tq + lax.broadcasted_iota(jnp.int32, (tq, win), 0)) >> CHUNK_SHIFT
    k_chunk = (s0 + lax.broadcasted_iota(jnp.int32, (tq, win), 1)) >> CHUNK_SHIFT
    bias = jnp.where(k_chunk <= q_chunk,
                     jnp.where(k_chunk >= q_chunk - SWA_WIN_CHUNKS, 0.0, NEG_BIG), NEG_BIG)
    kv_w = SWA_KV_HEADS * HEAD_DIM
    ks = [kv_ref[0, pl.ds(s0, win), hk * HEAD_DIM:(hk + 1) * HEAD_DIM].astype(BF16)
          for hk in range(SWA_KV_HEADS)]
    vs = [kv_ref[0, pl.ds(s0, win), kv_w + hk * HEAD_DIM:kv_w + (hk + 1) * HEAD_DIM].astype(BF16)
          for hk in range(SWA_KV_HEADS)]

    def logits(h):
        q = (q_ref[0, :, h * HEAD_DIM:(h + 1) * HEAD_DIM] * ATTN_SCALE).astype(BF16)
        return _nt(q, ks[h // SWA_GROUP]) + bias

    def finish(h, s):
        sink = sink_ref[h]
        m = jnp.maximum(jnp.max(s, axis=-1, keepdims=True), sink)
        e = jnp.exp(s - m)
        denom = jnp.sum(e, axis=-1, keepdims=True) + jnp.exp(sink - m)
        o_ref[0, :, h * HEAD_DIM:(h + 1) * HEAD_DIM] = _dot(e.astype(BF16), vs[h // SWA_GROUP]) / denom

    _skewed(SWA_HEADS, logits, finish)


def _swa(p3, sinks, tq=256):
    b, s, _ = p3.shape
    return pl.pallas_call(
        functools.partial(_swa_kernel, tq=tq),
        grid=(b, s // tq),
        in_specs=[pl.BlockSpec(memory_space=pltpu.SMEM),
                  pl.BlockSpec((1, tq, 512), lambda bi, i: (bi, i, COL_SWA_Q // 512)),
                  pl.BlockSpec((1, s, 256), lambda bi, i: (bi, 0, COL_SWA_KV // 256))],
        out_specs=pl.BlockSpec((1, tq, 512), lambda bi, i: (bi, i, 0)),
        out_shape=jax.ShapeDtypeStruct((b, s, 512), F32),
        compiler_params=_cparams(("parallel", "parallel")),
        name="swa",
    )(sinks, p3, p3)


def _fox_c_kernel(f_ref, bias_ref, c_ref, *, blk):
    s = f_ref.shape[1]
    r = lax.broadcasted_iota(jnp.int32, (blk, blk), 0)
    c = lax.broadcasted_iota(jnp.int32, (blk, blk), 1)
    tri = jnp.where(c <= r, 1.0, 0.0).astype(F32)
    carry = jnp.zeros((1, 128), F32)
    for j in range(s // blk):
        x = f_ref[0, j * blk:(j + 1) * blk, :] + bias_ref[...]
        log_f = jnp.minimum(x, 0.0) - jnp.log1p(jnp.exp(-jnp.abs(x)))
        cs = _dot(tri, log_f, HIGHEST) + carry
        carry = cs[blk - 1:blk, :]
        for h in range(FOX_HEADS):
            c_ref[0, h, j * blk:(j + 1) * blk, :] = jnp.broadcast_to(cs[:, h:h + 1], (blk, 128))


def _fox_c(p3, f_bias):
    b, s, _ = p3.shape
    bias_row = jnp.pad(f_bias, (0, 128 - FOX_HEADS)).reshape(1, 128)
    return pl.pallas_call(
        functools.partial(_fox_c_kernel, blk=256),
        grid=(b,),
        in_specs=[pl.BlockSpec((1, s, 128), lambda bi: (bi, 0, COL_MISC // 128)),
                  pl.BlockSpec((1, 128), lambda bi: (0, 0))],
        out_specs=pl.BlockSpec((1, FOX_HEADS, s, 128), lambda bi: (bi, 0, 0, 0)),
        out_shape=jax.ShapeDtypeStruct((b, FOX_HEADS, s, 128), F32),
        compiler_params=_cparams(("parallel",)),
        name="fox_cumsum",
    )(p3, bias_row)


def _fox_kernel(q_ref, k_ref, vt_ref, c_ref, o_ref, lg_ref, acc_ref, *, tq):
    i = pl.program_id(1)
    tk = tq
    key_i = lax.broadcasted_iota(jnp.int32, (tk, tq), 0)
    qry_i = lax.broadcasted_iota(jnp.int32, (tk, tq), 1)
    diag_bias = jnp.where(key_i <= qry_i, 0.0, NEG_BIG)
    q = [(q_ref[0, :, h * HEAD_DIM:(h + 1) * HEAD_DIM] * ATTN_SCALE).astype(BF16) for h in range(FOX_HEADS)]

    def logits_block(j, maxes, masked):
        ks = pl.multiple_of(j * tk, tk)
        dots = [_nt(k_ref[0, pl.ds(ks, tk), h * HEAD_DIM:(h + 1) * HEAD_DIM].astype(BF16), q[h])
                for h in range(FOX_HEADS)]
        new_maxes = []
        for h in range(FOX_HEADS):
            c_k = c_ref[0, h, pl.ds(ks, tk), :]
            s = dots[h] - jnp.concatenate([c_k] * (tq // 128), axis=1)
            if masked:
                s = s + diag_bias
            lg_ref[h, pl.ds(ks, tk), :] = s
            new_maxes.append(jnp.maximum(maxes[h], jnp.max(s, axis=0, keepdims=True)))
        return tuple(new_maxes)

    maxes = tuple(jnp.full((1, tq), NEG_BIG, F32) for _ in range(FOX_HEADS))
    maxes = lax.fori_loop(0, i, lambda j, mx: logits_block(j, mx, False), maxes)
    maxes = logits_block(i, maxes, True)
    acc_ref[...] = jnp.zeros_like(acc_ref)

    def attend_block(j, sums):
        ks = pl.multiple_of(j * tk, tk)
        new_sums = []
        for h in range(FOX_HEADS):
            p = jnp.exp(lg_ref[h, pl.ds(ks, tk), :] - maxes[h])
            v_t = vt_ref[0, h * HEAD_DIM:(h + 1) * HEAD_DIM, pl.ds(ks, tk)].astype(BF16)
            acc_ref[h] += _dot(v_t, p.astype(BF16))
            new_sums.append(sums[h] + jnp.sum(p, axis=0, keepdims=True))
        return tuple(new_sums)

    sums = lax.fori_loop(0, i + 1, attend_block, tuple(jnp.zeros((1, tq), F32) for _ in range(FOX_HEADS)))
    out_t = jnp.concatenate([acc_ref[h] / sums[h] for h in range(FOX_HEADS)], axis=0)
    o_ref[0] = out_t.T


def _fox(p3, v_t, c, tq=256):
    b, s, _ = p3.shape
    w = FOX_HEADS * HEAD_DIM
    return pl.pallas_call(
        functools.partial(_fox_kernel, tq=tq),
        grid=(b, s // tq),
        in_specs=[pl.BlockSpec((1, tq, w), lambda bi, i: (bi, i, COL_FOX_Q // w)),
                  pl.BlockSpec((1, s, w), lambda bi, i: (bi, 0, COL_FOX_KV // w)),
                  pl.BlockSpec((1, w, s), lambda bi, i: (bi, 0, 0)),
                  pl.BlockSpec((1, FOX_HEADS, s, 128), lambda bi, i: (bi, 0, 0, 0))],
        out_specs=pl.BlockSpec((1, tq, w), lambda bi, i: (bi, i, 0)),
        out_shape=jax.ShapeDtypeStruct((b, s, w), F32),
        scratch_shapes=[pltpu.VMEM((FOX_HEADS, s, tq), F32), pltpu.VMEM((FOX_HEADS, HEAD_DIM, tq), F32)],
        compiler_params=_cparams(("parallel", "parallel")),
        name="fox",
    )(p3, p3, v_t, c)


def _key_to_score(key):
    return pltpu.bitcast(key ^ ((key >> 31) & 0x7FFFFFFF), F32)


def _dsa_kernel(qq_ref, misc_ref, kv_ref, vt_ref, o_ref, sc_ref, lg_ref, acc_ref, *, tq, kb, topk):
    i = pl.program_id(1)
    n_kb = lax.div((i + 1) * tq + kb - 1, kb)
    lane8 = lax.broadcasted_iota(jnp.int32, (8, 128), 1)
    row8 = lax.broadcasted_iota(jnp.int32, (8, 128), 0)
    pick = jnp.where((lane8 == row8 + 4) & (row8 < IDX_HEADS), 1.0, 0.0).astype(F32)
    w_t = _nt(pick, misc_ref[0], HIGHEST) * (IDX_W_SCALE * IDX_SCALE)
    q_chunk = (i * tq + lax.broadcasted_iota(jnp.int32, (1, tq), 1)) >> CHUNK_SHIFT
    qw = DSA_HEADS * HEAD_DIM
    q_idx = [qq_ref[0, :, qw + h * IDX_DIM:qw + (h + 1) * IDX_DIM].astype(BF16) for h in range(IDX_HEADS)]

    def score_block(j, carry):
        ks = pl.multiple_of(j * kb, kb)
        k_idx = kv_ref[0, pl.ds(ks, kb), 2 * HEAD_DIM:2 * HEAD_DIM + IDX_DIM].astype(BF16)
        dots = [_nt(k_idx, q_idx[h]) for h in range(IDX_HEADS)]
        score = jnp.maximum(dots[0], 0.0) * w_t[0:1, :]
        for h in range(1, IDX_HEADS):
            score = score + jnp.maximum(dots[h], 0.0) * w_t[h:h + 1, :]
        k_chunk = (ks + lax.broadcasted_iota(jnp.int32, (kb, tq), 0)) >> CHUNK_SHIFT
        sc_ref[pl.ds(ks, kb), :] = jnp.where(k_chunk <= q_chunk, score, -jnp.inf)
        return carry

    lax.fori_loop(0, n_kb, score_block, 0)

    def count(pred):
        def body(j, acc):
            blk = sc_ref[pl.ds(pl.multiple_of(j * kb, kb), kb), :]
            hit = jnp.where(pred(blk), 1, 0).astype(jnp.int32)
            return acc + jnp.sum(hit.reshape(kb // 8, 8, tq), axis=0)
        acc = lax.fori_loop(0, n_kb, body, jnp.zeros((8, tq), jnp.int32))
        return jnp.sum(acc.astype(F32), axis=0, keepdims=True)

    zero = jnp.zeros((1, tq), F32)
    thr_key = jnp.where(count(lambda blk: blk >= zero) >= topk, 0, INT_MIN).astype(jnp.int32)

    def bit_step(bi, thr_key):
        cand = thr_key | lax.shift_left(jnp.int32(1), 30 - bi)
        cand_score = _key_to_score(cand)
        return jnp.where(count(lambda blk: blk >= cand_score) >= topk, cand, thr_key)

    thr_key = lax.fori_loop(0, 31, bit_step, thr_key)
    thr = jnp.where(thr_key == INT_MIN, -jnp.inf, _key_to_score(thr_key))
    need = topk - count(lambda blk: blk > thr)

    r = lax.broadcasted_iota(jnp.int32, (kb, kb), 0)
    c = lax.broadcasted_iota(jnp.int32, (kb, kb), 1)
    earlier = jnp.where(c < r, 1.0, 0.0).astype(BF16)
    q_att = [(qq_ref[0, :, h * HEAD_DIM:(h + 1) * HEAD_DIM] * ATTN_SCALE).astype(BF16)
             for h in range(DSA_HEADS)]

    def logits_block(j, carry):
        ties_before, maxes = carry
        ks = pl.multiple_of(j * kb, kb)
        k = kv_ref[0, pl.ds(ks, kb), 0:HEAD_DIM].astype(BF16)
        dots = [_nt(k, q_att[h]) for h in range(DSA_HEADS)]
        score = sc_ref[pl.ds(ks, kb), :]
        k_chunk = (ks + lax.broadcasted_iota(jnp.int32, (kb, tq), 0)) >> CHUNK_SHIFT
        tie = jnp.where(score == thr, jnp.where(k_chunk <= q_chunk, 1.0, 0.0), 0.0)
        rank = _dot(earlier, tie.astype(BF16)) + ties_before
        tie_taken = jnp.where(rank < need, tie, 0.0)
        bias = jnp.where(score > thr, 0.0, jnp.where(tie_taken > 0.5, 0.0, NEG_BIG))
        new_maxes = []
        for h in range(DSA_HEADS):
            s = dots[h] + bias
            lg_ref[h, pl.ds(ks, kb), :] = s
            new_maxes.append(jnp.maximum(maxes[h], jnp.max(s, axis=0, keepdims=True)))
        return ties_before + jnp.sum(tie, axis=0, keepdims=True), tuple(new_maxes)

    init = (jnp.zeros((1, tq), F32), tuple(jnp.full((1, tq), NEG_BIG, F32) for _ in range(DSA_HEADS)))
    _, maxes = lax.fori_loop(0, n_kb, logits_block, init)

    acc_ref[...] = jnp.zeros_like(acc_ref)

    def attend_block(j, sums):
        ks = pl.multiple_of(j * kb, kb)
        v_t = vt_ref[0, :, pl.ds(ks, kb)].astype(BF16)
        new_sums = []
        for h in range(DSA_HEADS):
            p = jnp.exp(lg_ref[h, pl.ds(ks, kb), :] - maxes[h])
            acc_ref[h] += _dot(v_t, p.astype(BF16))
            new_sums.append(sums[h] + jnp.sum(p, axis=0, keepdims=True))
        return tuple(new_sums)

    sums = lax.fori_loop(0, n_kb, attend_block, tuple(jnp.zeros((1, tq), F32) for _ in range(DSA_HEADS)))
    out_t = jnp.concatenate([acc_ref[h] / sums[h] for h in range(DSA_HEADS)], axis=0)
    o_ref[0] = out_t.T


def _dsa(p3, v_t, tq=256, kb=256):
    b, s, _ = p3.shape
    topk = min(DSA_TOPK_MAX, s // 4)
    return pl.pallas_call(
        functools.partial(_dsa_kernel, tq=tq, kb=kb, topk=topk),
        grid=(b, s // tq),
        in_specs=[pl.BlockSpec((1, tq, 512), lambda bi, i: (bi, i, COL_DSA_QQ // 512)),
                  pl.BlockSpec((1, tq, 128), lambda bi, i: (bi, i, COL_MISC // 128)),
                  pl.BlockSpec((1, s, 256), lambda bi, i: (bi, 0, COL_DSA_KV // 256)),
                  pl.BlockSpec((1, HEAD_DIM, s), lambda bi, i: (bi, 0, 0))],
        out_specs=pl.BlockSpec((1, tq, 256), lambda bi, i: (bi, i, 0)),
        out_shape=jax.ShapeDtypeStruct((b, s, 256), F32),
        scratch_shapes=[pltpu.VMEM((s, tq), F32), pltpu.VMEM((DSA_HEADS, s, tq), F32),
                        pltpu.VMEM((DSA_HEADS, HEAD_DIM, tq), F32)],
        compiler_params=_cparams(("parallel", "parallel")),
        name="dsa",
    )(p3, p3, p3, v_t)


def _rwkv_pre_kernel(p_ref, mu_ref, w0_ref, wup_ref, a0_ref, aup_ref, gup_ref, kk_ref, ka_ref, rk_ref,
                     r_o, lw_o, k_o, v_o, kn_o, b_o, g_o, bonus_o, last_ref, *, tt):
    t = pl.program_id(1)

    @pl.when(t == 0)
    def _():
        last_ref[...] = jnp.zeros_like(last_ref)

    p = p_ref[0]
    row = lax.broadcasted_iota(jnp.int32, p.shape, 0)
    p_prev = jnp.where(row == 0, last_ref[...], pltpu.roll(p, 1, axis=0))
    last_ref[...] = p[tt - 1:tt, :]
    ps = p + mu_ref[...] * (p_prev - p)
    w = RWKV_W
    r, k, v = ps[:, 0:w], ps[:, w:2 * w], ps[:, 2 * w:3 * w]
    w_lo, a_lo, g_lo = ps[:, 3 * w:3 * w + 64], ps[:, 3 * w + 64:3 * w + 128], ps[:, 3 * w + 128:3 * w + 256]
    ww = w0_ref[...] + _dot(jnp.tanh(w_lo), wup_ref[...], HIGHEST)
    softplus_neg = jnp.maximum(-ww, 0.0) + jnp.log1p(jnp.exp(-jnp.abs(ww)))
    log_w = -jnp.exp(-softplus_neg - 0.5)
    a = _sigmoid(a0_ref[...] + _dot(a_lo, aup_ref[...], HIGHEST))
    g = _dot(_sigmoid(g_lo), gup_ref[...], HIGHEST)
    head_sum = _head_block_ones(w, 1.0)
    kn = k * kk_ref[...]
    kn = kn * lax.rsqrt(_dot(kn * kn, head_sum, HIGHEST) + 1e-12)
    k2 = k * (1.0 + (a - 1.0) * ka_ref[...])
    bonus = _dot(r * k2 * rk_ref[...], head_sum, HIGHEST) * v
    r_o[0] = r
    lw_o[0] = log_w
    k_o[0] = k2
    v_o[0] = v
    kn_o[0] = kn
    b_o[0] = kn * a
    g_o[0] = g
    bonus_o[0] = bonus


def _rwkv_pre(p3, mu, w0, w_up, a0, a_up, g_up, k_k, k_a, r_k, tt=256):
    b, s, _ = p3.shape
    w = RWKV_W
    row = lambda x: x.reshape(1, -1)
    full = lambda shape: pl.BlockSpec(shape, lambda bi, t: (0,) * len(shape))
    out = jax.ShapeDtypeStruct((b, s, w), F32)
    return pl.pallas_call(
        functools.partial(_rwkv_pre_kernel, tt=tt),
        grid=(b, s // tt),
        in_specs=[pl.BlockSpec((1, tt, 1024), lambda bi, t: (bi, t, COL_RWKV // 1024)),
                  full((1, 1024)), full((1, w)), full((64, w)), full((1, w)), full((64, w)),
                  full((128, w)), full((1, w)), full((1, w)), full((1, w))],
        out_specs=[pl.BlockSpec((1, tt, w), lambda bi, t: (bi, t, 0))] * 8,
        out_shape=[out] * 8,
        scratch_shapes=[pltpu.VMEM((1, 1024), F32)],
        compiler_params=_cparams(("parallel", "arbitrary")),
        name="rwkv_pre",
    )(p3, row(mu), row(w0), w_up, row(a0), a_up, g_up, row(k_k), row(k_a), row(r_k))


def _bdot(a, b):
    return _dot(a.astype(BF16), b.astype(BF16))


def _bnt(a, b):
    return _nt(a.astype(BF16), b.astype(BF16))


def _btn(a, b):
    return _tn(a.astype(BF16), b.astype(BF16))


def _rwkv_chunk_kernel(r_ref, lw_ref, k_ref, v_ref, kn_ref, b_ref, y_ref, s_ref, *, cs, nch):
    c = pl.program_id(1)

    @pl.when(c == 0)
    def _():
        s_ref[...] = jnp.zeros_like(s_ref)

    rows = cs * nch
    big_row = lax.broadcasted_iota(jnp.int32, (rows, rows), 0)
    big_col = lax.broadcasted_iota(jnp.int32, (rows, rows), 1)
    same_chunk = (big_row >> CHUNK_SHIFT) == (big_col >> CHUNK_SHIFT)
    tri = jnp.where(same_chunk, jnp.where(big_col <= big_row, 1.0, 0.0), 0.0).astype(F32)
    row = lax.broadcasted_iota(jnp.int32, (cs, cs), 0)
    col = lax.broadcasted_iota(jnp.int32, (cs, cs), 1)
    incl = col <= row
    strict = col < row
    eye = jnp.where(col == row, 1.0, 0.0).astype(F32)
    lw = lw_ref[0]
    cum = _dot(tri, lw, HIGHEST)
    e_incl = jnp.exp(cum)
    e_neg = jnp.exp(-cum)
    abar_all = -kn_ref[0] * jnp.exp(cum - lw)
    rbar_all = r_ref[0] * e_incl
    bt_all = b_ref[0] * e_neg
    kt_all = k_ref[0] * e_neg
    v_all = v_ref[0]
    n_double = max(cs.bit_length() - 2, 0)
    probs = [(j, h) for j in range(nch) for h in range(RWKV_HEADS)]
    cut = lambda t, jh: t[jh[0] * cs:(jh[0] + 1) * cs, jh[1] * HEAD_DIM:(jh[1] + 1) * HEAD_DIM]
    each = lambda fn: {jh: fn(jh) for jh in probs}
    abar, rbar = each(lambda jh: cut(abar_all, jh)), each(lambda jh: cut(rbar_all, jh))
    bt, kt, v = each(lambda jh: cut(bt_all, jh)), each(lambda jh: cut(kt_all, jh)), each(lambda jh: cut(v_all, jh))
    p_last = each(lambda jh: e_incl[(jh[0] + 1) * cs - 1:(jh[0] + 1) * cs, jh[1] * HEAD_DIM:(jh[1] + 1) * HEAD_DIM])
    ar = each(lambda jh: jnp.concatenate([abar[jh], rbar[jh]], axis=0))
    ar_b = each(lambda jh: _bnt(ar[jh], bt[jh]))
    ar_k = each(lambda jh: _bnt(ar[jh], kt[jh]))
    a_ab = each(lambda jh: jnp.where(strict, ar_b[jh][:cs], 0.0))
    a_ak = each(lambda jh: jnp.where(strict, ar_k[jh][:cs], 0.0))
    a_rb = each(lambda jh: jnp.where(incl, ar_b[jh][cs:], 0.0))
    a_rk = each(lambda jh: jnp.where(incl, ar_k[jh][cs:], 0.0))
    ak_v = each(lambda jh: _bdot(a_ak[jh], v[jh]))
    inv = each(lambda jh: eye + a_ab[jh])
    power = a_ab
    for _ in range(n_double):
        power = each(lambda jh: _bdot(power[jh], power[jh]))
        inv = each(lambda jh: inv[jh] + _bdot(inv[jh], power[jh]))
    a_hat = each(lambda jh: _bdot(inv[jh], abar[jh]))
    u_hat = each(lambda jh: _bdot(inv[jh], ak_v[jh]))
    r_hat = each(lambda jh: rbar[jh] + _bdot(a_rb[jh], a_hat[jh]))
    y_hat = each(lambda jh: _bdot(a_rb[jh], u_hat[jh]) + _bdot(a_rk[jh], v[jh]))
    g_mat = each(lambda jh: eye * p_last[jh] + _btn(a_hat[jh], bt[jh] * p_last[jh]))
    h_mat = each(lambda jh: _btn(jnp.concatenate([u_hat[jh], v[jh]], axis=0),
                                 jnp.concatenate([bt[jh], kt[jh]], axis=0) * p_last[jh]))
    states = [s_ref[h] for h in range(RWKV_HEADS)]
    for j in range(nch):
        for h in range(RWKV_HEADS):
            y_ref[0, j * cs:(j + 1) * cs, h * HEAD_DIM:(h + 1) * HEAD_DIM] = (
                _bnt(r_hat[j, h], states[h]) + y_hat[j, h])
        states = [_dot(states[h], g_mat[j, h], HIGHEST) + h_mat[j, h] for h in range(RWKV_HEADS)]
    for h in range(RWKV_HEADS):
        s_ref[h] = states[h]


def _rwkv_chunk(r, lw, k, v, kn, b_, cs=CHUNK, nch=4):
    b, s, w = r.shape
    spec = pl.BlockSpec((1, cs * nch, w), lambda bi, c: (bi, c, 0))
    return pl.pallas_call(
        functools.partial(_rwkv_chunk_kernel, cs=cs, nch=nch),
        grid=(b, s // (cs * nch)),
        in_specs=[spec] * 6,
        out_specs=spec,
        out_shape=jax.ShapeDtypeStruct((b, s, w), F32),
        scratch_shapes=[pltpu.VMEM((RWKV_HEADS, HEAD_DIM, HEAD_DIM), F32)],
        compiler_params=_cparams(("parallel", "arbitrary")),
        name="rwkv_chunk",
    )(r, lw, k, v, kn, b_)


def _merge_kernel(x_ref, xb_ref, swa_ref, y_ref, g_ref, bonus_ref, fox_ref, dsa_ref, gng_ref, gnb_ref,
                  gbias_ref, wg_ref, wb_ref, wo_ref, lng_ref, lnb_ref, o_ref, ob_ref):
    y = y_ref[...]
    xb = xb_ref[...]
    head_mean = _head_block_ones(RWKV_W, 1.0 / HEAD_DIM)
    yc = y - _dot(y, head_mean, HIGHEST)
    yv = _dot(yc * yc, head_mean, HIGHEST)
    o_rwkv = (yc * lax.rsqrt(yv + RWKV_GN_EPS) * gng_ref[...] + gnb_ref[...] + bonus_ref[...]) * g_ref[...]
    branches = (swa_ref[...], o_rwkv, fox_ref[...], dsa_ref[...])
    merged = jnp.zeros(o_ref.shape, F32)
    off = 0
    for i, o in enumerate(branches):
        width = MIX_WIDTHS[i]
        proj = _dot(o.astype(BF16), wb_ref[off:off + width, :])
        gate = _sigmoid(_dot(xb, wg_ref[:, i * D_MODEL:(i + 1) * D_MODEL]) + gbias_ref[i:i + 1, :])
        merged = merged + gate * proj
        off += width
    y_out = _dot(merged.astype(BF16), wo_ref[...])
    xn = _layer_norm(DN_ALPHA * x_ref[...] + y_out, lng_ref[...], lnb_ref[...])
    o_ref[...] = xn
    ob_ref[...] = xn.astype(BF16)


def _merge(xf, xb, o_swa, y, g, bonus, o_fox, o_dsa, gn_g, gn_b, gate_bias, w_gate, w_branch, w_out, ln_g, ln_b,
           tm=512):
    n, d = xf.shape
    row = lambda x: x.reshape(1, -1)
    tok = lambda wdt: pl.BlockSpec((tm, wdt), lambda i: (i, 0))
    full = lambda shape: pl.BlockSpec(shape, lambda i: (0,) * len(shape))
    return pl.pallas_call(
        _merge_kernel,
        grid=(n // tm,),
        in_specs=[tok(d), tok(d), tok(512), tok(256), tok(256), tok(256), tok(256), tok(256),
                  full((1, 256)), full((1, 256)), full((N_BRANCHES, d)), full((d, N_BRANCHES * d)),
                  full((sum(MIX_WIDTHS), d)), full((d, d)), full((1, d)), full((1, d))],
        out_specs=[tok(d), tok(d)],
        out_shape=[jax.ShapeDtypeStruct((n, d), F32), jax.ShapeDtypeStruct((n, d), BF16)],
        compiler_params=_cparams(("parallel",)),
        name="merge",
    )(xf, xb, o_swa, y, g, bonus, o_fox, o_dsa, row(gn_g), row(gn_b), gate_bias, w_gate, w_branch, w_out,
      row(ln_g), row(ln_b))


def _silu(x):
    return x * _sigmoid(x)


def _ffn_kernel(xb_ref, x_ref, w1_ref, w3_ref, w2_ref, lng_ref, lnb_ref, o_ref, ob_ref, acc_ref):
    f = pl.program_id(1)

    @pl.when(f == 0)
    def _():
        acc_ref[...] = jnp.zeros_like(acc_ref)

    xb = xb_ref[...]
    hidden = _silu(_dot(xb, w1_ref[...])) * _dot(xb, w3_ref[...])
    acc_ref[...] += _dot(hidden.astype(BF16), w2_ref[...])

    @pl.when(f == pl.num_programs(1) - 1)
    def _():
        xn = _layer_norm(DN_ALPHA * x_ref[...] + acc_ref[...], lng_ref[...], lnb_ref[...])
        o_ref[...] = xn
        ob_ref[...] = xn.astype(BF16)


def _ffn(xb, xf, w1, w3, w2, ln_g, ln_b, tm=512, tf=1408):
    n, d = xf.shape
    ff = w1.shape[1]
    row = lambda x: x.reshape(1, -1)
    return pl.pallas_call(
        _ffn_kernel,
        grid=(n // tm, ff // tf),
        in_specs=[pl.BlockSpec((tm, d), lambda i, f: (i, 0)),
                  pl.BlockSpec((tm, d), lambda i, f: (i, 0)),
                  pl.BlockSpec((d, tf), lambda i, f: (0, f)),
                  pl.BlockSpec((d, tf), lambda i, f: (0, f)),
                  pl.BlockSpec((tf, d), lambda i, f: (f, 0)),
                  pl.BlockSpec((1, d), lambda i, f: (0, 0)),
                  pl.BlockSpec((1, d), lambda i, f: (0, 0))],
        out_specs=[pl.BlockSpec((tm, d), lambda i, f: (i, 0))] * 2,
        out_shape=[jax.ShapeDtypeStruct((n, d), F32), jax.ShapeDtypeStruct((n, d), BF16)],
        scratch_shapes=[pltpu.VMEM((tm, d), F32)],
        compiler_params=_cparams(("parallel", "arbitrary")),
        name="ffn",
    )(xb, xf, w1, w3, w2, row(ln_g), row(ln_b))


def _moe_kernel(xb_ref, x_ref, rw_ref, rb_ref, w1_ref, w3_ref, w2_ref, lng_ref, lnb_ref, o_ref,
                comb_ref, rank_ref, cnt_ref, *, sb):
    e = pl.program_id(1)
    tm = x_ref.shape[0]
    lane = lax.broadcasted_iota(jnp.int32, (tm, 128), 1).astype(F32)

    @pl.when(e == 0)
    def _():
        o_ref[...] = jnp.zeros_like(o_ref)
        logits = _dot(x_ref[...], rw_ref[...], HIGHEST) + rb_ref[...]
        logits = jnp.where(lane < N_EXPERTS, logits, NEG_BIG)
        m1 = jnp.max(logits, axis=-1, keepdims=True)
        i1 = jnp.min(jnp.where(logits == m1, lane, 128.0), axis=-1, keepdims=True)
        rest = jnp.where(lane == i1, NEG_BIG, logits)
        m2 = jnp.max(rest, axis=-1, keepdims=True)
        i2 = jnp.min(jnp.where(rest == m2, lane, 128.0), axis=-1, keepdims=True)
        e2 = jnp.exp(m2 - m1)
        comb_ref[...] = jnp.where(lane == i1, 1.0 / (1.0 + e2), 0.0) + jnp.where(lane == i2, e2 / (1.0 + e2), 0.0)
        chosen = jnp.where(lane == i1, 1.0, 0.0) + jnp.where(lane == i2, 1.0, 0.0)
        r = lax.broadcasted_iota(jnp.int32, (tm, tm), 0)
        c = lax.broadcasted_iota(jnp.int32, (tm, tm), 1)
        earlier = jnp.where(c < r, 1.0, 0.0).astype(BF16)
        rank = jnp.where(chosen > 0.5, _dot(earlier, chosen.astype(BF16)), -1.0)
        lane8 = lax.broadcasted_iota(jnp.int32, (8, 128), 1)
        row8 = lax.broadcasted_iota(jnp.int32, (8, 128), 0)
        rank_ref[...] = _nt(jnp.where(lane8 == row8, 1.0, 0.0).astype(F32), rank, HIGHEST)
        cnt_ref[...] = jnp.sum(chosen, axis=0, keepdims=True)

    e_f = e.astype(F32)
    n_tok = jnp.sum(jnp.where(lane[0:1, :] == e_f, cnt_ref[...], 0.0)).astype(jnp.int32)
    n_blocks = lax.div(n_tok + sb - 1, sb)
    rank_row = rank_ref[pl.ds(e, 1), :]
    weight = jnp.sum(jnp.where(lane == e_f, comb_ref[...], 0.0), axis=-1, keepdims=True)
    slot0 = lax.broadcasted_iota(jnp.int32, (sb, tm), 0).astype(F32)

    def block(s, carry):
        pick = jnp.where(rank_row == slot0 + (s * sb).astype(F32), 1.0, 0.0).astype(BF16)
        xs = _dot(pick, xb_ref[...]).astype(BF16)
        hidden = _silu(_dot(xs, w1_ref[0])) * _dot(xs, w3_ref[0])
        out = _dot(hidden.astype(BF16), w2_ref[0]).astype(BF16)
        o_ref[...] += _tn(pick, out) * weight
        return carry

    lax.fori_loop(0, n_blocks, block, 0)

    @pl.when(e == pl.num_programs(1) - 1)
    def _():
        o_ref[...] = _layer_norm(DN_ALPHA * x_ref[...] + o_ref[...], lng_ref[...], lnb_ref[...])


def _moe(xb, xf, router_w, router_b, w1, w3, w2, ln_g, ln_b, tm=1024, sb=288):
    n, d = xf.shape
    ne, _, fe = w1.shape
    tm = min(tm, n)
    rw = jnp.pad(router_w, ((0, 0), (0, 128 - ne)))
    rb = jnp.pad(router_b, (0, 128 - ne)).reshape(1, 128)
    row = lambda x: x.reshape(1, -1)
    return pl.pallas_call(
        functools.partial(_moe_kernel, sb=sb),
        grid=(n // tm, ne),
        in_specs=[pl.BlockSpec((tm, d), lambda i, e: (i, 0)),
                  pl.BlockSpec((tm, d), lambda i, e: (i, 0)),
                  pl.BlockSpec((d, 128), lambda i, e: (0, 0)),
                  pl.BlockSpec((1, 128), lambda i, e: (0, 0)),
                  pl.BlockSpec((1, d, fe), lambda i, e: (e, 0, 0)),
                  pl.BlockSpec((1, d, fe), lambda i, e: (e, 0, 0)),
                  pl.BlockSpec((1, fe, d), lambda i, e: (e, 0, 0)),
                  pl.BlockSpec((1, d), lambda i, e: (0, 0)),
                  pl.BlockSpec((1, d), lambda i, e: (0, 0))],
        out_specs=pl.BlockSpec((tm, d), lambda i, e: (i, 0)),
        out_shape=jax.ShapeDtypeStruct((n, d), F32),
        scratch_shapes=[pltpu.VMEM((tm, 128), F32), pltpu.VMEM((8, tm), F32), pltpu.VMEM((1, 128), F32)],
        compiler_params=pltpu.CompilerParams(dimension_semantics=("parallel", "arbitrary"),
                                             vmem_limit_bytes=MOE_VMEM_LIMIT),
        name="moe",
    )(xb, xf, rw, rb, w1, w3, w2, row(ln_g), row(ln_b))


def _mixer_layer(xf, xb, bsz, seq, w_in, sinks, mu, w0, w_up, a0, a_up, g_up, k_k, k_a, r_k, gn_g, gn_b,
                 f_bias, gate_bias, w_branch, w_out, ln_g, ln_b):
    w_mix, w_gate = _layout_w_in(w_in)
    p2 = _matmul(xb, w_mix.astype(BF16), tm=1024, tn=512)
    p3 = p2.reshape(bsz, seq, P_WIDTH)
    o_swa = _swa(p3, sinks)
    fox_w = FOX_HEADS * HEAD_DIM
    fox_v_t = jnp.swapaxes(p3[:, :, COL_FOX_KV + fox_w:COL_FOX_KV + 2 * fox_w], 1, 2)
    o_fox = _fox(p3, fox_v_t, _fox_c(p3, f_bias))
    v_t = jnp.swapaxes(p3[:, :, COL_DSA_KV + HEAD_DIM:COL_DSA_KV + 2 * HEAD_DIM], 1, 2)
    o_dsa = _dsa(p3, v_t)
    r, lw, k2, v, kn, b_, g, bonus = _rwkv_pre(p3, mu, w0, w_up, a0, a_up, g_up, k_k, k_a, r_k)
    y = _rwkv_chunk(r, lw, k2, v, kn, b_)
    n = bsz * seq
    flat = lambda t: t.reshape(n, t.shape[-1])
    return _merge(xf, xb, flat(o_swa), flat(y), flat(g), flat(bonus), flat(o_fox), flat(o_dsa), gn_g, gn_b,
                  gate_bias, w_gate.astype(BF16), w_branch.astype(BF16), w_out.astype(BF16), ln_g, ln_b)


def kernel(x, w_in, swa_sinks, rwkv_mu, rwkv_w0, rwkv_w_up, rwkv_a0, rwkv_a_up, rwkv_g_up, rwkv_k_k, rwkv_k_a,
           rwkv_r_k, rwkv_gn_g, rwkv_gn_b, fox_f_bias, gate_bias, w_branch, w_out, ln_g, ln_b, ffn_w1, ffn_w3,
           ffn_w2, router_w, router_b, exp_w1, exp_w3, exp_w2):
    bsz, seq, d = x.shape
    xf = x.reshape(bsz * seq, d)
    xb = xf.astype(BF16)
    for layer in range(DEPTH):
        xf, xb = _mixer_layer(xf, xb, bsz, seq, w_in[layer], swa_sinks[layer], rwkv_mu[layer], rwkv_w0[layer],
                              rwkv_w_up[layer], rwkv_a0[layer], rwkv_a_up[layer], rwkv_g_up[layer],
                              rwkv_k_k[layer], rwkv_k_a[layer], rwkv_r_k[layer], rwkv_gn_g[layer],
                              rwkv_gn_b[layer], fox_f_bias[layer], gate_bias[layer], w_branch[layer],
                              w_out[layer], ln_g[layer, 0], ln_b[layer, 0])
        j = layer // 2
        if layer % 2 == 0:
            xf, xb = _ffn(xb, xf, ffn_w1[j].astype(BF16), ffn_w3[j].astype(BF16), ffn_w2[j].astype(BF16),
                          ln_g[layer, 1], ln_b[layer, 1])
        else:
            xf = _moe(xb, xf, router_w[j], router_b[j], exp_w1[j].astype(BF16), exp_w3[j].astype(BF16),
                      exp_w2[j].astype(BF16), ln_g[layer, 1], ln_b[layer, 1])
            xb = xf.astype(BF16) if layer + 1 < DEPTH else None
    return xf.reshape(bsz, seq, d)
```

```python
import functools

import jax
import jax.numpy as jnp
from jax import lax
from jax.experimental import pallas as pl
from jax.experimental.pallas import tpu as pltpu

F32 = jnp.float32
BF16 = jnp.bfloat16
HIGHEST = lax.Precision.HIGHEST

D_MODEL = 1024
DEPTH = 2
CHUNK = 64
HEAD_DIM = 64
CHUNK_SHIFT = 6
HEAD_SHIFT = 6
SWA_HEADS = 8
SWA_KV_HEADS = 2
SWA_GROUP = SWA_HEADS // SWA_KV_HEADS
SWA_WINDOW = 128
SWA_WIN_CHUNKS = SWA_WINDOW // CHUNK
RWKV_HEADS = 4
RWKV_W = RWKV_HEADS * HEAD_DIM
RWKV_GN_EPS = 64e-5
FOX_HEADS = 4
DSA_HEADS = 4
IDX_HEADS = 4
IDX_DIM = 64
DSA_TOPK_MAX = 256
D_FF = 2816
N_EXPERTS = 8
D_FF_EXPERT = 1408
N_BRANCHES = 4
DN_ALPHA = (2 * DEPTH) ** 0.25
LN_EPS = 1e-5
ATTN_SCALE = HEAD_DIM ** -0.5
IDX_SCALE = IDX_DIM ** -0.5
IDX_W_SCALE = IDX_HEADS ** -0.5
MIX_WIDTHS = (SWA_HEADS * HEAD_DIM, RWKV_W, FOX_HEADS * HEAD_DIM, DSA_HEADS * HEAD_DIM)

COL_RWKV = 0
COL_SWA_Q = 1024
COL_FOX_KV = 1536
COL_DSA_QQ = 2048
COL_FOX_Q = 2560
COL_SWA_KV = 2816
COL_DSA_KV = 3072
COL_MISC = 3328
P_WIDTH = 3584

NEG_BIG = -1e30
INT_MIN = -(2 ** 31)
VMEM_LIMIT = 48 * 1024 * 1024
MOE_VMEM_LIMIT = 58 * 1024 * 1024


def _layout_w_in(w):
    d = w.shape[0]
    z = lambda n: jnp.zeros((d, n), w.dtype)
    swa, rwkv, fox, dsa, gate = 0, 768, 1792, 2564, 3272
    parts = [
        w[:, rwkv:rwkv + 1024],
        w[:, swa:swa + 512],
        w[:, fox + 256:fox + 768],
        w[:, dsa:dsa + 256], w[:, dsa + 384:dsa + 640],
        w[:, fox:fox + 256],
        w[:, swa + 512:swa + 768],
        w[:, dsa + 256:dsa + 384], w[:, dsa + 640:dsa + 704], z(64),
        w[:, fox + 768:fox + 772], w[:, dsa + 704:dsa + 708], z(120),
        z(128),
    ]
    return jnp.concatenate(parts, axis=1), w[:, gate:gate + N_BRANCHES * D_MODEL]


def _cparams(sem):
    return pltpu.CompilerParams(dimension_semantics=sem, vmem_limit_bytes=VMEM_LIMIT)


def _nt(a, b, precision=None):
    return lax.dot_general(a, b, (((1,), (1,)), ((), ())), precision=precision,
                           preferred_element_type=F32)


def _tn(a, b, precision=None):
    return lax.dot_general(a, b, (((0,), (0,)), ((), ())), precision=precision,
                           preferred_element_type=F32)


def _dot(a, b, precision=None):
    return jnp.dot(a, b, precision=precision, preferred_element_type=F32)


def _sigmoid(x):
    return 1.0 / (1.0 + jnp.exp(-x))


def _layer_norm(z, g, b):
    mu = jnp.mean(z, axis=-1, keepdims=True)
    zc = z - mu
    var = jnp.mean(zc * zc, axis=-1, keepdims=True)
    return zc * lax.rsqrt(var + LN_EPS) * g + b


def _skewed(n, first, second):
    out = []
    staged = first(0)
    for h in range(n):
        nxt = first(h + 1) if h + 1 < n else None
        out.append(second(h, staged))
        staged = nxt
    return out


def _paired_blocks(n, body, carry):
    carry = lax.fori_loop(0, lax.div(n, 2), lambda t, c: body((2 * t, 2 * t + 1), c), carry)
    return lax.cond(lax.rem(n, 2) == 1, lambda c: body((n - 1,), c), lambda c: c, carry)


def _head_block_ones(n, scale):
    r = lax.broadcasted_iota(jnp.int32, (n, n), 0) >> HEAD_SHIFT
    c = lax.broadcasted_iota(jnp.int32, (n, n), 1) >> HEAD_SHIFT
    return jnp.where(r == c, scale, 0.0).astype(F32)


def _mm_kernel(a_ref, b_ref, o_ref):
    o_ref[...] = _dot(a_ref[...], b_ref[...])


def _matmul(a, b, tm, tn):
    m, k = a.shape
    n = b.shape[1]
    tm = min(tm, m)
    return pl.pallas_call(
        _mm_kernel,
        grid=(m // tm, n // tn),
        in_specs=[pl.BlockSpec((tm, k), lambda i, j: (i, 0)),
                  pl.BlockSpec((k, tn), lambda i, j: (0, j))],
        out_specs=pl.BlockSpec((tm, tn), lambda i, j: (i, j)),
        out_shape=jax.ShapeDtypeStruct((m, n), F32),
        compiler_params=_cparams(("parallel", "parallel")),
        name="in_proj",
    )(a, b)


def _swa_kernel(sink_ref, q_ref, kv_ref, vt_ref, o_ref, *, tq):
    i = pl.program_id(1)
    win = tq + SWA_WINDOW
    s0 = pl.multiple_of(jnp.maximum(i * tq - SWA_WINDOW, 0), SWA_WINDOW)
    k_chunk = (s0 + lax.broadcasted_iota(jnp.int32, (win, tq), 0)) >> CHUNK_SHIFT
    q_chunk = (i * tq + lax.broadcasted_iota(jnp.int32, (win, tq), 1)) >> CHUNK_SHIFT
    bias = jnp.where(k_chunk <= q_chunk,
                     jnp.where(k_chunk >= q_chunk - SWA_WIN_CHUNKS, 0.0, NEG_BIG), NEG_BIG)
    ks = [kv_ref[0, pl.ds(s0, win), hk * HEAD_DIM:(hk + 1) * HEAD_DIM].astype(BF16)
          for hk in range(SWA_KV_HEADS)]
    v_ts = [vt_ref[0, hk * HEAD_DIM:(hk + 1) * HEAD_DIM, pl.ds(s0, win)].astype(BF16)
            for hk in range(SWA_KV_HEADS)]

    def logits(h):
        q = (q_ref[0, :, h * HEAD_DIM:(h + 1) * HEAD_DIM] * ATTN_SCALE).astype(BF16)
        return _nt(ks[h // SWA_GROUP], q) + bias

    def finish(h, s):
        sink = sink_ref[h]
        m = jnp.maximum(jnp.max(s, axis=0, keepdims=True), sink)
        e = jnp.exp(s - m)
        denom = jnp.sum(e, axis=0, keepdims=True) + jnp.exp(sink - m)
        return _dot(v_ts[h // SWA_GROUP], e.astype(BF16)) / denom

    o_ref[0] = jnp.concatenate(_skewed(SWA_HEADS, logits, finish), axis=0).T


def _swa(p3, v_t, sinks, tq=256):
    b, s, _ = p3.shape
    kv_w = SWA_KV_HEADS * HEAD_DIM
    return pl.pallas_call(
        functools.partial(_swa_kernel, tq=tq),
        grid=(b, s // tq),
        in_specs=[pl.BlockSpec(memory_space=pltpu.SMEM),
                  pl.BlockSpec((1, tq, 512), lambda bi, i: (bi, i, COL_SWA_Q // 512)),
                  pl.BlockSpec((1, s, kv_w), lambda bi, i: (bi, 0, COL_SWA_KV // kv_w)),
                  pl.BlockSpec((1, kv_w, s), lambda bi, i: (bi, 0, 0))],
        out_specs=pl.BlockSpec((1, tq, 512), lambda bi, i: (bi, i, 0)),
        out_shape=jax.ShapeDtypeStruct((b, s, 512), F32),
        compiler_params=_cparams(("parallel", "parallel")),
        name="swa",
    )(sinks, p3, p3, v_t)


def _fox_c_kernel(f_ref, bias_ref, c_ref, *, blk):
    s = f_ref.shape[1]
    r = lax.broadcasted_iota(jnp.int32, (blk, blk), 0)
    c = lax.broadcasted_iota(jnp.int32, (blk, blk), 1)
    tri = jnp.where(c <= r, 1.0, 0.0).astype(F32)
    carry = jnp.zeros((1, 128), F32)
    for j in range(s // blk):
        x = f_ref[0, j * blk:(j + 1) * blk, :] + bias_ref[...]
        log_f = jnp.minimum(x, 0.0) - jnp.log1p(jnp.exp(-jnp.abs(x)))
        cs = _dot(tri, log_f, HIGHEST) + carry
        carry = cs[blk - 1:blk, :]
        for h in range(FOX_HEADS):
            c_ref[0, h, j * blk:(j + 1) * blk, :] = jnp.broadcast_to(cs[:, h:h + 1], (blk, 128))


def _fox_c(p3, f_bias):
    b, s, _ = p3.shape
    bias_row = jnp.pad(f_bias, (0, 128 - FOX_HEADS)).reshape(1, 128)
    return pl.pallas_call(
        functools.partial(_fox_c_kernel, blk=256),
        grid=(b,),
        in_specs=[pl.BlockSpec((1, s, 128), lambda bi: (bi, 0, COL_MISC // 128)),
                  pl.BlockSpec((1, 128), lambda bi: (0, 0))],
        out_specs=pl.BlockSpec((1, FOX_HEADS, s, 128), lambda bi: (bi, 0, 0, 0)),
        out_shape=jax.ShapeDtypeStruct((b, FOX_HEADS, s, 128), F32),
        compiler_params=_cparams(("parallel",)),
        name="fox_cumsum",
    )(p3, bias_row)


def _fox_kernel(q_ref, k_ref, vt_ref, c_ref, o_ref, lg_ref, acc_ref, *, tq):
    i = pl.program_id(1)
    tk = tq
    key_i = lax.broadcasted_iota(jnp.int32, (tk, tq), 0)
    qry_i = lax.broadcasted_iota(jnp.int32, (tk, tq), 1)
    diag_bias = jnp.where(key_i <= qry_i, 0.0, NEG_BIG)
    q = [(q_ref[0, :, h * HEAD_DIM:(h + 1) * HEAD_DIM] * ATTN_SCALE).astype(BF16) for h in range(FOX_HEADS)]

    def logits_blocks(blocks, maxes, masked=False):
        starts = [pl.multiple_of(j * tk, tk) for j in blocks]
        dots = [[_nt(k_ref[0, pl.ds(ks, tk), h * HEAD_DIM:(h + 1) * HEAD_DIM].astype(BF16), q[h])
                 for h in range(FOX_HEADS)] for ks in starts]
        maxes = list(maxes)
        for b, ks in enumerate(starts):
            for h in range(FOX_HEADS):
                c_k = c_ref[0, h, pl.ds(ks, tk), :]
                s = dots[b][h] - jnp.concatenate([c_k] * (tq // 128), axis=1)
                if masked:
                    s = s + diag_bias
                lg_ref[h, pl.ds(ks, tk), :] = s
                maxes[h] = jnp.maximum(maxes[h], jnp.max(s, axis=0, keepdims=True))
        return tuple(maxes)

    maxes = tuple(jnp.full((1, tq), NEG_BIG, F32) for _ in range(FOX_HEADS))
    maxes = _paired_blocks(i, logits_blocks, maxes)
    maxes = logits_blocks((i,), maxes, masked=True)
    acc_ref[...] = jnp.zeros_like(acc_ref)

    def attend_blocks(blocks, sums):
        starts = [pl.multiple_of(j * tk, tk) for j in blocks]
        sums = list(sums)
        for h in range(FOX_HEADS):
            update = None
            for ks in starts:
                p = jnp.exp(lg_ref[h, pl.ds(ks, tk), :] - maxes[h])
                v_t = vt_ref[0, h * HEAD_DIM:(h + 1) * HEAD_DIM, pl.ds(ks, tk)].astype(BF16)
                pv = _dot(v_t, p.astype(BF16))
                update = pv if update is None else update + pv
                sums[h] = sums[h] + jnp.sum(p, axis=0, keepdims=True)
            acc_ref[h] += update
        return tuple(sums)

    sums = _paired_blocks(i + 1, attend_blocks, tuple(jnp.zeros((1, tq), F32) for _ in range(FOX_HEADS)))
    out_t = jnp.concatenate([acc_ref[h] / sums[h] for h in range(FOX_HEADS)], axis=0)
    o_ref[0] = out_t.T


def _fox(p3, v_t, c, tq=256):
    b, s, _ = p3.shape
    w = FOX_HEADS * HEAD_DIM
    return pl.pallas_call(
        functools.partial(_fox_kernel, tq=tq),
        grid=(b, s // tq),
        in_specs=[pl.BlockSpec((1, tq, w), lambda bi, i: (bi, i, COL_FOX_Q // w)),
                  pl.BlockSpec((1, s, w), lambda bi, i: (bi, 0, COL_FOX_KV // w)),
                  pl.BlockSpec((1, w, s), lambda bi, i: (bi, 0, 0)),
                  pl.BlockSpec((1, FOX_HEADS, s, 128), lambda bi, i: (bi, 0, 0, 0))],
        out_specs=pl.BlockSpec((1, tq, w), lambda bi, i: (bi, i, 0)),
        out_shape=jax.ShapeDtypeStruct((b, s, w), F32),
        scratch_shapes=[pltpu.VMEM((FOX_HEADS, s, tq), F32), pltpu.VMEM((FOX_HEADS, HEAD_DIM, tq), F32)],
        compiler_params=_cparams(("parallel", "parallel")),
        name="fox",
    )(p3, p3, v_t, c)


def _key_to_score(key):
    return pltpu.bitcast(key ^ ((key >> 31) & 0x7FFFFFFF), F32)


def _dsa_kernel(qq_ref, misc_ref, kv_ref, vt_ref, o_ref, sc_ref, lg_ref, acc_ref, *, tq, kb, topk):
    i = pl.program_id(1)
    n_kb = lax.div((i + 1) * tq + kb - 1, kb)
    lane8 = lax.broadcasted_iota(jnp.int32, (8, 128), 1)
    row8 = lax.broadcasted_iota(jnp.int32, (8, 128), 0)
    pick = jnp.where((lane8 == row8 + 4) & (row8 < IDX_HEADS), 1.0, 0.0).astype(F32)
    w_t = _nt(pick, misc_ref[0], HIGHEST) * (IDX_W_SCALE * IDX_SCALE)
    q_chunk = (i * tq + lax.broadcasted_iota(jnp.int32, (1, tq), 1)) >> CHUNK_SHIFT
    qw = DSA_HEADS * HEAD_DIM
    q_idx = [qq_ref[0, :, qw + h * IDX_DIM:qw + (h + 1) * IDX_DIM].astype(BF16) for h in range(IDX_HEADS)]

    def score_blocks(blocks, carry):
        starts = [pl.multiple_of(j * kb, kb) for j in blocks]
        dots = []
        for ks in starts:
            k_idx = kv_ref[0, pl.ds(ks, kb), 2 * HEAD_DIM:2 * HEAD_DIM + IDX_DIM].astype(BF16)
            dots.append([_nt(k_idx, q_idx[h]) for h in range(IDX_HEADS)])
        for b, ks in enumerate(starts):
            score = jnp.maximum(dots[b][0], 0.0) * w_t[0:1, :]
            for h in range(1, IDX_HEADS):
                score = score + jnp.maximum(dots[b][h], 0.0) * w_t[h:h + 1, :]
            k_chunk = (ks + lax.broadcasted_iota(jnp.int32, (kb, tq), 0)) >> CHUNK_SHIFT
            sc_ref[pl.ds(ks, kb), :] = jnp.where(k_chunk <= q_chunk, score, -jnp.inf)
        return carry

    _paired_blocks(n_kb, score_blocks, 0)

    def count(pred):
        def body(j, acc):
            blk = sc_ref[pl.ds(pl.multiple_of(j * kb, kb), kb), :]
            hit = jnp.where(pred(blk), 1, 0).astype(jnp.int32).reshape(kb // 8, 8, tq)
            parts = [hit[g] for g in range(kb // 8)]
            while len(parts) > 1:
                parts = [parts[g] + parts[g + 1] for g in range(0, len(parts), 2)]
            return acc + parts[0]
        acc = lax.fori_loop(0, n_kb, body, jnp.zeros((8, tq), jnp.int32))
        return jnp.sum(acc.astype(F32), axis=0, keepdims=True)

    zero = jnp.zeros((1, tq), F32)
    thr_key = jnp.where(count(lambda blk: blk >= zero) >= topk, 0, INT_MIN).astype(jnp.int32)

    def bit_step(bi, thr_key):
        cand = thr_key | lax.shift_left(jnp.int32(1), 30 - bi)
        cand_score = _key_to_score(cand)
        return jnp.where(count(lambda blk: blk >= cand_score) >= topk, cand, thr_key)

    thr_key = lax.fori_loop(0, 31, bit_step, thr_key)
    thr = jnp.where(thr_key == INT_MIN, -jnp.inf, _key_to_score(thr_key))
    need = topk - count(lambda blk: blk > thr)

    r = lax.broadcasted_iota(jnp.int32, (kb, kb), 0)
    c = lax.broadcasted_iota(jnp.int32, (kb, kb), 1)
    earlier = jnp.where(c < r, 1.0, 0.0).astype(BF16)
    q_att = [(qq_ref[0, :, h * HEAD_DIM:(h + 1) * HEAD_DIM] * ATTN_SCALE).astype(BF16)
             for h in range(DSA_HEADS)]

    def logits_blocks(blocks, carry):
        ties_before, maxes = carry
        maxes = list(maxes)
        starts = [pl.multiple_of(j * kb, kb) for j in blocks]
        dots, ties = [], []
        for ks in starts:
            k = kv_ref[0, pl.ds(ks, kb), 0:HEAD_DIM].astype(BF16)
            dots.append([_nt(k, q_att[h]) for h in range(DSA_HEADS)])
            k_chunk = (ks + lax.broadcasted_iota(jnp.int32, (kb, tq), 0)) >> CHUNK_SHIFT
            ties.append(jnp.where(sc_ref[pl.ds(ks, kb), :] == thr, jnp.where(k_chunk <= q_chunk, 1.0, 0.0), 0.0))
        ranks = [_dot(earlier, tie.astype(BF16)) for tie in ties]
        for b, ks in enumerate(starts):
            tie_taken = jnp.where(ranks[b] + ties_before < need, ties[b], 0.0)
            bias = jnp.where(sc_ref[pl.ds(ks, kb), :] > thr, 0.0, jnp.where(tie_taken > 0.5, 0.0, NEG_BIG))
            ties_before = ties_before + jnp.sum(ties[b], axis=0, keepdims=True)
            for h in range(DSA_HEADS):
                s = dots[b][h] + bias
                lg_ref[h, pl.ds(ks, kb), :] = s
                maxes[h] = jnp.maximum(maxes[h], jnp.max(s, axis=0, keepdims=True))
        return ties_before, tuple(maxes)

    init = (jnp.zeros((1, tq), F32), tuple(jnp.full((1, tq), NEG_BIG, F32) for _ in range(DSA_HEADS)))
    _, maxes = _paired_blocks(n_kb, logits_blocks, init)

    acc_ref[...] = jnp.zeros_like(acc_ref)

    def attend_blocks(blocks, sums):
        starts = [pl.multiple_of(j * kb, kb) for j in blocks]
        v_ts = [vt_ref[0, :, pl.ds(ks, kb)].astype(BF16) for ks in starts]
        sums = list(sums)
        for h in range(DSA_HEADS):
            update = None
            for b, ks in enumerate(starts):
                p = jnp.exp(lg_ref[h, pl.ds(ks, kb), :] - maxes[h])
                pv = _dot(v_ts[b], p.astype(BF16))
                update = pv if update is None else update + pv
                sums[h] = sums[h] + jnp.sum(p, axis=0, keepdims=True)
            acc_ref[h] += update
        return tuple(sums)

    sums = _paired_blocks(n_kb, attend_blocks, tuple(jnp.zeros((1, tq), F32) for _ in range(DSA_HEADS)))
    out_t = jnp.concatenate([acc_ref[h] / sums[h] for h in range(DSA_HEADS)], axis=0)
    o_ref[0] = out_t.T


def _dsa(p3, v_t, tq=256, kb=256):
    b, s, _ = p3.shape
    topk = min(DSA_TOPK_MAX, s // 4)
    return pl.pallas_call(
        functools.partial(_dsa_kernel, tq=tq, kb=kb, topk=topk),
        grid=(b, s // tq),
        in_specs=[pl.BlockSpec((1, tq, 512), lambda bi, i: (bi, i, COL_DSA_QQ // 512)),
                  pl.BlockSpec((1, tq, 128), lambda bi, i: (bi, i, COL_MISC // 128)),
                  pl.BlockSpec((1, s, 256), lambda bi, i: (bi, 0, COL_DSA_KV // 256)),
                  pl.BlockSpec((1, HEAD_DIM, s), lambda bi, i: (bi, 0, 0))],
        out_specs=pl.BlockSpec((1, tq, 256), lambda bi, i: (bi, i, 0)),
        out_shape=jax.ShapeDtypeStruct((b, s, 256), F32),
        scratch_shapes=[pltpu.VMEM((s, tq), F32), pltpu.VMEM((DSA_HEADS, s, tq), F32),
                        pltpu.VMEM((DSA_HEADS, HEAD_DIM, tq), F32)],
        compiler_params=_cparams(("parallel", "parallel")),
        name="dsa",
    )(p3, p3, p3, v_t)


def _rwkv_pre_kernel(p_ref, mu_ref, w0_ref, wup_ref, a0_ref, aup_ref, gup_ref, kk_ref, ka_ref, rk_ref,
                     r_o, lw_o, k_o, v_o, kn_o, b_o, g_o, bonus_o, last_ref, *, tt):
    t = pl.program_id(1)

    @pl.when(t == 0)
    def _():
        last_ref[...] = jnp.zeros_like(last_ref)

    p = p_ref[0]
    row = lax.broadcasted_iota(jnp.int32, p.shape, 0)
    p_prev = jnp.where(row == 0, last_ref[...], pltpu.roll(p, 1, axis=0))
    last_ref[...] = p[tt - 1:tt, :]
    ps = p + mu_ref[...] * (p_prev - p)
    w = RWKV_W
    r, k, v = ps[:, 0:w], ps[:, w:2 * w], ps[:, 2 * w:3 * w]
    w_lo, a_lo, g_lo = ps[:, 3 * w:3 * w + 64], ps[:, 3 * w + 64:3 * w + 128], ps[:, 3 * w + 128:3 * w + 256]
    ww = w0_ref[...] + _dot(jnp.tanh(w_lo), wup_ref[...], HIGHEST)
    softplus_neg = jnp.maximum(-ww, 0.0) + jnp.log1p(jnp.exp(-jnp.abs(ww)))
    log_w = -jnp.exp(-softplus_neg - 0.5)
    a = _sigmoid(a0_ref[...] + _dot(a_lo, aup_ref[...], HIGHEST))
    g = _dot(_sigmoid(g_lo), gup_ref[...], HIGHEST)
    head_sum = _head_block_ones(w, 1.0)
    kn = k * kk_ref[...]
    kn = kn * lax.rsqrt(_dot(kn * kn, head_sum, HIGHEST) + 1e-12)
    k2 = k * (1.0 + (a - 1.0) * ka_ref[...])
    bonus = _dot(r * k2 * rk_ref[...], head_sum, HIGHEST) * v
    r_o[0] = r
    lw_o[0] = log_w
    k_o[0] = k2
    v_o[0] = v
    kn_o[0] = kn
    b_o[0] = kn * a
    g_o[0] = g
    bonus_o[0] = bonus


def _rwkv_pre(p3, mu, w0, w_up, a0, a_up, g_up, k_k, k_a, r_k, tt=256):
    b, s, _ = p3.shape
    w = RWKV_W
    row = lambda x: x.reshape(1, -1)
    full = lambda shape: pl.BlockSpec(shape, lambda bi, t: (0,) * len(shape))
    out = jax.ShapeDtypeStruct((b, s, w), F32)
    return pl.pallas_call(
        functools.partial(_rwkv_pre_kernel, tt=tt),
        grid=(b, s // tt),
        in_specs=[pl.BlockSpec((1, tt, 1024), lambda bi, t: (bi, t, COL_RWKV // 1024)),
                  full((1, 1024)), full((1, w)), full((64, w)), full((1, w)), full((64, w)),
                  full((128, w)), full((1, w)), full((1, w)), full((1, w))],
        out_specs=[pl.BlockSpec((1, tt, w), lambda bi, t: (bi, t, 0))] * 8,
        out_shape=[out] * 8,
        scratch_shapes=[pltpu.VMEM((1, 1024), F32)],
        compiler_params=_cparams(("parallel", "arbitrary")),
        name="rwkv_pre",
    )(p3, row(mu), row(w0), w_up, row(a0), a_up, g_up, row(k_k), row(k_a), row(r_k))


def _bdot(a, b):
    return _dot(a.astype(BF16), b.astype(BF16))


def _bnt(a, b):
    return _nt(a.astype(BF16), b.astype(BF16))


def _btn(a, b):
    return _tn(a.astype(BF16), b.astype(BF16))


def _rwkv_chunk_kernel(r_ref, lw_ref, k_ref, v_ref, kn_ref, b_ref, y_ref, s_ref, *, cs, nch):
    c = pl.program_id(1)

    @pl.when(c == 0)
    def _():
        s_ref[...] = jnp.zeros_like(s_ref)

    rows = cs * nch
    big_row = lax.broadcasted_iota(jnp.int32, (rows, rows), 0)
    big_col = lax.broadcasted_iota(jnp.int32, (rows, rows), 1)
    same_chunk = (big_row >> CHUNK_SHIFT) == (big_col >> CHUNK_SHIFT)
    tri = jnp.where(same_chunk, jnp.where(big_col <= big_row, 1.0, 0.0), 0.0).astype(F32)
    row = lax.broadcasted_iota(jnp.int32, (cs, cs), 0)
    col = lax.broadcasted_iota(jnp.int32, (cs, cs), 1)
    incl = col <= row
    strict = col < row
    eye = jnp.where(col == row, 1.0, 0.0).astype(F32)
    lw = lw_ref[0]
    cum = _dot(tri, lw, HIGHEST)
    e_incl = jnp.exp(cum)
    e_neg = jnp.exp(-cum)
    abar_all = -kn_ref[0] * jnp.exp(cum - lw)
    rbar_all = r_ref[0] * e_incl
    bt_all = b_ref[0] * e_neg
    kt_all = k_ref[0] * e_neg
    v_all = v_ref[0]
    n_double = max(cs.bit_length() - 2, 0)
    probs = [(j, h) for j in range(nch) for h in range(RWKV_HEADS)]
    cut = lambda t, jh: t[jh[0] * cs:(jh[0] + 1) * cs, jh[1] * HEAD_DIM:(jh[1] + 1) * HEAD_DIM]
    each = lambda fn: {jh: fn(jh) for jh in probs}
    abar, rbar = each(lambda jh: cut(abar_all, jh)), each(lambda jh: cut(rbar_all, jh))
    bt, kt, v = each(lambda jh: cut(bt_all, jh)), each(lambda jh: cut(kt_all, jh)), each(lambda jh: cut(v_all, jh))
    p_last = each(lambda jh: e_incl[(jh[0] + 1) * cs - 1:(jh[0] + 1) * cs, jh[1] * HEAD_DIM:(jh[1] + 1) * HEAD_DIM])
    ar = each(lambda jh: jnp.concatenate([abar[jh], rbar[jh]], axis=0))
    ar_b = each(lambda jh: _bnt(ar[jh], bt[jh]))
    ar_k = each(lambda jh: _bnt(ar[jh], kt[jh]))
    a_ab = each(lambda jh: jnp.where(strict, ar_b[jh][:cs], 0.0))
    a_ak = each(lambda jh: jnp.where(strict, ar_k[jh][:cs], 0.0))
    a_rb = each(lambda jh: jnp.where(incl, ar_b[jh][cs:], 0.0))
    a_rk = each(lambda jh: jnp.where(incl, ar_k[jh][cs:], 0.0))
    ak_v = each(lambda jh: _bdot(a_ak[jh], v[jh]))
    inv = each(lambda jh: eye + a_ab[jh])
    power = a_ab
    for _ in range(n_double):
        power = each(lambda jh: _bdot(power[jh], power[jh]))
        inv = each(lambda jh: inv[jh] + _bdot(inv[jh], power[jh]))
    a_hat = each(lambda jh: _bdot(inv[jh], abar[jh]))
    u_hat = each(lambda jh: _bdot(inv[jh], ak_v[jh]))
    r_hat = each(lambda jh: rbar[jh] + _bdot(a_rb[jh], a_hat[jh]))
    y_hat = each(lambda jh: _bdot(a_rb[jh], u_hat[jh]) + _bdot(a_rk[jh], v[jh]))
    g_mat = each(lambda jh: eye * p_last[jh] + _btn(a_hat[jh], bt[jh] * p_last[jh]))
    h_mat = each(lambda jh: _btn(jnp.concatenate([u_hat[jh], v[jh]], axis=0),
                                 jnp.concatenate([bt[jh], kt[jh]], axis=0) * p_last[jh]))
    states = [s_ref[h] for h in range(RWKV_HEADS)]
    for j in range(nch):
        for h in range(RWKV_HEADS):
            y_ref[0, j * cs:(j + 1) * cs, h * HEAD_DIM:(h + 1) * HEAD_DIM] = (
                _bnt(r_hat[j, h], states[h]) + y_hat[j, h])
        states = [_dot(states[h], g_mat[j, h], HIGHEST) + h_mat[j, h] for h in range(RWKV_HEADS)]
    for h in range(RWKV_HEADS):
        s_ref[h] = states[h]


def _rwkv_chunk(r, lw, k, v, kn, b_, cs=CHUNK, nch=4):
    b, s, w = r.shape
    spec = pl.BlockSpec((1, cs * nch, w), lambda bi, c: (bi, c, 0))
    return pl.pallas_call(
        functools.partial(_rwkv_chunk_kernel, cs=cs, nch=nch),
        grid=(b, s // (cs * nch)),
        in_specs=[spec] * 6,
        out_specs=spec,
        out_shape=jax.ShapeDtypeStruct((b, s, w), F32),
        scratch_shapes=[pltpu.VMEM((RWKV_HEADS, HEAD_DIM, HEAD_DIM), F32)],
        compiler_params=_cparams(("parallel", "arbitrary")),
        name="rwkv_chunk",
    )(r, lw, k, v, kn, b_)


def _merge_kernel(x_ref, xb_ref, swa_ref, y_ref, g_ref, bonus_ref, fox_ref, dsa_ref, gng_ref, gnb_ref,
                  gbias_ref, wg_ref, wb_ref, wo_ref, lng_ref, lnb_ref, o_ref, ob_ref):
    y = y_ref[...]
    xb = xb_ref[...]
    head_mean = _head_block_ones(RWKV_W, 1.0 / HEAD_DIM)
    yc = y - _dot(y, head_mean, HIGHEST)
    yv = _dot(yc * yc, head_mean, HIGHEST)
    o_rwkv = (yc * lax.rsqrt(yv + RWKV_GN_EPS) * gng_ref[...] + gnb_ref[...] + bonus_ref[...]) * g_ref[...]
    branches = (swa_ref[...], o_rwkv, fox_ref[...], dsa_ref[...])
    merged = jnp.zeros(o_ref.shape, F32)
    off = 0
    for i, o in enumerate(branches):
        width = MIX_WIDTHS[i]
        proj = _dot(o.astype(BF16), wb_ref[off:off + width, :])
        gate = _sigmoid(_dot(xb, wg_ref[:, i * D_MODEL:(i + 1) * D_MODEL]) + gbias_ref[i:i + 1, :])
        merged = merged + gate * proj
        off += width
    y_out = _dot(merged.astype(BF16), wo_ref[...])
    xn = _layer_norm(DN_ALPHA * x_ref[...] + y_out, lng_ref[...], lnb_ref[...])
    o_ref[...] = xn
    ob_ref[...] = xn.astype(BF16)


def _merge(xf, xb, o_swa, y, g, bonus, o_fox, o_dsa, gn_g, gn_b, gate_bias, w_gate, w_branch, w_out, ln_g, ln_b,
           tm=512):
    n, d = xf.shape
    row = lambda x: x.reshape(1, -1)
    tok = lambda wdt: pl.BlockSpec((tm, wdt), lambda i: (i, 0))
    full = lambda shape: pl.BlockSpec(shape, lambda i: (0,) * len(shape))
    return pl.pallas_call(
        _merge_kernel,
        grid=(n // tm,),
        in_specs=[tok(d), tok(d), tok(512), tok(256), tok(256), tok(256), tok(256), tok(256),
                  full((1, 256)), full((1, 256)), full((N_BRANCHES, d)), full((d, N_BRANCHES * d)),
                  full((sum(MIX_WIDTHS), d)), full((d, d)), full((1, d)), full((1, d))],
        out_specs=[tok(d), tok(d)],
        out_shape=[jax.ShapeDtypeStruct((n, d), F32), jax.ShapeDtypeStruct((n, d), BF16)],
        compiler_params=_cparams(("parallel",)),
        name="merge",
    )(xf, xb, o_swa, y, g, bonus, o_fox, o_dsa, row(gn_g), row(gn_b), gate_bias, w_gate, w_branch, w_out,
      row(ln_g), row(ln_b))


def _silu(x):
    return x * _sigmoid(x)


def _ffn_kernel(xb_ref, x_ref, w1_ref, w3_ref, w2_ref, lng_ref, lnb_ref, o_ref, ob_ref, acc_ref):
    f = pl.program_id(1)

    @pl.when(f == 0)
    def _():
        acc_ref[...] = jnp.zeros_like(acc_ref)

    xb = xb_ref[...]
    hidden = _silu(_dot(xb, w1_ref[...])) * _dot(xb, w3_ref[...])
    acc_ref[...] += _dot(hidden.astype(BF16), w2_ref[...])

    @pl.when(f == pl.num_programs(1) - 1)
    def _():
        xn = _layer_norm(DN_ALPHA * x_ref[...] + acc_ref[...], lng_ref[...], lnb_ref[...])
        o_ref[...] = xn
        ob_ref[...] = xn.astype(BF16)


def _ffn(xb, xf, w1, w3, w2, ln_g, ln_b, tm=512, tf=1408):
    n, d = xf.shape
    ff = w1.shape[1]
    row = lambda x: x.reshape(1, -1)
    return pl.pallas_call(
        _ffn_kernel,
        grid=(n // tm, ff // tf),
        in_specs=[pl.BlockSpec((tm, d), lambda i, f: (i, 0)),
                  pl.BlockSpec((tm, d), lambda i, f: (i, 0)),
                  pl.BlockSpec((d, tf), lambda i, f: (0, f)),
                  pl.BlockSpec((d, tf), lambda i, f: (0, f)),
                  pl.BlockSpec((tf, d), lambda i, f: (f, 0)),
                  pl.BlockSpec((1, d), lambda i, f: (0, 0)),
                  pl.BlockSpec((1, d), lambda i, f: (0, 0))],
        out_specs=[pl.BlockSpec((tm, d), lambda i, f: (i, 0))] * 2,
        out_shape=[jax.ShapeDtypeStruct((n, d), F32), jax.ShapeDtypeStruct((n, d), BF16)],
        scratch_shapes=[pltpu.VMEM((tm, d), F32)],
        compiler_params=_cparams(("parallel", "arbitrary")),
        name="ffn",
    )(xb, xf, w1, w3, w2, row(ln_g), row(ln_b))


def _moe_kernel(xb_ref, x_ref, rw_ref, rb_ref, w1_ref, w3_ref, w2_ref, lng_ref, lnb_ref, o_ref,
                comb_ref, rank_ref, cnt_ref, *, sb):
    e = pl.program_id(1)
    tm = x_ref.shape[0]
    lane = lax.broadcasted_iota(jnp.int32, (tm, 128), 1).astype(F32)

    @pl.when(e == 0)
    def _():
        o_ref[...] = jnp.zeros_like(o_ref)
        logits = _dot(x_ref[...], rw_ref[...], HIGHEST) + rb_ref[...]
        logits = jnp.where(lane < N_EXPERTS, logits, NEG_BIG)
        m1 = jnp.max(logits, axis=-1, keepdims=True)
        i1 = jnp.min(jnp.where(logits == m1, lane, 128.0), axis=-1, keepdims=True)
        rest = jnp.where(lane == i1, NEG_BIG, logits)
        m2 = jnp.max(rest, axis=-1, keepdims=True)
        i2 = jnp.min(jnp.where(rest == m2, lane, 128.0), axis=-1, keepdims=True)
        e2 = jnp.exp(m2 - m1)
        comb_ref[...] = jnp.where(lane == i1, 1.0 / (1.0 + e2), 0.0) + jnp.where(lane == i2, e2 / (1.0 + e2), 0.0)
        chosen = jnp.where(lane == i1, 1.0, 0.0) + jnp.where(lane == i2, 1.0, 0.0)
        r = lax.broadcasted_iota(jnp.int32, (tm, tm), 0)
        c = lax.broadcasted_iota(jnp.int32, (tm, tm), 1)
        earlier = jnp.where(c < r, 1.0, 0.0).astype(BF16)
        rank = jnp.where(chosen > 0.5, _dot(earlier, chosen.astype(BF16)), -1.0)
        lane8 = lax.broadcasted_iota(jnp.int32, (8, 128), 1)
        row8 = lax.broadcasted_iota(jnp.int32, (8, 128), 0)
        rank_ref[...] = _nt(jnp.where(lane8 == row8, 1.0, 0.0).astype(F32), rank, HIGHEST)
        cnt_ref[...] = jnp.sum(chosen, axis=0, keepdims=True)

    e_f = e.astype(F32)
    n_tok = jnp.sum(jnp.where(lane[0:1, :] == e_f, cnt_ref[...], 0.0)).astype(jnp.int32)
    n_blocks = lax.div(n_tok + sb - 1, sb)
    rank_row = rank_ref[pl.ds(e, 1), :]
    weight = jnp.sum(jnp.where(lane == e_f, comb_ref[...], 0.0), axis=-1, keepdims=True)
    slot0 = lax.broadcasted_iota(jnp.int32, (sb, tm), 0).astype(F32)

    def block(s, carry):
        pick = jnp.where(rank_row == slot0 + (s * sb).astype(F32), 1.0, 0.0).astype(BF16)
        xs = _dot(pick, xb_ref[...]).astype(BF16)
        hidden = _silu(_dot(xs, w1_ref[0])) * _dot(xs, w3_ref[0])
        out = _dot(hidden.astype(BF16), w2_ref[0]).astype(BF16)
        o_ref[...] += _tn(pick, out) * weight
        return carry

    lax.fori_loop(0, n_blocks, block, 0)

    @pl.when(e == pl.num_programs(1) - 1)
    def _():
        o_ref[...] = _layer_norm(DN_ALPHA * x_ref[...] + o_ref[...], lng_ref[...], lnb_ref[...])


def _moe(xb, xf, router_w, router_b, w1, w3, w2, ln_g, ln_b, tm=1024, sb=288):
    n, d = xf.shape
    ne, _, fe = w1.shape
    tm = min(tm, n)
    rw = jnp.pad(router_w, ((0, 0), (0, 128 - ne)))
    rb = jnp.pad(router_b, (0, 128 - ne)).reshape(1, 128)
    row = lambda x: x.reshape(1, -1)
    return pl.pallas_call(
        functools.partial(_moe_kernel, sb=sb),
        grid=(n // tm, ne),
        in_specs=[pl.BlockSpec((tm, d), lambda i, e: (i, 0)),
                  pl.BlockSpec((tm, d), lambda i, e: (i, 0)),
                  pl.BlockSpec((d, 128), lambda i, e: (0, 0)),
                  pl.BlockSpec((1, 128), lambda i, e: (0, 0)),
                  pl.BlockSpec((1, d, fe), lambda i, e: (e, 0, 0)),
                  pl.BlockSpec((1, d, fe), lambda i, e: (e, 0, 0)),
                  pl.BlockSpec((1, fe, d), lambda i, e: (e, 0, 0)),
                  pl.BlockSpec((1, d), lambda i, e: (0, 0)),
                  pl.BlockSpec((1, d), lambda i, e: (0, 0))],
        out_specs=pl.BlockSpec((tm, d), lambda i, e: (i, 0)),
        out_shape=jax.ShapeDtypeStruct((n, d), F32),
        scratch_shapes=[pltpu.VMEM((tm, 128), F32), pltpu.VMEM((8, tm), F32), pltpu.VMEM((1, 128), F32)],
        compiler_params=pltpu.CompilerParams(dimension_semantics=("parallel", "arbitrary"),
                                             vmem_limit_bytes=MOE_VMEM_LIMIT),
        name="moe",
    )(xb, xf, rw, rb, w1, w3, w2, row(ln_g), row(ln_b))


def _mixer_layer(xf, xb, bsz, seq, w_in, sinks, mu, w0, w_up, a0, a_up, g_up, k_k, k_a, r_k, gn_g, gn_b,
                 f_bias, gate_bias, w_branch, w_out, ln_g, ln_b):
    w_mix, w_gate = _layout_w_in(w_in.astype(BF16))
    p2 = _matmul(xb, w_mix, tm=1024, tn=512)
    p3 = p2.reshape(bsz, seq, P_WIDTH)
    swa_kv_w = SWA_KV_HEADS * HEAD_DIM
    swa_v_t = jnp.swapaxes(p3[:, :, COL_SWA_KV + swa_kv_w:COL_SWA_KV + 2 * swa_kv_w], 1, 2)
    o_swa = _swa(p3, swa_v_t, sinks)
    fox_w = FOX_HEADS * HEAD_DIM
    fox_v_t = jnp.swapaxes(p3[:, :, COL_FOX_KV + fox_w:COL_FOX_KV + 2 * fox_w], 1, 2)
    o_fox = _fox(p3, fox_v_t, _fox_c(p3, f_bias))
    v_t = jnp.swapaxes(p3[:, :, COL_DSA_KV + HEAD_DIM:COL_DSA_KV + 2 * HEAD_DIM], 1, 2)
    o_dsa = _dsa(p3, v_t)
    r, lw, k2, v, kn, b_, g, bonus = _rwkv_pre(p3, mu, w0, w_up, a0, a_up, g_up, k_k, k_a, r_k)
    y = _rwkv_chunk(r, lw, k2, v, kn, b_)
    n = bsz * seq
    flat = lambda t: t.reshape(n, t.shape[-1])
    return _merge(xf, xb, flat(o_swa), flat(y), flat(g), flat(bonus), flat(o_fox), flat(o_dsa), gn_g, gn_b,
                  gate_bias, w_gate, w_branch.astype(BF16), w_out.astype(BF16), ln_g, ln_b)


def kernel(x, w_in, swa_sinks, rwkv_mu, rwkv_w0, rwkv_w_up, rwkv_a0, rwkv_a_up, rwkv_g_up, rwkv_k_k, rwkv_k_a,
           rwkv_r_k, rwkv_gn_g, rwkv_gn_b, fox_f_bias, gate_bias, w_branch, w_out, ln_g, ln_b, ffn_w1, ffn_w3,
           ffn_w2, router_w, router_b, exp_w1, exp_w3, exp_w2):
    bsz, seq, d = x.shape
    xf = x.reshape(bsz * seq, d)
    xb = xf.astype(BF16)
    for layer in range(DEPTH):
        xf, xb = _mixer_layer(xf, xb, bsz, seq, w_in[layer], swa_sinks[layer], rwkv_mu[layer], rwkv_w0[layer],
                              rwkv_w_up[layer], rwkv_a0[layer], rwkv_a_up[layer], rwkv_g_up[layer],
                              rwkv_k_k[layer], rwkv_k_a[layer], rwkv_r_k[layer], rwkv_gn_g[layer],
                              rwkv_gn_b[layer], fox_f_bias[layer], gate_bias[layer], w_branch[layer],
                              w_out[layer], ln_g[layer, 0], ln_b[layer, 0])
        j = layer // 2
        if layer % 2 == 0:
            xf, xb = _ffn(xb, xf, ffn_w1[j].astype(BF16), ffn_w3[j].astype(BF16), ffn_w2[j].astype(BF16),
                          ln_g[layer, 1], ln_b[layer, 1])
        else:
            xf = _moe(xb, xf, router_w[j], router_b[j], exp_w1[j].astype(BF16), exp_w3[j].astype(BF16),
                      exp_w2[j].astype(BF16), ln_g[layer, 1], ln_b[layer, 1])
            xb = xf.astype(BF16) if layer + 1 < DEPTH else None
    return xf.reshape(bsz, seq, d)
```

```python
import functools

import jax
import jax.numpy as jnp
from jax import lax
from jax.experimental import pallas as pl
from jax.experimental.pallas import tpu as pltpu

F32 = jnp.float32
BF16 = jnp.bfloat16
HIGHEST = lax.Precision.HIGHEST

D_MODEL = 1024
DEPTH = 2
CHUNK = 64
HEAD_DIM = 64
CHUNK_SHIFT = 6
HEAD_SHIFT = 6
SWA_HEADS = 8
SWA_KV_HEADS = 2
SWA_GROUP = SWA_HEADS // SWA_KV_HEADS
SWA_WINDOW = 128
SWA_WIN_CHUNKS = SWA_WINDOW // CHUNK
RWKV_HEADS = 4
RWKV_W = RWKV_HEADS * HEAD_DIM
RWKV_GN_EPS = 64e-5
FOX_HEADS = 4
DSA_HEADS = 4
IDX_HEADS = 4
IDX_DIM = 64
DSA_TOPK_MAX = 256
D_FF = 2816
N_EXPERTS = 8
D_FF_EXPERT = 1408
N_BRANCHES = 4
DN_ALPHA = (2 * DEPTH) ** 0.25
LN_EPS = 1e-5
ATTN_SCALE = HEAD_DIM ** -0.5
IDX_SCALE = IDX_DIM ** -0.5
IDX_W_SCALE = IDX_HEADS ** -0.5
MIX_WIDTHS = (SWA_HEADS * HEAD_DIM, RWKV_W, FOX_HEADS * HEAD_DIM, DSA_HEADS * HEAD_DIM)

COL_RWKV = 0
COL_SWA_Q = 1024
COL_FOX_KV = 1536
COL_DSA_QQ = 2048
COL_FOX_Q = 2560
COL_SWA_KV = 2816
COL_DSA_KV = 3072
COL_MISC = 3328
P_WIDTH = 3584

NEG_BIG = -1e30
INT_MIN = -(2 ** 31)
VMEM_LIMIT = 48 * 1024 * 1024
MOE_VMEM_LIMIT = 58 * 1024 * 1024


def _layout_w_in(w):
    d = w.shape[0]
    z = lambda n: jnp.zeros((d, n), w.dtype)
    swa, rwkv, fox, dsa, gate = 0, 768, 1792, 2564, 3272
    parts = [
        w[:, rwkv:rwkv + 1024],
        w[:, swa:swa + 512],
        w[:, fox + 256:fox + 768],
        w[:, dsa:dsa + 256], w[:, dsa + 384:dsa + 640],
        w[:, fox:fox + 256],
        w[:, swa + 512:swa + 768],
        w[:, dsa + 256:dsa + 384], w[:, dsa + 640:dsa + 704], z(64),
        w[:, fox + 768:fox + 772], w[:, dsa + 704:dsa + 708], z(120),
        z(128),
    ]
    return jnp.concatenate(parts, axis=1), w[:, gate:gate + N_BRANCHES * D_MODEL]


def _cparams(sem):
    return pltpu.CompilerParams(dimension_semantics=sem, vmem_limit_bytes=VMEM_LIMIT)


def _nt(a, b, precision=None):
    return lax.dot_general(a, b, (((1,), (1,)), ((), ())), precision=precision,
                           preferred_element_type=F32)


def _tn(a, b, precision=None):
    return lax.dot_general(a, b, (((0,), (0,)), ((), ())), precision=precision,
                           preferred_element_type=F32)


def _dot(a, b, precision=None):
    return jnp.dot(a, b, precision=precision, preferred_element_type=F32)


def _sigmoid(x):
    return 1.0 / (1.0 + jnp.exp(-x))


def _layer_norm(z, g, b):
    mu = jnp.mean(z, axis=-1, keepdims=True)
    zc = z - mu
    var = jnp.mean(zc * zc, axis=-1, keepdims=True)
    return zc * lax.rsqrt(var + LN_EPS) * g + b


def _skewed(n, first, second):
    out = []
    staged = first(0)
    for h in range(n):
        nxt = first(h + 1) if h + 1 < n else None
        out.append(second(h, staged))
        staged = nxt
    return out


def _paired_blocks(n, body, carry):
    carry = lax.fori_loop(0, lax.div(n, 2), lambda t, c: body((2 * t, 2 * t + 1), c), carry)
    return lax.cond(lax.rem(n, 2) == 1, lambda c: body((n - 1,), c), lambda c: c, carry)


def _head_block_ones(n, scale):
    r = lax.broadcasted_iota(jnp.int32, (n, n), 0) >> HEAD_SHIFT
    c = lax.broadcasted_iota(jnp.int32, (n, n), 1) >> HEAD_SHIFT
    return jnp.where(r == c, scale, 0.0).astype(F32)


def _mm_kernel(a_ref, b_ref, o_ref):
    o_ref[...] = _dot(a_ref[...], b_ref[...])


def _matmul(a, b, tm, tn):
    m, k = a.shape
    n = b.shape[1]
    tm = min(tm, m)
    return pl.pallas_call(
        _mm_kernel,
        grid=(m // tm, n // tn),
        in_specs=[pl.BlockSpec((tm, k), lambda i, j: (i, 0)),
                  pl.BlockSpec((k, tn), lambda i, j: (0, j))],
        out_specs=pl.BlockSpec((tm, tn), lambda i, j: (i, j)),
        out_shape=jax.ShapeDtypeStruct((m, n), F32),
        compiler_params=_cparams(("parallel", "parallel")),
        name="in_proj",
    )(a, b)


def _swa_kernel(sink_ref, q_ref, kv_ref, vt_ref, o_ref, *, tq):
    i = pl.program_id(1)
    win = tq + SWA_WINDOW
    s0 = pl.multiple_of(jnp.maximum(i * tq - SWA_WINDOW, 0), SWA_WINDOW)
    k_chunk = (s0 + lax.broadcasted_iota(jnp.int32, (win, tq), 0)) >> CHUNK_SHIFT
    q_chunk = (i * tq + lax.broadcasted_iota(jnp.int32, (win, tq), 1)) >> CHUNK_SHIFT
    bias = jnp.where(k_chunk <= q_chunk,
                     jnp.where(k_chunk >= q_chunk - SWA_WIN_CHUNKS, 0.0, NEG_BIG), NEG_BIG)
    ks = [kv_ref[0, pl.ds(s0, win), hk * HEAD_DIM:(hk + 1) * HEAD_DIM].astype(BF16)
          for hk in range(SWA_KV_HEADS)]
    v_ts = [vt_ref[0, hk * HEAD_DIM:(hk + 1) * HEAD_DIM, pl.ds(s0, win)].astype(BF16)
            for hk in range(SWA_KV_HEADS)]

    def logits(h):
        q = (q_ref[0, :, h * HEAD_DIM:(h + 1) * HEAD_DIM] * ATTN_SCALE).astype(BF16)
        return _nt(ks[h // SWA_GROUP], q) + bias

    def finish(h, s):
        sink = sink_ref[h]
        m = jnp.maximum(jnp.max(s, axis=0, keepdims=True), sink)
        e = jnp.exp(s - m)
        denom = jnp.sum(e, axis=0, keepdims=True) + jnp.exp(sink - m)
        return _dot(v_ts[h // SWA_GROUP], e.astype(BF16)) / denom

    o_ref[0] = jnp.concatenate(_skewed(SWA_HEADS, logits, finish), axis=0).T


def _swa(p3, v_t, sinks, tq=256):
    b, s, _ = p3.shape
    kv_w = SWA_KV_HEADS * HEAD_DIM
    return pl.pallas_call(
        functools.partial(_swa_kernel, tq=tq),
        grid=(b, s // tq),
        in_specs=[pl.BlockSpec(memory_space=pltpu.SMEM),
                  pl.BlockSpec((1, tq, 512), lambda bi, i: (bi, i, COL_SWA_Q // 512)),
                  pl.BlockSpec((1, s, kv_w), lambda bi, i: (bi, 0, COL_SWA_KV // kv_w)),
                  pl.BlockSpec((1, kv_w, s), lambda bi, i: (bi, 0, 0))],
        out_specs=pl.BlockSpec((1, tq, 512), lambda bi, i: (bi, i, 0)),
        out_shape=jax.ShapeDtypeStruct((b, s, 512), F32),
        compiler_params=_cparams(("parallel", "parallel")),
        name="swa",
    )(sinks, p3, p3, v_t)


def _fox_c_kernel(f_ref, bias_ref, c_ref, *, blk):
    s = f_ref.shape[1]
    r = lax.broadcasted_iota(jnp.int32, (blk, blk), 0)
    c = lax.broadcasted_iota(jnp.int32, (blk, blk), 1)
    tri = jnp.where(c <= r, 1.0, 0.0).astype(F32)
    carry = jnp.zeros((1, 128), F32)
    for j in range(s // blk):
        x = f_ref[0, j * blk:(j + 1) * blk, :] + bias_ref[...]
        log_f = jnp.minimum(x, 0.0) - jnp.log1p(jnp.exp(-jnp.abs(x)))
        cs = _dot(tri, log_f, HIGHEST) + carry
        carry = cs[blk - 1:blk, :]
        for h in range(FOX_HEADS):
            c_ref[0, h, j * blk:(j + 1) * blk, :] = jnp.broadcast_to(cs[:, h:h + 1], (blk, 128))


def _fox_c(p3, f_bias):
    b, s, _ = p3.shape
    bias_row = jnp.pad(f_bias, (0, 128 - FOX_HEADS)).reshape(1, 128)
    return pl.pallas_call(
        functools.partial(_fox_c_kernel, blk=256),
        grid=(b,),
        in_specs=[pl.BlockSpec((1, s, 128), lambda bi: (bi, 0, COL_MISC // 128)),
                  pl.BlockSpec((1, 128), lambda bi: (0, 0))],
        out_specs=pl.BlockSpec((1, FOX_HEADS, s, 128), lambda bi: (bi, 0, 0, 0)),
        out_shape=jax.ShapeDtypeStruct((b, FOX_HEADS, s, 128), F32),
        compiler_params=_cparams(("parallel",)),
        name="fox_cumsum",
    )(p3, bias_row)


def _fox_kernel(q_ref, k_ref, vt_ref, c_ref, o_ref, lg_ref, acc_ref, *, tq):
    i = pl.program_id(1)
    tk = tq
    key_i = lax.broadcasted_iota(jnp.int32, (tk, tq), 0)
    qry_i = lax.broadcasted_iota(jnp.int32, (tk, tq), 1)
    diag_bias = jnp.where(key_i <= qry_i, 0.0, NEG_BIG)
    q = [(q_ref[0, :, h * HEAD_DIM:(h + 1) * HEAD_DIM] * ATTN_SCALE).astype(BF16) for h in range(FOX_HEADS)]

    def logits_blocks(blocks, maxes, masked=False):
        starts = [pl.multiple_of(j * tk, tk) for j in blocks]
        dots = [[_nt(k_ref[0, pl.ds(ks, tk), h * HEAD_DIM:(h + 1) * HEAD_DIM].astype(BF16), q[h])
                 for h in range(FOX_HEADS)] for ks in starts]
        maxes = list(maxes)
        for b, ks in enumerate(starts):
            for h in range(FOX_HEADS):
                c_k = c_ref[0, h, pl.ds(ks, tk), :]
                s = dots[b][h] - jnp.concatenate([c_k] * (tq // 128), axis=1)
                if masked:
                    s = s + diag_bias
                lg_ref[h, pl.ds(ks, tk), :] = s
                maxes[h] = jnp.maximum(maxes[h], jnp.max(s, axis=0, keepdims=True))
        return tuple(maxes)

    maxes = tuple(jnp.full((1, tq), NEG_BIG, F32) for _ in range(FOX_HEADS))
    maxes = _paired_blocks(i, logits_blocks, maxes)
    maxes = logits_blocks((i,), maxes, masked=True)
    acc_ref[...] = jnp.zeros_like(acc_ref)

    def attend_blocks(blocks, sums):
        starts = [pl.multiple_of(j * tk, tk) for j in blocks]
        sums = list(sums)
        for h in range(FOX_HEADS):
            update = None
            for ks in starts:
                p = jnp.exp(lg_ref[h, pl.ds(ks, tk), :] - maxes[h])
                v_t = vt_ref[0, h * HEAD_DIM:(h + 1) * HEAD_DIM, pl.ds(ks, tk)].astype(BF16)
                pv = _dot(v_t, p.astype(BF16))
                update = pv if update is None else update + pv
                sums[h] = sums[h] + jnp.sum(p, axis=0, keepdims=True)
            acc_ref[h] += update
        return tuple(sums)

    sums = _paired_blocks(i + 1, attend_blocks, tuple(jnp.zeros((1, tq), F32) for _ in range(FOX_HEADS)))
    out_t = jnp.concatenate([acc_ref[h] / sums[h] for h in range(FOX_HEADS)], axis=0)
    o_ref[0] = out_t.T


def _fox(p3, v_t, c, tq=256):
    b, s, _ = p3.shape
    w = FOX_HEADS * HEAD_DIM
    return pl.pallas_call(
        functools.partial(_fox_kernel, tq=tq),
        grid=(b, s // tq),
        in_specs=[pl.BlockSpec((1, tq, w), lambda bi, i: (bi, i, COL_FOX_Q // w)),
                  pl.BlockSpec((1, s, w), lambda bi, i: (bi, 0, COL_FOX_KV // w)),
                  pl.BlockSpec((1, w, s), lambda bi, i: (bi, 0, 0)),
                  pl.BlockSpec((1, FOX_HEADS, s, 128), lambda bi, i: (bi, 0, 0, 0))],
        out_specs=pl.BlockSpec((1, tq, w), lambda bi, i: (bi, i, 0)),
        out_shape=jax.ShapeDtypeStruct((b, s, w), F32),
        scratch_shapes=[pltpu.VMEM((FOX_HEADS, s, tq), F32), pltpu.VMEM((FOX_HEADS, HEAD_DIM, tq), F32)],
        compiler_params=_cparams(("parallel", "parallel")),
        name="fox",
    )(p3, p3, v_t, c)


def _key_to_score(key):
    return pltpu.bitcast(key ^ ((key >> 31) & 0x7FFFFFFF), F32)


def _bit_planes(score):
    bits = pltpu.bitcast(score, jnp.int32)
    image = bits ^ ((bits >> 31) | INT_MIN)
    rows = image.reshape(32, 8, image.shape[1])
    v = [rows[j] for j in range(32)]
    for d, mask in ((16, 0x0000FFFF), (8, 0x00FF00FF), (4, 0x0F0F0F0F), (2, 0x33333333), (1, 0x55555555)):
        for k in range(32):
            if k & d == 0:
                t = (v[k] ^ lax.shift_right_logical(v[k + d], d)) & mask
                v[k] = v[k] ^ t
                v[k + d] = v[k + d] ^ lax.shift_left(t, d)
    return v


def _dsa_kernel(qq_ref, misc_ref, kv_ref, vt_ref, o_ref, sc_ref, lg_ref, acc_ref, pl_ref, *, tq, kb, topk):
    i = pl.program_id(1)
    n_kb = lax.div((i + 1) * tq + kb - 1, kb)
    lane8 = lax.broadcasted_iota(jnp.int32, (8, 128), 1)
    row8 = lax.broadcasted_iota(jnp.int32, (8, 128), 0)
    pick = jnp.where((lane8 == row8 + 4) & (row8 < IDX_HEADS), 1.0, 0.0).astype(F32)
    w_t = _nt(pick, misc_ref[0], HIGHEST) * (IDX_W_SCALE * IDX_SCALE)
    q_chunk = (i * tq + lax.broadcasted_iota(jnp.int32, (1, tq), 1)) >> CHUNK_SHIFT
    qw = DSA_HEADS * HEAD_DIM
    q_idx = [qq_ref[0, :, qw + h * IDX_DIM:qw + (h + 1) * IDX_DIM].astype(BF16) for h in range(IDX_HEADS)]

    def score_blocks(blocks, carry):
        starts = [pl.multiple_of(j * kb, kb) for j in blocks]
        dots = []
        for ks in starts:
            k_idx = kv_ref[0, pl.ds(ks, kb), 2 * HEAD_DIM:2 * HEAD_DIM + IDX_DIM].astype(BF16)
            dots.append([_nt(k_idx, q_idx[h]) for h in range(IDX_HEADS)])
        for b, ks in enumerate(starts):
            score = jnp.maximum(dots[b][0], 0.0) * w_t[0:1, :]
            for h in range(1, IDX_HEADS):
                score = score + jnp.maximum(dots[b][h], 0.0) * w_t[h:h + 1, :]
            k_chunk = (ks + lax.broadcasted_iota(jnp.int32, (kb, tq), 0)) >> CHUNK_SHIFT
            score = jnp.where(k_chunk <= q_chunk, score, -jnp.inf)
            sc_ref[pl.ds(ks, kb), :] = score
            planes = _bit_planes(score)
            for t in range(32):
                pl_ref[blocks[b], t] = planes[t]
        return carry

    _paired_blocks(n_kb, score_blocks, 0)
    n_blocks = pl_ref.shape[0]

    def clear_planes(j, carry):
        pl_ref[j] = jnp.zeros(pl_ref.shape[1:], jnp.int32)
        return carry

    lax.fori_loop(n_kb, n_blocks, clear_planes, 0)

    def count(pred):
        def body(j, acc):
            blk = sc_ref[pl.ds(pl.multiple_of(j * kb, kb), kb), :]
            hit = jnp.where(pred(blk), 1, 0).astype(jnp.int32).reshape(kb // 8, 8, tq)
            parts = [hit[g] for g in range(kb // 8)]
            while len(parts) > 1:
                parts = [parts[g] + parts[g + 1] for g in range(0, len(parts), 2)]
            return acc + parts[0]
        acc = lax.fori_loop(0, n_kb, body, jnp.zeros((8, tq), jnp.int32))
        return jnp.sum(acc.astype(F32), axis=0, keepdims=True)

    def search_by_compares():
        zero = jnp.zeros((1, tq), F32)
        thr_key = jnp.where(count(lambda blk: blk >= zero) >= topk, 0, INT_MIN).astype(jnp.int32)

        def bit_step(bi, thr_key):
            cand = thr_key | lax.shift_left(jnp.int32(1), 30 - bi)
            cand_score = _key_to_score(cand)
            return jnp.where(count(lambda blk: blk >= cand_score) >= topk, cand, thr_key)

        thr_key = lax.fori_loop(0, 31, bit_step, thr_key)
        t = jnp.where(thr_key == INT_MIN, -jnp.inf, _key_to_score(thr_key))
        return t, count(lambda blk: blk > t)

    def search_by_planes():
        def bit_pass(t, carry):
            prefix, above, alive = carry
            ones = [alive[j] & pl_ref[j, t] for j in range(n_blocks)]
            parts = [lax.population_count(o) for o in ones]
            while len(parts) > 1:
                parts = [parts[g] + parts[g + 1] for g in range(0, len(parts), 2)]
            reach = above + jnp.sum(parts[0].astype(F32), axis=0, keepdims=True)
            take = reach >= topk
            prefix = jnp.where(take, prefix | lax.shift_left(jnp.int32(1), 31 - t), prefix)
            alive = tuple(jnp.where(take, ones[j], alive[j] ^ ones[j]) for j in range(n_blocks))
            return prefix, jnp.where(take, above, reach), alive

        init = (jnp.zeros((1, tq), jnp.int32), jnp.zeros((1, tq), F32),
                tuple(jnp.full((8, tq), -1, jnp.int32) for _ in range(n_blocks)))
        image, _, _ = lax.fori_loop(0, 32, bit_pass, init)
        return pltpu.bitcast(image ^ (~(image >> 31) | INT_MIN), F32)

    thr_planes = search_by_planes()
    n_above = count(lambda blk: blk > thr_planes)
    n_reach = count(lambda blk: blk >= thr_planes)
    confirmed = jnp.min(jnp.where(n_above < topk, jnp.where(n_reach >= topk, 1.0, 0.0), 0.0)) > 0.5
    thr, n_above = lax.cond(confirmed, lambda: (thr_planes, n_above), search_by_compares)
    need = topk - n_above

    r = lax.broadcasted_iota(jnp.int32, (kb, kb), 0)
    c = lax.broadcasted_iota(jnp.int32, (kb, kb), 1)
    earlier = jnp.where(c < r, 1.0, 0.0).astype(BF16)
    q_att = [(qq_ref[0, :, h * HEAD_DIM:(h + 1) * HEAD_DIM] * ATTN_SCALE).astype(BF16)
             for h in range(DSA_HEADS)]

    def logits_blocks(blocks, carry):
        ties_before, maxes = carry
        maxes = list(maxes)
        starts = [pl.multiple_of(j * kb, kb) for j in blocks]
        dots, ties = [], []
        for ks in starts:
            k = kv_ref[0, pl.ds(ks, kb), 0:HEAD_DIM].astype(BF16)
            dots.append([_nt(k, q_att[h]) for h in range(DSA_HEADS)])
            k_chunk = (ks + lax.broadcasted_iota(jnp.int32, (kb, tq), 0)) >> CHUNK_SHIFT
            ties.append(jnp.where(sc_ref[pl.ds(ks, kb), :] == thr, jnp.where(k_chunk <= q_chunk, 1.0, 0.0), 0.0))
        ranks = [_dot(earlier, tie.astype(BF16)) for tie in ties]
        for b, ks in enumerate(starts):
            tie_taken = jnp.where(ranks[b] + ties_before < need, ties[b], 0.0)
            bias = jnp.where(sc_ref[pl.ds(ks, kb), :] > thr, 0.0, jnp.where(tie_taken > 0.5, 0.0, NEG_BIG))
            ties_before = ties_before + jnp.sum(ties[b], axis=0, keepdims=True)
            for h in range(DSA_HEADS):
                s = dots[b][h] + bias
                lg_ref[h, pl.ds(ks, kb), :] = s
                maxes[h] = jnp.maximum(maxes[h], jnp.max(s, axis=0, keepdims=True))
        return ties_before, tuple(maxes)

    init = (jnp.zeros((1, tq), F32), tuple(jnp.full((1, tq), NEG_BIG, F32) for _ in range(DSA_HEADS)))
    _, maxes = _paired_blocks(n_kb, logits_blocks, init)

    acc_ref[...] = jnp.zeros_like(acc_ref)

    def attend_blocks(blocks, sums):
        starts = [pl.multiple_of(j * kb, kb) for j in blocks]
        v_ts = [vt_ref[0, :, pl.ds(ks, kb)].astype(BF16) for ks in starts]
        sums = list(sums)
        for h in range(DSA_HEADS):
            update = None
            for b, ks in enumerate(starts):
                p = jnp.exp(lg_ref[h, pl.ds(ks, kb), :] - maxes[h])
                pv = _dot(v_ts[b], p.astype(BF16))
                update = pv if update is None else update + pv
                sums[h] = sums[h] + jnp.sum(p, axis=0, keepdims=True)
            acc_ref[h] += update
        return tuple(sums)

    sums = _paired_blocks(n_kb, attend_blocks, tuple(jnp.zeros((1, tq), F32) for _ in range(DSA_HEADS)))
    out_t = jnp.concatenate([acc_ref[h] / sums[h] for h in range(DSA_HEADS)], axis=0)
    o_ref[0] = out_t.T


def _dsa(p3, v_t, tq=256, kb=256):
    b, s, _ = p3.shape
    assert kb == 8 * 32, "a key block is the 32 row groups of one bit-plane word"
    topk = min(DSA_TOPK_MAX, s // 4)
    return pl.pallas_call(
        functools.partial(_dsa_kernel, tq=tq, kb=kb, topk=topk),
        grid=(b, s // tq),
        in_specs=[pl.BlockSpec((1, tq, 512), lambda bi, i: (bi, i, COL_DSA_QQ // 512)),
                  pl.BlockSpec((1, tq, 128), lambda bi, i: (bi, i, COL_MISC // 128)),
                  pl.BlockSpec((1, s, 256), lambda bi, i: (bi, 0, COL_DSA_KV // 256)),
                  pl.BlockSpec((1, HEAD_DIM, s), lambda bi, i: (bi, 0, 0))],
        out_specs=pl.BlockSpec((1, tq, 256), lambda bi, i: (bi, i, 0)),
        out_shape=jax.ShapeDtypeStruct((b, s, 256), F32),
        scratch_shapes=[pltpu.VMEM((s, tq), F32), pltpu.VMEM((DSA_HEADS, s, tq), F32),
                        pltpu.VMEM((DSA_HEADS, HEAD_DIM, tq), F32), pltpu.VMEM((s // kb, 32, 8, tq), jnp.int32)],
        compiler_params=_cparams(("parallel", "parallel")),
        name="dsa",
    )(p3, p3, p3, v_t)


def _rwkv_pre_kernel(p_ref, mu_ref, w0_ref, wup_ref, a0_ref, aup_ref, gup_ref, kk_ref, ka_ref, rk_ref,
                     r_o, lw_o, k_o, v_o, kn_o, b_o, g_o, bonus_o, last_ref, *, tt):
    t = pl.program_id(1)

    @pl.when(t == 0)
    def _():
        last_ref[...] = jnp.zeros_like(last_ref)

    p = p_ref[0]
    row = lax.broadcasted_iota(jnp.int32, p.shape, 0)
    p_prev = jnp.where(row == 0, last_ref[...], pltpu.roll(p, 1, axis=0))
    last_ref[...] = p[tt - 1:tt, :]
    ps = p + mu_ref[...] * (p_prev - p)
    w = RWKV_W
    r, k, v = ps[:, 0:w], ps[:, w:2 * w], ps[:, 2 * w:3 * w]
    w_lo, a_lo, g_lo = ps[:, 3 * w:3 * w + 64], ps[:, 3 * w + 64:3 * w + 128], ps[:, 3 * w + 128:3 * w + 256]
    ww = w0_ref[...] + _dot(jnp.tanh(w_lo), wup_ref[...], HIGHEST)
    softplus_neg = jnp.maximum(-ww, 0.0) + jnp.log1p(jnp.exp(-jnp.abs(ww)))
    log_w = -jnp.exp(-softplus_neg - 0.5)
    a = _sigmoid(a0_ref[...] + _dot(a_lo, aup_ref[...], HIGHEST))
    g = _dot(_sigmoid(g_lo), gup_ref[...], HIGHEST)
    head_sum = _head_block_ones(w, 1.0)
    kn = k * kk_ref[...]
    kn = kn * lax.rsqrt(_dot(kn * kn, head_sum, HIGHEST) + 1e-12)
    k2 = k * (1.0 + (a - 1.0) * ka_ref[...])
    bonus = _dot(r * k2 * rk_ref[...], head_sum, HIGHEST) * v
    r_o[0] = r
    lw_o[0] = log_w
    k_o[0] = k2
    v_o[0] = v
    kn_o[0] = kn
    b_o[0] = kn * a
    g_o[0] = g
    bonus_o[0] = bonus


def _rwkv_pre(p3, mu, w0, w_up, a0, a_up, g_up, k_k, k_a, r_k, tt=256):
    b, s, _ = p3.shape
    w = RWKV_W
    row = lambda x: x.reshape(1, -1)
    full = lambda shape: pl.BlockSpec(shape, lambda bi, t: (0,) * len(shape))
    out = jax.ShapeDtypeStruct((b, s, w), F32)
    return pl.pallas_call(
        functools.partial(_rwkv_pre_kernel, tt=tt),
        grid=(b, s // tt),
        in_specs=[pl.BlockSpec((1, tt, 1024), lambda bi, t: (bi, t, COL_RWKV // 1024)),
                  full((1, 1024)), full((1, w)), full((64, w)), full((1, w)), full((64, w)),
                  full((128, w)), full((1, w)), full((1, w)), full((1, w))],
        out_specs=[pl.BlockSpec((1, tt, w), lambda bi, t: (bi, t, 0))] * 8,
        out_shape=[out] * 8,
        scratch_shapes=[pltpu.VMEM((1, 1024), F32)],
        compiler_params=_cparams(("parallel", "arbitrary")),
        name="rwkv_pre",
    )(p3, row(mu), row(w0), w_up, row(a0), a_up, g_up, row(k_k), row(k_a), row(r_k))


def _bdot(a, b):
    return _dot(a.astype(BF16), b.astype(BF16))


def _bnt(a, b):
    return _nt(a.astype(BF16), b.astype(BF16))


def _btn(a, b):
    return _tn(a.astype(BF16), b.astype(BF16))


def _rwkv_chunk_kernel(r_ref, lw_ref, k_ref, v_ref, kn_ref, b_ref, y_ref, s_ref, *, cs, nch):
    c = pl.program_id(1)

    @pl.when(c == 0)
    def _():
        s_ref[...] = jnp.zeros_like(s_ref)

    rows = cs * nch
    big_row = lax.broadcasted_iota(jnp.int32, (rows, rows), 0)
    big_col = lax.broadcasted_iota(jnp.int32, (rows, rows), 1)
    same_chunk = (big_row >> CHUNK_SHIFT) == (big_col >> CHUNK_SHIFT)
    tri = jnp.where(same_chunk, jnp.where(big_col <= big_row, 1.0, 0.0), 0.0).astype(F32)
    row = lax.broadcasted_iota(jnp.int32, (cs, cs), 0)
    col = lax.broadcasted_iota(jnp.int32, (cs, cs), 1)
    incl = col <= row
    strict = col < row
    eye = jnp.where(col == row, 1.0, 0.0).astype(F32)
    lw = lw_ref[0]
    cum = _dot(tri, lw, HIGHEST)
    e_incl = jnp.exp(cum)
    e_neg = jnp.exp(-cum)
    abar_all = -kn_ref[0] * jnp.exp(cum - lw)
    rbar_all = r_ref[0] * e_incl
    bt_all = b_ref[0] * e_neg
    kt_all = k_ref[0] * e_neg
    v_all = v_ref[0]
    n_double = max(cs.bit_length() - 2, 0)
    probs = [(j, h) for j in range(nch) for h in range(RWKV_HEADS)]
    cut = lambda t, jh: t[jh[0] * cs:(jh[0] + 1) * cs, jh[1] * HEAD_DIM:(jh[1] + 1) * HEAD_DIM]
    each = lambda fn: {jh: fn(jh) for jh in probs}
    abar, rbar = each(lambda jh: cut(abar_all, jh)), each(lambda jh: cut(rbar_all, jh))
    bt, kt, v = each(lambda jh: cut(bt_all, jh)), each(lambda jh: cut(kt_all, jh)), each(lambda jh: cut(v_all, jh))
    p_last = each(lambda jh: e_incl[(jh[0] + 1) * cs - 1:(jh[0] + 1) * cs, jh[1] * HEAD_DIM:(jh[1] + 1) * HEAD_DIM])
    ar = each(lambda jh: jnp.concatenate([abar[jh], rbar[jh]], axis=0))
    ar_b = each(lambda jh: _bnt(ar[jh], bt[jh]))
    ar_k = each(lambda jh: _bnt(ar[jh], kt[jh]))
    a_ab = each(lambda jh: jnp.where(strict, ar_b[jh][:cs], 0.0))
    a_ak = each(lambda jh: jnp.where(strict, ar_k[jh][:cs], 0.0))
    a_rb = each(lambda jh: jnp.where(incl, ar_b[jh][cs:], 0.0))
    a_rk = each(lambda jh: jnp.where(incl, ar_k[jh][cs:], 0.0))
    ak_v = each(lambda jh: _bdot(a_ak[jh], v[jh]))
    inv = each(lambda jh: eye + a_ab[jh])
    power = a_ab
    for _ in range(n_double):
        power = each(lambda jh: _bdot(power[jh], power[jh]))
        inv = each(lambda jh: inv[jh] + _bdot(inv[jh], power[jh]))
    a_hat = each(lambda jh: _bdot(inv[jh], abar[jh]))
    u_hat = each(lambda jh: _bdot(inv[jh], ak_v[jh]))
    r_hat = each(lambda jh: rbar[jh] + _bdot(a_rb[jh], a_hat[jh]))
    y_hat = each(lambda jh: _bdot(a_rb[jh], u_hat[jh]) + _bdot(a_rk[jh], v[jh]))
    g_mat = each(lambda jh: eye * p_last[jh] + _btn(a_hat[jh], bt[jh] * p_last[jh]))
    h_mat = each(lambda jh: _btn(jnp.concatenate([u_hat[jh], v[jh]], axis=0),
                                 jnp.concatenate([bt[jh], kt[jh]], axis=0) * p_last[jh]))
    states = [s_ref[h] for h in range(RWKV_HEADS)]
    for j in range(nch):
        for h in range(RWKV_HEADS):
            y_ref[0, j * cs:(j + 1) * cs, h * HEAD_DIM:(h + 1) * HEAD_DIM] = (
                _bnt(r_hat[j, h], states[h]) + y_hat[j, h])
        states = [_dot(states[h], g_mat[j, h], HIGHEST) + h_mat[j, h] for h in range(RWKV_HEADS)]
    for h in range(RWKV_HEADS):
        s_ref[h] = states[h]


def _rwkv_chunk(r, lw, k, v, kn, b_, cs=CHUNK, nch=4):
    b, s, w = r.shape
    spec = pl.BlockSpec((1, cs * nch, w), lambda bi, c: (bi, c, 0))
    return pl.pallas_call(
        functools.partial(_rwkv_chunk_kernel, cs=cs, nch=nch),
        grid=(b, s // (cs * nch)),
        in_specs=[spec] * 6,
        out_specs=spec,
        out_shape=jax.ShapeDtypeStruct((b, s, w), F32),
        scratch_shapes=[pltpu.VMEM((RWKV_HEADS, HEAD_DIM, HEAD_DIM), F32)],
        compiler_params=_cparams(("parallel", "arbitrary")),
        name="rwkv_chunk",
    )(r, lw, k, v, kn, b_)


def _merge_kernel(x_ref, xb_ref, swa_ref, y_ref, g_ref, bonus_ref, fox_ref, dsa_ref, gng_ref, gnb_ref,
                  gbias_ref, wg_ref, wb_ref, wo_ref, lng_ref, lnb_ref, o_ref, ob_ref):
    y = y_ref[...]
    xb = xb_ref[...]
    head_mean = _head_block_ones(RWKV_W, 1.0 / HEAD_DIM)
    yc = y - _dot(y, head_mean, HIGHEST)
    yv = _dot(yc * yc, head_mean, HIGHEST)
    o_rwkv = (yc * lax.rsqrt(yv + RWKV_GN_EPS) * gng_ref[...] + gnb_ref[...] + bonus_ref[...]) * g_ref[...]
    branches = (swa_ref[...], o_rwkv, fox_ref[...], dsa_ref[...])
    merged = jnp.zeros(o_ref.shape, F32)
    off = 0
    for i, o in enumerate(branches):
        width = MIX_WIDTHS[i]
        proj = _dot(o.astype(BF16), wb_ref[off:off + width, :])
        gate = _sigmoid(_dot(xb, wg_ref[:, i * D_MODEL:(i + 1) * D_MODEL]) + gbias_ref[i:i + 1, :])
        merged = merged + gate * proj
        off += width
    y_out = _dot(merged.astype(BF16), wo_ref[...])
    xn = _layer_norm(DN_ALPHA * x_ref[...] + y_out, lng_ref[...], lnb_ref[...])
    o_ref[...] = xn
    ob_ref[...] = xn.astype(BF16)


def _merge(xf, xb, o_swa, y, g, bonus, o_fox, o_dsa, gn_g, gn_b, gate_bias, w_gate, w_branch, w_out, ln_g, ln_b,
           tm=512):
    n, d = xf.shape
    row = lambda x: x.reshape(1, -1)
    tok = lambda wdt: pl.BlockSpec((tm, wdt), lambda i: (i, 0))
    full = lambda shape: pl.BlockSpec(shape, lambda i: (0,) * len(shape))
    return pl.pallas_call(
        _merge_kernel,
        grid=(n // tm,),
        in_specs=[tok(d), tok(d), tok(512), tok(256), tok(256), tok(256), tok(256), tok(256),
                  full((1, 256)), full((1, 256)), full((N_BRANCHES, d)), full((d, N_BRANCHES * d)),
                  full((sum(MIX_WIDTHS), d)), full((d, d)), full((1, d)), full((1, d))],
        out_specs=[tok(d), tok(d)],
        out_shape=[jax.ShapeDtypeStruct((n, d), F32), jax.ShapeDtypeStruct((n, d), BF16)],
        compiler_params=_cparams(("parallel",)),
        name="merge",
    )(xf, xb, o_swa, y, g, bonus, o_fox, o_dsa, row(gn_g), row(gn_b), gate_bias, w_gate, w_branch, w_out,
      row(ln_g), row(ln_b))


def _silu(x):
    return x * _sigmoid(x)


def _ffn_kernel(xb_ref, x_ref, w1_ref, w3_ref, w2_ref, lng_ref, lnb_ref, o_ref, ob_ref, acc_ref):
    f = pl.program_id(1)

    @pl.when(f == 0)
    def _():
        acc_ref[...] = jnp.zeros_like(acc_ref)

    xb = xb_ref[...]
    hidden = _silu(_dot(xb, w1_ref[...])) * _dot(xb, w3_ref[...])
    acc_ref[...] += _dot(hidden.astype(BF16), w2_ref[...])

    @pl.when(f == pl.num_programs(1) - 1)
    def _():
        xn = _layer_norm(DN_ALPHA * x_ref[...] + acc_ref[...], lng_ref[...], lnb_ref[...])
        o_ref[...] = xn
        ob_ref[...] = xn.astype(BF16)


def _ffn(xb, xf, w1, w3, w2, ln_g, ln_b, tm=512, tf=1408):
    n, d = xf.shape
    ff = w1.shape[1]
    row = lambda x: x.reshape(1, -1)
    return pl.pallas_call(
        _ffn_kernel,
        grid=(n // tm, ff // tf),
        in_specs=[pl.BlockSpec((tm, d), lambda i, f: (i, 0)),
                  pl.BlockSpec((tm, d), lambda i, f: (i, 0)),
                  pl.BlockSpec((d, tf), lambda i, f: (0, f)),
                  pl.BlockSpec((d, tf), lambda i, f: (0, f)),
                  pl.BlockSpec((tf, d), lambda i, f: (f, 0)),
                  pl.BlockSpec((1, d), lambda i, f: (0, 0)),
                  pl.BlockSpec((1, d), lambda i, f: (0, 0))],
        out_specs=[pl.BlockSpec((tm, d), lambda i, f: (i, 0))] * 2,
        out_shape=[jax.ShapeDtypeStruct((n, d), F32), jax.ShapeDtypeStruct((n, d), BF16)],
        scratch_shapes=[pltpu.VMEM((tm, d), F32)],
        compiler_params=_cparams(("parallel", "arbitrary")),
        name="ffn",
    )(xb, xf, w1, w3, w2, row(ln_g), row(ln_b))


def _moe_kernel(xb_ref, x_ref, rw_ref, rb_ref, w1_ref, w3_ref, w2_ref, lng_ref, lnb_ref, o_ref,
                comb_ref, rank_ref, cnt_ref, *, sb):
    e = pl.program_id(1)
    tm = x_ref.shape[0]
    lane = lax.broadcasted_iota(jnp.int32, (tm, 128), 1).astype(F32)

    @pl.when(e == 0)
    def _():
        o_ref[...] = jnp.zeros_like(o_ref)
        logits = _dot(x_ref[...], rw_ref[...], HIGHEST) + rb_ref[...]
        logits = jnp.where(lane < N_EXPERTS, logits, NEG_BIG)
        m1 = jnp.max(logits, axis=-1, keepdims=True)
        i1 = jnp.min(jnp.where(logits == m1, lane, 128.0), axis=-1, keepdims=True)
        rest = jnp.where(lane == i1, NEG_BIG, logits)
        m2 = jnp.max(rest, axis=-1, keepdims=True)
        i2 = jnp.min(jnp.where(rest == m2, lane, 128.0), axis=-1, keepdims=True)
        e2 = jnp.exp(m2 - m1)
        comb_ref[...] = jnp.where(lane == i1, 1.0 / (1.0 + e2), 0.0) + jnp.where(lane == i2, e2 / (1.0 + e2), 0.0)
        chosen = jnp.where(lane == i1, 1.0, 0.0) + jnp.where(lane == i2, 1.0, 0.0)
        r = lax.broadcasted_iota(jnp.int32, (tm, tm), 0)
        c = lax.broadcasted_iota(jnp.int32, (tm, tm), 1)
        earlier = jnp.where(c < r, 1.0, 0.0).astype(BF16)
        rank = jnp.where(chosen > 0.5, _dot(earlier, chosen.astype(BF16)), -1.0)
        lane8 = lax.broadcasted_iota(jnp.int32, (8, 128), 1)
        row8 = lax.broadcasted_iota(jnp.int32, (8, 128), 0)
        rank_ref[...] = _nt(jnp.where(lane8 == row8, 1.0, 0.0).astype(F32), rank, HIGHEST)
        cnt_ref[...] = jnp.sum(chosen, axis=0, keepdims=True)

    e_f = e.astype(F32)
    n_tok = jnp.sum(jnp.where(lane[0:1, :] == e_f, cnt_ref[...], 0.0)).astype(jnp.int32)
    n_blocks = lax.div(n_tok + sb - 1, sb)
    rank_row = rank_ref[pl.ds(e, 1), :]
    weight = jnp.sum(jnp.where(lane == e_f, comb_ref[...], 0.0), axis=-1, keepdims=True)
    slot0 = lax.broadcasted_iota(jnp.int32, (sb, tm), 0).astype(F32)

    def block(s, carry):
        pick = jnp.where(rank_row == slot0 + (s * sb).astype(F32), 1.0, 0.0).astype(BF16)
        xs = _dot(pick, xb_ref[...]).astype(BF16)
        hidden = _silu(_dot(xs, w1_ref[0])) * _dot(xs, w3_ref[0])
        out = _dot(hidden.astype(BF16), w2_ref[0]).astype(BF16)
        o_ref[...] += _tn(pick, out) * weight
        return carry

    lax.fori_loop(0, n_blocks, block, 0)

    @pl.when(e == pl.num_programs(1) - 1)
    def _():
        o_ref[...] = _layer_norm(DN_ALPHA * x_ref[...] + o_ref[...], lng_ref[...], lnb_ref[...])


def _moe(xb, xf, router_w, router_b, w1, w3, w2, ln_g, ln_b, tm=1024, sb=288):
    n, d = xf.shape
    ne, _, fe = w1.shape
    tm = min(tm, n)
    rw = jnp.pad(router_w, ((0, 0), (0, 128 - ne)))
    rb = jnp.pad(router_b, (0, 128 - ne)).reshape(1, 128)
    row = lambda x: x.reshape(1, -1)
    return pl.pallas_call(
        functools.partial(_moe_kernel, sb=sb),
        grid=(n // tm, ne),
        in_specs=[pl.BlockSpec((tm, d), lambda i, e: (i, 0)),
                  pl.BlockSpec((tm, d), lambda i, e: (i, 0)),
                  pl.BlockSpec((d, 128), lambda i, e: (0, 0)),
                  pl.BlockSpec((1, 128), lambda i, e: (0, 0)),
                  pl.BlockSpec((1, d, fe), lambda i, e: (e, 0, 0)),
                  pl.BlockSpec((1, d, fe), lambda i, e: (e, 0, 0)),
                  pl.BlockSpec((1, fe, d), lambda i, e: (e, 0, 0)),
                  pl.BlockSpec((1, d), lambda i, e: (0, 0)),
                  pl.BlockSpec((1, d), lambda i, e: (0, 0))],
        out_specs=pl.BlockSpec((tm, d), lambda i, e: (i, 0)),
        out_shape=jax.ShapeDtypeStruct((n, d), F32),
        scratch_shapes=[pltpu.VMEM((tm, 128), F32), pltpu.VMEM((8, tm), F32), pltpu.VMEM((1, 128), F32)],
        compiler_params=pltpu.CompilerParams(dimension_semantics=("parallel", "arbitrary"),
                                             vmem_limit_bytes=MOE_VMEM_LIMIT),
        name="moe",
    )(xb, xf, rw, rb, w1, w3, w2, row(ln_g), row(ln_b))


def _mixer_layer(xf, xb, bsz, seq, w_in, sinks, mu, w0, w_up, a0, a_up, g_up, k_k, k_a, r_k, gn_g, gn_b,
                 f_bias, gate_bias, w_branch, w_out, ln_g, ln_b):
    w_mix, w_gate = _layout_w_in(w_in.astype(BF16))
    p2 = _matmul(xb, w_mix, tm=1024, tn=512)
    p3 = p2.reshape(bsz, seq, P_WIDTH)
    swa_kv_w = SWA_KV_HEADS * HEAD_DIM
    swa_v_t = jnp.swapaxes(p3[:, :, COL_SWA_KV + swa_kv_w:COL_SWA_KV + 2 * swa_kv_w], 1, 2)
    o_swa = _swa(p3, swa_v_t, sinks)
    fox_w = FOX_HEADS * HEAD_DIM
    fox_v_t = jnp.swapaxes(p3[:, :, COL_FOX_KV + fox_w:COL_FOX_KV + 2 * fox_w], 1, 2)
    o_fox = _fox(p3, fox_v_t, _fox_c(p3, f_bias))
    v_t = jnp.swapaxes(p3[:, :, COL_DSA_KV + HEAD_DIM:COL_DSA_KV + 2 * HEAD_DIM], 1, 2)
    o_dsa = _dsa(p3, v_t)
    r, lw, k2, v, kn, b_, g, bonus = _rwkv_pre(p3, mu, w0, w_up, a0, a_up, g_up, k_k, k_a, r_k)
    y = _rwkv_chunk(r, lw, k2, v, kn, b_)
    n = bsz * seq
    flat = lambda t: t.reshape(n, t.shape[-1])
    return _merge(xf, xb, flat(o_swa), flat(y), flat(g), flat(bonus), flat(o_fox), flat(o_dsa), gn_g, gn_b,
                  gate_bias, w_gate, w_branch.astype(BF16), w_out.astype(BF16), ln_g, ln_b)


def kernel(x, w_in, swa_sinks, rwkv_mu, rwkv_w0, rwkv_w_up, rwkv_a0, rwkv_a_up, rwkv_g_up, rwkv_k_k, rwkv_k_a,
           rwkv_r_k, rwkv_gn_g, rwkv_gn_b, fox_f_bias, gate_bias, w_branch, w_out, ln_g, ln_b, ffn_w1, ffn_w3,
           ffn_w2, router_w, router_b, exp_w1, exp_w3, exp_w2):
    bsz, seq, d = x.shape
    xf = x.reshape(bsz * seq, d)
    xb = xf.astype(BF16)
    for layer in range(DEPTH):
        xf, xb = _mixer_layer(xf, xb, bsz, seq, w_in[layer], swa_sinks[layer], rwkv_mu[layer], rwkv_w0[layer],
                              rwkv_w_up[layer], rwkv_a0[layer], rwkv_a_up[layer], rwkv_g_up[layer],
                              rwkv_k_k[layer], rwkv_k_a[layer], rwkv_r_k[layer], rwkv_gn_g[layer],
                              rwkv_gn_b[layer], fox_f_bias[layer], gate_bias[layer], w_branch[layer],
                              w_out[layer], ln_g[layer, 0], ln_b[layer, 0])
        j = layer // 2
        if layer % 2 == 0:
            xf, xb = _ffn(xb, xf, ffn_w1[j].astype(BF16), ffn_w3[j].astype(BF16), ffn_w2[j].astype(BF16),
                          ln_g[layer, 1], ln_b[layer, 1])
        else:
            xf = _moe(xb, xf, router_w[j], router_b[j], exp_w1[j].astype(BF16), exp_w3[j].astype(BF16),
                      exp_w2[j].astype(BF16), ln_g[layer, 1], ln_b[layer, 1])
            xb = xf.astype(BF16) if layer + 1 < DEPTH else None
    return xf.reshape(bsz, seq, d)
```

```python
import functools

import jax
import jax.numpy as jnp
from jax import lax
from jax.experimental import pallas as pl
from jax.experimental.pallas import tpu as pltpu

F32 = jnp.float32
BF16 = jnp.bfloat16
HIGHEST = lax.Precision.HIGHEST

D_MODEL = 1024
DEPTH = 2
CHUNK = 64
HEAD_DIM = 64
CHUNK_SHIFT = 6
HEAD_SHIFT = 6
SWA_HEADS = 8
SWA_KV_HEADS = 2
SWA_GROUP = SWA_HEADS // SWA_KV_HEADS
SWA_WINDOW = 128
SWA_WIN_CHUNKS = SWA_WINDOW // CHUNK
RWKV_HEADS = 4
RWKV_W = RWKV_HEADS * HEAD_DIM
RWKV_GN_EPS = 64e-5
FOX_HEADS = 4
DSA_HEADS = 4
IDX_HEADS = 4
IDX_DIM = 64
DSA_TOPK_MAX = 256
D_FF = 2816
N_EXPERTS = 8
D_FF_EXPERT = 1408
N_BRANCHES = 4
DN_ALPHA = (2 * DEPTH) ** 0.25
LN_EPS = 1e-5
ATTN_SCALE = HEAD_DIM ** -0.5
IDX_SCALE = IDX_DIM ** -0.5
IDX_W_SCALE = IDX_HEADS ** -0.5
MIX_WIDTHS = (SWA_HEADS * HEAD_DIM, RWKV_W, FOX_HEADS * HEAD_DIM, DSA_HEADS * HEAD_DIM)

COL_RWKV = 0
COL_SWA_Q = 1024
COL_FOX_KV = 1536
COL_DSA_QQ = 2048
COL_FOX_Q = 2560
COL_SWA_KV = 2816
COL_DSA_KV = 3072
COL_MISC = 3328
P_WIDTH = 3584

NEG_BIG = -1e30
INT_MIN = -(2 ** 31)
VMEM_LIMIT = 48 * 1024 * 1024
MOE_VMEM_LIMIT = 58 * 1024 * 1024


IN_OFF_SWA, IN_OFF_RWKV, IN_OFF_FOX, IN_OFF_DSA, IN_OFF_GATE = 0, 768, 1792, 2564, 3272
D_IN = IN_OFF_GATE + N_BRANCHES * D_MODEL


def _layout_w_in(w):
    d = w.shape[0]
    z = lambda n: jnp.zeros((d, n), w.dtype)
    swa, rwkv, fox, dsa = IN_OFF_SWA, IN_OFF_RWKV, IN_OFF_FOX, IN_OFF_DSA
    parts = [
        w[:, rwkv:rwkv + 1024],
        w[:, swa:swa + 512],
        w[:, fox + 256:fox + 768],
        w[:, dsa:dsa + 256], w[:, dsa + 384:dsa + 640],
        w[:, fox:fox + 256],
        w[:, swa + 512:swa + 768],
        w[:, dsa + 256:dsa + 384], w[:, dsa + 640:dsa + 704], z(64),
        w[:, fox + 768:fox + 772], w[:, dsa + 704:dsa + 708], z(120),
        z(128),
    ]
    return jnp.concatenate(parts, axis=1)


def _cparams(sem):
    return pltpu.CompilerParams(dimension_semantics=sem, vmem_limit_bytes=VMEM_LIMIT)


def _gate_cols_kernel(a_ref, b_ref, o_ref, *, shift):
    both = jnp.concatenate([a_ref[0], b_ref[0]], axis=1)
    o_ref[0] = both[:, shift:shift + 128].astype(BF16)


def _gate_weights(w_in):
    n_layers, d, d_in = w_in.shape
    first, shift = divmod(IN_OFF_GATE, 128)
    last = (d_in - 1) // 128
    n_out = N_BRANCHES * D_MODEL
    return pl.pallas_call(
        functools.partial(_gate_cols_kernel, shift=shift),
        grid=(n_layers, n_out // 128),
        in_specs=[pl.BlockSpec((1, d, 128), lambda l, j: (l, 0, first + j)),
                  pl.BlockSpec((1, d, 128), lambda l, j: (l, 0, jnp.minimum(first + j + 1, last)))],
        out_specs=pl.BlockSpec((1, d, 128), lambda l, j: (l, 0, j)),
        out_shape=jax.ShapeDtypeStruct((n_layers, d, n_out), BF16),
        compiler_params=_cparams(("parallel", "parallel")),
        name="gate_weights",
    )(w_in, w_in)


def _nt(a, b, precision=None):
    return lax.dot_general(a, b, (((1,), (1,)), ((), ())), precision=precision,
                           preferred_element_type=F32)


def _tn(a, b, precision=None):
    return lax.dot_general(a, b, (((0,), (0,)), ((), ())), precision=precision,
                           preferred_element_type=F32)


def _dot(a, b, precision=None):
    return jnp.dot(a, b, precision=precision, preferred_element_type=F32)


def _sigmoid(x):
    return 1.0 / (1.0 + jnp.exp(-x))


def _layer_norm(z, g, b):
    mu = jnp.mean(z, axis=-1, keepdims=True)
    zc = z - mu
    var = jnp.mean(zc * zc, axis=-1, keepdims=True)
    return zc * lax.rsqrt(var + LN_EPS) * g + b


def _skewed(n, first, second):
    out = []
    staged = first(0)
    for h in range(n):
        nxt = first(h + 1) if h + 1 < n else None
        out.append(second(h, staged))
        staged = nxt
    return out


def _paired_blocks(n, body, carry):
    carry = lax.fori_loop(0, lax.div(n, 2), lambda t, c: body((2 * t, 2 * t + 1), c), carry)
    return lax.cond(lax.rem(n, 2) == 1, lambda c: body((n - 1,), c), lambda c: c, carry)


def _head_block_ones(n, scale):
    r = lax.broadcasted_iota(jnp.int32, (n, n), 0) >> HEAD_SHIFT
    c = lax.broadcasted_iota(jnp.int32, (n, n), 1) >> HEAD_SHIFT
    return jnp.where(r == c, scale, 0.0).astype(F32)


def _mm_kernel(a_ref, b_ref, o_ref):
    o_ref[...] = _dot(a_ref[...], b_ref[...])


def _matmul(a, b, tm, tn):
    m, k = a.shape
    n = b.shape[1]
    tm = min(tm, m)
    return pl.pallas_call(
        _mm_kernel,
        grid=(m // tm, n // tn),
        in_specs=[pl.BlockSpec((tm, k), lambda i, j: (i, 0)),
                  pl.BlockSpec((k, tn), lambda i, j: (0, j))],
        out_specs=pl.BlockSpec((tm, tn), lambda i, j: (i, j)),
        out_shape=jax.ShapeDtypeStruct((m, n), F32),
        compiler_params=_cparams(("parallel", "parallel")),
        name="in_proj",
    )(a, b)


def _swa_kernel(sink_ref, q_ref, kv_ref, vt_ref, o_ref, *, tq):
    i = pl.program_id(1)
    win = tq + SWA_WINDOW
    s0 = pl.multiple_of(jnp.maximum(i * tq - SWA_WINDOW, 0), SWA_WINDOW)
    k_chunk = (s0 + lax.broadcasted_iota(jnp.int32, (win, tq), 0)) >> CHUNK_SHIFT
    q_chunk = (i * tq + lax.broadcasted_iota(jnp.int32, (win, tq), 1)) >> CHUNK_SHIFT
    bias = jnp.where(k_chunk <= q_chunk,
                     jnp.where(k_chunk >= q_chunk - SWA_WIN_CHUNKS, 0.0, NEG_BIG), NEG_BIG)
    ks = [kv_ref[0, pl.ds(s0, win), hk * HEAD_DIM:(hk + 1) * HEAD_DIM].astype(BF16)
          for hk in range(SWA_KV_HEADS)]
    v_ts = [vt_ref[0, hk * HEAD_DIM:(hk + 1) * HEAD_DIM, pl.ds(s0, win)].astype(BF16)
            for hk in range(SWA_KV_HEADS)]

    def logits(h):
        q = (q_ref[0, :, h * HEAD_DIM:(h + 1) * HEAD_DIM] * ATTN_SCALE).astype(BF16)
        return _nt(ks[h // SWA_GROUP], q) + bias

    def finish(h, s):
        sink = sink_ref[h]
        m = jnp.maximum(jnp.max(s, axis=0, keepdims=True), sink)
        e = jnp.exp(s - m)
        denom = jnp.sum(e, axis=0, keepdims=True) + jnp.exp(sink - m)
        return _dot(v_ts[h // SWA_GROUP], e.astype(BF16)) / denom

    o_ref[0] = jnp.concatenate(_skewed(SWA_HEADS, logits, finish), axis=0).T


def _swa(p3, v_t, sinks, tq=256):
    b, s, _ = p3.shape
    kv_w = SWA_KV_HEADS * HEAD_DIM
    return pl.pallas_call(
        functools.partial(_swa_kernel, tq=tq),
        grid=(b, s // tq),
        in_specs=[pl.BlockSpec(memory_space=pltpu.SMEM),
                  pl.BlockSpec((1, tq, 512), lambda bi, i: (bi, i, COL_SWA_Q // 512)),
                  pl.BlockSpec((1, s, kv_w), lambda bi, i: (bi, 0, COL_SWA_KV // kv_w)),
                  pl.BlockSpec((1, kv_w, s), lambda bi, i: (bi, 0, 0))],
        out_specs=pl.BlockSpec((1, tq, 512), lambda bi, i: (bi, i, 0)),
        out_shape=jax.ShapeDtypeStruct((b, s, 512), F32),
        compiler_params=_cparams(("parallel", "parallel")),
        name="swa",
    )(sinks, p3, p3, v_t)


def _fox_c_kernel(f_ref, bias_ref, c_ref, *, blk):
    s = f_ref.shape[1]
    r = lax.broadcasted_iota(jnp.int32, (blk, blk), 0)
    c = lax.broadcasted_iota(jnp.int32, (blk, blk), 1)
    tri = jnp.where(c <= r, 1.0, 0.0).astype(F32)
    carry = jnp.zeros((1, 128), F32)
    for j in range(s // blk):
        x = f_ref[0, j * blk:(j + 1) * blk, :] + bias_ref[...]
        log_f = jnp.minimum(x, 0.0) - jnp.log1p(jnp.exp(-jnp.abs(x)))
        cs = _dot(tri, log_f, HIGHEST) + carry
        carry = cs[blk - 1:blk, :]
        for h in range(FOX_HEADS):
            c_ref[0, h, j * blk:(j + 1) * blk, :] = jnp.broadcast_to(cs[:, h:h + 1], (blk, 128))


def _fox_c(p3, f_bias):
    b, s, _ = p3.shape
    bias_row = jnp.pad(f_bias, (0, 128 - FOX_HEADS)).reshape(1, 128)
    return pl.pallas_call(
        functools.partial(_fox_c_kernel, blk=256),
        grid=(b,),
        in_specs=[pl.BlockSpec((1, s, 128), lambda bi: (bi, 0, COL_MISC // 128)),
                  pl.BlockSpec((1, 128), lambda bi: (0, 0))],
        out_specs=pl.BlockSpec((1, FOX_HEADS, s, 128), lambda bi: (bi, 0, 0, 0)),
        out_shape=jax.ShapeDtypeStruct((b, FOX_HEADS, s, 128), F32),
        compiler_params=_cparams(("parallel",)),
        name="fox_cumsum",
    )(p3, bias_row)


def _fox_kernel(q_ref, k_ref, vt_ref, c_ref, o_ref, lg_ref, acc_ref, *, tq):
    i = pl.program_id(1)
    tk = tq
    key_i = lax.broadcasted_iota(jnp.int32, (tk, tq), 0)
    qry_i = lax.broadcasted_iota(jnp.int32, (tk, tq), 1)
    diag_bias = jnp.where(key_i <= qry_i, 0.0, NEG_BIG)
    q = [(q_ref[0, :, h * HEAD_DIM:(h + 1) * HEAD_DIM] * ATTN_SCALE).astype(BF16) for h in range(FOX_HEADS)]

    def logits_blocks(blocks, maxes, masked=False):
        starts = [pl.multiple_of(j * tk, tk) for j in blocks]
        dots = [[_nt(k_ref[0, pl.ds(ks, tk), h * HEAD_DIM:(h + 1) * HEAD_DIM].astype(BF16), q[h])
                 for h in range(FOX_HEADS)] for ks in starts]
        maxes = list(maxes)
        for b, ks in enumerate(starts):
            for h in range(FOX_HEADS):
                c_k = c_ref[0, h, pl.ds(ks, tk), :]
                s = dots[b][h] - jnp.concatenate([c_k] * (tq // 128), axis=1)
                if masked:
                    s = s + diag_bias
                lg_ref[h, pl.ds(ks, tk), :] = s
                maxes[h] = jnp.maximum(maxes[h], jnp.max(s, axis=0, keepdims=True))
        return tuple(maxes)

    maxes = tuple(jnp.full((1, tq), NEG_BIG, F32) for _ in range(FOX_HEADS))
    maxes = _paired_blocks(i, logits_blocks, maxes)
    maxes = logits_blocks((i,), maxes, masked=True)
    acc_ref[...] = jnp.zeros_like(acc_ref)

    def attend_blocks(blocks, sums):
        starts = [pl.multiple_of(j * tk, tk) for j in blocks]
        sums = list(sums)
        for h in range(FOX_HEADS):
            update = None
            for ks in starts:
                p = jnp.exp(lg_ref[h, pl.ds(ks, tk), :] - maxes[h])
                v_t = vt_ref[0, h * HEAD_DIM:(h + 1) * HEAD_DIM, pl.ds(ks, tk)].astype(BF16)
                pv = _dot(v_t, p.astype(BF16))
                update = pv if update is None else update + pv
                sums[h] = sums[h] + jnp.sum(p, axis=0, keepdims=True)
            acc_ref[h] += update
        return tuple(sums)

    sums = _paired_blocks(i + 1, attend_blocks, tuple(jnp.zeros((1, tq), F32) for _ in range(FOX_HEADS)))
    out_t = jnp.concatenate([acc_ref[h] / sums[h] for h in range(FOX_HEADS)], axis=0)
    o_ref[0] = out_t.T


def _fox(p3, v_t, c, tq=256):
    b, s, _ = p3.shape
    w = FOX_HEADS * HEAD_DIM
    return pl.pallas_call(
        functools.partial(_fox_kernel, tq=tq),
        grid=(b, s // tq),
        in_specs=[pl.BlockSpec((1, tq, w), lambda bi, i: (bi, i, COL_FOX_Q // w)),
                  pl.BlockSpec((1, s, w), lambda bi, i: (bi, 0, COL_FOX_KV // w)),
                  pl.BlockSpec((1, w, s), lambda bi, i: (bi, 0, 0)),
                  pl.BlockSpec((1, FOX_HEADS, s, 128), lambda bi, i: (bi, 0, 0, 0))],
        out_specs=pl.BlockSpec((1, tq, w), lambda bi, i: (bi, i, 0)),
        out_shape=jax.ShapeDtypeStruct((b, s, w), F32),
        scratch_shapes=[pltpu.VMEM((FOX_HEADS, s, tq), F32), pltpu.VMEM((FOX_HEADS, HEAD_DIM, tq), F32)],
        compiler_params=_cparams(("parallel", "parallel")),
        name="fox",
    )(p3, p3, v_t, c)


def _key_to_score(key):
    return pltpu.bitcast(key ^ ((key >> 31) & 0x7FFFFFFF), F32)


def _bit_planes(score):
    bits = pltpu.bitcast(score, jnp.int32)
    image = bits ^ ((bits >> 31) | INT_MIN)
    rows = image.reshape(32, 8, image.shape[1])
    v = [rows[j] for j in range(32)]
    for d, mask in ((16, 0x0000FFFF), (8, 0x00FF00FF), (4, 0x0F0F0F0F), (2, 0x33333333), (1, 0x55555555)):
        for k in range(32):
            if k & d == 0:
                t = (v[k] ^ lax.shift_right_logical(v[k + d], d)) & mask
                v[k] = v[k] ^ t
                v[k + d] = v[k + d] ^ lax.shift_left(t, d)
    return v


def _dsa_kernel(qq_ref, misc_ref, kv_ref, vt_ref, o_ref, sc_ref, lg_ref, acc_ref, pl_ref, *, tq, kb, topk):
    i = pl.program_id(1)
    n_kb = lax.div((i + 1) * tq + kb - 1, kb)
    lane8 = lax.broadcasted_iota(jnp.int32, (8, 128), 1)
    row8 = lax.broadcasted_iota(jnp.int32, (8, 128), 0)
    pick = jnp.where((lane8 == row8 + 4) & (row8 < IDX_HEADS), 1.0, 0.0).astype(F32)
    w_t = _nt(pick, misc_ref[0], HIGHEST) * (IDX_W_SCALE * IDX_SCALE)
    q_chunk = (i * tq + lax.broadcasted_iota(jnp.int32, (1, tq), 1)) >> CHUNK_SHIFT
    qw = DSA_HEADS * HEAD_DIM
    q_idx = [qq_ref[0, :, qw + h * IDX_DIM:qw + (h + 1) * IDX_DIM].astype(BF16) for h in range(IDX_HEADS)]

    def score_blocks(blocks, carry):
        starts = [pl.multiple_of(j * kb, kb) for j in blocks]
        dots = []
        for ks in starts:
            k_idx = kv_ref[0, pl.ds(ks, kb), 2 * HEAD_DIM:2 * HEAD_DIM + IDX_DIM].astype(BF16)
            dots.append([_nt(k_idx, q_idx[h]) for h in range(IDX_HEADS)])
        for b, ks in enumerate(starts):
            score = jnp.maximum(dots[b][0], 0.0) * w_t[0:1, :]
            for h in range(1, IDX_HEADS):
                score = score + jnp.maximum(dots[b][h], 0.0) * w_t[h:h + 1, :]
            k_chunk = (ks + lax.broadcasted_iota(jnp.int32, (kb, tq), 0)) >> CHUNK_SHIFT
            score = jnp.where(k_chunk <= q_chunk, score, -jnp.inf)
            sc_ref[pl.ds(ks, kb), :] = score
            planes = _bit_planes(score)
            for t in range(32):
                pl_ref[blocks[b], t] = planes[t]
        return carry

    _paired_blocks(n_kb, score_blocks, 0)
    n_blocks = pl_ref.shape[0]

    def clear_planes(j, carry):
        pl_ref[j] = jnp.zeros(pl_ref.shape[1:], jnp.int32)
        return carry

    lax.fori_loop(n_kb, n_blocks, clear_planes, 0)

    def count(pred):
        def body(j, acc):
            blk = sc_ref[pl.ds(pl.multiple_of(j * kb, kb), kb), :]
            hit = jnp.where(pred(blk), 1, 0).astype(jnp.int32).reshape(kb // 8, 8, tq)
            parts = [hit[g] for g in range(kb // 8)]
            while len(parts) > 1:
                parts = [parts[g] + parts[g + 1] for g in range(0, len(parts), 2)]
            return acc + parts[0]
        acc = lax.fori_loop(0, n_kb, body, jnp.zeros((8, tq), jnp.int32))
        return jnp.sum(acc.astype(F32), axis=0, keepdims=True)

    def search_by_compares():
        zero = jnp.zeros((1, tq), F32)
        thr_key = jnp.where(count(lambda blk: blk >= zero) >= topk, 0, INT_MIN).astype(jnp.int32)

        def bit_step(bi, thr_key):
            cand = thr_key | lax.shift_left(jnp.int32(1), 30 - bi)
            cand_score = _key_to_score(cand)
            return jnp.where(count(lambda blk: blk >= cand_score) >= topk, cand, thr_key)

        thr_key = lax.fori_loop(0, 31, bit_step, thr_key)
        t = jnp.where(thr_key == INT_MIN, -jnp.inf, _key_to_score(thr_key))
        return t, count(lambda blk: blk > t)

    def search_by_planes():
        def bit_pass(t, carry):
            prefix, above, alive = carry
            ones = [alive[j] & pl_ref[j, t] for j in range(n_blocks)]
            parts = [lax.population_count(o) for o in ones]
            while len(parts) > 1:
                parts = [parts[g] + parts[g + 1] for g in range(0, len(parts), 2)]
            reach = above + jnp.sum(parts[0].astype(F32), axis=0, keepdims=True)
            take = reach >= topk
            prefix = jnp.where(take, prefix | lax.shift_left(jnp.int32(1), 31 - t), prefix)
            alive = tuple(jnp.where(take, ones[j], alive[j] ^ ones[j]) for j in range(n_blocks))
            return prefix, jnp.where(take, above, reach), alive

        init = (jnp.zeros((1, tq), jnp.int32), jnp.zeros((1, tq), F32),
                tuple(jnp.full((8, tq), -1, jnp.int32) for _ in range(n_blocks)))
        image, _, _ = lax.fori_loop(0, 32, bit_pass, init)
        return pltpu.bitcast(image ^ (~(image >> 31) | INT_MIN), F32)

    thr_planes = search_by_planes()
    n_above = count(lambda blk: blk > thr_planes)
    n_reach = count(lambda blk: blk >= thr_planes)
    confirmed = jnp.min(jnp.where(n_above < topk, jnp.where(n_reach >= topk, 1.0, 0.0), 0.0)) > 0.5
    thr, n_above = lax.cond(confirmed, lambda: (thr_planes, n_above), search_by_compares)
    need = topk - n_above

    r = lax.broadcasted_iota(jnp.int32, (kb, kb), 0)
    c = lax.broadcasted_iota(jnp.int32, (kb, kb), 1)
    earlier = jnp.where(c < r, 1.0, 0.0).astype(BF16)
    q_att = [(qq_ref[0, :, h * HEAD_DIM:(h + 1) * HEAD_DIM] * ATTN_SCALE).astype(BF16)
             for h in range(DSA_HEADS)]

    def logits_blocks(blocks, carry):
        ties_before, maxes = carry
        maxes = list(maxes)
        starts = [pl.multiple_of(j * kb, kb) for j in blocks]
        dots, ties = [], []
        for ks in starts:
            k = kv_ref[0, pl.ds(ks, kb), 0:HEAD_DIM].astype(BF16)
            dots.append([_nt(k, q_att[h]) for h in range(DSA_HEADS)])
            k_chunk = (ks + lax.broadcasted_iota(jnp.int32, (kb, tq), 0)) >> CHUNK_SHIFT
            ties.append(jnp.where(sc_ref[pl.ds(ks, kb), :] == thr, jnp.where(k_chunk <= q_chunk, 1.0, 0.0), 0.0))
        ranks = [_dot(earlier, tie.astype(BF16)) for tie in ties]
        for b, ks in enumerate(starts):
            tie_taken = jnp.where(ranks[b] + ties_before < need, ties[b], 0.0)
            bias = jnp.where(sc_ref[pl.ds(ks, kb), :] > thr, 0.0, jnp.where(tie_taken > 0.5, 0.0, NEG_BIG))
            ties_before = ties_before + jnp.sum(ties[b], axis=0, keepdims=True)
            for h in range(DSA_HEADS):
                s = dots[b][h] + bias
                lg_ref[h, pl.ds(ks, kb), :] = s
                maxes[h] = jnp.maximum(maxes[h], jnp.max(s, axis=0, keepdims=True))
        return ties_before, tuple(maxes)

    init = (jnp.zeros((1, tq), F32), tuple(jnp.full((1, tq), NEG_BIG, F32) for _ in range(DSA_HEADS)))
    _, maxes = _paired_blocks(n_kb, logits_blocks, init)

    acc_ref[...] = jnp.zeros_like(acc_ref)

    def attend_blocks(blocks, sums):
        starts = [pl.multiple_of(j * kb, kb) for j in blocks]
        v_ts = [vt_ref[0, :, pl.ds(ks, kb)].astype(BF16) for ks in starts]
        sums = list(sums)
        for h in range(DSA_HEADS):
            update = None
            for b, ks in enumerate(starts):
                p = jnp.exp(lg_ref[h, pl.ds(ks, kb), :] - maxes[h])
                pv = _dot(v_ts[b], p.astype(BF16))
                update = pv if update is None else update + pv
                sums[h] = sums[h] + jnp.sum(p, axis=0, keepdims=True)
            acc_ref[h] += update
        return tuple(sums)

    sums = _paired_blocks(n_kb, attend_blocks, tuple(jnp.zeros((1, tq), F32) for _ in range(DSA_HEADS)))
    out_t = jnp.concatenate([acc_ref[h] / sums[h] for h in range(DSA_HEADS)], axis=0)
    o_ref[0] = out_t.T


def _dsa(p3, v_t, tq=256, kb=256):
    b, s, _ = p3.shape
    assert kb == 8 * 32, "a key block is the 32 row groups of one bit-plane word"
    topk = min(DSA_TOPK_MAX, s // 4)
    return pl.pallas_call(
        functools.partial(_dsa_kernel, tq=tq, kb=kb, topk=topk),
        grid=(b, s // tq),
        in_specs=[pl.BlockSpec((1, tq, 512), lambda bi, i: (bi, i, COL_DSA_QQ // 512)),
                  pl.BlockSpec((1, tq, 128), lambda bi, i: (bi, i, COL_MISC // 128)),
                  pl.BlockSpec((1, s, 256), lambda bi, i: (bi, 0, COL_DSA_KV // 256)),
                  pl.BlockSpec((1, HEAD_DIM, s), lambda bi, i: (bi, 0, 0))],
        out_specs=pl.BlockSpec((1, tq, 256), lambda bi, i: (bi, i, 0)),
        out_shape=jax.ShapeDtypeStruct((b, s, 256), F32),
        scratch_shapes=[pltpu.VMEM((s, tq), F32), pltpu.VMEM((DSA_HEADS, s, tq), F32),
                        pltpu.VMEM((DSA_HEADS, HEAD_DIM, tq), F32), pltpu.VMEM((s // kb, 32, 8, tq), jnp.int32)],
        compiler_params=_cparams(("parallel", "parallel")),
        name="dsa",
    )(p3, p3, p3, v_t)


def _rwkv_pre_kernel(p_ref, mu_ref, w0_ref, wup_ref, a0_ref, aup_ref, gup_ref, kk_ref, ka_ref, rk_ref,
                     r_o, lw_o, k_o, v_o, kn_o, b_o, g_o, bonus_o, last_ref, *, tt):
    t = pl.program_id(1)

    @pl.when(t == 0)
    def _():
        last_ref[...] = jnp.zeros_like(last_ref)

    p = p_ref[0]
    row = lax.broadcasted_iota(jnp.int32, p.shape, 0)
    p_prev = jnp.where(row == 0, last_ref[...], pltpu.roll(p, 1, axis=0))
    last_ref[...] = p[tt - 1:tt, :]
    ps = p + mu_ref[...] * (p_prev - p)
    w = RWKV_W
    r, k, v = ps[:, 0:w], ps[:, w:2 * w], ps[:, 2 * w:3 * w]
    w_lo, a_lo, g_lo = ps[:, 3 * w:3 * w + 64], ps[:, 3 * w + 64:3 * w + 128], ps[:, 3 * w + 128:3 * w + 256]
    ww = w0_ref[...] + _dot(jnp.tanh(w_lo), wup_ref[...], HIGHEST)
    softplus_neg = jnp.maximum(-ww, 0.0) + jnp.log1p(jnp.exp(-jnp.abs(ww)))
    log_w = -jnp.exp(-softplus_neg - 0.5)
    a = _sigmoid(a0_ref[...] + _dot(a_lo, aup_ref[...], HIGHEST))
    g = _dot(_sigmoid(g_lo), gup_ref[...], HIGHEST)
    head_sum = _head_block_ones(w, 1.0)
    kn = k * kk_ref[...]
    kn = kn * lax.rsqrt(_dot(kn * kn, head_sum, HIGHEST) + 1e-12)
    k2 = k * (1.0 + (a - 1.0) * ka_ref[...])
    bonus = _dot(r * k2 * rk_ref[...], head_sum, HIGHEST) * v
    r_o[0] = r
    lw_o[0] = log_w
    k_o[0] = k2
    v_o[0] = v
    kn_o[0] = kn
    b_o[0] = kn * a
    g_o[0] = g
    bonus_o[0] = bonus


def _rwkv_pre(p3, mu, w0, w_up, a0, a_up, g_up, k_k, k_a, r_k, tt=256):
    b, s, _ = p3.shape
    w = RWKV_W
    row = lambda x: x.reshape(1, -1)
    full = lambda shape: pl.BlockSpec(shape, lambda bi, t: (0,) * len(shape))
    out = jax.ShapeDtypeStruct((b, s, w), F32)
    return pl.pallas_call(
        functools.partial(_rwkv_pre_kernel, tt=tt),
        grid=(b, s // tt),
        in_specs=[pl.BlockSpec((1, tt, 1024), lambda bi, t: (bi, t, COL_RWKV // 1024)),
                  full((1, 1024)), full((1, w)), full((64, w)), full((1, w)), full((64, w)),
                  full((128, w)), full((1, w)), full((1, w)), full((1, w))],
        out_specs=[pl.BlockSpec((1, tt, w), lambda bi, t: (bi, t, 0))] * 8,
        out_shape=[out] * 8,
        scratch_shapes=[pltpu.VMEM((1, 1024), F32)],
        compiler_params=_cparams(("parallel", "arbitrary")),
        name="rwkv_pre",
    )(p3, row(mu), row(w0), w_up, row(a0), a_up, g_up, row(k_k), row(k_a), row(r_k))


def _bdot(a, b):
    return _dot(a.astype(BF16), b.astype(BF16))


def _bnt(a, b):
    return _nt(a.astype(BF16), b.astype(BF16))


def _btn(a, b):
    return _tn(a.astype(BF16), b.astype(BF16))


def _rwkv_chunk_kernel(r_ref, lw_ref, k_ref, v_ref, kn_ref, b_ref, y_ref, s_ref, *, cs, nch):
    c = pl.program_id(1)

    @pl.when(c == 0)
    def _():
        s_ref[...] = jnp.zeros_like(s_ref)

    rows = cs * nch
    big_row = lax.broadcasted_iota(jnp.int32, (rows, rows), 0)
    big_col = lax.broadcasted_iota(jnp.int32, (rows, rows), 1)
    same_chunk = (big_row >> CHUNK_SHIFT) == (big_col >> CHUNK_SHIFT)
    tri = jnp.where(same_chunk, jnp.where(big_col <= big_row, 1.0, 0.0), 0.0).astype(F32)
    row = lax.broadcasted_iota(jnp.int32, (cs, cs), 0)
    col = lax.broadcasted_iota(jnp.int32, (cs, cs), 1)
    incl = col <= row
    strict = col < row
    eye = jnp.where(col == row, 1.0, 0.0).astype(F32)
    lw = lw_ref[0]
    cum = _dot(tri, lw, HIGHEST)
    e_incl = jnp.exp(cum)
    e_neg = jnp.exp(-cum)
    abar_all = -kn_ref[0] * jnp.exp(cum - lw)
    rbar_all = r_ref[0] * e_incl
    bt_all = b_ref[0] * e_neg
    kt_all = k_ref[0] * e_neg
    v_all = v_ref[0]
    n_double = max(cs.bit_length() - 2, 0)
    probs = [(j, h) for j in range(nch) for h in range(RWKV_HEADS)]
    cut = lambda t, jh: t[jh[0] * cs:(jh[0] + 1) * cs, jh[1] * HEAD_DIM:(jh[1] + 1) * HEAD_DIM]
    each = lambda fn: {jh: fn(jh) for jh in probs}
    abar, rbar = each(lambda jh: cut(abar_all, jh)), each(lambda jh: cut(rbar_all, jh))
    bt, kt, v = each(lambda jh: cut(bt_all, jh)), each(lambda jh: cut(kt_all, jh)), each(lambda jh: cut(v_all, jh))
    p_last = each(lambda jh: e_incl[(jh[0] + 1) * cs - 1:(jh[0] + 1) * cs, jh[1] * HEAD_DIM:(jh[1] + 1) * HEAD_DIM])
    ar = each(lambda jh: jnp.concatenate([abar[jh], rbar[jh]], axis=0))
    ar_b = each(lambda jh: _bnt(ar[jh], bt[jh]))
    ar_k = each(lambda jh: _bnt(ar[jh], kt[jh]))
    a_ab = each(lambda jh: jnp.where(strict, ar_b[jh][:cs], 0.0))
    a_ak = each(lambda jh: jnp.where(strict, ar_k[jh][:cs], 0.0))
    a_rb = each(lambda jh: jnp.where(incl, ar_b[jh][cs:], 0.0))
    a_rk = each(lambda jh: jnp.where(incl, ar_k[jh][cs:], 0.0))
    ak_v = each(lambda jh: _bdot(a_ak[jh], v[jh]))
    inv = each(lambda jh: eye + a_ab[jh])
    power = a_ab
    for _ in range(n_double):
        power = each(lambda jh: _bdot(power[jh], power[jh]))
        inv = each(lambda jh: inv[jh] + _bdot(inv[jh], power[jh]))
    a_hat = each(lambda jh: _bdot(inv[jh], abar[jh]))
    u_hat = each(lambda jh: _bdot(inv[jh], ak_v[jh]))
    r_hat = each(lambda jh: rbar[jh] + _bdot(a_rb[jh], a_hat[jh]))
    y_hat = each(lambda jh: _bdot(a_rb[jh], u_hat[jh]) + _bdot(a_rk[jh], v[jh]))
    g_mat = each(lambda jh: eye * p_last[jh] + _btn(a_hat[jh], bt[jh] * p_last[jh]))
    h_mat = each(lambda jh: _btn(jnp.concatenate([u_hat[jh], v[jh]], axis=0),
                                 jnp.concatenate([bt[jh], kt[jh]], axis=0) * p_last[jh]))
    states = [s_ref[h] for h in range(RWKV_HEADS)]
    for j in range(nch):
        for h in range(RWKV_HEADS):
            y_ref[0, j * cs:(j + 1) * cs, h * HEAD_DIM:(h + 1) * HEAD_DIM] = (
                _bnt(r_hat[j, h], states[h]) + y_hat[j, h])
        states = [_dot(states[h], g_mat[j, h], HIGHEST) + h_mat[j, h] for h in range(RWKV_HEADS)]
    for h in range(RWKV_HEADS):
        s_ref[h] = states[h]


def _rwkv_chunk(r, lw, k, v, kn, b_, cs=CHUNK, nch=4):
    b, s, w = r.shape
    spec = pl.BlockSpec((1, cs * nch, w), lambda bi, c: (bi, c, 0))
    return pl.pallas_call(
        functools.partial(_rwkv_chunk_kernel, cs=cs, nch=nch),
        grid=(b, s // (cs * nch)),
        in_specs=[spec] * 6,
        out_specs=spec,
        out_shape=jax.ShapeDtypeStruct((b, s, w), F32),
        scratch_shapes=[pltpu.VMEM((RWKV_HEADS, HEAD_DIM, HEAD_DIM), F32)],
        compiler_params=_cparams(("parallel", "arbitrary")),
        name="rwkv_chunk",
    )(r, lw, k, v, kn, b_)


def _merge_kernel(x_ref, xb_ref, swa_ref, y_ref, g_ref, bonus_ref, fox_ref, dsa_ref, gng_ref, gnb_ref,
                  gbias_ref, wg_ref, wb_ref, wo_ref, lng_ref, lnb_ref, o_ref, ob_ref):
    y = y_ref[...]
    xb = xb_ref[...]
    head_mean = _head_block_ones(RWKV_W, 1.0 / HEAD_DIM)
    yc = y - _dot(y, head_mean, HIGHEST)
    yv = _dot(yc * yc, head_mean, HIGHEST)
    o_rwkv = (yc * lax.rsqrt(yv + RWKV_GN_EPS) * gng_ref[...] + gnb_ref[...] + bonus_ref[...]) * g_ref[...]
    branches = (swa_ref[...], o_rwkv, fox_ref[...], dsa_ref[...])
    merged = jnp.zeros(o_ref.shape, F32)
    off = 0
    for i, o in enumerate(branches):
        width = MIX_WIDTHS[i]
        proj = _dot(o.astype(BF16), wb_ref[off:off + width, :])
        gate = _sigmoid(_dot(xb, wg_ref[0, :, i * D_MODEL:(i + 1) * D_MODEL]) + gbias_ref[i:i + 1, :])
        merged = merged + gate * proj
        off += width
    y_out = _dot(merged.astype(BF16), wo_ref[...])
    xn = _layer_norm(DN_ALPHA * x_ref[...] + y_out, lng_ref[...], lnb_ref[...])
    o_ref[...] = xn
    ob_ref[...] = xn.astype(BF16)


def _merge(xf, xb, o_swa, y, g, bonus, o_fox, o_dsa, gn_g, gn_b, gate_bias, w_gate_all, layer, w_branch, w_out,
           ln_g, ln_b, tm=512):
    n, d = xf.shape
    w_gate = w_gate_all
    row = lambda x: x.reshape(1, -1)
    tok = lambda wdt: pl.BlockSpec((tm, wdt), lambda i: (i, 0))
    full = lambda shape: pl.BlockSpec(shape, lambda i: (0,) * len(shape))
    return pl.pallas_call(
        _merge_kernel,
        grid=(n // tm,),
        in_specs=[tok(d), tok(d), tok(512), tok(256), tok(256), tok(256), tok(256), tok(256),
                  full((1, 256)), full((1, 256)), full((N_BRANCHES, d)),
                  pl.BlockSpec((1, d, N_BRANCHES * d), lambda i: (layer, 0, 0)),
                  full((sum(MIX_WIDTHS), d)), full((d, d)), full((1, d)), full((1, d))],
        out_specs=[tok(d), tok(d)],
        out_shape=[jax.ShapeDtypeStruct((n, d), F32), jax.ShapeDtypeStruct((n, d), BF16)],
        compiler_params=_cparams(("parallel",)),
        name="merge",
    )(xf, xb, o_swa, y, g, bonus, o_fox, o_dsa, row(gn_g), row(gn_b), gate_bias, w_gate, w_branch, w_out,
      row(ln_g), row(ln_b))


def _silu(x):
    return x * _sigmoid(x)


def _ffn_kernel(xb_ref, x_ref, w1_ref, w3_ref, w2_ref, lng_ref, lnb_ref, o_ref, ob_ref, *, tf):
    xb = xb_ref[...]
    y = None
    for f in range(0, w1_ref.shape[1], tf):
        hidden = _silu(_dot(xb, w1_ref[:, f:f + tf])) * _dot(xb, w3_ref[:, f:f + tf])
        part = _dot(hidden.astype(BF16), w2_ref[f:f + tf, :])
        y = part if y is None else y + part
    xn = _layer_norm(DN_ALPHA * x_ref[...] + y, lng_ref[...], lnb_ref[...])
    o_ref[...] = xn
    ob_ref[...] = xn.astype(BF16)


def _ffn(xb, xf, w1, w3, w2, ln_g, ln_b, tm=512, tf=1408):
    n, d = xf.shape
    ff = w1.shape[1]
    row = lambda x: x.reshape(1, -1)
    tok = pl.BlockSpec((tm, d), lambda i: (i, 0))
    resident = lambda shape: pl.BlockSpec(shape, lambda i: (0, 0), pipeline_mode=pl.Buffered(1))
    return pl.pallas_call(
        functools.partial(_ffn_kernel, tf=tf),
        grid=(n // tm,),
        in_specs=[tok, tok, resident((d, ff)), resident((d, ff)), resident((ff, d)),
                  resident((1, d)), resident((1, d))],
        out_specs=[tok, tok],
        out_shape=[jax.ShapeDtypeStruct((n, d), F32), jax.ShapeDtypeStruct((n, d), BF16)],
        compiler_params=_cparams(("parallel",)),
        name="ffn",
    )(xb, xf, w1, w3, w2, row(ln_g), row(ln_b))


def _moe_kernel(xb_ref, x_ref, rw_ref, rb_ref, w1_ref, w3_ref, w2_ref, lng_ref, lnb_ref, o_ref,
                comb_ref, rank_ref, cnt_ref, *, sb):
    e = pl.program_id(1)
    tm = x_ref.shape[0]
    lane = lax.broadcasted_iota(jnp.int32, (tm, 128), 1).astype(F32)

    @pl.when(e == 0)
    def _():
        o_ref[...] = jnp.zeros_like(o_ref)
        logits = _dot(x_ref[...], rw_ref[...], HIGHEST) + rb_ref[...]
        logits = jnp.where(lane < N_EXPERTS, logits, NEG_BIG)
        m1 = jnp.max(logits, axis=-1, keepdims=True)
        i1 = jnp.min(jnp.where(logits == m1, lane, 128.0), axis=-1, keepdims=True)
        rest = jnp.where(lane == i1, NEG_BIG, logits)
        m2 = jnp.max(rest, axis=-1, keepdims=True)
        i2 = jnp.min(jnp.where(rest == m2, lane, 128.0), axis=-1, keepdims=True)
        e2 = jnp.exp(m2 - m1)
        comb_ref[...] = jnp.where(lane == i1, 1.0 / (1.0 + e2), 0.0) + jnp.where(lane == i2, e2 / (1.0 + e2), 0.0)
        chosen = jnp.where(lane == i1, 1.0, 0.0) + jnp.where(lane == i2, 1.0, 0.0)
        r = lax.broadcasted_iota(jnp.int32, (tm, tm), 0)
        c = lax.broadcasted_iota(jnp.int32, (tm, tm), 1)
        earlier = jnp.where(c < r, 1.0, 0.0).astype(BF16)
        rank = jnp.where(chosen > 0.5, _dot(earlier, chosen.astype(BF16)), -1.0)
        lane8 = lax.broadcasted_iota(jnp.int32, (8, 128), 1)
        row8 = lax.broadcasted_iota(jnp.int32, (8, 128), 0)
        rank_ref[...] = _nt(jnp.where(lane8 == row8, 1.0, 0.0).astype(F32), rank, HIGHEST)
        cnt_ref[...] = jnp.sum(chosen, axis=0, keepdims=True)

    e_f = e.astype(F32)
    n_tok = jnp.sum(jnp.where(lane[0:1, :] == e_f, cnt_ref[...], 0.0)).astype(jnp.int32)
    n_blocks = lax.div(n_tok + sb - 1, sb)
    rank_row = rank_ref[pl.ds(e, 1), :]
    weight = jnp.sum(jnp.where(lane == e_f, comb_ref[...], 0.0), axis=-1, keepdims=True)
    slot0 = lax.broadcasted_iota(jnp.int32, (sb, tm), 0).astype(F32)

    def block(s, carry):
        pick = jnp.where(rank_row == slot0 + (s * sb).astype(F32), 1.0, 0.0).astype(BF16)
        xs = _dot(pick, xb_ref[...]).astype(BF16)
        hidden = _silu(_dot(xs, w1_ref[0])) * _dot(xs, w3_ref[0])
        out = _dot(hidden.astype(BF16), w2_ref[0]).astype(BF16)
        o_ref[...] += _tn(pick, out) * weight
        return carry

    lax.fori_loop(0, n_blocks, block, 0)

    @pl.when(e == pl.num_programs(1) - 1)
    def _():
        o_ref[...] = _layer_norm(DN_ALPHA * x_ref[...] + o_ref[...], lng_ref[...], lnb_ref[...])


def _moe(xb, xf, router_w, router_b, w1, w3, w2, ln_g, ln_b, tm=1024, sb=288):
    n, d = xf.shape
    ne, _, fe = w1.shape
    tm = min(tm, n)
    rw = jnp.pad(router_w, ((0, 0), (0, 128 - ne)))
    rb = jnp.pad(router_b, (0, 128 - ne)).reshape(1, 128)
    row = lambda x: x.reshape(1, -1)
    return pl.pallas_call(
        functools.partial(_moe_kernel, sb=sb),
        grid=(n // tm, ne),
        in_specs=[pl.BlockSpec((tm, d), lambda i, e: (i, 0)),
                  pl.BlockSpec((tm, d), lambda i, e: (i, 0)),
                  pl.BlockSpec((d, 128), lambda i, e: (0, 0)),
                  pl.BlockSpec((1, 128), lambda i, e: (0, 0)),
                  pl.BlockSpec((1, d, fe), lambda i, e: (e, 0, 0)),
                  pl.BlockSpec((1, d, fe), lambda i, e: (e, 0, 0)),
                  pl.BlockSpec((1, fe, d), lambda i, e: (e, 0, 0)),
                  pl.BlockSpec((1, d), lambda i, e: (0, 0)),
                  pl.BlockSpec((1, d), lambda i, e: (0, 0))],
        out_specs=pl.BlockSpec((tm, d), lambda i, e: (i, 0)),
        out_shape=jax.ShapeDtypeStruct((n, d), F32),
        scratch_shapes=[pltpu.VMEM((tm, 128), F32), pltpu.VMEM((8, tm), F32), pltpu.VMEM((1, 128), F32)],
        compiler_params=pltpu.CompilerParams(dimension_semantics=("parallel", "arbitrary"),
                                             vmem_limit_bytes=MOE_VMEM_LIMIT),
        name="moe",
    )(xb, xf, rw, rb, w1, w3, w2, row(ln_g), row(ln_b))


def _mixer_layer(xf, xb, bsz, seq, layer, w_in, w_gate_all, sinks, mu, w0, w_up, a0, a_up, g_up, k_k, k_a, r_k,
                 gn_g, gn_b, f_bias, gate_bias, w_branch, w_out, ln_g, ln_b):
    w_mix = _layout_w_in(w_in[layer, :, :IN_OFF_GATE].astype(BF16))
    p2 = _matmul(xb, w_mix, tm=1024, tn=P_WIDTH // 2)
    p3 = p2.reshape(bsz, seq, P_WIDTH)
    swa_kv_w = SWA_KV_HEADS * HEAD_DIM
    swa_v_t = jnp.swapaxes(p3[:, :, COL_SWA_KV + swa_kv_w:COL_SWA_KV + 2 * swa_kv_w], 1, 2)
    o_swa = _swa(p3, swa_v_t, sinks)
    fox_w = FOX_HEADS * HEAD_DIM
    fox_v_t = jnp.swapaxes(p3[:, :, COL_FOX_KV + fox_w:COL_FOX_KV + 2 * fox_w], 1, 2)
    o_fox = _fox(p3, fox_v_t, _fox_c(p3, f_bias))
    v_t = jnp.swapaxes(p3[:, :, COL_DSA_KV + HEAD_DIM:COL_DSA_KV + 2 * HEAD_DIM], 1, 2)
    o_dsa = _dsa(p3, v_t)
    r, lw, k2, v, kn, b_, g, bonus = _rwkv_pre(p3, mu, w0, w_up, a0, a_up, g_up, k_k, k_a, r_k)
    y = _rwkv_chunk(r, lw, k2, v, kn, b_)
    n = bsz * seq
    flat = lambda t: t.reshape(n, t.shape[-1])
    return _merge(xf, xb, flat(o_swa), flat(y), flat(g), flat(bonus), flat(o_fox), flat(o_dsa), gn_g, gn_b,
                  gate_bias, w_gate_all, layer, w_branch.astype(BF16), w_out.astype(BF16), ln_g, ln_b)


def kernel(x, w_in, swa_sinks, rwkv_mu, rwkv_w0, rwkv_w_up, rwkv_a0, rwkv_a_up, rwkv_g_up, rwkv_k_k, rwkv_k_a,
           rwkv_r_k, rwkv_gn_g, rwkv_gn_b, fox_f_bias, gate_bias, w_branch, w_out, ln_g, ln_b, ffn_w1, ffn_w3,
           ffn_w2, router_w, router_b, exp_w1, exp_w3, exp_w2):
    bsz, seq, d = x.shape
    xf = x.reshape(bsz * seq, d)
    xb = xf.astype(BF16)
    w_gate_all = _gate_weights(w_in)
    for layer in range(DEPTH):
        xf, xb = _mixer_layer(xf, xb, bsz, seq, layer, w_in, w_gate_all, swa_sinks[layer], rwkv_mu[layer], rwkv_w0[layer],
                              rwkv_w_up[layer], rwkv_a0[layer], rwkv_a_up[layer], rwkv_g_up[layer],
                              rwkv_k_k[layer], rwkv_k_a[layer], rwkv_r_k[layer], rwkv_gn_g[layer],
                              rwkv_gn_b[layer], fox_f_bias[layer], gate_bias[layer], w_branch[layer],
                              w_out[layer], ln_g[layer, 0], ln_b[layer, 0])
        j = layer // 2
        if layer % 2 == 0:
            xf, xb = _ffn(xb, xf, ffn_w1[j].astype(BF16), ffn_w3[j].astype(BF16), ffn_w2[j].astype(BF16),
                          ln_g[layer, 1], ln_b[layer, 1])
        else:
            xf = _moe(xb, xf, router_w[j], router_b[j], exp_w1[j].astype(BF16), exp_w3[j].astype(BF16),
                      exp_w2[j].astype(BF16), ln_g[layer, 1], ln_b[layer, 1])
            xb = xf.astype(BF16) if layer + 1 < DEPTH else None
    return xf.reshape(bsz, seq, d)
```

```python
import functools

import jax
import jax.numpy as jnp
from jax import lax
from jax.experimental import pallas as pl
from jax.experimental.pallas import tpu as pltpu

F32 = jnp.float32
BF16 = jnp.bfloat16
HIGHEST = lax.Precision.HIGHEST

D_MODEL = 1024
DEPTH = 2
CHUNK = 64
HEAD_DIM = 64
CHUNK_SHIFT = 6
HEAD_SHIFT = 6
SWA_HEADS = 8
SWA_KV_HEADS = 2
SWA_GROUP = SWA_HEADS // SWA_KV_HEADS
SWA_WINDOW = 128
SWA_WIN_CHUNKS = SWA_WINDOW // CHUNK
RWKV_HEADS = 4
RWKV_W = RWKV_HEADS * HEAD_DIM
RWKV_GN_EPS = 64e-5
FOX_HEADS = 4
DSA_HEADS = 4
IDX_HEADS = 4
IDX_DIM = 64
DSA_TOPK_MAX = 256
D_FF = 2816
N_EXPERTS = 8
D_FF_EXPERT = 1408
N_BRANCHES = 4
DN_ALPHA = (2 * DEPTH) ** 0.25
LN_EPS = 1e-5
ATTN_SCALE = HEAD_DIM ** -0.5
IDX_SCALE = IDX_DIM ** -0.5
IDX_W_SCALE = IDX_HEADS ** -0.5
MIX_WIDTHS = (SWA_HEADS * HEAD_DIM, RWKV_W, FOX_HEADS * HEAD_DIM, DSA_HEADS * HEAD_DIM)

COL_RWKV = 0
COL_SWA_Q = 1024
COL_FOX_KV = 1536
COL_DSA_QQ = 2048
COL_FOX_Q = 2560
COL_SWA_KV = 2816
COL_DSA_KV = 3072
COL_MISC = 3328
P_WIDTH = 3584

NEG_BIG = -1e30
INT_MIN = -(2 ** 31)
VMEM_LIMIT = 48 * 1024 * 1024
MOE_VMEM_LIMIT = 58 * 1024 * 1024


IN_OFF_SWA, IN_OFF_RWKV, IN_OFF_FOX, IN_OFF_DSA, IN_OFF_GATE = 0, 768, 1792, 2564, 3272
D_IN = IN_OFF_GATE + N_BRANCHES * D_MODEL


def _layout_w_in(w):
    d = w.shape[0]
    z = lambda n: jnp.zeros((d, n), w.dtype)
    swa, rwkv, fox, dsa = IN_OFF_SWA, IN_OFF_RWKV, IN_OFF_FOX, IN_OFF_DSA
    parts = [
        w[:, rwkv:rwkv + 1024],
        w[:, swa:swa + 512],
        w[:, fox + 256:fox + 768],
        w[:, dsa:dsa + 256], w[:, dsa + 384:dsa + 640],
        w[:, fox:fox + 256],
        w[:, swa + 512:swa + 768],
        w[:, dsa + 256:dsa + 384], w[:, dsa + 640:dsa + 704], z(64),
        w[:, fox + 768:fox + 772], w[:, dsa + 704:dsa + 708], z(120),
        z(128),
    ]
    return jnp.concatenate(parts, axis=1)


def _cparams(sem):
    return pltpu.CompilerParams(dimension_semantics=sem, vmem_limit_bytes=VMEM_LIMIT)


def _nt(a, b, precision=None):
    return lax.dot_general(a, b, (((1,), (1,)), ((), ())), precision=precision,
                           preferred_element_type=F32)


def _tn(a, b, precision=None):
    return lax.dot_general(a, b, (((0,), (0,)), ((), ())), precision=precision,
                           preferred_element_type=F32)


def _dot(a, b, precision=None):
    return jnp.dot(a, b, precision=precision, preferred_element_type=F32)


def _sigmoid(x):
    return 1.0 / (1.0 + jnp.exp(-x))


def _layer_norm(z, g, b):
    mu = jnp.mean(z, axis=-1, keepdims=True)
    zc = z - mu
    var = jnp.mean(zc * zc, axis=-1, keepdims=True)
    return zc * lax.rsqrt(var + LN_EPS) * g + b


def _skewed(n, first, second):
    out = []
    staged = first(0)
    for h in range(n):
        nxt = first(h + 1) if h + 1 < n else None
        out.append(second(h, staged))
        staged = nxt
    return out


def _paired_blocks(n, body, carry):
    carry = lax.fori_loop(0, lax.div(n, 2), lambda t, c: body((2 * t, 2 * t + 1), c), carry)
    return lax.cond(lax.rem(n, 2) == 1, lambda c: body((n - 1,), c), lambda c: c, carry)


def _head_block_ones(n, scale):
    r = lax.broadcasted_iota(jnp.int32, (n, n), 0) >> HEAD_SHIFT
    c = lax.broadcasted_iota(jnp.int32, (n, n), 1) >> HEAD_SHIFT
    return jnp.where(r == c, scale, 0.0).astype(F32)


def _mm_kernel(a_ref, b_ref, o_ref):
    o_ref[...] = _dot(a_ref[...], b_ref[...])


def _matmul(a, b, tm, tn):
    m, k = a.shape
    n = b.shape[1]
    tm = min(tm, m)
    return pl.pallas_call(
        _mm_kernel,
        grid=(m // tm, n // tn),
        in_specs=[pl.BlockSpec((tm, k), lambda i, j: (i, 0)),
                  pl.BlockSpec((k, tn), lambda i, j: (0, j))],
        out_specs=pl.BlockSpec((tm, tn), lambda i, j: (i, j)),
        out_shape=jax.ShapeDtypeStruct((m, n), F32),
        compiler_params=_cparams(("parallel", "parallel")),
        name="in_proj",
    )(a, b)


def _swa_kernel(sink_ref, q_ref, kv_ref, o_ref, *, tq):
    i = pl.program_id(1)
    win = tq + SWA_WINDOW
    s0 = pl.multiple_of(jnp.maximum(i * tq - SWA_WINDOW, 0), SWA_WINDOW)
    k_chunk = (s0 + lax.broadcasted_iota(jnp.int32, (win, tq), 0)) >> CHUNK_SHIFT
    q_chunk = (i * tq + lax.broadcasted_iota(jnp.int32, (win, tq), 1)) >> CHUNK_SHIFT
    bias = jnp.where(k_chunk <= q_chunk,
                     jnp.where(k_chunk >= q_chunk - SWA_WIN_CHUNKS, 0.0, NEG_BIG), NEG_BIG)
    ks = [kv_ref[0, pl.ds(s0, win), hk * HEAD_DIM:(hk + 1) * HEAD_DIM].astype(BF16)
          for hk in range(SWA_KV_HEADS)]
    kv_w = SWA_KV_HEADS * HEAD_DIM
    vs = [kv_ref[0, pl.ds(s0, win), kv_w + hk * HEAD_DIM:kv_w + (hk + 1) * HEAD_DIM].astype(BF16)
          for hk in range(SWA_KV_HEADS)]

    def logits(h):
        q = (q_ref[0, :, h * HEAD_DIM:(h + 1) * HEAD_DIM] * ATTN_SCALE).astype(BF16)
        return _nt(ks[h // SWA_GROUP], q) + bias

    def finish(h, s):
        sink = sink_ref[h]
        m = jnp.maximum(jnp.max(s, axis=0, keepdims=True), sink)
        e = jnp.exp(s - m)
        denom = jnp.sum(e, axis=0, keepdims=True) + jnp.exp(sink - m)
        return _tn(vs[h // SWA_GROUP], e.astype(BF16)) / denom

    o_ref[0] = jnp.concatenate(_skewed(SWA_HEADS, logits, finish), axis=0).T


def _swa(p3, sinks, tq=256):
    b, s, _ = p3.shape
    kv_w = 2 * SWA_KV_HEADS * HEAD_DIM
    return pl.pallas_call(
        functools.partial(_swa_kernel, tq=tq),
        grid=(b, s // tq),
        in_specs=[pl.BlockSpec(memory_space=pltpu.SMEM),
                  pl.BlockSpec((1, tq, 512), lambda bi, i: (bi, i, COL_SWA_Q // 512)),
                  pl.BlockSpec((1, s, kv_w), lambda bi, i: (bi, 0, COL_SWA_KV // kv_w))],
        out_specs=pl.BlockSpec((1, tq, 512), lambda bi, i: (bi, i, 0)),
        out_shape=jax.ShapeDtypeStruct((b, s, 512), F32),
        compiler_params=_cparams(("parallel", "parallel")),
        name="swa",
    )(sinks, p3, p3)


def _fox_c_kernel(f_ref, bias_ref, c_ref, *, blk):
    s = f_ref.shape[1]
    r = lax.broadcasted_iota(jnp.int32, (blk, blk), 0)
    c = lax.broadcasted_iota(jnp.int32, (blk, blk), 1)
    tri = jnp.where(c <= r, 1.0, 0.0).astype(F32)
    carry = jnp.zeros((1, 128), F32)
    for j in range(s // blk):
        x = f_ref[0, j * blk:(j + 1) * blk, :] + bias_ref[...]
        log_f = jnp.minimum(x, 0.0) - jnp.log1p(jnp.exp(-jnp.abs(x)))
        cs = _dot(tri, log_f, HIGHEST) + carry
        carry = cs[blk - 1:blk, :]
        for h in range(FOX_HEADS):
            c_ref[0, h, j * blk:(j + 1) * blk, :] = jnp.broadcast_to(cs[:, h:h + 1], (blk, 128))


def _fox_c(p3, f_bias):
    b, s, _ = p3.shape
    bias_row = jnp.pad(f_bias, (0, 128 - FOX_HEADS)).reshape(1, 128)
    return pl.pallas_call(
        functools.partial(_fox_c_kernel, blk=256),
        grid=(b,),
        in_specs=[pl.BlockSpec((1, s, 128), lambda bi: (bi, 0, COL_MISC // 128)),
                  pl.BlockSpec((1, 128), lambda bi: (0, 0))],
        out_specs=pl.BlockSpec((1, FOX_HEADS, s, 128), lambda bi: (bi, 0, 0, 0)),
        out_shape=jax.ShapeDtypeStruct((b, FOX_HEADS, s, 128), F32),
        compiler_params=_cparams(("parallel",)),
        name="fox_cumsum",
    )(p3, bias_row)


def _fox_kernel(q_ref, k_ref, c_ref, o_ref, lg_ref, acc_ref, *, tq):
    i = pl.program_id(1)
    tk = tq
    v_col = FOX_HEADS * HEAD_DIM
    key_i = lax.broadcasted_iota(jnp.int32, (tk, tq), 0)
    qry_i = lax.broadcasted_iota(jnp.int32, (tk, tq), 1)
    diag_bias = jnp.where(key_i <= qry_i, 0.0, NEG_BIG)
    q = [(q_ref[0, :, h * HEAD_DIM:(h + 1) * HEAD_DIM] * ATTN_SCALE).astype(BF16) for h in range(FOX_HEADS)]

    def logits_blocks(blocks, maxes, masked=False):
        starts = [pl.multiple_of(j * tk, tk) for j in blocks]
        dots = [[_nt(k_ref[0, pl.ds(ks, tk), h * HEAD_DIM:(h + 1) * HEAD_DIM].astype(BF16), q[h])
                 for h in range(FOX_HEADS)] for ks in starts]
        maxes = list(maxes)
        for b, ks in enumerate(starts):
            for h in range(FOX_HEADS):
                c_k = c_ref[0, h, pl.ds(ks, tk), :]
                s = dots[b][h] - jnp.concatenate([c_k] * (tq // 128), axis=1)
                if masked:
                    s = s + diag_bias
                lg_ref[h, pl.ds(ks, tk), :] = s
                maxes[h] = jnp.maximum(maxes[h], jnp.max(s, axis=0, keepdims=True))
        return tuple(maxes)

    maxes = tuple(jnp.full((1, tq), NEG_BIG, F32) for _ in range(FOX_HEADS))
    maxes = _paired_blocks(i, logits_blocks, maxes)
    maxes = logits_blocks((i,), maxes, masked=True)
    acc_ref[...] = jnp.zeros_like(acc_ref)

    def attend_blocks(blocks, sums):
        starts = [pl.multiple_of(j * tk, tk) for j in blocks]
        sums = list(sums)
        for h in range(FOX_HEADS):
            update = None
            for ks in starts:
                p = jnp.exp(lg_ref[h, pl.ds(ks, tk), :] - maxes[h])
                v = k_ref[0, pl.ds(ks, tk), v_col + h * HEAD_DIM:v_col + (h + 1) * HEAD_DIM].astype(BF16)
                pv = _tn(v, p.astype(BF16))
                update = pv if update is None else update + pv
                sums[h] = sums[h] + jnp.sum(p, axis=0, keepdims=True)
            acc_ref[h] += update
        return tuple(sums)

    sums = _paired_blocks(i + 1, attend_blocks, tuple(jnp.zeros((1, tq), F32) for _ in range(FOX_HEADS)))
    out_t = jnp.concatenate([acc_ref[h] / sums[h] for h in range(FOX_HEADS)], axis=0)
    o_ref[0] = out_t.T


def _fox(p3, c, tq=256):
    b, s, _ = p3.shape
    w = FOX_HEADS * HEAD_DIM
    return pl.pallas_call(
        functools.partial(_fox_kernel, tq=tq),
        grid=(b, s // tq),
        in_specs=[pl.BlockSpec((1, tq, w), lambda bi, i: (bi, i, COL_FOX_Q // w)),
                  pl.BlockSpec((1, s, 2 * w), lambda bi, i: (bi, 0, COL_FOX_KV // (2 * w))),
                  pl.BlockSpec((1, FOX_HEADS, s, 128), lambda bi, i: (bi, 0, 0, 0))],
        out_specs=pl.BlockSpec((1, tq, w), lambda bi, i: (bi, i, 0)),
        out_shape=jax.ShapeDtypeStruct((b, s, w), F32),
        scratch_shapes=[pltpu.VMEM((FOX_HEADS, s, tq), F32), pltpu.VMEM((FOX_HEADS, HEAD_DIM, tq), F32)],
        compiler_params=_cparams(("parallel", "parallel")),
        name="fox",
    )(p3, p3, c)


def _key_to_score(key):
    return pltpu.bitcast(key ^ ((key >> 31) & 0x7FFFFFFF), F32)


def _bit_planes(score):
    bits = pltpu.bitcast(score, jnp.int32)
    image = bits ^ ((bits >> 31) | INT_MIN)
    rows = image.reshape(32, 8, image.shape[1])
    v = [rows[j] for j in range(32)]
    for d, mask in ((16, 0x0000FFFF), (8, 0x00FF00FF), (4, 0x0F0F0F0F), (2, 0x33333333), (1, 0x55555555)):
        for k in range(32):
            if k & d == 0:
                t = (v[k] ^ lax.shift_right_logical(v[k + d], d)) & mask
                v[k] = v[k] ^ t
                v[k + d] = v[k + d] ^ lax.shift_left(t, d)
    return v


def _dsa_kernel(qq_ref, misc_ref, kv_ref, o_ref, sc_ref, lg_ref, acc_ref, pl_ref, *, tq, kb, topk):
    i = pl.program_id(1)
    n_kb = lax.div((i + 1) * tq + kb - 1, kb)
    lane8 = lax.broadcasted_iota(jnp.int32, (8, 128), 1)
    row8 = lax.broadcasted_iota(jnp.int32, (8, 128), 0)
    pick = jnp.where((lane8 == row8 + 4) & (row8 < IDX_HEADS), 1.0, 0.0).astype(F32)
    w_t = _nt(pick, misc_ref[0], HIGHEST) * (IDX_W_SCALE * IDX_SCALE)
    q_chunk = (i * tq + lax.broadcasted_iota(jnp.int32, (1, tq), 1)) >> CHUNK_SHIFT
    qw = DSA_HEADS * HEAD_DIM
    q_idx = [qq_ref[0, :, qw + h * IDX_DIM:qw + (h + 1) * IDX_DIM].astype(BF16) for h in range(IDX_HEADS)]

    def score_blocks(blocks, carry):
        starts = [pl.multiple_of(j * kb, kb) for j in blocks]
        dots = []
        for ks in starts:
            k_idx = kv_ref[0, pl.ds(ks, kb), 2 * HEAD_DIM:2 * HEAD_DIM + IDX_DIM].astype(BF16)
            dots.append([_nt(k_idx, q_idx[h]) for h in range(IDX_HEADS)])
        for b, ks in enumerate(starts):
            score = jnp.maximum(dots[b][0], 0.0) * w_t[0:1, :]
            for h in range(1, IDX_HEADS):
                score = score + jnp.maximum(dots[b][h], 0.0) * w_t[h:h + 1, :]
            k_chunk = (ks + lax.broadcasted_iota(jnp.int32, (kb, tq), 0)) >> CHUNK_SHIFT
            score = jnp.where(k_chunk <= q_chunk, score, -jnp.inf)
            sc_ref[pl.ds(ks, kb), :] = score
            planes = _bit_planes(score)
            for t in range(32):
                pl_ref[blocks[b], t] = planes[t]
        return carry

    _paired_blocks(n_kb, score_blocks, 0)
    n_blocks = pl_ref.shape[0]

    def clear_planes(j, carry):
        pl_ref[j] = jnp.zeros(pl_ref.shape[1:], jnp.int32)
        return carry

    lax.fori_loop(n_kb, n_blocks, clear_planes, 0)

    def count(pred):
        def body(j, acc):
            blk = sc_ref[pl.ds(pl.multiple_of(j * kb, kb), kb), :]
            hit = jnp.where(pred(blk), 1, 0).astype(jnp.int32).reshape(kb // 8, 8, tq)
            parts = [hit[g] for g in range(kb // 8)]
            while len(parts) > 1:
                parts = [parts[g] + parts[g + 1] for g in range(0, len(parts), 2)]
            return acc + parts[0]
        acc = lax.fori_loop(0, n_kb, body, jnp.zeros((8, tq), jnp.int32))
        return jnp.sum(acc.astype(F32), axis=0, keepdims=True)

    def search_by_compares():
        zero = jnp.zeros((1, tq), F32)
        thr_key = jnp.where(count(lambda blk: blk >= zero) >= topk, 0, INT_MIN).astype(jnp.int32)

        def bit_step(bi, thr_key):
            cand = thr_key | lax.shift_left(jnp.int32(1), 30 - bi)
            cand_score = _key_to_score(cand)
            return jnp.where(count(lambda blk: blk >= cand_score) >= topk, cand, thr_key)

        thr_key = lax.fori_loop(0, 31, bit_step, thr_key)
        t = jnp.where(thr_key == INT_MIN, -jnp.inf, _key_to_score(thr_key))
        return t, count(lambda blk: blk > t)

    def search_by_planes():
        def bit_pass(t, carry):
            prefix, above, alive = carry
            ones = [alive[j] & pl_ref[j, t] for j in range(n_blocks)]
            parts = [lax.population_count(o) for o in ones]
            while len(parts) > 1:
                parts = [parts[g] + parts[g + 1] for g in range(0, len(parts), 2)]
            reach = above + jnp.sum(parts[0].astype(F32), axis=0, keepdims=True)
            take = reach >= topk
            prefix = jnp.where(take, prefix | lax.shift_left(jnp.int32(1), 31 - t), prefix)
            alive = tuple(jnp.where(take, ones[j], alive[j] ^ ones[j]) for j in range(n_blocks))
            return prefix, jnp.where(take, above, reach), alive

        init = (jnp.zeros((1, tq), jnp.int32), jnp.zeros((1, tq), F32),
                tuple(jnp.full((8, tq), -1, jnp.int32) for _ in range(n_blocks)))
        image, _, _ = lax.fori_loop(0, 32, bit_pass, init)
        return pltpu.bitcast(image ^ (~(image >> 31) | INT_MIN), F32)

    thr_planes = search_by_planes()
    n_above = count(lambda blk: blk > thr_planes)
    n_reach = count(lambda blk: blk >= thr_planes)
    confirmed = jnp.min(jnp.where(n_above < topk, jnp.where(n_reach >= topk, 1.0, 0.0), 0.0)) > 0.5
    thr, n_above = lax.cond(confirmed, lambda: (thr_planes, n_above), search_by_compares)
    need = topk - n_above

    r = lax.broadcasted_iota(jnp.int32, (kb, kb), 0)
    c = lax.broadcasted_iota(jnp.int32, (kb, kb), 1)
    earlier = jnp.where(c < r, 1.0, 0.0).astype(BF16)
    q_att = [(qq_ref[0, :, h * HEAD_DIM:(h + 1) * HEAD_DIM] * ATTN_SCALE).astype(BF16)
             for h in range(DSA_HEADS)]

    def logits_blocks(blocks, carry):
        ties_before, maxes = carry
        maxes = list(maxes)
        starts = [pl.multiple_of(j * kb, kb) for j in blocks]
        dots, ties = [], []
        for ks in starts:
            k = kv_ref[0, pl.ds(ks, kb), 0:HEAD_DIM].astype(BF16)
            dots.append([_nt(k, q_att[h]) for h in range(DSA_HEADS)])
            k_chunk = (ks + lax.broadcasted_iota(jnp.int32, (kb, tq), 0)) >> CHUNK_SHIFT
            ties.append(jnp.where(sc_ref[pl.ds(ks, kb), :] == thr, jnp.where(k_chunk <= q_chunk, 1.0, 0.0), 0.0))
        ranks = [_dot(earlier, tie.astype(BF16)) for tie in ties]
        for b, ks in enumerate(starts):
            tie_taken = jnp.where(ranks[b] + ties_before < need, ties[b], 0.0)
            bias = jnp.where(sc_ref[pl.ds(ks, kb), :] > thr, 0.0, jnp.where(tie_taken > 0.5, 0.0, NEG_BIG))
            ties_before = ties_before + jnp.sum(ties[b], axis=0, keepdims=True)
            for h in range(DSA_HEADS):
                s = dots[b][h] + bias
                lg_ref[h, pl.ds(ks, kb), :] = s
                maxes[h] = jnp.maximum(maxes[h], jnp.max(s, axis=0, keepdims=True))
        return ties_before, tuple(maxes)

    init = (jnp.zeros((1, tq), F32), tuple(jnp.full((1, tq), NEG_BIG, F32) for _ in range(DSA_HEADS)))
    _, maxes = _paired_blocks(n_kb, logits_blocks, init)

    acc_ref[...] = jnp.zeros_like(acc_ref)

    def attend_blocks(blocks, sums):
        starts = [pl.multiple_of(j * kb, kb) for j in blocks]
        vs = [kv_ref[0, pl.ds(ks, kb), HEAD_DIM:2 * HEAD_DIM].astype(BF16) for ks in starts]
        sums = list(sums)
        for h in range(DSA_HEADS):
            update = None
            for b, ks in enumerate(starts):
                p = jnp.exp(lg_ref[h, pl.ds(ks, kb), :] - maxes[h])
                pv = _tn(vs[b], p.astype(BF16))
                update = pv if update is None else update + pv
                sums[h] = sums[h] + jnp.sum(p, axis=0, keepdims=True)
            acc_ref[h] += update
        return tuple(sums)

    sums = _paired_blocks(n_kb, attend_blocks, tuple(jnp.zeros((1, tq), F32) for _ in range(DSA_HEADS)))
    out_t = jnp.concatenate([acc_ref[h] / sums[h] for h in range(DSA_HEADS)], axis=0)
    o_ref[0] = out_t.T


def _dsa(p3, tq=256, kb=256):
    b, s, _ = p3.shape
    assert kb == 8 * 32, "a key block is the 32 row groups of one bit-plane word"
    topk = min(DSA_TOPK_MAX, s // 4)
    return pl.pallas_call(
        functools.partial(_dsa_kernel, tq=tq, kb=kb, topk=topk),
        grid=(b, s // tq),
        in_specs=[pl.BlockSpec((1, tq, 512), lambda bi, i: (bi, i, COL_DSA_QQ // 512)),
                  pl.BlockSpec((1, tq, 128), lambda bi, i: (bi, i, COL_MISC // 128)),
                  pl.BlockSpec((1, s, 256), lambda bi, i: (bi, 0, COL_DSA_KV // 256))],
        out_specs=pl.BlockSpec((1, tq, 256), lambda bi, i: (bi, i, 0)),
        out_shape=jax.ShapeDtypeStruct((b, s, 256), F32),
        scratch_shapes=[pltpu.VMEM((s, tq), F32), pltpu.VMEM((DSA_HEADS, s, tq), F32),
                        pltpu.VMEM((DSA_HEADS, HEAD_DIM, tq), F32), pltpu.VMEM((s // kb, 32, 8, tq), jnp.int32)],
        compiler_params=_cparams(("parallel", "parallel")),
        name="dsa",
    )(p3, p3, p3)


def _rwkv_pre_kernel(p_ref, mu_ref, w0_ref, wup_ref, a0_ref, aup_ref, gup_ref, kk_ref, ka_ref, rk_ref,
                     r_o, lw_o, k_o, v_o, kn_o, b_o, g_o, bonus_o, last_ref, *, tt):
    t = pl.program_id(1)

    @pl.when(t == 0)
    def _():
        last_ref[...] = jnp.zeros_like(last_ref)

    p = p_ref[0]
    row = lax.broadcasted_iota(jnp.int32, p.shape, 0)
    p_prev = jnp.where(row == 0, last_ref[...], pltpu.roll(p, 1, axis=0))
    last_ref[...] = p[tt - 1:tt, :]
    ps = p + mu_ref[...] * (p_prev - p)
    w = RWKV_W
    r, k, v = ps[:, 0:w], ps[:, w:2 * w], ps[:, 2 * w:3 * w]
    w_lo, a_lo, g_lo = ps[:, 3 * w:3 * w + 64], ps[:, 3 * w + 64:3 * w + 128], ps[:, 3 * w + 128:3 * w + 256]
    ww = w0_ref[...] + _dot(jnp.tanh(w_lo), wup_ref[...], HIGHEST)
    softplus_neg = jnp.maximum(-ww, 0.0) + jnp.log1p(jnp.exp(-jnp.abs(ww)))
    log_w = -jnp.exp(-softplus_neg - 0.5)
    a = _sigmoid(a0_ref[...] + _dot(a_lo, aup_ref[...], HIGHEST))
    g = _dot(_sigmoid(g_lo), gup_ref[...], HIGHEST)
    head_sum = _head_block_ones(w, 1.0)
    kn = k * kk_ref[...]
    kn = kn * lax.rsqrt(_dot(kn * kn, head_sum, HIGHEST) + 1e-12)
    k2 = k * (1.0 + (a - 1.0) * ka_ref[...])
    bonus = _dot(r * k2 * rk_ref[...], head_sum, HIGHEST) * v
    r_o[0] = r
    lw_o[0] = log_w
    k_o[0] = k2
    v_o[0] = v
    kn_o[0] = kn
    b_o[0] = kn * a
    g_o[0] = g
    bonus_o[0] = bonus


def _rwkv_pre(p3, mu, w0, w_up, a0, a_up, g_up, k_k, k_a, r_k, tt=256):
    b, s, _ = p3.shape
    w = RWKV_W
    row = lambda x: x.reshape(1, -1)
    full = lambda shape: pl.BlockSpec(shape, lambda bi, t: (0,) * len(shape))
    out = jax.ShapeDtypeStruct((b, s, w), F32)
    return pl.pallas_call(
        functools.partial(_rwkv_pre_kernel, tt=tt),
        grid=(b, s // tt),
        in_specs=[pl.BlockSpec((1, tt, 1024), lambda bi, t: (bi, t, COL_RWKV // 1024)),
                  full((1, 1024)), full((1, w)), full((64, w)), full((1, w)), full((64, w)),
                  full((128, w)), full((1, w)), full((1, w)), full((1, w))],
        out_specs=[pl.BlockSpec((1, tt, w), lambda bi, t: (bi, t, 0))] * 8,
        out_shape=[out] * 8,
        scratch_shapes=[pltpu.VMEM((1, 1024), F32)],
        compiler_params=_cparams(("parallel", "arbitrary")),
        name="rwkv_pre",
    )(p3, row(mu), row(w0), w_up, row(a0), a_up, g_up, row(k_k), row(k_a), row(r_k))


def _bdot(a, b):
    return _dot(a.astype(BF16), b.astype(BF16))


def _bnt(a, b):
    return _nt(a.astype(BF16), b.astype(BF16))


def _btn(a, b):
    return _tn(a.astype(BF16), b.astype(BF16))


def _rwkv_chunk_kernel(r_ref, lw_ref, k_ref, v_ref, kn_ref, b_ref, y_ref, s_ref, *, cs, nch):
    c = pl.program_id(1)

    @pl.when(c == 0)
    def _():
        s_ref[...] = jnp.zeros_like(s_ref)

    rows = cs * nch
    big_row = lax.broadcasted_iota(jnp.int32, (rows, rows), 0)
    big_col = lax.broadcasted_iota(jnp.int32, (rows, rows), 1)
    same_chunk = (big_row >> CHUNK_SHIFT) == (big_col >> CHUNK_SHIFT)
    tri = jnp.where(same_chunk, jnp.where(big_col <= big_row, 1.0, 0.0), 0.0).astype(F32)
    row = lax.broadcasted_iota(jnp.int32, (cs, cs), 0)
    col = lax.broadcasted_iota(jnp.int32, (cs, cs), 1)
    incl = col <= row
    strict = col < row
    eye = jnp.where(col == row, 1.0, 0.0).astype(F32)
    lw = lw_ref[0]
    cum = _dot(tri, lw, HIGHEST)
    e_incl = jnp.exp(cum)
    e_neg = jnp.exp(-cum)
    abar_all = -kn_ref[0] * jnp.exp(cum - lw)
    rbar_all = r_ref[0] * e_incl
    bt_all = b_ref[0] * e_neg
    kt_all = k_ref[0] * e_neg
    v_all = v_ref[0]
    n_double = max(cs.bit_length() - 2, 0)
    probs = [(j, h) for j in range(nch) for h in range(RWKV_HEADS)]
    cut = lambda t, jh: t[jh[0] * cs:(jh[0] + 1) * cs, jh[1] * HEAD_DIM:(jh[1] + 1) * HEAD_DIM]
    each = lambda fn: {jh: fn(jh) for jh in probs}
    abar, rbar = each(lambda jh: cut(abar_all, jh)), each(lambda jh: cut(rbar_all, jh))
    bt, kt, v = each(lambda jh: cut(bt_all, jh)), each(lambda jh: cut(kt_all, jh)), each(lambda jh: cut(v_all, jh))
    p_last = each(lambda jh: e_incl[(jh[0] + 1) * cs - 1:(jh[0] + 1) * cs, jh[1] * HEAD_DIM:(jh[1] + 1) * HEAD_DIM])
    ar = each(lambda jh: jnp.concatenate([abar[jh], rbar[jh]], axis=0))
    ar_b = each(lambda jh: _bnt(ar[jh], bt[jh]))
    ar_k = each(lambda jh: _bnt(ar[jh], kt[jh]))
    a_ab = each(lambda jh: jnp.where(strict, ar_b[jh][:cs], 0.0))
    a_ak = each(lambda jh: jnp.where(strict, ar_k[jh][:cs], 0.0))
    a_rb = each(lambda jh: jnp.where(incl, ar_b[jh][cs:], 0.0))
    a_rk = each(lambda jh: jnp.where(incl, ar_k[jh][cs:], 0.0))
    ak_v = each(lambda jh: _bdot(a_ak[jh], v[jh]))
    inv = each(lambda jh: eye + a_ab[jh])
    power = a_ab
    for _ in range(n_double):
        power = each(lambda jh: _bdot(power[jh], power[jh]))
        inv = each(lambda jh: inv[jh] + _bdot(inv[jh], power[jh]))
    a_hat = each(lambda jh: _bdot(inv[jh], abar[jh]))
    u_hat = each(lambda jh: _bdot(inv[jh], ak_v[jh]))
    r_hat = each(lambda jh: rbar[jh] + _bdot(a_rb[jh], a_hat[jh]))
    y_hat = each(lambda jh: _bdot(a_rb[jh], u_hat[jh]) + _bdot(a_rk[jh], v[jh]))
    g_mat = each(lambda jh: eye * p_last[jh] + _btn(a_hat[jh], bt[jh] * p_last[jh]))
    h_mat = each(lambda jh: _btn(jnp.concatenate([u_hat[jh], v[jh]], axis=0),
                                 jnp.concatenate([bt[jh], kt[jh]], axis=0) * p_last[jh]))
    states = [s_ref[h] for h in range(RWKV_HEADS)]
    for j in range(nch):
        for h in range(RWKV_HEADS):
            y_ref[0, j * cs:(j + 1) * cs, h * HEAD_DIM:(h + 1) * HEAD_DIM] = (
                _bnt(r_hat[j, h], states[h]) + y_hat[j, h])
        states = [_dot(states[h], g_mat[j, h], HIGHEST) + h_mat[j, h] for h in range(RWKV_HEADS)]
    for h in range(RWKV_HEADS):
        s_ref[h] = states[h]


def _rwkv_chunk(r, lw, k, v, kn, b_, cs=CHUNK, nch=4):
    b, s, w = r.shape
    spec = pl.BlockSpec((1, cs * nch, w), lambda bi, c: (bi, c, 0))
    return pl.pallas_call(
        functools.partial(_rwkv_chunk_kernel, cs=cs, nch=nch),
        grid=(b, s // (cs * nch)),
        in_specs=[spec] * 6,
        out_specs=spec,
        out_shape=jax.ShapeDtypeStruct((b, s, w), F32),
        scratch_shapes=[pltpu.VMEM((RWKV_HEADS, HEAD_DIM, HEAD_DIM), F32)],
        compiler_params=_cparams(("parallel", "arbitrary")),
        name="rwkv_chunk",
    )(r, lw, k, v, kn, b_)


def _merge_kernel(x_ref, xb_ref, swa_ref, y_ref, g_ref, bonus_ref, fox_ref, dsa_ref, gng_ref, gnb_ref,
                  gbias_ref, wg_ref, wb_ref, wo_ref, lng_ref, lnb_ref, o_ref, ob_ref):
    y = y_ref[...]
    xb = xb_ref[...]
    head_mean = _head_block_ones(RWKV_W, 1.0 / HEAD_DIM)
    yc = y - _dot(y, head_mean, HIGHEST)
    yv = _dot(yc * yc, head_mean, HIGHEST)
    o_rwkv = (yc * lax.rsqrt(yv + RWKV_GN_EPS) * gng_ref[...] + gnb_ref[...] + bonus_ref[...]) * g_ref[...]
    branches = (swa_ref[...], o_rwkv, fox_ref[...], dsa_ref[...])
    merged = jnp.zeros(o_ref.shape, F32)
    off = 0
    for i, o in enumerate(branches):
        width = MIX_WIDTHS[i]
        proj = _dot(o.astype(BF16), wb_ref[off:off + width, :])
        gate = _sigmoid(_dot(xb, wg_ref[0, :, i * D_MODEL:(i + 1) * D_MODEL]) + gbias_ref[i:i + 1, :])
        merged = merged + gate * proj
        off += width
    y_out = _dot(merged.astype(BF16), wo_ref[...])
    xn = _layer_norm(DN_ALPHA * x_ref[...] + y_out, lng_ref[...], lnb_ref[...])
    o_ref[...] = xn
    ob_ref[...] = xn.astype(BF16)


def _merge(xf, xb, o_swa, y, g, bonus, o_fox, o_dsa, gn_g, gn_b, gate_bias, w_gate_all, layer, w_branch, w_out,
           ln_g, ln_b, tm=512):
    n, d = xf.shape
    w_gate = w_gate_all
    row = lambda x: x.reshape(1, -1)
    tok = lambda wdt: pl.BlockSpec((tm, wdt), lambda i: (i, 0))
    full = lambda shape: pl.BlockSpec(shape, lambda i: (0,) * len(shape))
    return pl.pallas_call(
        _merge_kernel,
        grid=(n // tm,),
        in_specs=[tok(d), tok(d), tok(512), tok(256), tok(256), tok(256), tok(256), tok(256),
                  full((1, 256)), full((1, 256)), full((N_BRANCHES, d)),
                  pl.BlockSpec((1, d, N_BRANCHES * d), lambda i: (layer, 0, 0)),
                  full((sum(MIX_WIDTHS), d)), full((d, d)), full((1, d)), full((1, d))],
        out_specs=[tok(d), tok(d)],
        out_shape=[jax.ShapeDtypeStruct((n, d), F32), jax.ShapeDtypeStruct((n, d), BF16)],
        compiler_params=_cparams(("parallel",)),
        name="merge",
    )(xf, xb, o_swa, y, g, bonus, o_fox, o_dsa, row(gn_g), row(gn_b), gate_bias, w_gate, w_branch, w_out,
      row(ln_g), row(ln_b))


def _silu(x):
    return x * _sigmoid(x)


def _ffn_kernel(xb_ref, x_ref, w1_ref, w3_ref, w2_ref, lng_ref, lnb_ref, o_ref, ob_ref, *, tf):
    xb = xb_ref[...]
    y = None
    for f in range(0, w1_ref.shape[1], tf):
        hidden = _silu(_dot(xb, w1_ref[:, f:f + tf])) * _dot(xb, w3_ref[:, f:f + tf])
        part = _dot(hidden.astype(BF16), w2_ref[f:f + tf, :])
        y = part if y is None else y + part
    xn = _layer_norm(DN_ALPHA * x_ref[...] + y, lng_ref[...], lnb_ref[...])
    o_ref[...] = xn
    ob_ref[...] = xn.astype(BF16)


def _ffn(xb, xf, w1, w3, w2, ln_g, ln_b, tm=512, tf=1408):
    n, d = xf.shape
    ff = w1.shape[1]
    row = lambda x: x.reshape(1, -1)
    tok = pl.BlockSpec((tm, d), lambda i: (i, 0))
    resident = lambda shape: pl.BlockSpec(shape, lambda i: (0, 0), pipeline_mode=pl.Buffered(1))
    return pl.pallas_call(
        functools.partial(_ffn_kernel, tf=tf),
        grid=(n // tm,),
        in_specs=[tok, tok, resident((d, ff)), resident((d, ff)), resident((ff, d)),
                  resident((1, d)), resident((1, d))],
        out_specs=[tok, tok],
        out_shape=[jax.ShapeDtypeStruct((n, d), F32), jax.ShapeDtypeStruct((n, d), BF16)],
        compiler_params=_cparams(("parallel",)),
        name="ffn",
    )(xb, xf, w1, w3, w2, row(ln_g), row(ln_b))


def _moe_kernel(xb_ref, x_ref, rw_ref, rb_ref, w1_ref, w3_ref, w2_ref, lng_ref, lnb_ref, o_ref,
                comb_ref, rank_ref, cnt_ref, *, sb):
    e = pl.program_id(1)
    tm = x_ref.shape[0]
    lane = lax.broadcasted_iota(jnp.int32, (tm, 128), 1).astype(F32)

    @pl.when(e == 0)
    def _():
        o_ref[...] = jnp.zeros_like(o_ref)
        logits = _dot(x_ref[...], rw_ref[...], HIGHEST) + rb_ref[...]
        logits = jnp.where(lane < N_EXPERTS, logits, NEG_BIG)
        m1 = jnp.max(logits, axis=-1, keepdims=True)
        i1 = jnp.min(jnp.where(logits == m1, lane, 128.0), axis=-1, keepdims=True)
        rest = jnp.where(lane == i1, NEG_BIG, logits)
        m2 = jnp.max(rest, axis=-1, keepdims=True)
        i2 = jnp.min(jnp.where(rest == m2, lane, 128.0), axis=-1, keepdims=True)
        e2 = jnp.exp(m2 - m1)
        comb_ref[...] = jnp.where(lane == i1, 1.0 / (1.0 + e2), 0.0) + jnp.where(lane == i2, e2 / (1.0 + e2), 0.0)
        chosen = jnp.where(lane == i1, 1.0, 0.0) + jnp.where(lane == i2, 1.0, 0.0)
        r = lax.broadcasted_iota(jnp.int32, (tm, tm), 0)
        c = lax.broadcasted_iota(jnp.int32, (tm, tm), 1)
        earlier = jnp.where(c < r, 1.0, 0.0).astype(BF16)
        rank = jnp.where(chosen > 0.5, _dot(earlier, chosen.astype(BF16)), -1.0)
        lane8 = lax.broadcasted_iota(jnp.int32, (8, 128), 1)
        row8 = lax.broadcasted_iota(jnp.int32, (8, 128), 0)
        rank_ref[...] = _nt(jnp.where(lane8 == row8, 1.0, 0.0).astype(F32), rank, HIGHEST)
        cnt_ref[...] = jnp.sum(chosen, axis=0, keepdims=True)

    e_f = e.astype(F32)
    n_tok = jnp.sum(jnp.where(lane[0:1, :] == e_f, cnt_ref[...], 0.0)).astype(jnp.int32)
    n_blocks = lax.div(n_tok + sb - 1, sb)
    rank_row = rank_ref[pl.ds(e, 1), :]
    weight = jnp.sum(jnp.where(lane == e_f, comb_ref[...], 0.0), axis=-1, keepdims=True)
    slot0 = lax.broadcasted_iota(jnp.int32, (sb, tm), 0).astype(F32)

    def block(s, carry):
        pick = jnp.where(rank_row == slot0 + (s * sb).astype(F32), 1.0, 0.0).astype(BF16)
        xs = _dot(pick, xb_ref[...]).astype(BF16)
        hidden = _silu(_dot(xs, w1_ref[0])) * _dot(xs, w3_ref[0])
        out = _dot(hidden.astype(BF16), w2_ref[0]).astype(BF16)
        o_ref[...] += _tn(pick, out) * weight
        return carry

    lax.fori_loop(0, n_blocks, block, 0)

    @pl.when(e == pl.num_programs(1) - 1)
    def _():
        o_ref[...] = _layer_norm(DN_ALPHA * x_ref[...] + o_ref[...], lng_ref[...], lnb_ref[...])


def _moe(xb, xf, router_w, router_b, w1, w3, w2, ln_g, ln_b, tm=1024, sb=288):
    n, d = xf.shape
    ne, _, fe = w1.shape
    tm = min(tm, n)
    rw = jnp.pad(router_w, ((0, 0), (0, 128 - ne)))
    rb = jnp.pad(router_b, (0, 128 - ne)).reshape(1, 128)
    row = lambda x: x.reshape(1, -1)
    return pl.pallas_call(
        functools.partial(_moe_kernel, sb=sb),
        grid=(n // tm, ne),
        in_specs=[pl.BlockSpec((tm, d), lambda i, e: (i, 0)),
                  pl.BlockSpec((tm, d), lambda i, e: (i, 0)),
                  pl.BlockSpec((d, 128), lambda i, e: (0, 0)),
                  pl.BlockSpec((1, 128), lambda i, e: (0, 0)),
                  pl.BlockSpec((1, d, fe), lambda i, e: (e, 0, 0)),
                  pl.BlockSpec((1, d, fe), lambda i, e: (e, 0, 0)),
                  pl.BlockSpec((1, fe, d), lambda i, e: (e, 0, 0)),
                  pl.BlockSpec((1, d), lambda i, e: (0, 0)),
                  pl.BlockSpec((1, d), lambda i, e: (0, 0))],
        out_specs=pl.BlockSpec((tm, d), lambda i, e: (i, 0)),
        out_shape=jax.ShapeDtypeStruct((n, d), F32),
        scratch_shapes=[pltpu.VMEM((tm, 128), F32), pltpu.VMEM((8, tm), F32), pltpu.VMEM((1, 128), F32)],
        compiler_params=pltpu.CompilerParams(dimension_semantics=("parallel", "arbitrary"),
                                             vmem_limit_bytes=MOE_VMEM_LIMIT),
        name="moe",
    )(xb, xf, rw, rb, w1, w3, w2, row(ln_g), row(ln_b))


def _mixer_layer(xf, xb, bsz, seq, layer, w_in, w_gate_all, sinks, mu, w0, w_up, a0, a_up, g_up, k_k, k_a, r_k,
                 gn_g, gn_b, f_bias, gate_bias, w_branch, w_out, ln_g, ln_b):
    p2 = _matmul(xb, _layout_w_in(w_in[layer]).astype(BF16), tm=1024, tn=P_WIDTH // 2)
    p3 = p2.reshape(bsz, seq, P_WIDTH)
    o_swa = _swa(p3, sinks)
    o_fox = _fox(p3, _fox_c(p3, f_bias))
    o_dsa = _dsa(p3)
    r, lw, k2, v, kn, b_, g, bonus = _rwkv_pre(p3, mu, w0, w_up, a0, a_up, g_up, k_k, k_a, r_k)
    y = _rwkv_chunk(r, lw, k2, v, kn, b_)
    n = bsz * seq
    flat = lambda t: t.reshape(n, t.shape[-1])
    return _merge(xf, xb, flat(o_swa), flat(y), flat(g), flat(bonus), flat(o_fox), flat(o_dsa), gn_g, gn_b,
                  gate_bias, w_gate_all, layer, w_branch.astype(BF16), w_out.astype(BF16), ln_g, ln_b)


def kernel(x, w_in, swa_sinks, rwkv_mu, rwkv_w0, rwkv_w_up, rwkv_a0, rwkv_a_up, rwkv_g_up, rwkv_k_k, rwkv_k_a,
           rwkv_r_k, rwkv_gn_g, rwkv_gn_b, fox_f_bias, gate_bias, w_branch, w_out, ln_g, ln_b, ffn_w1, ffn_w3,
           ffn_w2, router_w, router_b, exp_w1, exp_w3, exp_w2):
    bsz, seq, d = x.shape
    xf = x.reshape(bsz * seq, d)
    xb = xf.astype(BF16)
    w_gate_all = w_in[:, :, IN_OFF_GATE:].astype(BF16)
    for layer in range(DEPTH):
        xf, xb = _mixer_layer(xf, xb, bsz, seq, layer, w_in, w_gate_all, swa_sinks[layer], rwkv_mu[layer], rwkv_w0[layer],
                              rwkv_w_up[layer], rwkv_a0[layer], rwkv_a_up[layer], rwkv_g_up[layer],
                              rwkv_k_k[layer], rwkv_k_a[layer], rwkv_r_k[layer], rwkv_gn_g[layer],
                              rwkv_gn_b[layer], fox_f_bias[layer], gate_bias[layer], w_branch[layer],
                              w_out[layer], ln_g[layer, 0], ln_b[layer, 0])
        j = layer // 2
        if layer % 2 == 0:
            xf, xb = _ffn(xb, xf, ffn_w1[j].astype(BF16), ffn_w3[j].astype(BF16), ffn_w2[j].astype(BF16),
                          ln_g[layer, 1], ln_b[layer, 1])
        else:
            xf = _moe(xb, xf, router_w[j], router_b[j], exp_w1[j].astype(BF16), exp_w3[j].astype(BF16),
                      exp_w2[j].astype(BF16), ln_g[layer, 1], ln_b[layer, 1])
            xb = xf.astype(BF16) if layer + 1 < DEPTH else None
    return xf.reshape(bsz, seq, d)
```

```python
import functools

import jax
import jax.numpy as jnp
from jax import lax
from jax.experimental import pallas as pl
from jax.experimental.pallas import tpu as pltpu

F32 = jnp.float32
BF16 = jnp.bfloat16
HIGHEST = lax.Precision.HIGHEST

D_MODEL = 1024
DEPTH = 2
CHUNK = 64
HEAD_DIM = 64
CHUNK_SHIFT = 6
HEAD_SHIFT = 6
SWA_HEADS = 8
SWA_KV_HEADS = 2
SWA_GROUP = SWA_HEADS // SWA_KV_HEADS
SWA_WINDOW = 128
SWA_WIN_CHUNKS = SWA_WINDOW // CHUNK
RWKV_HEADS = 4
RWKV_W = RWKV_HEADS * HEAD_DIM
RWKV_GN_EPS = 64e-5
FOX_HEADS = 4
DSA_HEADS = 4
IDX_HEADS = 4
IDX_DIM = 64
DSA_TOPK_MAX = 256
D_FF = 2816
N_EXPERTS = 8
D_FF_EXPERT = 1408
N_BRANCHES = 4
DN_ALPHA = (2 * DEPTH) ** 0.25
LN_EPS = 1e-5
ATTN_SCALE = HEAD_DIM ** -0.5
IDX_SCALE = IDX_DIM ** -0.5
IDX_W_SCALE = IDX_HEADS ** -0.5
MIX_WIDTHS = (SWA_HEADS * HEAD_DIM, RWKV_W, FOX_HEADS * HEAD_DIM, DSA_HEADS * HEAD_DIM)

COL_RWKV = 0
COL_SWA_Q = 1024
COL_FOX_KV = 1536
COL_DSA_QQ = 2048
COL_FOX_Q = 2560
COL_SWA_KV = 2816
COL_DSA_KV = 3072
COL_MISC = 3328
P_WIDTH = 3584

NEG_BIG = -1e30
INT_MIN = -(2 ** 31)
VMEM_LIMIT = 48 * 1024 * 1024
MOE_VMEM_LIMIT = 58 * 1024 * 1024


IN_OFF_SWA, IN_OFF_RWKV, IN_OFF_FOX, IN_OFF_DSA, IN_OFF_GATE = 0, 768, 1792, 2564, 3272
D_IN = IN_OFF_GATE + N_BRANCHES * D_MODEL


def _layout_w_in(w):
    d = w.shape[0]
    z = lambda n: jnp.zeros((d, n), w.dtype)
    swa, rwkv, fox, dsa = IN_OFF_SWA, IN_OFF_RWKV, IN_OFF_FOX, IN_OFF_DSA
    parts = [
        w[:, rwkv:rwkv + 1024],
        w[:, swa:swa + 512],
        w[:, fox + 256:fox + 768],
        w[:, dsa:dsa + 256], w[:, dsa + 384:dsa + 640],
        w[:, fox:fox + 256],
        w[:, swa + 512:swa + 768],
        w[:, dsa + 256:dsa + 384], w[:, dsa + 640:dsa + 704], z(64),
        w[:, fox + 768:fox + 772], w[:, dsa + 704:dsa + 708], z(120),
        z(128),
    ]
    return jnp.concatenate(parts, axis=1)


def _cparams(sem):
    return pltpu.CompilerParams(dimension_semantics=sem, vmem_limit_bytes=VMEM_LIMIT)


def _nt(a, b, precision=None):
    return lax.dot_general(a, b, (((1,), (1,)), ((), ())), precision=precision,
                           preferred_element_type=F32)


def _tn(a, b, precision=None):
    return lax.dot_general(a, b, (((0,), (0,)), ((), ())), precision=precision,
                           preferred_element_type=F32)


def _dot(a, b, precision=None):
    return jnp.dot(a, b, precision=precision, preferred_element_type=F32)


def _sigmoid(x):
    return 1.0 / (1.0 + jnp.exp(-x))


def _layer_norm(z, g, b):
    mu = jnp.mean(z, axis=-1, keepdims=True)
    zc = z - mu
    var = jnp.mean(zc * zc, axis=-1, keepdims=True)
    return zc * lax.rsqrt(var + LN_EPS) * g + b


def _skewed(n, first, second):
    out = []
    staged = first(0)
    for h in range(n):
        nxt = first(h + 1) if h + 1 < n else None
        out.append(second(h, staged))
        staged = nxt
    return out


def _paired_blocks(n, body, carry):
    carry = lax.fori_loop(0, lax.div(n, 2), lambda t, c: body((2 * t, 2 * t + 1), c), carry)
    return lax.cond(lax.rem(n, 2) == 1, lambda c: body((n - 1,), c), lambda c: c, carry)


def _head_block_ones(n, scale):
    r = lax.broadcasted_iota(jnp.int32, (n, n), 0) >> HEAD_SHIFT
    c = lax.broadcasted_iota(jnp.int32, (n, n), 1) >> HEAD_SHIFT
    return jnp.where(r == c, scale, 0.0).astype(F32)


def _mm_kernel(a_ref, b_ref, o_ref):
    o_ref[...] = _dot(a_ref[...], b_ref[...])


def _matmul(a, b, tm, tn):
    m, k = a.shape
    n = b.shape[1]
    tm = min(tm, m)
    return pl.pallas_call(
        _mm_kernel,
        grid=(m // tm, n // tn),
        in_specs=[pl.BlockSpec((tm, k), lambda i, j: (i, 0)),
                  pl.BlockSpec((k, tn), lambda i, j: (0, j))],
        out_specs=pl.BlockSpec((tm, tn), lambda i, j: (i, j)),
        out_shape=jax.ShapeDtypeStruct((m, n), F32),
        compiler_params=_cparams(("parallel", "parallel")),
        name="in_proj",
    )(a, b)


def _swa_kernel(sink_ref, q_ref, kv_ref, o_ref, *, tq):
    i = pl.program_id(1)
    win = tq + SWA_WINDOW
    s0 = pl.multiple_of(jnp.maximum(i * tq - SWA_WINDOW, 0), SWA_WINDOW)
    k_chunk = (s0 + lax.broadcasted_iota(jnp.int32, (win, tq), 0)) >> CHUNK_SHIFT
    q_chunk = (i * tq + lax.broadcasted_iota(jnp.int32, (win, tq), 1)) >> CHUNK_SHIFT
    bias = jnp.where(k_chunk <= q_chunk,
                     jnp.where(k_chunk >= q_chunk - SWA_WIN_CHUNKS, 0.0, NEG_BIG), NEG_BIG)
    ks = [kv_ref[0, pl.ds(s0, win), hk * HEAD_DIM:(hk + 1) * HEAD_DIM].astype(BF16)
          for hk in range(SWA_KV_HEADS)]
    kv_w = SWA_KV_HEADS * HEAD_DIM
    vs = [kv_ref[0, pl.ds(s0, win), kv_w + hk * HEAD_DIM:kv_w + (hk + 1) * HEAD_DIM].astype(BF16)
          for hk in range(SWA_KV_HEADS)]

    def logits(h):
        q = (q_ref[0, :, h * HEAD_DIM:(h + 1) * HEAD_DIM] * ATTN_SCALE).astype(BF16)
        return _nt(ks[h // SWA_GROUP], q) + bias

    def finish(h, s):
        sink = sink_ref[h]
        m = jnp.maximum(jnp.max(s, axis=0, keepdims=True), sink)
        e = jnp.exp(s - m)
        denom = jnp.sum(e, axis=0, keepdims=True) + jnp.exp(sink - m)
        return _tn(vs[h // SWA_GROUP], e.astype(BF16)) / denom

    o_ref[0] = jnp.concatenate(_skewed(SWA_HEADS, logits, finish), axis=0).T


def _swa(p3, sinks, tq=256):
    b, s, _ = p3.shape
    kv_w = 2 * SWA_KV_HEADS * HEAD_DIM
    return pl.pallas_call(
        functools.partial(_swa_kernel, tq=tq),
        grid=(b, s // tq),
        in_specs=[pl.BlockSpec(memory_space=pltpu.SMEM),
                  pl.BlockSpec((1, tq, 512), lambda bi, i: (bi, i, COL_SWA_Q // 512)),
                  pl.BlockSpec((1, s, kv_w), lambda bi, i: (bi, 0, COL_SWA_KV // kv_w))],
        out_specs=pl.BlockSpec((1, tq, 512), lambda bi, i: (bi, i, 0)),
        out_shape=jax.ShapeDtypeStruct((b, s, 512), F32),
        compiler_params=_cparams(("parallel", "parallel")),
        name="swa",
    )(sinks, p3, p3)


def _fox_c_kernel(f_ref, bias_ref, c_ref, *, blk):
    s = f_ref.shape[1]
    r = lax.broadcasted_iota(jnp.int32, (blk, blk), 0)
    c = lax.broadcasted_iota(jnp.int32, (blk, blk), 1)
    tri = jnp.where(c <= r, 1.0, 0.0).astype(F32)
    carry = jnp.zeros((1, 128), F32)
    for j in range(s // blk):
        x = f_ref[0, j * blk:(j + 1) * blk, :] + bias_ref[...]
        log_f = jnp.minimum(x, 0.0) - jnp.log1p(jnp.exp(-jnp.abs(x)))
        cs = _dot(tri, log_f, HIGHEST) + carry
        carry = cs[blk - 1:blk, :]
        for h in range(FOX_HEADS):
            c_ref[0, h, j * blk:(j + 1) * blk, :] = jnp.broadcast_to(cs[:, h:h + 1], (blk, 128))


def _fox_c(p3, f_bias):
    b, s, _ = p3.shape
    bias_row = jnp.pad(f_bias, (0, 128 - FOX_HEADS)).reshape(1, 128)
    return pl.pallas_call(
        functools.partial(_fox_c_kernel, blk=256),
        grid=(b,),
        in_specs=[pl.BlockSpec((1, s, 128), lambda bi: (bi, 0, COL_MISC // 128)),
                  pl.BlockSpec((1, 128), lambda bi: (0, 0))],
        out_specs=pl.BlockSpec((1, FOX_HEADS, s, 128), lambda bi: (bi, 0, 0, 0)),
        out_shape=jax.ShapeDtypeStruct((b, FOX_HEADS, s, 128), F32),
        compiler_params=_cparams(("parallel",)),
        name="fox_cumsum",
    )(p3, bias_row)


def _fox_kernel(q_ref, k_ref, c_ref, o_ref, lg_ref, acc_ref, *, tq):
    i = pl.program_id(1)
    tk = tq
    v_col = FOX_HEADS * HEAD_DIM
    key_i = lax.broadcasted_iota(jnp.int32, (tk, tq), 0)
    qry_i = lax.broadcasted_iota(jnp.int32, (tk, tq), 1)
    diag_bias = jnp.where(key_i <= qry_i, 0.0, NEG_BIG)
    q = [(q_ref[0, :, h * HEAD_DIM:(h + 1) * HEAD_DIM] * ATTN_SCALE).astype(BF16) for h in range(FOX_HEADS)]

    def logits_blocks(blocks, maxes, masked=False):
        starts = [pl.multiple_of(j * tk, tk) for j in blocks]
        dots = [[_nt(k_ref[0, pl.ds(ks, tk), h * HEAD_DIM:(h + 1) * HEAD_DIM].astype(BF16), q[h])
                 for h in range(FOX_HEADS)] for ks in starts]
        maxes = list(maxes)
        for b, ks in enumerate(starts):
            for h in range(FOX_HEADS):
                c_k = c_ref[0, h, pl.ds(ks, tk), :]
                s = dots[b][h] - jnp.concatenate([c_k] * (tq // 128), axis=1)
                if masked:
                    s = s + diag_bias
                lg_ref[h, pl.ds(ks, tk), :] = s
                maxes[h] = jnp.maximum(maxes[h], jnp.max(s, axis=0, keepdims=True))
        return tuple(maxes)

    maxes = tuple(jnp.full((1, tq), NEG_BIG, F32) for _ in range(FOX_HEADS))
    maxes = _paired_blocks(i, logits_blocks, maxes)
    maxes = logits_blocks((i,), maxes, masked=True)
    acc_ref[...] = jnp.zeros_like(acc_ref)

    def attend_blocks(blocks, sums):
        starts = [pl.multiple_of(j * tk, tk) for j in blocks]
        sums = list(sums)
        for h in range(FOX_HEADS):
            update = None
            for ks in starts:
                p = jnp.exp(lg_ref[h, pl.ds(ks, tk), :] - maxes[h])
                v = k_ref[0, pl.ds(ks, tk), v_col + h * HEAD_DIM:v_col + (h + 1) * HEAD_DIM].astype(BF16)
                pv = _tn(v, p.astype(BF16))
                update = pv if update is None else update + pv
                sums[h] = sums[h] + jnp.sum(p, axis=0, keepdims=True)
            acc_ref[h] += update
        return tuple(sums)

    sums = _paired_blocks(i + 1, attend_blocks, tuple(jnp.zeros((1, tq), F32) for _ in range(FOX_HEADS)))
    out_t = jnp.concatenate([acc_ref[h] / sums[h] for h in range(FOX_HEADS)], axis=0)
    o_ref[0] = out_t.T


def _fox(p3, c, tq=256):
    b, s, _ = p3.shape
    w = FOX_HEADS * HEAD_DIM
    return pl.pallas_call(
        functools.partial(_fox_kernel, tq=tq),
        grid=(b, s // tq),
        in_specs=[pl.BlockSpec((1, tq, w), lambda bi, i: (bi, i, COL_FOX_Q // w)),
                  pl.BlockSpec((1, s, 2 * w), lambda bi, i: (bi, 0, COL_FOX_KV // (2 * w))),
                  pl.BlockSpec((1, FOX_HEADS, s, 128), lambda bi, i: (bi, 0, 0, 0))],
        out_specs=pl.BlockSpec((1, tq, w), lambda bi, i: (bi, i, 0)),
        out_shape=jax.ShapeDtypeStruct((b, s, w), F32),
        scratch_shapes=[pltpu.VMEM((FOX_HEADS, s, tq), F32), pltpu.VMEM((FOX_HEADS, HEAD_DIM, tq), F32)],
        compiler_params=_cparams(("parallel", "parallel")),
        name="fox",
    )(p3, p3, c)


def _key_to_score(key):
    return pltpu.bitcast(key ^ ((key >> 31) & 0x7FFFFFFF), F32)


def _bit_planes(score):
    bits = pltpu.bitcast(score, jnp.int32)
    image = bits ^ ((bits >> 31) | INT_MIN)
    rows = image.reshape(32, 8, image.shape[1])
    v = [rows[j] for j in range(32)]
    for d, mask in ((16, 0x0000FFFF), (8, 0x00FF00FF), (4, 0x0F0F0F0F), (2, 0x33333333), (1, 0x55555555)):
        for k in range(32):
            if k & d == 0:
                t = (v[k] ^ lax.shift_right_logical(v[k + d], d)) & mask
                v[k] = v[k] ^ t
                v[k + d] = v[k + d] ^ lax.shift_left(t, d)
    return v


def _dsa_kernel(qq_ref, misc_ref, kv_ref, o_ref, sc_ref, lg_ref, acc_ref, pl_ref, *, tq, kb, topk):
    i = pl.program_id(1)
    n_kb = lax.div((i + 1) * tq + kb - 1, kb)
    lane8 = lax.broadcasted_iota(jnp.int32, (8, 128), 1)
    row8 = lax.broadcasted_iota(jnp.int32, (8, 128), 0)
    pick = jnp.where((lane8 == row8 + 4) & (row8 < IDX_HEADS), 1.0, 0.0).astype(F32)
    w_t = _nt(pick, misc_ref[0], HIGHEST) * (IDX_W_SCALE * IDX_SCALE)
    q_chunk = (i * tq + lax.broadcasted_iota(jnp.int32, (1, tq), 1)) >> CHUNK_SHIFT
    qw = DSA_HEADS * HEAD_DIM
    q_idx = [qq_ref[0, :, qw + h * IDX_DIM:qw + (h + 1) * IDX_DIM].astype(BF16) for h in range(IDX_HEADS)]

    def score_blocks(blocks, carry):
        starts = [pl.multiple_of(j * kb, kb) for j in blocks]
        dots = []
        for ks in starts:
            k_idx = kv_ref[0, pl.ds(ks, kb), 2 * HEAD_DIM:2 * HEAD_DIM + IDX_DIM].astype(BF16)
            dots.append([_nt(k_idx, q_idx[h]) for h in range(IDX_HEADS)])
        for b, ks in enumerate(starts):
            score = jnp.maximum(dots[b][0], 0.0) * w_t[0:1, :]
            for h in range(1, IDX_HEADS):
                score = score + jnp.maximum(dots[b][h], 0.0) * w_t[h:h + 1, :]
            k_chunk = (ks + lax.broadcasted_iota(jnp.int32, (kb, tq), 0)) >> CHUNK_SHIFT
            score = jnp.where(k_chunk <= q_chunk, score, -jnp.inf)
            sc_ref[pl.ds(ks, kb), :] = score
            planes = _bit_planes(score)
            for t in range(32):
                pl_ref[blocks[b], t] = planes[t]
        return carry

    _paired_blocks(n_kb, score_blocks, 0)
    n_blocks = pl_ref.shape[0]

    def clear_planes(j, carry):
        pl_ref[j] = jnp.zeros(pl_ref.shape[1:], jnp.int32)
        return carry

    lax.fori_loop(n_kb, n_blocks, clear_planes, 0)

    def count(pred):
        def body(j, acc):
            blk = sc_ref[pl.ds(pl.multiple_of(j * kb, kb), kb), :]
            hit = jnp.where(pred(blk), 1, 0).astype(jnp.int32).reshape(kb // 8, 8, tq)
            parts = [hit[g] for g in range(kb // 8)]
            while len(parts) > 1:
                parts = [parts[g] + parts[g + 1] for g in range(0, len(parts), 2)]
            return acc + parts[0]
        acc = lax.fori_loop(0, n_kb, body, jnp.zeros((8, tq), jnp.int32))
        return jnp.sum(acc.astype(F32), axis=0, keepdims=True)

    def search_by_compares():
        zero = jnp.zeros((1, tq), F32)
        thr_key = jnp.where(count(lambda blk: blk >= zero) >= topk, 0, INT_MIN).astype(jnp.int32)

        def bit_step(bi, thr_key):
            cand = thr_key | lax.shift_left(jnp.int32(1), 30 - bi)
            cand_score = _key_to_score(cand)
            return jnp.where(count(lambda blk: blk >= cand_score) >= topk, cand, thr_key)

        thr_key = lax.fori_loop(0, 31, bit_step, thr_key)
        t = jnp.where(thr_key == INT_MIN, -jnp.inf, _key_to_score(thr_key))
        return t, count(lambda blk: blk > t)

    def search_by_planes():
        def bit_pass(t, carry):
            prefix, above, alive = carry
            ones = [alive[j] & pl_ref[j, t] for j in range(n_blocks)]
            parts = [lax.population_count(o) for o in ones]
            while len(parts) > 1:
                parts = [parts[g] + parts[g + 1] for g in range(0, len(parts), 2)]
            reach = above + jnp.sum(parts[0].astype(F32), axis=0, keepdims=True)
            take = reach >= topk
            prefix = jnp.where(take, prefix | lax.shift_left(jnp.int32(1), 31 - t), prefix)
            alive = tuple(jnp.where(take, ones[j], alive[j] ^ ones[j]) for j in range(n_blocks))
            return prefix, jnp.where(take, above, reach), alive

        init = (jnp.zeros((1, tq), jnp.int32), jnp.zeros((1, tq), F32),
                tuple(jnp.full((8, tq), -1, jnp.int32) for _ in range(n_blocks)))
        image, _, _ = lax.fori_loop(0, 32, bit_pass, init)
        return pltpu.bitcast(image ^ (~(image >> 31) | INT_MIN), F32)

    thr_planes = search_by_planes()
    n_above = count(lambda blk: blk > thr_planes)
    n_reach = count(lambda blk: blk >= thr_planes)
    confirmed = jnp.min(jnp.where(n_above < topk, jnp.where(n_reach >= topk, 1.0, 0.0), 0.0)) > 0.5
    thr, n_above = lax.cond(confirmed, lambda: (thr_planes, n_above), search_by_compares)
    need = topk - n_above

    r = lax.broadcasted_iota(jnp.int32, (kb, kb), 0)
    c = lax.broadcasted_iota(jnp.int32, (kb, kb), 1)
    earlier = jnp.where(c < r, 1.0, 0.0).astype(BF16)
    q_att = [(qq_ref[0, :, h * HEAD_DIM:(h + 1) * HEAD_DIM] * ATTN_SCALE).astype(BF16)
             for h in range(DSA_HEADS)]

    def logits_blocks(blocks, carry):
        ties_before, maxes = carry
        maxes = list(maxes)
        starts = [pl.multiple_of(j * kb, kb) for j in blocks]
        dots, ties = [], []
        for ks in starts:
            k = kv_ref[0, pl.ds(ks, kb), 0:HEAD_DIM].astype(BF16)
            dots.append([_nt(k, q_att[h]) for h in range(DSA_HEADS)])
            k_chunk = (ks + lax.broadcasted_iota(jnp.int32, (kb, tq), 0)) >> CHUNK_SHIFT
            ties.append(jnp.where(sc_ref[pl.ds(ks, kb), :] == thr, jnp.where(k_chunk <= q_chunk, 1.0, 0.0), 0.0))
        ranks = [_dot(earlier, tie.astype(BF16)) for tie in ties]
        for b, ks in enumerate(starts):
            tie_taken = jnp.where(ranks[b] + ties_before < need, ties[b], 0.0)
            bias = jnp.where(sc_ref[pl.ds(ks, kb), :] > thr, 0.0, jnp.where(tie_taken > 0.5, 0.0, NEG_BIG))
            ties_before = ties_before + jnp.sum(ties[b], axis=0, keepdims=True)
            for h in range(DSA_HEADS):
                s = dots[b][h] + bias
                lg_ref[h, pl.ds(ks, kb), :] = s
                maxes[h] = jnp.maximum(maxes[h], jnp.max(s, axis=0, keepdims=True))
        return ties_before, tuple(maxes)

    init = (jnp.zeros((1, tq), F32), tuple(jnp.full((1, tq), NEG_BIG, F32) for _ in range(DSA_HEADS)))
    _, maxes = _paired_blocks(n_kb, logits_blocks, init)

    acc_ref[...] = jnp.zeros_like(acc_ref)

    def attend_blocks(blocks, sums):
        starts = [pl.multiple_of(j * kb, kb) for j in blocks]
        vs = [kv_ref[0, pl.ds(ks, kb), HEAD_DIM:2 * HEAD_DIM].astype(BF16) for ks in starts]
        sums = list(sums)
        for h in range(DSA_HEADS):
            update = None
            for b, ks in enumerate(starts):
                p = jnp.exp(lg_ref[h, pl.ds(ks, kb), :] - maxes[h])
                pv = _tn(vs[b], p.astype(BF16))
                update = pv if update is None else update + pv
                sums[h] = sums[h] + jnp.sum(p, axis=0, keepdims=True)
            acc_ref[h] += update
        return tuple(sums)

    sums = _paired_blocks(n_kb, attend_blocks, tuple(jnp.zeros((1, tq), F32) for _ in range(DSA_HEADS)))
    out_t = jnp.concatenate([acc_ref[h] / sums[h] for h in range(DSA_HEADS)], axis=0)
    o_ref[0] = out_t.T


def _dsa(p3, tq=256, kb=256):
    b, s, _ = p3.shape
    assert kb == 8 * 32, "a key block is the 32 row groups of one bit-plane word"
    topk = min(DSA_TOPK_MAX, s // 4)
    return pl.pallas_call(
        functools.partial(_dsa_kernel, tq=tq, kb=kb, topk=topk),
        grid=(b, s // tq),
        in_specs=[pl.BlockSpec((1, tq, 512), lambda bi, i: (bi, i, COL_DSA_QQ // 512)),
                  pl.BlockSpec((1, tq, 128), lambda bi, i: (bi, i, COL_MISC // 128)),
                  pl.BlockSpec((1, s, 256), lambda bi, i: (bi, 0, COL_DSA_KV // 256))],
        out_specs=pl.BlockSpec((1, tq, 256), lambda bi, i: (bi, i, 0)),
        out_shape=jax.ShapeDtypeStruct((b, s, 256), F32),
        scratch_shapes=[pltpu.VMEM((s, tq), F32), pltpu.VMEM((DSA_HEADS, s, tq), F32),
                        pltpu.VMEM((DSA_HEADS, HEAD_DIM, tq), F32), pltpu.VMEM((s // kb, 32, 8, tq), jnp.int32)],
        compiler_params=_cparams(("parallel", "parallel")),
        name="dsa",
    )(p3, p3, p3)


def _rwkv_pre_kernel(p_ref, mu_ref, w0_ref, wup_ref, a0_ref, aup_ref, gup_ref, kk_ref, ka_ref, rk_ref,
                     r_o, lw_o, k_o, v_o, kn_o, b_o, g_o, bonus_o, last_ref, *, tt):
    t = pl.program_id(1)

    @pl.when(t == 0)
    def _():
        last_ref[...] = jnp.zeros_like(last_ref)

    p = p_ref[0]
    row = lax.broadcasted_iota(jnp.int32, p.shape, 0)
    p_prev = jnp.where(row == 0, last_ref[...], pltpu.roll(p, 1, axis=0))
    last_ref[...] = p[tt - 1:tt, :]
    ps = p + mu_ref[...] * (p_prev - p)
    w = RWKV_W
    r, k, v = ps[:, 0:w], ps[:, w:2 * w], ps[:, 2 * w:3 * w]
    w_lo, a_lo, g_lo = ps[:, 3 * w:3 * w + 64], ps[:, 3 * w + 64:3 * w + 128], ps[:, 3 * w + 128:3 * w + 256]
    ww = w0_ref[...] + _dot(jnp.tanh(w_lo), wup_ref[...], HIGHEST)
    softplus_neg = jnp.maximum(-ww, 0.0) + jnp.log1p(jnp.exp(-jnp.abs(ww)))
    log_w = -jnp.exp(-softplus_neg - 0.5)
    a = _sigmoid(a0_ref[...] + _dot(a_lo, aup_ref[...], HIGHEST))
    g = _dot(_sigmoid(g_lo), gup_ref[...], HIGHEST)
    head_sum = _head_block_ones(w, 1.0)
    kn = k * kk_ref[...]
    kn = kn * lax.rsqrt(_dot(kn * kn, head_sum, HIGHEST) + 1e-12)
    k2 = k * (1.0 + (a - 1.0) * ka_ref[...])
    bonus = _dot(r * k2 * rk_ref[...], head_sum, HIGHEST) * v
    r_o[0] = r
    lw_o[0] = log_w
    k_o[0] = k2
    v_o[0] = v
    kn_o[0] = kn
    b_o[0] = kn * a
    g_o[0] = g
    bonus_o[0] = bonus


def _rwkv_pre(p3, mu, w0, w_up, a0, a_up, g_up, k_k, k_a, r_k, tt=256):
    b, s, _ = p3.shape
    w = RWKV_W
    row = lambda x: x.reshape(1, -1)
    full = lambda shape: pl.BlockSpec(shape, lambda bi, t: (0,) * len(shape))
    out = jax.ShapeDtypeStruct((b, s, w), F32)
    return pl.pallas_call(
        functools.partial(_rwkv_pre_kernel, tt=tt),
        grid=(b, s // tt),
        in_specs=[pl.BlockSpec((1, tt, 1024), lambda bi, t: (bi, t, COL_RWKV // 1024)),
                  full((1, 1024)), full((1, w)), full((64, w)), full((1, w)), full((64, w)),
                  full((128, w)), full((1, w)), full((1, w)), full((1, w))],
        out_specs=[pl.BlockSpec((1, tt, w), lambda bi, t: (bi, t, 0))] * 8,
        out_shape=[out] * 8,
        scratch_shapes=[pltpu.VMEM((1, 1024), F32)],
        compiler_params=_cparams(("parallel", "arbitrary")),
        name="rwkv_pre",
    )(p3, row(mu), row(w0), w_up, row(a0), a_up, g_up, row(k_k), row(k_a), row(r_k))


def _bdot(a, b):
    return _dot(a.astype(BF16), b.astype(BF16))


def _bnt(a, b):
    return _nt(a.astype(BF16), b.astype(BF16))


def _btn(a, b):
    return _tn(a.astype(BF16), b.astype(BF16))


def _rwkv_chunk_kernel(r_ref, lw_ref, k_ref, v_ref, kn_ref, b_ref, y_ref, s_ref, *, cs, nch):
    c = pl.program_id(1)

    @pl.when(c == 0)
    def _():
        s_ref[...] = jnp.zeros_like(s_ref)

    rows = cs * min(nch, 4)
    big_row = lax.broadcasted_iota(jnp.int32, (rows, rows), 0)
    big_col = lax.broadcasted_iota(jnp.int32, (rows, rows), 1)
    same_chunk = (big_row >> CHUNK_SHIFT) == (big_col >> CHUNK_SHIFT)
    tri = jnp.where(same_chunk, jnp.where(big_col <= big_row, 1.0, 0.0), 0.0).astype(F32)
    pw = 2 * HEAD_DIM
    row = lax.broadcasted_iota(jnp.int32, (cs, pw), 0)
    lane = lax.broadcasted_iota(jnp.int32, (cs, pw), 1)
    col = lane & (HEAD_DIM - 1)
    first_head = lane < HEAD_DIM
    incl = col <= row
    strict = col < row
    eye = jnp.where(col == row, 1.0, 0.0).astype(F32)

    def block_diag(x):
        return jnp.concatenate([jnp.where(first_head, x, 0.0), jnp.where(first_head, 0.0, x)], axis=0)

    def diag_blocks(full):
        return jnp.where(first_head, full[:cs], full[cs:])

    lw = lw_ref[0]
    cum = jnp.concatenate([_dot(tri, lw[g:g + rows], HIGHEST) for g in range(0, cs * nch, rows)],
                          axis=0)
    e_incl = jnp.exp(cum)
    e_neg = jnp.exp(-cum)
    abar_all = -kn_ref[0] * jnp.exp(cum - lw)
    rbar_all = r_ref[0] * e_incl
    bt_all = b_ref[0] * e_neg
    kt_all = k_ref[0] * e_neg
    v_all = v_ref[0]
    n_double = max(cs.bit_length() - 2, 0)
    n_pairs = RWKV_HEADS // 2
    probs = [(j, p) for j in range(nch) for p in range(n_pairs)]
    cut = lambda t, jp: t[jp[0] * cs:(jp[0] + 1) * cs, jp[1] * pw:(jp[1] + 1) * pw]
    each = lambda fn: {jp: fn(jp) for jp in probs}
    abar, rbar = each(lambda jp: cut(abar_all, jp)), each(lambda jp: cut(rbar_all, jp))
    bt, kt, v = each(lambda jp: cut(bt_all, jp)), each(lambda jp: cut(kt_all, jp)), each(lambda jp: cut(v_all, jp))
    p_last = each(lambda jp: e_incl[(jp[0] + 1) * cs - 1:(jp[0] + 1) * cs, jp[1] * pw:(jp[1] + 1) * pw])
    ar = each(lambda jp: jnp.concatenate([abar[jp], rbar[jp]], axis=0))
    ar_b = each(lambda jp: _bnt(ar[jp], block_diag(bt[jp])))
    ar_k = each(lambda jp: _bnt(ar[jp], block_diag(kt[jp])))
    a_ab = each(lambda jp: jnp.where(strict, ar_b[jp][:cs], 0.0))
    a_ak = each(lambda jp: jnp.where(strict, ar_k[jp][:cs], 0.0))
    a_rb = each(lambda jp: jnp.where(incl, ar_b[jp][cs:], 0.0))
    a_rk = each(lambda jp: jnp.where(incl, ar_k[jp][cs:], 0.0))
    v_bd = each(lambda jp: block_diag(v[jp]))
    ak_v = each(lambda jp: _bdot(a_ak[jp], v_bd[jp]))
    inv = each(lambda jp: eye + a_ab[jp])
    power = a_ab
    power_bd = each(lambda jp: block_diag(power[jp]))
    for _ in range(n_double):
        power = each(lambda jp: _bdot(power[jp], power_bd[jp]))
        power_bd = each(lambda jp: block_diag(power[jp]))
        inv = each(lambda jp: inv[jp] + _bdot(inv[jp], power_bd[jp]))
    a_hat = each(lambda jp: _bdot(inv[jp], block_diag(abar[jp])))
    u_hat = each(lambda jp: _bdot(inv[jp], block_diag(ak_v[jp])))
    r_hat = each(lambda jp: rbar[jp] + _bdot(a_rb[jp], block_diag(a_hat[jp])))
    y_hat = each(lambda jp: _bdot(a_rb[jp], block_diag(u_hat[jp])) + _bdot(a_rk[jp], v_bd[jp]))
    g_mat = each(lambda jp: eye * p_last[jp] + diag_blocks(_btn(a_hat[jp], bt[jp] * p_last[jp])))
    h_mat = each(lambda jp: diag_blocks(_btn(jnp.concatenate([u_hat[jp], v[jp]], axis=0),
                                             jnp.concatenate([bt[jp], kt[jp]], axis=0) * p_last[jp])))
    states = [s_ref[p] for p in range(n_pairs)]
    for j in range(nch):
        for p in range(n_pairs):
            y_ref[0, j * cs:(j + 1) * cs, p * pw:(p + 1) * pw] = (
                _bnt(r_hat[j, p], block_diag(states[p])) + y_hat[j, p])
        states = [_dot(states[p], block_diag(g_mat[j, p]), HIGHEST) + h_mat[j, p] for p in range(n_pairs)]
    for p in range(n_pairs):
        s_ref[p] = states[p]


def _rwkv_chunk(r, lw, k, v, kn, b_, cs=CHUNK, nch=8):
    b, s, w = r.shape
    spec = pl.BlockSpec((1, cs * nch, w), lambda bi, c: (bi, c, 0))
    return pl.pallas_call(
        functools.partial(_rwkv_chunk_kernel, cs=cs, nch=nch),
        grid=(b, s // (cs * nch)),
        in_specs=[spec] * 6,
        out_specs=spec,
        out_shape=jax.ShapeDtypeStruct((b, s, w), F32),
        scratch_shapes=[pltpu.VMEM((RWKV_HEADS // 2, HEAD_DIM, 2 * HEAD_DIM), F32)],
        compiler_params=_cparams(("parallel", "arbitrary")),
        name="rwkv_chunk",
    )(r, lw, k, v, kn, b_)


def _merge_kernel(x_ref, xb_ref, swa_ref, y_ref, g_ref, bonus_ref, fox_ref, dsa_ref, gng_ref, gnb_ref,
                  gbias_ref, wg_ref, wb_ref, wo_ref, lng_ref, lnb_ref, o_ref, ob_ref):
    y = y_ref[...]
    xb = xb_ref[...]
    head_mean = _head_block_ones(RWKV_W, 1.0 / HEAD_DIM)
    yc = y - _dot(y, head_mean, HIGHEST)
    yv = _dot(yc * yc, head_mean, HIGHEST)
    o_rwkv = (yc * lax.rsqrt(yv + RWKV_GN_EPS) * gng_ref[...] + gnb_ref[...] + bonus_ref[...]) * g_ref[...]
    branches = (swa_ref[...], o_rwkv, fox_ref[...], dsa_ref[...])
    merged = jnp.zeros(o_ref.shape, F32)
    off = 0
    for i, o in enumerate(branches):
        width = MIX_WIDTHS[i]
        proj = _dot(o.astype(BF16), wb_ref[off:off + width, :])
        gate = _sigmoid(_dot(xb, wg_ref[0, :, i * D_MODEL:(i + 1) * D_MODEL]) + gbias_ref[i:i + 1, :])
        merged = merged + gate * proj
        off += width
    y_out = _dot(merged.astype(BF16), wo_ref[...])
    xn = _layer_norm(DN_ALPHA * x_ref[...] + y_out, lng_ref[...], lnb_ref[...])
    o_ref[...] = xn
    ob_ref[...] = xn.astype(BF16)


def _merge(xf, xb, o_swa, y, g, bonus, o_fox, o_dsa, gn_g, gn_b, gate_bias, w_gate_all, layer, w_branch, w_out,
           ln_g, ln_b, tm=512):
    n, d = xf.shape
    w_gate = w_gate_all
    row = lambda x: x.reshape(1, -1)
    tok = lambda wdt: pl.BlockSpec((tm, wdt), lambda i: (i, 0))
    full = lambda shape: pl.BlockSpec(shape, lambda i: (0,) * len(shape))
    return pl.pallas_call(
        _merge_kernel,
        grid=(n // tm,),
        in_specs=[tok(d), tok(d), tok(512), tok(256), tok(256), tok(256), tok(256), tok(256),
                  full((1, 256)), full((1, 256)), full((N_BRANCHES, d)),
                  pl.BlockSpec((1, d, N_BRANCHES * d), lambda i: (layer, 0, 0)),
                  full((sum(MIX_WIDTHS), d)), full((d, d)), full((1, d)), full((1, d))],
        out_specs=[tok(d), tok(d)],
        out_shape=[jax.ShapeDtypeStruct((n, d), F32), jax.ShapeDtypeStruct((n, d), BF16)],
        compiler_params=_cparams(("parallel",)),
        name="merge",
    )(xf, xb, o_swa, y, g, bonus, o_fox, o_dsa, row(gn_g), row(gn_b), gate_bias, w_gate, w_branch, w_out,
      row(ln_g), row(ln_b))


def _silu(x):
    return x * _sigmoid(x)


def _ffn_kernel(xb_ref, x_ref, w1_ref, w3_ref, w2_ref, lng_ref, lnb_ref, o_ref, ob_ref, *, tf):
    xb = xb_ref[...]
    y = None
    for f in range(0, w1_ref.shape[1], tf):
        hidden = _silu(_dot(xb, w1_ref[:, f:f + tf])) * _dot(xb, w3_ref[:, f:f + tf])
        part = _dot(hidden.astype(BF16), w2_ref[f:f + tf, :])
        y = part if y is None else y + part
    xn = _layer_norm(DN_ALPHA * x_ref[...] + y, lng_ref[...], lnb_ref[...])
    o_ref[...] = xn
    ob_ref[...] = xn.astype(BF16)


def _ffn(xb, xf, w1, w3, w2, ln_g, ln_b, tm=512, tf=1408):
    n, d = xf.shape
    ff = w1.shape[1]
    row = lambda x: x.reshape(1, -1)
    tok = pl.BlockSpec((tm, d), lambda i: (i, 0))
    resident = lambda shape: pl.BlockSpec(shape, lambda i: (0, 0), pipeline_mode=pl.Buffered(1))
    return pl.pallas_call(
        functools.partial(_ffn_kernel, tf=tf),
        grid=(n // tm,),
        in_specs=[tok, tok, resident((d, ff)), resident((d, ff)), resident((ff, d)),
                  resident((1, d)), resident((1, d))],
        out_specs=[tok, tok],
        out_shape=[jax.ShapeDtypeStruct((n, d), F32), jax.ShapeDtypeStruct((n, d), BF16)],
        compiler_params=_cparams(("parallel",)),
        name="ffn",
    )(xb, xf, w1, w3, w2, row(ln_g), row(ln_b))


def _moe_kernel(xb_ref, x_ref, rw_ref, rb_ref, w1_ref, w3_ref, w2_ref, lng_ref, lnb_ref, o_ref,
                comb_ref, rank_ref, cnt_ref, *, sb, sb_small):
    e = pl.program_id(1)
    tm = x_ref.shape[0]
    lane = lax.broadcasted_iota(jnp.int32, (tm, 128), 1).astype(F32)

    @pl.when(e == 0)
    def _():
        o_ref[...] = jnp.zeros_like(o_ref)
        logits = _dot(x_ref[...], rw_ref[...], HIGHEST) + rb_ref[...]
        logits = jnp.where(lane < N_EXPERTS, logits, NEG_BIG)
        m1 = jnp.max(logits, axis=-1, keepdims=True)
        i1 = jnp.min(jnp.where(logits == m1, lane, 128.0), axis=-1, keepdims=True)
        rest = jnp.where(lane == i1, NEG_BIG, logits)
        m2 = jnp.max(rest, axis=-1, keepdims=True)
        i2 = jnp.min(jnp.where(rest == m2, lane, 128.0), axis=-1, keepdims=True)
        e2 = jnp.exp(m2 - m1)
        comb_ref[...] = jnp.where(lane == i1, 1.0 / (1.0 + e2), 0.0) + jnp.where(lane == i2, e2 / (1.0 + e2), 0.0)
        chosen = jnp.where(lane == i1, 1.0, 0.0) + jnp.where(lane == i2, 1.0, 0.0)
        r = lax.broadcasted_iota(jnp.int32, (tm, tm), 0)
        c = lax.broadcasted_iota(jnp.int32, (tm, tm), 1)
        earlier = jnp.where(c < r, 1.0, 0.0).astype(BF16)
        rank = jnp.where(chosen > 0.5, _dot(earlier, chosen.astype(BF16)), -1.0)
        lane8 = lax.broadcasted_iota(jnp.int32, (8, 128), 1)
        row8 = lax.broadcasted_iota(jnp.int32, (8, 128), 0)
        rank_ref[...] = _nt(jnp.where(lane8 == row8, 1.0, 0.0).astype(F32), rank, HIGHEST)
        cnt_ref[...] = jnp.sum(chosen, axis=0, keepdims=True)

    e_f = e.astype(F32)
    n_tok = jnp.sum(jnp.where(lane[0:1, :] == e_f, cnt_ref[...], 0.0)).astype(jnp.int32)
    rank_row = rank_ref[pl.ds(e, 1), :]
    weight = jnp.sum(jnp.where(lane == e_f, comb_ref[...], 0.0), axis=-1, keepdims=True)

    def block(first_rank, rows):
        slot = lax.broadcasted_iota(jnp.int32, (rows, tm), 0).astype(F32) + first_rank
        pick = jnp.where(rank_row == slot, 1.0, 0.0).astype(BF16)
        xs = _dot(pick, xb_ref[...]).astype(BF16)
        hidden = _silu(_dot(xs, w1_ref[0])) * _dot(xs, w3_ref[0])
        out = _dot(hidden.astype(BF16), w2_ref[0]).astype(BF16)
        o_ref[...] += _tn(pick, out) * weight

    @pl.when(n_tok <= sb_small)
    def _():
        block(0.0, sb_small)

    @pl.when(n_tok > sb_small)
    def _():
        def body(s, carry):
            block((s * sb).astype(F32), sb)
            return carry
        lax.fori_loop(0, lax.div(n_tok + sb - 1, sb), body, 0)

    @pl.when(e == pl.num_programs(1) - 1)
    def _():
        o_ref[...] = _layer_norm(DN_ALPHA * x_ref[...] + o_ref[...], lng_ref[...], lnb_ref[...])


def _moe(xb, xf, router_w, router_b, w1, w3, w2, ln_g, ln_b, tm=1024, sb=320, sb_small=256):
    n, d = xf.shape
    ne, _, fe = w1.shape
    tm = min(tm, n)
    rw = jnp.pad(router_w, ((0, 0), (0, 128 - ne)))
    rb = jnp.pad(router_b, (0, 128 - ne)).reshape(1, 128)
    row = lambda x: x.reshape(1, -1)
    return pl.pallas_call(
        functools.partial(_moe_kernel, sb=sb, sb_small=sb_small),
        grid=(n // tm, ne),
        in_specs=[pl.BlockSpec((tm, d), lambda i, e: (i, 0)),
                  pl.BlockSpec((tm, d), lambda i, e: (i, 0)),
                  pl.BlockSpec((d, 128), lambda i, e: (0, 0)),
                  pl.BlockSpec((1, 128), lambda i, e: (0, 0)),
                  pl.BlockSpec((1, d, fe), lambda i, e: (e, 0, 0)),
                  pl.BlockSpec((1, d, fe), lambda i, e: (e, 0, 0)),
                  pl.BlockSpec((1, fe, d), lambda i, e: (e, 0, 0)),
                  pl.BlockSpec((1, d), lambda i, e: (0, 0)),
                  pl.BlockSpec((1, d), lambda i, e: (0, 0))],
        out_specs=pl.BlockSpec((tm, d), lambda i, e: (i, 0)),
        out_shape=jax.ShapeDtypeStruct((n, d), F32),
        scratch_shapes=[pltpu.VMEM((tm, 128), F32), pltpu.VMEM((8, tm), F32), pltpu.VMEM((1, 128), F32)],
        compiler_params=pltpu.CompilerParams(dimension_semantics=("parallel", "arbitrary"),
                                             vmem_limit_bytes=MOE_VMEM_LIMIT),
        name="moe",
    )(xb, xf, rw, rb, w1, w3, w2, row(ln_g), row(ln_b))


def _mixer_layer(xf, xb, bsz, seq, layer, w_in, w_gate_all, sinks, mu, w0, w_up, a0, a_up, g_up, k_k, k_a, r_k,
                 gn_g, gn_b, f_bias, gate_bias, w_branch, w_out, ln_g, ln_b):
    p2 = _matmul(xb, _layout_w_in(w_in[layer]).astype(BF16), tm=1024, tn=P_WIDTH // 2)
    p3 = p2.reshape(bsz, seq, P_WIDTH)
    o_swa = _swa(p3, sinks)
    o_fox = _fox(p3, _fox_c(p3, f_bias))
    o_dsa = _dsa(p3)
    r, lw, k2, v, kn, b_, g, bonus = _rwkv_pre(p3, mu, w0, w_up, a0, a_up, g_up, k_k, k_a, r_k)
    y = _rwkv_chunk(r, lw, k2, v, kn, b_)
    n = bsz * seq
    flat = lambda t: t.reshape(n, t.shape[-1])
    return _merge(xf, xb, flat(o_swa), flat(y), flat(g), flat(bonus), flat(o_fox), flat(o_dsa), gn_g, gn_b,
                  gate_bias, w_gate_all, layer, w_branch.astype(BF16), w_out.astype(BF16), ln_g, ln_b)


def kernel(x, w_in, swa_sinks, rwkv_mu, rwkv_w0, rwkv_w_up, rwkv_a0, rwkv_a_up, rwkv_g_up, rwkv_k_k, rwkv_k_a,
           rwkv_r_k, rwkv_gn_g, rwkv_gn_b, fox_f_bias, gate_bias, w_branch, w_out, ln_g, ln_b, ffn_w1, ffn_w3,
           ffn_w2, router_w, router_b, exp_w1, exp_w3, exp_w2):
    bsz, seq, d = x.shape
    xf = x.reshape(bsz * seq, d)
    xb = xf.astype(BF16)
    w_gate_all = w_in[:, :, IN_OFF_GATE:].astype(BF16)
    for layer in range(DEPTH):
        xf, xb = _mixer_layer(xf, xb, bsz, seq, layer, w_in, w_gate_all, swa_sinks[layer], rwkv_mu[layer], rwkv_w0[layer],
                              rwkv_w_up[layer], rwkv_a0[layer], rwkv_a_up[layer], rwkv_g_up[layer],
                              rwkv_k_k[layer], rwkv_k_a[layer], rwkv_r_k[layer], rwkv_gn_g[layer],
                              rwkv_gn_b[layer], fox_f_bias[layer], gate_bias[layer], w_branch[layer],
                              w_out[layer], ln_g[layer, 0], ln_b[layer, 0])
        j = layer // 2
        if layer % 2 == 0:
            xf, xb = _ffn(xb, xf, ffn_w1[j].astype(BF16), ffn_w3[j].astype(BF16), ffn_w2[j].astype(BF16),
                          ln_g[layer, 1], ln_b[layer, 1])
        else:
            xf = _moe(xb, xf, router_w[j], router_b[j], exp_w1[j].astype(BF16), exp_w3[j].astype(BF16),
                      exp_w2[j].astype(BF16), ln_g[layer, 1], ln_b[layer, 1])
            xb = xf.astype(BF16) if layer + 1 < DEPTH else None
    return xf.reshape(bsz, seq, d)
```

```python
import functools

import jax
import jax.numpy as jnp
from jax import lax
from jax.experimental import pallas as pl
from jax.experimental.pallas import tpu as pltpu

F32 = jnp.float32
BF16 = jnp.bfloat16
HIGHEST = lax.Precision.HIGHEST

D_MODEL = 1024
DEPTH = 2
CHUNK = 64
HEAD_DIM = 64
CHUNK_SHIFT = 6
HEAD_SHIFT = 6
SWA_HEADS = 8
SWA_KV_HEADS = 2
SWA_GROUP = SWA_HEADS // SWA_KV_HEADS
SWA_WINDOW = 128
SWA_WIN_CHUNKS = SWA_WINDOW // CHUNK
RWKV_HEADS = 4
RWKV_W = RWKV_HEADS * HEAD_DIM
RWKV_GN_EPS = 64e-5
FOX_HEADS = 4
DSA_HEADS = 4
IDX_HEADS = 4
IDX_DIM = 64
DSA_TOPK_MAX = 256
D_FF = 2816
N_EXPERTS = 8
D_FF_EXPERT = 1408
N_BRANCHES = 4
DN_ALPHA = (2 * DEPTH) ** 0.25
LN_EPS = 1e-5
ATTN_SCALE = HEAD_DIM ** -0.5
IDX_SCALE = IDX_DIM ** -0.5
IDX_W_SCALE = IDX_HEADS ** -0.5
MIX_WIDTHS = (SWA_HEADS * HEAD_DIM, RWKV_W, FOX_HEADS * HEAD_DIM, DSA_HEADS * HEAD_DIM)

COL_RWKV = 0
COL_SWA_Q = 1024
COL_FOX_KV = 1536
COL_DSA_QQ = 2048
COL_FOX_Q = 2560
COL_SWA_KV = 2816
COL_DSA_KV = 3072
COL_MISC = 3328
P_WIDTH = 3584

NEG_BIG = -1e30
INT_MIN = -(2 ** 31)
VMEM_LIMIT = 48 * 1024 * 1024
MOE_VMEM_LIMIT = 58 * 1024 * 1024


IN_OFF_SWA, IN_OFF_RWKV, IN_OFF_FOX, IN_OFF_DSA, IN_OFF_GATE = 0, 768, 1792, 2564, 3272
D_IN = IN_OFF_GATE + N_BRANCHES * D_MODEL


def _layout_w_in(w):
    d = w.shape[0]
    z = lambda n: jnp.zeros((d, n), w.dtype)
    swa, rwkv, fox, dsa = IN_OFF_SWA, IN_OFF_RWKV, IN_OFF_FOX, IN_OFF_DSA
    parts = [
        w[:, rwkv:rwkv + 1024],
        w[:, swa:swa + 512],
        w[:, fox + 256:fox + 768],
        w[:, dsa:dsa + 256], w[:, dsa + 384:dsa + 640],
        w[:, fox:fox + 256],
        w[:, swa + 512:swa + 768],
        w[:, dsa + 256:dsa + 384], w[:, dsa + 640:dsa + 704], z(64),
        w[:, fox + 768:fox + 772], w[:, dsa + 704:dsa + 708], z(120),
        z(128),
    ]
    return jnp.concatenate(parts, axis=1)


def _cparams(sem):
    return pltpu.CompilerParams(dimension_semantics=sem, vmem_limit_bytes=VMEM_LIMIT)


def _nt(a, b, precision=None):
    return lax.dot_general(a, b, (((1,), (1,)), ((), ())), precision=precision,
                           preferred_element_type=F32)


def _tn(a, b, precision=None):
    return lax.dot_general(a, b, (((0,), (0,)), ((), ())), precision=precision,
                           preferred_element_type=F32)


def _dot(a, b, precision=None):
    return jnp.dot(a, b, precision=precision, preferred_element_type=F32)


def _sigmoid(x):
    return 1.0 / (1.0 + jnp.exp(-x))


def _layer_norm(z, g, b):
    mu = jnp.mean(z, axis=-1, keepdims=True)
    zc = z - mu
    var = jnp.mean(zc * zc, axis=-1, keepdims=True)
    return zc * lax.rsqrt(var + LN_EPS) * g + b


def _skewed(n, first, second):
    out = []
    staged = first(0)
    for h in range(n):
        nxt = first(h + 1) if h + 1 < n else None
        out.append(second(h, staged))
        staged = nxt
    return out


def _paired_blocks(n, body, carry):
    carry = lax.fori_loop(0, lax.div(n, 2), lambda t, c: body((2 * t, 2 * t + 1), c), carry)
    return lax.cond(lax.rem(n, 2) == 1, lambda c: body((n - 1,), c), lambda c: c, carry)


def _head_block_ones(n, scale):
    r = lax.broadcasted_iota(jnp.int32, (n, n), 0) >> HEAD_SHIFT
    c = lax.broadcasted_iota(jnp.int32, (n, n), 1) >> HEAD_SHIFT
    return jnp.where(r == c, scale, 0.0).astype(BF16)


def _head_reduce(x, head_ones):
    hi = x.astype(BF16)
    rest = x - hi.astype(F32)
    mid = rest.astype(BF16)
    low = (rest - mid.astype(F32)).astype(BF16)
    return _dot(hi, head_ones) + _dot(mid, head_ones) + _dot(low, head_ones)


def _mm_kernel(a_ref, b_ref, o_ref):
    o_ref[...] = _dot(a_ref[...].astype(BF16), b_ref[...])


def _matmul(a, b, tm, tn):
    m, k = a.shape
    n = b.shape[1]
    tm = min(tm, m)
    return pl.pallas_call(
        _mm_kernel,
        grid=(m // tm, n // tn),
        in_specs=[pl.BlockSpec((tm, k), lambda i, j: (i, 0)),
                  pl.BlockSpec((k, tn), lambda i, j: (0, j))],
        out_specs=pl.BlockSpec((tm, tn), lambda i, j: (i, j)),
        out_shape=jax.ShapeDtypeStruct((m, n), F32),
        compiler_params=_cparams(("parallel", "parallel")),
        name="in_proj",
    )(a, b)


def _swa_kernel(sink_ref, q_ref, kv_ref, o_ref, *, tq):
    i = pl.program_id(1)
    win = tq + SWA_WINDOW
    s0 = pl.multiple_of(jnp.maximum(i * tq - SWA_WINDOW, 0), SWA_WINDOW)
    k_chunk = (s0 + lax.broadcasted_iota(jnp.int32, (win, tq), 0)) >> CHUNK_SHIFT
    q_chunk = (i * tq + lax.broadcasted_iota(jnp.int32, (win, tq), 1)) >> CHUNK_SHIFT
    bias = jnp.where(k_chunk <= q_chunk,
                     jnp.where(k_chunk >= q_chunk - SWA_WIN_CHUNKS, 0.0, NEG_BIG), NEG_BIG)
    ks = [kv_ref[0, pl.ds(s0, win), hk * HEAD_DIM:(hk + 1) * HEAD_DIM].astype(BF16)
          for hk in range(SWA_KV_HEADS)]
    kv_w = SWA_KV_HEADS * HEAD_DIM
    vs = [kv_ref[0, pl.ds(s0, win), kv_w + hk * HEAD_DIM:kv_w + (hk + 1) * HEAD_DIM].astype(BF16)
          for hk in range(SWA_KV_HEADS)]

    def logits(h):
        q = (q_ref[0, :, h * HEAD_DIM:(h + 1) * HEAD_DIM] * ATTN_SCALE).astype(BF16)
        return _nt(ks[h // SWA_GROUP], q) + bias

    def finish(h, s):
        sink = sink_ref[h]
        m = jnp.maximum(jnp.max(s, axis=0, keepdims=True), sink)
        e = jnp.exp(s - m)
        denom = jnp.sum(e, axis=0, keepdims=True) + jnp.exp(sink - m)
        return _tn(vs[h // SWA_GROUP], e.astype(BF16)) / denom

    o_ref[0] = jnp.concatenate(_skewed(SWA_HEADS, logits, finish), axis=0).T


def _swa(p3, sinks, tq=256):
    b, s, _ = p3.shape
    kv_w = 2 * SWA_KV_HEADS * HEAD_DIM
    return pl.pallas_call(
        functools.partial(_swa_kernel, tq=tq),
        grid=(b, s // tq),
        in_specs=[pl.BlockSpec(memory_space=pltpu.SMEM),
                  pl.BlockSpec((1, tq, 512), lambda bi, i: (bi, i, COL_SWA_Q // 512)),
                  pl.BlockSpec((1, s, kv_w), lambda bi, i: (bi, 0, COL_SWA_KV // kv_w))],
        out_specs=pl.BlockSpec((1, tq, 512), lambda bi, i: (bi, i, 0)),
        out_shape=jax.ShapeDtypeStruct((b, s, 512), F32),
        compiler_params=_cparams(("parallel", "parallel")),
        name="swa",
    )(sinks, p3, p3)


def _fox_c_kernel(f_ref, bias_ref, c_ref, *, blk):
    s = f_ref.shape[1]
    r = lax.broadcasted_iota(jnp.int32, (blk, blk), 0)
    c = lax.broadcasted_iota(jnp.int32, (blk, blk), 1)
    tri = jnp.where(c <= r, 1.0, 0.0).astype(F32)
    carry = jnp.zeros((1, 128), F32)
    for j in range(s // blk):
        x = f_ref[0, j * blk:(j + 1) * blk, :] + bias_ref[...]
        log_f = jnp.minimum(x, 0.0) - jnp.log1p(jnp.exp(-jnp.abs(x)))
        cs = _dot(tri, log_f, HIGHEST) + carry
        carry = cs[blk - 1:blk, :]
        for h in range(FOX_HEADS):
            c_ref[0, h, j * blk:(j + 1) * blk, :] = jnp.broadcast_to(cs[:, h:h + 1], (blk, 128))


def _fox_c(p3, f_bias):
    b, s, _ = p3.shape
    bias_row = jnp.pad(f_bias, (0, 128 - FOX_HEADS)).reshape(1, 128)
    return pl.pallas_call(
        functools.partial(_fox_c_kernel, blk=256),
        grid=(b,),
        in_specs=[pl.BlockSpec((1, s, 128), lambda bi: (bi, 0, COL_MISC // 128)),
                  pl.BlockSpec((1, 128), lambda bi: (0, 0))],
        out_specs=pl.BlockSpec((1, FOX_HEADS, s, 128), lambda bi: (bi, 0, 0, 0)),
        out_shape=jax.ShapeDtypeStruct((b, FOX_HEADS, s, 128), F32),
        compiler_params=_cparams(("parallel",)),
        name="fox_cumsum",
    )(p3, bias_row)


def _fox_kernel(q_ref, k_ref, c_ref, o_ref, lg_ref, acc_ref, *, tq):
    i = pl.program_id(1)
    tk = tq
    v_col = FOX_HEADS * HEAD_DIM
    key_i = lax.broadcasted_iota(jnp.int32, (tk, tq), 0)
    qry_i = lax.broadcasted_iota(jnp.int32, (tk, tq), 1)
    diag_bias = jnp.where(key_i <= qry_i, 0.0, NEG_BIG)
    q = [(q_ref[0, :, h * HEAD_DIM:(h + 1) * HEAD_DIM] * ATTN_SCALE).astype(BF16) for h in range(FOX_HEADS)]

    def logits_blocks(blocks, maxes, masked=False):
        starts = [pl.multiple_of(j * tk, tk) for j in blocks]
        dots = [[_nt(k_ref[0, pl.ds(ks, tk), h * HEAD_DIM:(h + 1) * HEAD_DIM].astype(BF16), q[h])
                 for h in range(FOX_HEADS)] for ks in starts]
        maxes = list(maxes)
        for b, ks in enumerate(starts):
            for h in range(FOX_HEADS):
                c_k = c_ref[0, h, pl.ds(ks, tk), :]
                s = dots[b][h] - jnp.concatenate([c_k] * (tq // 128), axis=1)
                if masked:
                    s = s + diag_bias
                lg_ref[h, pl.ds(ks, tk), :] = s
                maxes[h] = jnp.maximum(maxes[h], jnp.max(s, axis=0, keepdims=True))
        return tuple(maxes)

    maxes = tuple(jnp.full((1, tq), NEG_BIG, F32) for _ in range(FOX_HEADS))
    maxes = _paired_blocks(i, logits_blocks, maxes)
    maxes = logits_blocks((i,), maxes, masked=True)
    acc_ref[...] = jnp.zeros_like(acc_ref)

    def attend_blocks(blocks, sums):
        starts = [pl.multiple_of(j * tk, tk) for j in blocks]
        sums = list(sums)
        for h in range(FOX_HEADS):
            update = None
            for ks in starts:
                p = jnp.exp(lg_ref[h, pl.ds(ks, tk), :] - maxes[h])
                v = k_ref[0, pl.ds(ks, tk), v_col + h * HEAD_DIM:v_col + (h + 1) * HEAD_DIM].astype(BF16)
                pv = _tn(v, p.astype(BF16))
                update = pv if update is None else update + pv
                sums[h] = sums[h] + jnp.sum(p, axis=0, keepdims=True)
            acc_ref[h] += update
        return tuple(sums)

    sums = _paired_blocks(i + 1, attend_blocks, tuple(jnp.zeros((1, tq), F32) for _ in range(FOX_HEADS)))
    out_t = jnp.concatenate([acc_ref[h] / sums[h] for h in range(FOX_HEADS)], axis=0)
    o_ref[0] = out_t.T


def _fox(p3, c, tq=256):
    b, s, _ = p3.shape
    w = FOX_HEADS * HEAD_DIM
    return pl.pallas_call(
        functools.partial(_fox_kernel, tq=tq),
        grid=(b, s // tq),
        in_specs=[pl.BlockSpec((1, tq, w), lambda bi, i: (bi, i, COL_FOX_Q // w)),
                  pl.BlockSpec((1, s, 2 * w), lambda bi, i: (bi, 0, COL_FOX_KV // (2 * w))),
                  pl.BlockSpec((1, FOX_HEADS, s, 128), lambda bi, i: (bi, 0, 0, 0))],
        out_specs=pl.BlockSpec((1, tq, w), lambda bi, i: (bi, i, 0)),
        out_shape=jax.ShapeDtypeStruct((b, s, w), F32),
        scratch_shapes=[pltpu.VMEM((FOX_HEADS, s, tq), F32), pltpu.VMEM((FOX_HEADS, HEAD_DIM, tq), F32)],
        compiler_params=_cparams(("parallel", "parallel")),
        name="fox",
    )(p3, p3, c)


def _key_to_score(key):
    return pltpu.bitcast(key ^ ((key >> 31) & 0x7FFFFFFF), F32)


def _bit_planes(score):
    bits = pltpu.bitcast(score, jnp.int32)
    image = bits ^ ((bits >> 31) | INT_MIN)
    rows = image.reshape(32, 8, image.shape[1])
    v = [rows[j] for j in range(32)]
    for d, mask in ((16, 0x0000FFFF), (8, 0x00FF00FF), (4, 0x0F0F0F0F), (2, 0x33333333), (1, 0x55555555)):
        for k in range(32):
            if k & d == 0:
                t = (v[k] ^ lax.shift_right_logical(v[k + d], d)) & mask
                v[k] = v[k] ^ t
                v[k + d] = v[k + d] ^ lax.shift_left(t, d)
    return v


def _dsa_kernel(qq_ref, misc_ref, kv_ref, o_ref, sc_ref, lg_ref, acc_ref, pl_ref, *, tq, kb, topk):
    i = pl.program_id(1)
    n_kb = lax.div((i + 1) * tq + kb - 1, kb)
    lane8 = lax.broadcasted_iota(jnp.int32, (8, 128), 1)
    row8 = lax.broadcasted_iota(jnp.int32, (8, 128), 0)
    pick = jnp.where((lane8 == row8 + 4) & (row8 < IDX_HEADS), 1.0, 0.0).astype(F32)
    w_t = _nt(pick, misc_ref[0], HIGHEST) * (IDX_W_SCALE * IDX_SCALE)
    q_chunk = (i * tq + lax.broadcasted_iota(jnp.int32, (1, tq), 1)) >> CHUNK_SHIFT
    qw = DSA_HEADS * HEAD_DIM
    q_idx = [qq_ref[0, :, qw + h * IDX_DIM:qw + (h + 1) * IDX_DIM].astype(BF16) for h in range(IDX_HEADS)]

    def score_blocks(blocks, carry):
        starts = [pl.multiple_of(j * kb, kb) for j in blocks]
        dots = []
        for ks in starts:
            k_idx = kv_ref[0, pl.ds(ks, kb), 2 * HEAD_DIM:2 * HEAD_DIM + IDX_DIM].astype(BF16)
            dots.append([_nt(k_idx, q_idx[h]) for h in range(IDX_HEADS)])
        for b, ks in enumerate(starts):
            score = jnp.maximum(dots[b][0], 0.0) * w_t[0:1, :]
            for h in range(1, IDX_HEADS):
                score = score + jnp.maximum(dots[b][h], 0.0) * w_t[h:h + 1, :]
            k_chunk = (ks + lax.broadcasted_iota(jnp.int32, (kb, tq), 0)) >> CHUNK_SHIFT
            score = jnp.where(k_chunk <= q_chunk, score, -jnp.inf)
            sc_ref[pl.ds(ks, kb), :] = score
            planes = _bit_planes(score)
            for t in range(32):
                pl_ref[blocks[b], t] = planes[t]
        return carry

    _paired_blocks(n_kb, score_blocks, 0)
    n_blocks = pl_ref.shape[0]

    def clear_planes(j, carry):
        pl_ref[j] = jnp.zeros(pl_ref.shape[1:], jnp.int32)
        return carry

    lax.fori_loop(n_kb, n_blocks, clear_planes, 0)

    def count(pred):
        def body(j, acc):
            blk = sc_ref[pl.ds(pl.multiple_of(j * kb, kb), kb), :]
            hit = jnp.where(pred(blk), 1, 0).astype(jnp.int32).reshape(kb // 8, 8, tq)
            parts = [hit[g] for g in range(kb // 8)]
            while len(parts) > 1:
                parts = [parts[g] + parts[g + 1] for g in range(0, len(parts), 2)]
            return acc + parts[0]
        acc = lax.fori_loop(0, n_kb, body, jnp.zeros((8, tq), jnp.int32))
        return jnp.sum(acc.astype(F32), axis=0, keepdims=True)

    def search_by_compares():
        zero = jnp.zeros((1, tq), F32)
        thr_key = jnp.where(count(lambda blk: blk >= zero) >= topk, 0, INT_MIN).astype(jnp.int32)

        def bit_step(bi, thr_key):
            cand = thr_key | lax.shift_left(jnp.int32(1), 30 - bi)
            cand_score = _key_to_score(cand)
            return jnp.where(count(lambda blk: blk >= cand_score) >= topk, cand, thr_key)

        thr_key = lax.fori_loop(0, 31, bit_step, thr_key)
        t = jnp.where(thr_key == INT_MIN, -jnp.inf, _key_to_score(thr_key))
        return t, count(lambda blk: blk > t)

    def search_by_planes():
        def bit_pass(t, carry):
            prefix, above, alive = carry
            ones = [alive[j] & pl_ref[j, t] for j in range(n_blocks)]
            parts = [lax.population_count(o) for o in ones]
            while len(parts) > 1:
                parts = [parts[g] + parts[g + 1] for g in range(0, len(parts), 2)]
            reach = above + jnp.sum(parts[0].astype(F32), axis=0, keepdims=True)
            take = reach >= topk
            prefix = jnp.where(take, prefix | lax.shift_left(jnp.int32(1), 31 - t), prefix)
            alive = tuple(jnp.where(take, ones[j], alive[j] ^ ones[j]) for j in range(n_blocks))
            return prefix, jnp.where(take, above, reach), alive

        init = (jnp.zeros((1, tq), jnp.int32), jnp.zeros((1, tq), F32),
                tuple(jnp.full((8, tq), -1, jnp.int32) for _ in range(n_blocks)))
        image, _, _ = lax.fori_loop(0, 32, bit_pass, init)
        return pltpu.bitcast(image ^ (~(image >> 31) | INT_MIN), F32)

    thr_planes = search_by_planes()
    n_above = count(lambda blk: blk > thr_planes)
    n_reach = count(lambda blk: blk >= thr_planes)
    confirmed = jnp.min(jnp.where(n_above < topk, jnp.where(n_reach >= topk, 1.0, 0.0), 0.0)) > 0.5
    thr, n_above = lax.cond(confirmed, lambda: (thr_planes, n_above), search_by_compares)
    need = topk - n_above

    r = lax.broadcasted_iota(jnp.int32, (kb, kb), 0)
    c = lax.broadcasted_iota(jnp.int32, (kb, kb), 1)
    earlier = jnp.where(c < r, 1.0, 0.0).astype(BF16)
    q_att = [(qq_ref[0, :, h * HEAD_DIM:(h + 1) * HEAD_DIM] * ATTN_SCALE).astype(BF16)
             for h in range(DSA_HEADS)]

    def logits_blocks(blocks, carry):
        ties_before, maxes = carry
        maxes = list(maxes)
        starts = [pl.multiple_of(j * kb, kb) for j in blocks]
        dots, ties = [], []
        for ks in starts:
            k = kv_ref[0, pl.ds(ks, kb), 0:HEAD_DIM].astype(BF16)
            dots.append([_nt(k, q_att[h]) for h in range(DSA_HEADS)])
            k_chunk = (ks + lax.broadcasted_iota(jnp.int32, (kb, tq), 0)) >> CHUNK_SHIFT
            ties.append(jnp.where(sc_ref[pl.ds(ks, kb), :] == thr, jnp.where(k_chunk <= q_chunk, 1.0, 0.0), 0.0))
        ranks = [_dot(earlier, tie.astype(BF16)) for tie in ties]
        for b, ks in enumerate(starts):
            tie_taken = jnp.where(ranks[b] + ties_before < need, ties[b], 0.0)
            bias = jnp.where(sc_ref[pl.ds(ks, kb), :] > thr, 0.0, jnp.where(tie_taken > 0.5, 0.0, NEG_BIG))
            ties_before = ties_before + jnp.sum(ties[b], axis=0, keepdims=True)
            for h in range(DSA_HEADS):
                s = dots[b][h] + bias
                lg_ref[h, pl.ds(ks, kb), :] = s
                maxes[h] = jnp.maximum(maxes[h], jnp.max(s, axis=0, keepdims=True))
        return ties_before, tuple(maxes)

    init = (jnp.zeros((1, tq), F32), tuple(jnp.full((1, tq), NEG_BIG, F32) for _ in range(DSA_HEADS)))
    _, maxes = _paired_blocks(n_kb, logits_blocks, init)

    acc_ref[...] = jnp.zeros_like(acc_ref)

    def attend_blocks(blocks, sums):
        starts = [pl.multiple_of(j * kb, kb) for j in blocks]
        vs = [kv_ref[0, pl.ds(ks, kb), HEAD_DIM:2 * HEAD_DIM].astype(BF16) for ks in starts]
        sums = list(sums)
        for h in range(DSA_HEADS):
            update = None
            for b, ks in enumerate(starts):
                p = jnp.exp(lg_ref[h, pl.ds(ks, kb), :] - maxes[h])
                pv = _tn(vs[b], p.astype(BF16))
                update = pv if update is None else update + pv
                sums[h] = sums[h] + jnp.sum(p, axis=0, keepdims=True)
            acc_ref[h] += update
        return tuple(sums)

    sums = _paired_blocks(n_kb, attend_blocks, tuple(jnp.zeros((1, tq), F32) for _ in range(DSA_HEADS)))
    out_t = jnp.concatenate([acc_ref[h] / sums[h] for h in range(DSA_HEADS)], axis=0)
    o_ref[0] = out_t.T


def _dsa(p3, tq=256, kb=256):
    b, s, _ = p3.shape
    assert kb == 8 * 32, "a key block is the 32 row groups of one bit-plane word"
    topk = min(DSA_TOPK_MAX, s // 4)
    return pl.pallas_call(
        functools.partial(_dsa_kernel, tq=tq, kb=kb, topk=topk),
        grid=(b, s // tq),
        in_specs=[pl.BlockSpec((1, tq, 512), lambda bi, i: (bi, i, COL_DSA_QQ // 512)),
                  pl.BlockSpec((1, tq, 128), lambda bi, i: (bi, i, COL_MISC // 128)),
                  pl.BlockSpec((1, s, 256), lambda bi, i: (bi, 0, COL_DSA_KV // 256))],
        out_specs=pl.BlockSpec((1, tq, 256), lambda bi, i: (bi, i, 0)),
        out_shape=jax.ShapeDtypeStruct((b, s, 256), F32),
        scratch_shapes=[pltpu.VMEM((s, tq), F32), pltpu.VMEM((DSA_HEADS, s, tq), F32),
                        pltpu.VMEM((DSA_HEADS, HEAD_DIM, tq), F32), pltpu.VMEM((s // kb, 32, 8, tq), jnp.int32)],
        compiler_params=_cparams(("parallel", "parallel")),
        name="dsa",
    )(p3, p3, p3)


def _rwkv_pre_kernel(p_ref, mu_ref, w0_ref, wup_ref, a0_ref, aup_ref, gup_ref, kk_ref, ka_ref, rk_ref,
                     r_o, lw_o, k_o, v_o, kn_o, b_o, g_o, bonus_o, last_ref, *, tt):
    t = pl.program_id(1)

    @pl.when(t == 0)
    def _():
        last_ref[...] = jnp.zeros_like(last_ref)

    p = p_ref[0]
    row = lax.broadcasted_iota(jnp.int32, p.shape, 0)
    p_prev = jnp.where(row == 0, last_ref[...], pltpu.roll(p, 1, axis=0))
    last_ref[...] = p[tt - 1:tt, :]
    ps = p + mu_ref[...] * (p_prev - p)
    w = RWKV_W
    r, k, v = ps[:, 0:w], ps[:, w:2 * w], ps[:, 2 * w:3 * w]
    w_lo, a_lo, g_lo = ps[:, 3 * w:3 * w + 64], ps[:, 3 * w + 64:3 * w + 128], ps[:, 3 * w + 128:3 * w + 256]
    ww = w0_ref[...] + _dot(jnp.tanh(w_lo), wup_ref[...], HIGHEST)
    softplus_neg = jnp.maximum(-ww, 0.0) + jnp.log1p(jnp.exp(-jnp.abs(ww)))
    log_w = -jnp.exp(-softplus_neg - 0.5)
    a = _sigmoid(a0_ref[...] + _dot(a_lo, aup_ref[...], HIGHEST))
    g = _dot(_sigmoid(g_lo), gup_ref[...], HIGHEST)
    head_sum = _head_block_ones(w, 1.0)
    kn = k * kk_ref[...]
    kn = kn * lax.rsqrt(_head_reduce(kn * kn, head_sum) + 1e-12)
    k2 = k * (1.0 + (a - 1.0) * ka_ref[...])
    bonus = _head_reduce(r * k2 * rk_ref[...], head_sum) * v
    r_o[0] = r
    lw_o[0] = log_w
    k_o[0] = k2
    v_o[0] = v
    kn_o[0] = kn
    b_o[0] = kn * a
    g_o[0] = g
    bonus_o[0] = bonus


def _rwkv_pre(p3, mu, w0, w_up, a0, a_up, g_up, k_k, k_a, r_k, tt=512):
    b, s, _ = p3.shape
    w = RWKV_W
    row = lambda x: x.reshape(1, -1)
    full = lambda shape: pl.BlockSpec(shape, lambda bi, t: (0,) * len(shape))
    out = jax.ShapeDtypeStruct((b, s, w), F32)
    return pl.pallas_call(
        functools.partial(_rwkv_pre_kernel, tt=tt),
        grid=(b, s // tt),
        in_specs=[pl.BlockSpec((1, tt, 1024), lambda bi, t: (bi, t, COL_RWKV // 1024)),
                  full((1, 1024)), full((1, w)), full((64, w)), full((1, w)), full((64, w)),
                  full((128, w)), full((1, w)), full((1, w)), full((1, w))],
        out_specs=[pl.BlockSpec((1, tt, w), lambda bi, t: (bi, t, 0))] * 8,
        out_shape=[out] * 8,
        scratch_shapes=[pltpu.VMEM((1, 1024), F32)],
        compiler_params=_cparams(("parallel", "arbitrary")),
        name="rwkv_pre",
    )(p3, row(mu), row(w0), w_up, row(a0), a_up, g_up, row(k_k), row(k_a), row(r_k))


def _bdot(a, b):
    return _dot(a.astype(BF16), b.astype(BF16))


def _bnt(a, b):
    return _nt(a.astype(BF16), b.astype(BF16))


def _btn(a, b):
    return _tn(a.astype(BF16), b.astype(BF16))


def _rwkv_chunk_kernel(r_ref, lw_ref, k_ref, v_ref, kn_ref, b_ref, y_ref, s_ref, *, cs, nch):
    c = pl.program_id(1)

    @pl.when(c == 0)
    def _():
        s_ref[...] = jnp.zeros_like(s_ref)

    rows = cs * min(nch, 4)
    big_row = lax.broadcasted_iota(jnp.int32, (rows, rows), 0)
    big_col = lax.broadcasted_iota(jnp.int32, (rows, rows), 1)
    same_chunk = (big_row >> CHUNK_SHIFT) == (big_col >> CHUNK_SHIFT)
    tri = jnp.where(same_chunk, jnp.where(big_col <= big_row, 1.0, 0.0), 0.0).astype(F32)
    pw = 2 * HEAD_DIM
    row = lax.broadcasted_iota(jnp.int32, (cs, pw), 0)
    lane = lax.broadcasted_iota(jnp.int32, (cs, pw), 1)
    col = lane & (HEAD_DIM - 1)
    first_head = lane < HEAD_DIM
    incl = col <= row
    strict = col < row
    eye = jnp.where(col == row, 1.0, 0.0).astype(F32)

    def block_diag(x):
        return jnp.concatenate([jnp.where(first_head, x, 0.0), jnp.where(first_head, 0.0, x)], axis=0)

    def diag_blocks(full):
        return jnp.where(first_head, full[:cs], full[cs:])

    lw = lw_ref[0]
    cum = jnp.concatenate([_dot(tri, lw[g:g + rows], HIGHEST) for g in range(0, cs * nch, rows)],
                          axis=0)
    e_incl = jnp.exp(cum)
    e_neg = jnp.exp(-cum)
    abar_all = -kn_ref[0] * jnp.exp(cum - lw)
    rbar_all = r_ref[0] * e_incl
    bt_all = b_ref[0] * e_neg
    kt_all = k_ref[0] * e_neg
    v_all = v_ref[0]
    n_double = max(cs.bit_length() - 2, 0)
    n_pairs = RWKV_HEADS // 2
    probs = [(j, p) for j in range(nch) for p in range(n_pairs)]
    cut = lambda t, jp: t[jp[0] * cs:(jp[0] + 1) * cs, jp[1] * pw:(jp[1] + 1) * pw]
    each = lambda fn: {jp: fn(jp) for jp in probs}
    abar, rbar = each(lambda jp: cut(abar_all, jp)), each(lambda jp: cut(rbar_all, jp))
    bt, kt, v = each(lambda jp: cut(bt_all, jp)), each(lambda jp: cut(kt_all, jp)), each(lambda jp: cut(v_all, jp))
    p_last = each(lambda jp: e_incl[(jp[0] + 1) * cs - 1:(jp[0] + 1) * cs, jp[1] * pw:(jp[1] + 1) * pw])
    ar = each(lambda jp: jnp.concatenate([abar[jp], rbar[jp]], axis=0))
    ar_b = each(lambda jp: _bnt(ar[jp], block_diag(bt[jp])))
    ar_k = each(lambda jp: _bnt(ar[jp], block_diag(kt[jp])))
    a_ab = each(lambda jp: jnp.where(strict, ar_b[jp][:cs], 0.0))
    a_ak = each(lambda jp: jnp.where(strict, ar_k[jp][:cs], 0.0))
    a_rb = each(lambda jp: jnp.where(incl, ar_b[jp][cs:], 0.0))
    a_rk = each(lambda jp: jnp.where(incl, ar_k[jp][cs:], 0.0))
    v_bd = each(lambda jp: block_diag(v[jp]))
    ak_v = each(lambda jp: _bdot(a_ak[jp], v_bd[jp]))
    inv = each(lambda jp: eye + a_ab[jp])
    power = a_ab
    power_bd = each(lambda jp: block_diag(power[jp]))
    for _ in range(n_double):
        power = each(lambda jp: _bdot(power[jp], power_bd[jp]))
        power_bd = each(lambda jp: block_diag(power[jp]))
        inv = each(lambda jp: inv[jp] + _bdot(inv[jp], power_bd[jp]))
    a_hat = each(lambda jp: _bdot(inv[jp], block_diag(abar[jp])))
    u_hat = each(lambda jp: _bdot(inv[jp], block_diag(ak_v[jp])))
    r_hat = each(lambda jp: rbar[jp] + _bdot(a_rb[jp], block_diag(a_hat[jp])))
    y_hat = each(lambda jp: _bdot(a_rb[jp], block_diag(u_hat[jp])) + _bdot(a_rk[jp], v_bd[jp]))
    g_mat = each(lambda jp: eye * p_last[jp] + diag_blocks(_btn(a_hat[jp], bt[jp] * p_last[jp])))
    h_mat = each(lambda jp: diag_blocks(_btn(jnp.concatenate([u_hat[jp], v[jp]], axis=0),
                                             jnp.concatenate([bt[jp], kt[jp]], axis=0) * p_last[jp])))
    states = [s_ref[p] for p in range(n_pairs)]
    for j in range(nch):
        for p in range(n_pairs):
            y_ref[0, j * cs:(j + 1) * cs, p * pw:(p + 1) * pw] = (
                _bnt(r_hat[j, p], block_diag(states[p])) + y_hat[j, p])
        states = [_dot(states[p], block_diag(g_mat[j, p]), HIGHEST) + h_mat[j, p] for p in range(n_pairs)]
    for p in range(n_pairs):
        s_ref[p] = states[p]


def _rwkv_chunk(r, lw, k, v, kn, b_, cs=CHUNK, nch=8):
    b, s, w = r.shape
    spec = pl.BlockSpec((1, cs * nch, w), lambda bi, c: (bi, c, 0))
    return pl.pallas_call(
        functools.partial(_rwkv_chunk_kernel, cs=cs, nch=nch),
        grid=(b, s // (cs * nch)),
        in_specs=[spec] * 6,
        out_specs=spec,
        out_shape=jax.ShapeDtypeStruct((b, s, w), F32),
        scratch_shapes=[pltpu.VMEM((RWKV_HEADS // 2, HEAD_DIM, 2 * HEAD_DIM), F32)],
        compiler_params=_cparams(("parallel", "arbitrary")),
        name="rwkv_chunk",
    )(r, lw, k, v, kn, b_)


def _merge_kernel(x_ref, swa_ref, y_ref, g_ref, bonus_ref, fox_ref, dsa_ref, gng_ref, gnb_ref,
                  gbias_ref, wg_ref, wb_ref, wo_ref, lng_ref, lnb_ref, o_ref):
    y = y_ref[...]
    xb = x_ref[...].astype(BF16)
    head_mean = _head_block_ones(RWKV_W, 1.0 / HEAD_DIM)
    yc = y - _head_reduce(y, head_mean)
    yv = _head_reduce(yc * yc, head_mean)
    o_rwkv = (yc * lax.rsqrt(yv + RWKV_GN_EPS) * gng_ref[...] + gnb_ref[...] + bonus_ref[...]) * g_ref[...]
    branches = (swa_ref[...], o_rwkv, fox_ref[...], dsa_ref[...])
    merged = jnp.zeros(o_ref.shape, F32)
    off = 0
    for i, o in enumerate(branches):
        width = MIX_WIDTHS[i]
        proj = _dot(o.astype(BF16), wb_ref[off:off + width, :])
        gate = _sigmoid(_dot(xb, wg_ref[0, :, i * D_MODEL:(i + 1) * D_MODEL]) + gbias_ref[i:i + 1, :])
        merged = merged + gate * proj
        off += width
    y_out = _dot(merged.astype(BF16), wo_ref[...])
    o_ref[...] = _layer_norm(DN_ALPHA * x_ref[...] + y_out, lng_ref[...], lnb_ref[...])


def _merge(xf, o_swa, y, g, bonus, o_fox, o_dsa, gn_g, gn_b, gate_bias, w_gate_all, layer, w_branch, w_out,
           ln_g, ln_b, tm=512):
    n, d = xf.shape
    w_gate = w_gate_all
    row = lambda x: x.reshape(1, -1)
    tok = lambda wdt: pl.BlockSpec((tm, wdt), lambda i: (i, 0))
    full = lambda shape: pl.BlockSpec(shape, lambda i: (0,) * len(shape))
    return pl.pallas_call(
        _merge_kernel,
        grid=(n // tm,),
        in_specs=[tok(d), tok(512), tok(256), tok(256), tok(256), tok(256), tok(256),
                  full((1, 256)), full((1, 256)), full((N_BRANCHES, d)),
                  pl.BlockSpec((1, d, N_BRANCHES * d), lambda i: (layer, 0, 0)),
                  full((sum(MIX_WIDTHS), d)), full((d, d)), full((1, d)), full((1, d))],
        out_specs=tok(d),
        out_shape=jax.ShapeDtypeStruct((n, d), F32),
        compiler_params=_cparams(("parallel",)),
        name="merge",
    )(xf, o_swa, y, g, bonus, o_fox, o_dsa, row(gn_g), row(gn_b), gate_bias, w_gate, w_branch, w_out,
      row(ln_g), row(ln_b))


def _silu(x):
    return x * _sigmoid(x)


def _ffn_kernel(x_ref, w1_ref, w3_ref, w2_ref, lng_ref, lnb_ref, o_ref, *, tf):
    xb = x_ref[...].astype(BF16)
    y = None
    for f in range(0, w1_ref.shape[1], tf):
        hidden = _silu(_dot(xb, w1_ref[:, f:f + tf])) * _dot(xb, w3_ref[:, f:f + tf])
        part = _dot(hidden.astype(BF16), w2_ref[f:f + tf, :])
        y = part if y is None else y + part
    o_ref[...] = _layer_norm(DN_ALPHA * x_ref[...] + y, lng_ref[...], lnb_ref[...])


def _ffn(xf, w1, w3, w2, ln_g, ln_b, tm=512, tf=1408):
    n, d = xf.shape
    ff = w1.shape[1]
    row = lambda x: x.reshape(1, -1)
    tok = pl.BlockSpec((tm, d), lambda i: (i, 0))
    resident = lambda shape: pl.BlockSpec(shape, lambda i: (0, 0), pipeline_mode=pl.Buffered(1))
    return pl.pallas_call(
        functools.partial(_ffn_kernel, tf=tf),
        grid=(n // tm,),
        in_specs=[tok, resident((d, ff)), resident((d, ff)), resident((ff, d)),
                  resident((1, d)), resident((1, d))],
        out_specs=tok,
        out_shape=jax.ShapeDtypeStruct((n, d), F32),
        compiler_params=_cparams(("parallel",)),
        name="ffn",
    )(xf, w1, w3, w2, row(ln_g), row(ln_b))


def _moe_kernel(x_ref, rw_ref, rb_ref, w1_ref, w3_ref, w2_ref, lng_ref, lnb_ref, o_ref,
                comb_ref, rank_ref, cnt_ref, xb_ref, *, sb, sb_small):
    e = pl.program_id(1)
    tm = x_ref.shape[0]
    lane = lax.broadcasted_iota(jnp.int32, (tm, 128), 1).astype(F32)

    @pl.when(e == 0)
    def _():
        o_ref[...] = jnp.zeros_like(o_ref)
        xb_ref[...] = x_ref[...].astype(BF16)
        logits = _dot(x_ref[...], rw_ref[...], HIGHEST) + rb_ref[...]
        logits = jnp.where(lane < N_EXPERTS, logits, NEG_BIG)
        m1 = jnp.max(logits, axis=-1, keepdims=True)
        i1 = jnp.min(jnp.where(logits == m1, lane, 128.0), axis=-1, keepdims=True)
        rest = jnp.where(lane == i1, NEG_BIG, logits)
        m2 = jnp.max(rest, axis=-1, keepdims=True)
        i2 = jnp.min(jnp.where(rest == m2, lane, 128.0), axis=-1, keepdims=True)
        e2 = jnp.exp(m2 - m1)
        comb_ref[...] = jnp.where(lane == i1, 1.0 / (1.0 + e2), 0.0) + jnp.where(lane == i2, e2 / (1.0 + e2), 0.0)
        chosen = jnp.where(lane == i1, 1.0, 0.0) + jnp.where(lane == i2, 1.0, 0.0)
        r = lax.broadcasted_iota(jnp.int32, (tm, tm), 0)
        c = lax.broadcasted_iota(jnp.int32, (tm, tm), 1)
        earlier = jnp.where(c < r, 1.0, 0.0).astype(BF16)
        rank = jnp.where(chosen > 0.5, _dot(earlier, chosen.astype(BF16)), -1.0)
        lane8 = lax.broadcasted_iota(jnp.int32, (8, 128), 1)
        row8 = lax.broadcasted_iota(jnp.int32, (8, 128), 0)
        rank_ref[...] = _nt(jnp.where(lane8 == row8, 1.0, 0.0).astype(F32), rank, HIGHEST)
        cnt_ref[...] = jnp.sum(chosen, axis=0, keepdims=True)

    e_f = e.astype(F32)
    n_tok = jnp.sum(jnp.where(lane[0:1, :] == e_f, cnt_ref[...], 0.0)).astype(jnp.int32)
    rank_row = rank_ref[pl.ds(e, 1), :]
    weight = jnp.sum(jnp.where(lane == e_f, comb_ref[...], 0.0), axis=-1, keepdims=True)

    def block(first_rank, rows):
        slot = lax.broadcasted_iota(jnp.int32, (rows, tm), 0).astype(F32) + first_rank
        pick = jnp.where(rank_row == slot, 1.0, 0.0).astype(BF16)
        xs = _dot(pick, xb_ref[...]).astype(BF16)
        hidden = _silu(_dot(xs, w1_ref[0])) * _dot(xs, w3_ref[0])
        out = _dot(hidden.astype(BF16), w2_ref[0]).astype(BF16)
        o_ref[...] += _tn(pick, out) * weight

    @pl.when(n_tok <= sb_small)
    def _():
        block(0.0, sb_small)

    @pl.when(n_tok > sb_small)
    def _():
        def body(s, carry):
            block((s * sb).astype(F32), sb)
            return carry
        lax.fori_loop(0, lax.div(n_tok + sb - 1, sb), body, 0)

    @pl.when(e == pl.num_programs(1) - 1)
    def _():
        o_ref[...] = _layer_norm(DN_ALPHA * x_ref[...] + o_ref[...], lng_ref[...], lnb_ref[...])


def _moe(xf, router_w, router_b, w1, w3, w2, ln_g, ln_b, tm=1024, sb=320, sb_small=256):
    n, d = xf.shape
    ne, _, fe = w1.shape
    tm = min(tm, n)
    rw = jnp.pad(router_w, ((0, 0), (0, 128 - ne)))
    rb = jnp.pad(router_b, (0, 128 - ne)).reshape(1, 128)
    row = lambda x: x.reshape(1, -1)
    return pl.pallas_call(
        functools.partial(_moe_kernel, sb=sb, sb_small=sb_small),
        grid=(n // tm, ne),
        in_specs=[pl.BlockSpec((tm, d), lambda i, e: (i, 0)),
                  pl.BlockSpec((d, 128), lambda i, e: (0, 0)),
                  pl.BlockSpec((1, 128), lambda i, e: (0, 0)),
                  pl.BlockSpec((1, d, fe), lambda i, e: (e, 0, 0)),
                  pl.BlockSpec((1, d, fe), lambda i, e: (e, 0, 0)),
                  pl.BlockSpec((1, fe, d), lambda i, e: (e, 0, 0)),
                  pl.BlockSpec((1, d), lambda i, e: (0, 0)),
                  pl.BlockSpec((1, d), lambda i, e: (0, 0))],
        out_specs=pl.BlockSpec((tm, d), lambda i, e: (i, 0)),
        out_shape=jax.ShapeDtypeStruct((n, d), F32),
        scratch_shapes=[pltpu.VMEM((tm, 128), F32), pltpu.VMEM((8, tm), F32), pltpu.VMEM((1, 128), F32),
                        pltpu.VMEM((tm, d), BF16)],
        compiler_params=pltpu.CompilerParams(dimension_semantics=("parallel", "arbitrary"),
                                             vmem_limit_bytes=MOE_VMEM_LIMIT),
        name="moe",
    )(xf, rw, rb, w1, w3, w2, row(ln_g), row(ln_b))


def _mixer_layer(xf, bsz, seq, layer, w_in, w_gate_all, sinks, mu, w0, w_up, a0, a_up, g_up, k_k, k_a, r_k,
                 gn_g, gn_b, f_bias, gate_bias, w_branch, w_out, ln_g, ln_b):
    p2 = _matmul(xf, _layout_w_in(w_in[layer]).astype(BF16), tm=1024, tn=P_WIDTH // 2)
    p3 = p2.reshape(bsz, seq, P_WIDTH)
    o_swa = _swa(p3, sinks)
    o_fox = _fox(p3, _fox_c(p3, f_bias))
    o_dsa = _dsa(p3)
    r, lw, k2, v, kn, b_, g, bonus = _rwkv_pre(p3, mu, w0, w_up, a0, a_up, g_up, k_k, k_a, r_k)
    y = _rwkv_chunk(r, lw, k2, v, kn, b_)
    n = bsz * seq
    flat = lambda t: t.reshape(n, t.shape[-1])
    return _merge(xf, flat(o_swa), flat(y), flat(g), flat(bonus), flat(o_fox), flat(o_dsa), gn_g, gn_b,
                  gate_bias, w_gate_all, layer, w_branch.astype(BF16), w_out.astype(BF16), ln_g, ln_b)


def kernel(x, w_in, swa_sinks, rwkv_mu, rwkv_w0, rwkv_w_up, rwkv_a0, rwkv_a_up, rwkv_g_up, rwkv_k_k, rwkv_k_a,
           rwkv_r_k, rwkv_gn_g, rwkv_gn_b, fox_f_bias, gate_bias, w_branch, w_out, ln_g, ln_b, ffn_w1, ffn_w3,
           ffn_w2, router_w, router_b, exp_w1, exp_w3, exp_w2):
    bsz, seq, d = x.shape
    xf = x.reshape(bsz * seq, d)
    w_gate_all = w_in[:, :, IN_OFF_GATE:].astype(BF16)
    for layer in range(DEPTH):
        xf = _mixer_layer(xf, bsz, seq, layer, w_in, w_gate_all, swa_sinks[layer], rwkv_mu[layer], rwkv_w0[layer],
                              rwkv_w_up[layer], rwkv_a0[layer], rwkv_a_up[layer], rwkv_g_up[layer],
                              rwkv_k_k[layer], rwkv_k_a[layer], rwkv_r_k[layer], rwkv_gn_g[layer],
                              rwkv_gn_b[layer], fox_f_bias[layer], gate_bias[layer], w_branch[layer],
                              w_out[layer], ln_g[layer, 0], ln_b[layer, 0])
        j = layer // 2
        if layer % 2 == 0:
            xf = _ffn(xf, ffn_w1[j].astype(BF16), ffn_w3[j].astype(BF16), ffn_w2[j].astype(BF16),
                      ln_g[layer, 1], ln_b[layer, 1])
        else:
            xf = _moe(xf, router_w[j], router_b[j], exp_w1[j].astype(BF16), exp_w3[j].astype(BF16),
                      exp_w2[j].astype(BF16), ln_g[layer, 1], ln_b[layer, 1])
    return xf.reshape(bsz, seq, d)
```

```python
import functools

import jax
import jax.numpy as jnp
from jax import lax
from jax.experimental import pallas as pl
from jax.experimental.pallas import tpu as pltpu

F32 = jnp.float32
BF16 = jnp.bfloat16
HIGHEST = lax.Precision.HIGHEST

D_MODEL = 1024
DEPTH = 2
CHUNK = 64
HEAD_DIM = 64
CHUNK_SHIFT = 6
HEAD_SHIFT = 6
SWA_HEADS = 8
SWA_KV_HEADS = 2
SWA_GROUP = SWA_HEADS // SWA_KV_HEADS
SWA_WINDOW = 128
SWA_WIN_CHUNKS = SWA_WINDOW // CHUNK
RWKV_HEADS = 4
RWKV_W = RWKV_HEADS * HEAD_DIM
RWKV_GN_EPS = 64e-5
FOX_HEADS = 4
DSA_HEADS = 4
IDX_HEADS = 4
IDX_DIM = 64
DSA_TOPK_MAX = 256
D_FF = 2816
N_EXPERTS = 8
D_FF_EXPERT = 1408
N_BRANCHES = 4
DN_ALPHA = (2 * DEPTH) ** 0.25
LN_EPS = 1e-5
ATTN_SCALE = HEAD_DIM ** -0.5
LOG2E = 1.4426950408889634
IDX_SCALE = IDX_DIM ** -0.5
IDX_W_SCALE = IDX_HEADS ** -0.5
MIX_WIDTHS = (SWA_HEADS * HEAD_DIM, RWKV_W, FOX_HEADS * HEAD_DIM, DSA_HEADS * HEAD_DIM)

COL_RWKV = 0
COL_SWA_Q = 1024
COL_FOX_KV = 1536
COL_DSA_QQ = 2048
COL_FOX_Q = 2560
COL_SWA_KV = 2816
COL_DSA_KV = 3072
COL_MISC = 3328
P_WIDTH = 3584

NEG_BIG = -1e30
INT_MIN = -(2 ** 31)
VMEM_LIMIT = 48 * 1024 * 1024
MOE_VMEM_LIMIT = 58 * 1024 * 1024


IN_OFF_SWA, IN_OFF_RWKV, IN_OFF_FOX, IN_OFF_DSA, IN_OFF_GATE = 0, 768, 1792, 2564, 3272
D_IN = IN_OFF_GATE + N_BRANCHES * D_MODEL


def _layout_w_in(w):
    d = w.shape[0]
    z = lambda n: jnp.zeros((d, n), w.dtype)
    swa, rwkv, fox, dsa = IN_OFF_SWA, IN_OFF_RWKV, IN_OFF_FOX, IN_OFF_DSA
    parts = [
        w[:, rwkv:rwkv + 1024],
        w[:, swa:swa + 512],
        w[:, fox + 256:fox + 768],
        w[:, dsa:dsa + 256], w[:, dsa + 384:dsa + 640],
        w[:, fox:fox + 256],
        w[:, swa + 512:swa + 768],
        w[:, dsa + 256:dsa + 384], w[:, dsa + 640:dsa + 704], z(64),
        w[:, fox + 768:fox + 772], w[:, dsa + 704:dsa + 708], z(120),
        z(128),
    ]
    return jnp.concatenate(parts, axis=1)


def _cparams(sem):
    return pltpu.CompilerParams(dimension_semantics=sem, vmem_limit_bytes=VMEM_LIMIT)


def _nt(a, b, precision=None):
    return lax.dot_general(a, b, (((1,), (1,)), ((), ())), precision=precision,
                           preferred_element_type=F32)


def _tn(a, b, precision=None):
    return lax.dot_general(a, b, (((0,), (0,)), ((), ())), precision=precision,
                           preferred_element_type=F32)


def _dot(a, b, precision=None):
    return jnp.dot(a, b, precision=precision, preferred_element_type=F32)


def _sigmoid(x):
    return 1.0 / (1.0 + jnp.exp(-x))


def _layer_norm(z, g, b):
    mu = jnp.mean(z, axis=-1, keepdims=True)
    zc = z - mu
    var = jnp.mean(zc * zc, axis=-1, keepdims=True)
    return zc * lax.rsqrt(var + LN_EPS) * g + b


def _skewed(n, first, second):
    out = []
    staged = first(0)
    for h in range(n):
        nxt = first(h + 1) if h + 1 < n else None
        out.append(second(h, staged))
        staged = nxt
    return out


def _paired_blocks(n, body, carry):
    carry = lax.fori_loop(0, lax.div(n, 2), lambda t, c: body((2 * t, 2 * t + 1), c), carry)
    return lax.cond(lax.rem(n, 2) == 1, lambda c: body((n - 1,), c), lambda c: c, carry)


def _head_block_ones(n, scale):
    r = lax.broadcasted_iota(jnp.int32, (n, n), 0) >> HEAD_SHIFT
    c = lax.broadcasted_iota(jnp.int32, (n, n), 1) >> HEAD_SHIFT
    return jnp.where(r == c, scale, 0.0).astype(BF16)


def _head_reduce(x, head_ones):
    hi = x.astype(BF16)
    rest = x - hi.astype(F32)
    mid = rest.astype(BF16)
    low = (rest - mid.astype(F32)).astype(BF16)
    return _dot(hi, head_ones) + _dot(mid, head_ones) + _dot(low, head_ones)


def _mm_kernel(a_ref, b_ref, o_ref):
    o_ref[...] = _dot(a_ref[...].astype(BF16), b_ref[...])


def _matmul(a, b, tm):
    m, k = a.shape
    n = b.shape[1]
    tm = min(tm, m)
    return pl.pallas_call(
        _mm_kernel,
        grid=(m // tm,),
        in_specs=[pl.BlockSpec((tm, k), lambda i: (i, 0)),
                  pl.BlockSpec((k, n), lambda i: (0, 0), pipeline_mode=pl.Buffered(1))],
        out_specs=pl.BlockSpec((tm, n), lambda i: (i, 0)),
        out_shape=jax.ShapeDtypeStruct((m, n), F32),
        compiler_params=_cparams(("parallel",)),
        name="in_proj",
    )(a, b)


def _swa_kernel(sink_ref, q_ref, kv_ref, o_ref, *, tq):
    i = pl.program_id(1)
    win = tq + SWA_WINDOW
    s0 = pl.multiple_of(jnp.maximum(i * tq - SWA_WINDOW, 0), SWA_WINDOW)
    k_chunk = (s0 + lax.broadcasted_iota(jnp.int32, (win, tq), 0)) >> CHUNK_SHIFT
    q_chunk = (i * tq + lax.broadcasted_iota(jnp.int32, (win, tq), 1)) >> CHUNK_SHIFT
    bias = jnp.where(k_chunk <= q_chunk,
                     jnp.where(k_chunk >= q_chunk - SWA_WIN_CHUNKS, 0.0, NEG_BIG), NEG_BIG)
    ks = [kv_ref[0, pl.ds(s0, win), hk * HEAD_DIM:(hk + 1) * HEAD_DIM].astype(BF16)
          for hk in range(SWA_KV_HEADS)]
    kv_w = SWA_KV_HEADS * HEAD_DIM
    vs = [kv_ref[0, pl.ds(s0, win), kv_w + hk * HEAD_DIM:kv_w + (hk + 1) * HEAD_DIM].astype(BF16)
          for hk in range(SWA_KV_HEADS)]

    def logits(h):
        q = (q_ref[0, :, h * HEAD_DIM:(h + 1) * HEAD_DIM] * (ATTN_SCALE * LOG2E)).astype(BF16)
        return _nt(ks[h // SWA_GROUP], q) + bias

    def finish(h, s):
        sink = sink_ref[h] * LOG2E
        m = jnp.maximum(jnp.max(s, axis=0, keepdims=True), sink)
        e = jnp.exp2(s - m)
        denom = jnp.sum(e, axis=0, keepdims=True) + jnp.exp2(sink - m)
        return _tn(vs[h // SWA_GROUP], e.astype(BF16)) / denom

    o_ref[0] = jnp.concatenate(_skewed(SWA_HEADS, logits, finish), axis=0).T


def _swa(p3, sinks, tq=256):
    b, s, _ = p3.shape
    kv_w = 2 * SWA_KV_HEADS * HEAD_DIM
    return pl.pallas_call(
        functools.partial(_swa_kernel, tq=tq),
        grid=(b, s // tq),
        in_specs=[pl.BlockSpec(memory_space=pltpu.SMEM),
                  pl.BlockSpec((1, tq, 512), lambda bi, i: (bi, i, COL_SWA_Q // 512)),
                  pl.BlockSpec((1, s, kv_w), lambda bi, i: (bi, 0, COL_SWA_KV // kv_w))],
        out_specs=pl.BlockSpec((1, tq, 512), lambda bi, i: (bi, i, 0)),
        out_shape=jax.ShapeDtypeStruct((b, s, 512), F32),
        compiler_params=_cparams(("parallel", "parallel")),
        name="swa",
    )(sinks, p3, p3)


def _fox_c_kernel(f_ref, bias_ref, c_ref, *, blk):
    s = f_ref.shape[1]
    r = lax.broadcasted_iota(jnp.int32, (blk, blk), 0)
    c = lax.broadcasted_iota(jnp.int32, (blk, blk), 1)
    tri = jnp.where(c <= r, 1.0, 0.0).astype(F32)
    carry = jnp.zeros((1, 128), F32)
    for j in range(s // blk):
        x = f_ref[0, j * blk:(j + 1) * blk, :] + bias_ref[...]
        log_f = jnp.minimum(x, 0.0) - jnp.log1p(jnp.exp(-jnp.abs(x)))
        cs = _dot(tri, log_f, HIGHEST) + carry
        carry = cs[blk - 1:blk, :]
        for h in range(FOX_HEADS):
            c_ref[0, h, j * blk:(j + 1) * blk, :] = jnp.broadcast_to(cs[:, h:h + 1] * LOG2E, (blk, 128))


def _fox_c(p3, f_bias):
    b, s, _ = p3.shape
    bias_row = jnp.pad(f_bias, (0, 128 - FOX_HEADS)).reshape(1, 128)
    return pl.pallas_call(
        functools.partial(_fox_c_kernel, blk=256),
        grid=(b,),
        in_specs=[pl.BlockSpec((1, s, 128), lambda bi: (bi, 0, COL_MISC // 128)),
                  pl.BlockSpec((1, 128), lambda bi: (0, 0))],
        out_specs=pl.BlockSpec((1, FOX_HEADS, s, 128), lambda bi: (bi, 0, 0, 0)),
        out_shape=jax.ShapeDtypeStruct((b, FOX_HEADS, s, 128), F32),
        compiler_params=_cparams(("parallel",)),
        name="fox_cumsum",
    )(p3, bias_row)


def _fox_kernel(q_ref, k_ref, c_ref, o_ref, lg_ref, acc_ref, *, tq):
    i = pl.program_id(1)
    tk = tq
    v_col = FOX_HEADS * HEAD_DIM
    key_i = lax.broadcasted_iota(jnp.int32, (tk, tq), 0)
    qry_i = lax.broadcasted_iota(jnp.int32, (tk, tq), 1)
    diag_bias = jnp.where(key_i <= qry_i, 0.0, NEG_BIG)
    q = [(q_ref[0, :, h * HEAD_DIM:(h + 1) * HEAD_DIM] * (ATTN_SCALE * LOG2E)).astype(BF16)
         for h in range(FOX_HEADS)]

    def logits_blocks(blocks, maxes, masked=False):
        starts = [pl.multiple_of(j * tk, tk) for j in blocks]
        dots = [[_nt(k_ref[0, pl.ds(ks, tk), h * HEAD_DIM:(h + 1) * HEAD_DIM].astype(BF16), q[h])
                 for h in range(FOX_HEADS)] for ks in starts]
        maxes = list(maxes)
        for b, ks in enumerate(starts):
            for h in range(FOX_HEADS):
                c_k = c_ref[0, h, pl.ds(ks, tk), :]
                s = dots[b][h] - jnp.concatenate([c_k] * (tq // 128), axis=1)
                if masked:
                    s = s + diag_bias
                lg_ref[h, pl.ds(ks, tk), :] = s
                maxes[h] = jnp.maximum(maxes[h], jnp.max(s, axis=0, keepdims=True))
        return tuple(maxes)

    maxes = tuple(jnp.full((1, tq), NEG_BIG, F32) for _ in range(FOX_HEADS))
    maxes = _paired_blocks(i, logits_blocks, maxes)
    maxes = logits_blocks((i,), maxes, masked=True)
    acc_ref[...] = jnp.zeros_like(acc_ref)

    def attend_blocks(blocks, sums):
        starts = [pl.multiple_of(j * tk, tk) for j in blocks]
        sums = list(sums)
        for h in range(FOX_HEADS):
            update = None
            for ks in starts:
                p = jnp.exp2(lg_ref[h, pl.ds(ks, tk), :] - maxes[h])
                v = k_ref[0, pl.ds(ks, tk), v_col + h * HEAD_DIM:v_col + (h + 1) * HEAD_DIM].astype(BF16)
                pv = _tn(v, p.astype(BF16))
                update = pv if update is None else update + pv
                sums[h] = sums[h] + jnp.sum(p, axis=0, keepdims=True)
            acc_ref[h] += update
        return tuple(sums)

    sums = _paired_blocks(i + 1, attend_blocks, tuple(jnp.zeros((1, tq), F32) for _ in range(FOX_HEADS)))
    out_t = jnp.concatenate([acc_ref[h] / sums[h] for h in range(FOX_HEADS)], axis=0)
    o_ref[0] = out_t.T


def _fox(p3, c, tq=256):
    b, s, _ = p3.shape
    w = FOX_HEADS * HEAD_DIM
    return pl.pallas_call(
        functools.partial(_fox_kernel, tq=tq),
        grid=(b, s // tq),
        in_specs=[pl.BlockSpec((1, tq, w), lambda bi, i: (bi, i, COL_FOX_Q // w)),
                  pl.BlockSpec((1, s, 2 * w), lambda bi, i: (bi, 0, COL_FOX_KV // (2 * w))),
                  pl.BlockSpec((1, FOX_HEADS, s, 128), lambda bi, i: (bi, 0, 0, 0))],
        out_specs=pl.BlockSpec((1, tq, w), lambda bi, i: (bi, i, 0)),
        out_shape=jax.ShapeDtypeStruct((b, s, w), F32),
        scratch_shapes=[pltpu.VMEM((FOX_HEADS, s, tq), F32), pltpu.VMEM((FOX_HEADS, HEAD_DIM, tq), F32)],
        compiler_params=_cparams(("parallel", "parallel")),
        name="fox",
    )(p3, p3, c)


def _key_to_score(key):
    return pltpu.bitcast(key ^ ((key >> 31) & 0x7FFFFFFF), F32)


def _bit_planes(score):
    bits = pltpu.bitcast(score, jnp.int32)
    image = bits ^ ((bits >> 31) | INT_MIN)
    rows = image.reshape(32, 8, image.shape[1])
    v = [rows[j] for j in range(32)]
    for d, mask in ((16, 0x0000FFFF), (8, 0x00FF00FF), (4, 0x0F0F0F0F), (2, 0x33333333), (1, 0x55555555)):
        for k in range(32):
            if k & d == 0:
                t = (v[k] ^ lax.shift_right_logical(v[k + d], d)) & mask
                v[k] = v[k] ^ t
                v[k + d] = v[k + d] ^ lax.shift_left(t, d)
    return v


def _dsa_kernel(qq_ref, misc_ref, kv_ref, o_ref, sc_ref, lg_ref, acc_ref, pl_ref, *, tq, kb, topk):
    i = pl.program_id(1)
    n_kb = lax.div((i + 1) * tq + kb - 1, kb)
    lane8 = lax.broadcasted_iota(jnp.int32, (8, 128), 1)
    row8 = lax.broadcasted_iota(jnp.int32, (8, 128), 0)
    pick = jnp.where((lane8 == row8 + 4) & (row8 < IDX_HEADS), 1.0, 0.0).astype(F32)
    w_t = _nt(pick, misc_ref[0], HIGHEST) * (IDX_W_SCALE * IDX_SCALE)
    q_chunk = (i * tq + lax.broadcasted_iota(jnp.int32, (1, tq), 1)) >> CHUNK_SHIFT
    qw = DSA_HEADS * HEAD_DIM
    q_idx = [qq_ref[0, :, qw + h * IDX_DIM:qw + (h + 1) * IDX_DIM].astype(BF16) for h in range(IDX_HEADS)]

    def score_blocks(blocks, carry):
        starts = [pl.multiple_of(j * kb, kb) for j in blocks]
        dots = []
        for ks in starts:
            k_idx = kv_ref[0, pl.ds(ks, kb), 2 * HEAD_DIM:2 * HEAD_DIM + IDX_DIM].astype(BF16)
            dots.append([_nt(k_idx, q_idx[h]) for h in range(IDX_HEADS)])
        for b, ks in enumerate(starts):
            score = jnp.maximum(dots[b][0], 0.0) * w_t[0:1, :]
            for h in range(1, IDX_HEADS):
                score = score + jnp.maximum(dots[b][h], 0.0) * w_t[h:h + 1, :]
            k_chunk = (ks + lax.broadcasted_iota(jnp.int32, (kb, tq), 0)) >> CHUNK_SHIFT
            score = jnp.where(k_chunk <= q_chunk, score, -jnp.inf)
            sc_ref[pl.ds(ks, kb), :] = score
            planes = _bit_planes(score)
            for t in range(32):
                pl_ref[blocks[b], t] = planes[t]
        return carry

    _paired_blocks(n_kb, score_blocks, 0)
    n_blocks = pl_ref.shape[0]

    def clear_planes(j, carry):
        pl_ref[j] = jnp.zeros(pl_ref.shape[1:], jnp.int32)
        return carry

    lax.fori_loop(n_kb, n_blocks, clear_planes, 0)

    def count(pred):
        def body(j, acc):
            blk = sc_ref[pl.ds(pl.multiple_of(j * kb, kb), kb), :]
            hit = jnp.where(pred(blk), 1, 0).astype(jnp.int32).reshape(kb // 8, 8, tq)
            parts = [hit[g] for g in range(kb // 8)]
            while len(parts) > 1:
                parts = [parts[g] + parts[g + 1] for g in range(0, len(parts), 2)]
            return acc + parts[0]
        acc = lax.fori_loop(0, n_kb, body, jnp.zeros((8, tq), jnp.int32))
        return jnp.sum(acc.astype(F32), axis=0, keepdims=True)

    def search_by_compares():
        zero = jnp.zeros((1, tq), F32)
        thr_key = jnp.where(count(lambda blk: blk >= zero) >= topk, 0, INT_MIN).astype(jnp.int32)

        def bit_step(bi, thr_key):
            cand = thr_key | lax.shift_left(jnp.int32(1), 30 - bi)
            cand_score = _key_to_score(cand)
            return jnp.where(count(lambda blk: blk >= cand_score) >= topk, cand, thr_key)

        thr_key = lax.fori_loop(0, 31, bit_step, thr_key)
        t = jnp.where(thr_key == INT_MIN, -jnp.inf, _key_to_score(thr_key))
        return t, count(lambda blk: blk > t)

    def search_by_planes():
        def bit_pass(t, carry):
            prefix, above, alive = carry
            ones = [alive[j] & pl_ref[j, t] for j in range(n_blocks)]
            parts = [lax.population_count(o) for o in ones]
            while len(parts) > 1:
                parts = [parts[g] + parts[g + 1] for g in range(0, len(parts), 2)]
            reach = above + jnp.sum(parts[0].astype(F32), axis=0, keepdims=True)
            take = reach >= topk
            prefix = jnp.where(take, prefix | lax.shift_left(jnp.int32(1), 31 - t), prefix)
            alive = tuple(jnp.where(take, ones[j], alive[j] ^ ones[j]) for j in range(n_blocks))
            return prefix, jnp.where(take, above, reach), alive

        init = (jnp.zeros((1, tq), jnp.int32), jnp.zeros((1, tq), F32),
                tuple(jnp.full((8, tq), -1, jnp.int32) for _ in range(n_blocks)))
        image, _, _ = lax.fori_loop(0, 32, bit_pass, init)
        return pltpu.bitcast(image ^ (~(image >> 31) | INT_MIN), F32)

    thr_planes = search_by_planes()
    n_above = count(lambda blk: blk > thr_planes)
    n_reach = count(lambda blk: blk >= thr_planes)
    confirmed = jnp.min(jnp.where(n_above < topk, jnp.where(n_reach >= topk, 1.0, 0.0), 0.0)) > 0.5
    thr, n_above = lax.cond(confirmed, lambda: (thr_planes, n_above), search_by_compares)
    need = topk - n_above

    r = lax.broadcasted_iota(jnp.int32, (kb, kb), 0)
    c = lax.broadcasted_iota(jnp.int32, (kb, kb), 1)
    earlier = jnp.where(c < r, 1.0, 0.0).astype(BF16)
    q_att = [(qq_ref[0, :, h * HEAD_DIM:(h + 1) * HEAD_DIM] * (ATTN_SCALE * LOG2E)).astype(BF16)
             for h in range(DSA_HEADS)]

    def logits_blocks(blocks, carry):
        ties_before, maxes = carry
        maxes = list(maxes)
        starts = [pl.multiple_of(j * kb, kb) for j in blocks]
        dots, ties = [], []
        for ks in starts:
            k = kv_ref[0, pl.ds(ks, kb), 0:HEAD_DIM].astype(BF16)
            dots.append([_nt(k, q_att[h]) for h in range(DSA_HEADS)])
            k_chunk = (ks + lax.broadcasted_iota(jnp.int32, (kb, tq), 0)) >> CHUNK_SHIFT
            ties.append(jnp.where(sc_ref[pl.ds(ks, kb), :] == thr, jnp.where(k_chunk <= q_chunk, 1.0, 0.0), 0.0))
        ranks = [_dot(earlier, tie.astype(BF16)) for tie in ties]
        for b, ks in enumerate(starts):
            tie_taken = jnp.where(ranks[b] + ties_before < need, ties[b], 0.0)
            bias = jnp.where(sc_ref[pl.ds(ks, kb), :] > thr, 0.0, jnp.where(tie_taken > 0.5, 0.0, NEG_BIG))
            ties_before = ties_before + jnp.sum(ties[b], axis=0, keepdims=True)
            for h in range(DSA_HEADS):
                s = dots[b][h] + bias
                lg_ref[h, pl.ds(ks, kb), :] = s
                maxes[h] = jnp.maximum(maxes[h], jnp.max(s, axis=0, keepdims=True))
        return ties_before, tuple(maxes)

    init = (jnp.zeros((1, tq), F32), tuple(jnp.full((1, tq), NEG_BIG, F32) for _ in range(DSA_HEADS)))
    _, maxes = _paired_blocks(n_kb, logits_blocks, init)

    acc_ref[...] = jnp.zeros_like(acc_ref)

    def attend_blocks(blocks, sums):
        starts = [pl.multiple_of(j * kb, kb) for j in blocks]
        vs = [kv_ref[0, pl.ds(ks, kb), HEAD_DIM:2 * HEAD_DIM].astype(BF16) for ks in starts]
        sums = list(sums)
        for h in range(DSA_HEADS):
            update = None
            for b, ks in enumerate(starts):
                p = jnp.exp2(lg_ref[h, pl.ds(ks, kb), :] - maxes[h])
                pv = _tn(vs[b], p.astype(BF16))
                update = pv if update is None else update + pv
                sums[h] = sums[h] + jnp.sum(p, axis=0, keepdims=True)
            acc_ref[h] += update
        return tuple(sums)

    sums = _paired_blocks(n_kb, attend_blocks, tuple(jnp.zeros((1, tq), F32) for _ in range(DSA_HEADS)))
    out_t = jnp.concatenate([acc_ref[h] / sums[h] for h in range(DSA_HEADS)], axis=0)
    o_ref[0] = out_t.T


def _dsa(p3, tq=256, kb=256):
    b, s, _ = p3.shape
    assert kb == 8 * 32, "a key block is the 32 row groups of one bit-plane word"
    topk = min(DSA_TOPK_MAX, s // 4)
    return pl.pallas_call(
        functools.partial(_dsa_kernel, tq=tq, kb=kb, topk=topk),
        grid=(b, s // tq),
        in_specs=[pl.BlockSpec((1, tq, 512), lambda bi, i: (bi, i, COL_DSA_QQ // 512)),
                  pl.BlockSpec((1, tq, 128), lambda bi, i: (bi, i, COL_MISC // 128)),
                  pl.BlockSpec((1, s, 256), lambda bi, i: (bi, 0, COL_DSA_KV // 256))],
        out_specs=pl.BlockSpec((1, tq, 256), lambda bi, i: (bi, i, 0)),
        out_shape=jax.ShapeDtypeStruct((b, s, 256), F32),
        scratch_shapes=[pltpu.VMEM((s, tq), F32), pltpu.VMEM((DSA_HEADS, s, tq), F32),
                        pltpu.VMEM((DSA_HEADS, HEAD_DIM, tq), F32), pltpu.VMEM((s // kb, 32, 8, tq), jnp.int32)],
        compiler_params=_cparams(("parallel", "parallel")),
        name="dsa",
    )(p3, p3, p3)


def _rwkv_pre_kernel(p_ref, mu_ref, w0_ref, wup_ref, a0_ref, aup_ref, gup_ref, kk_ref, ka_ref, rk_ref,
                     r_o, lw_o, k_o, v_o, kn_o, b_o, g_o, bonus_o, last_ref, *, tt):
    t = pl.program_id(1)

    @pl.when(t == 0)
    def _():
        last_ref[...] = jnp.zeros_like(last_ref)

    p = p_ref[0]
    row = lax.broadcasted_iota(jnp.int32, p.shape, 0)
    p_prev = jnp.where(row == 0, last_ref[...], pltpu.roll(p, 1, axis=0))
    last_ref[...] = p[tt - 1:tt, :]
    ps = p + mu_ref[...] * (p_prev - p)
    w = RWKV_W
    r, k, v = ps[:, 0:w], ps[:, w:2 * w], ps[:, 2 * w:3 * w]
    w_lo, a_lo, g_lo = ps[:, 3 * w:3 * w + 64], ps[:, 3 * w + 64:3 * w + 128], ps[:, 3 * w + 128:3 * w + 256]
    ww = w0_ref[...] + _dot(jnp.tanh(w_lo), wup_ref[...], HIGHEST)
    softplus_neg = jnp.maximum(-ww, 0.0) + jnp.log1p(jnp.exp(-jnp.abs(ww)))
    log_w = -jnp.exp(-softplus_neg - 0.5)
    a = _sigmoid(a0_ref[...] + _dot(a_lo, aup_ref[...], HIGHEST))
    g = _dot(_sigmoid(g_lo), gup_ref[...], HIGHEST)
    head_sum = _head_block_ones(w, 1.0)
    kn = k * kk_ref[...]
    kn = kn * lax.rsqrt(_head_reduce(kn * kn, head_sum) + 1e-12)
    k2 = k * (1.0 + (a - 1.0) * ka_ref[...])
    bonus = _head_reduce(r * k2 * rk_ref[...], head_sum) * v
    r_o[0] = r
    lw_o[0] = log_w
    k_o[0] = k2
    v_o[0] = v
    kn_o[0] = kn
    b_o[0] = kn * a
    g_o[0] = g
    bonus_o[0] = bonus


def _rwkv_pre(p3, mu, w0, w_up, a0, a_up, g_up, k_k, k_a, r_k, tt=512):
    b, s, _ = p3.shape
    w = RWKV_W
    row = lambda x: x.reshape(1, -1)
    full = lambda shape: pl.BlockSpec(shape, lambda bi, t: (0,) * len(shape))
    out = jax.ShapeDtypeStruct((b, s, w), F32)
    return pl.pallas_call(
        functools.partial(_rwkv_pre_kernel, tt=tt),
        grid=(b, s // tt),
        in_specs=[pl.BlockSpec((1, tt, 1024), lambda bi, t: (bi, t, COL_RWKV // 1024)),
                  full((1, 1024)), full((1, w)), full((64, w)), full((1, w)), full((64, w)),
                  full((128, w)), full((1, w)), full((1, w)), full((1, w))],
        out_specs=[pl.BlockSpec((1, tt, w), lambda bi, t: (bi, t, 0))] * 8,
        out_shape=[out] * 8,
        scratch_shapes=[pltpu.VMEM((1, 1024), F32)],
        compiler_params=_cparams(("parallel", "arbitrary")),
        name="rwkv_pre",
    )(p3, row(mu), row(w0), w_up, row(a0), a_up, g_up, row(k_k), row(k_a), row(r_k))


def _bdot(a, b):
    return _dot(a.astype(BF16), b.astype(BF16))


def _bnt(a, b):
    return _nt(a.astype(BF16), b.astype(BF16))


def _btn(a, b):
    return _tn(a.astype(BF16), b.astype(BF16))


def _rwkv_chunk_kernel(r_ref, lw_ref, k_ref, v_ref, kn_ref, b_ref, y_ref, s_ref, *, cs, nch):
    c = pl.program_id(1)

    @pl.when(c == 0)
    def _():
        s_ref[...] = jnp.zeros_like(s_ref)

    rows = cs * min(nch, 4)
    big_row = lax.broadcasted_iota(jnp.int32, (rows, rows), 0)
    big_col = lax.broadcasted_iota(jnp.int32, (rows, rows), 1)
    same_chunk = (big_row >> CHUNK_SHIFT) == (big_col >> CHUNK_SHIFT)
    tri = jnp.where(same_chunk, jnp.where(big_col <= big_row, 1.0, 0.0), 0.0).astype(F32)
    pw = 2 * HEAD_DIM
    row = lax.broadcasted_iota(jnp.int32, (cs, pw), 0)
    lane = lax.broadcasted_iota(jnp.int32, (cs, pw), 1)
    col = lane & (HEAD_DIM - 1)
    first_head = lane < HEAD_DIM
    incl = col <= row
    strict = col < row
    eye = jnp.where(col == row, 1.0, 0.0).astype(F32)

    def block_diag(x):
        return jnp.concatenate([jnp.where(first_head, x, 0.0), jnp.where(first_head, 0.0, x)], axis=0)

    def diag_blocks(full):
        return jnp.where(first_head, full[:cs], full[cs:])

    lw = lw_ref[0]
    cum = jnp.concatenate([_dot(tri, lw[g:g + rows], HIGHEST) for g in range(0, cs * nch, rows)],
                          axis=0)
    e_incl = jnp.exp(cum)
    e_neg = jnp.exp(-cum)
    abar_all = -kn_ref[0] * jnp.exp(cum - lw)
    rbar_all = r_ref[0] * e_incl
    bt_all = b_ref[0] * e_neg
    kt_all = k_ref[0] * e_neg
    v_all = v_ref[0]
    n_double = max(cs.bit_length() - 2, 0)
    n_pairs = RWKV_HEADS // 2
    probs = [(j, p) for j in range(nch) for p in range(n_pairs)]
    cut = lambda t, jp: t[jp[0] * cs:(jp[0] + 1) * cs, jp[1] * pw:(jp[1] + 1) * pw]
    each = lambda fn: {jp: fn(jp) for jp in probs}
    abar, rbar = each(lambda jp: cut(abar_all, jp)), each(lambda jp: cut(rbar_all, jp))
    bt, kt, v = each(lambda jp: cut(bt_all, jp)), each(lambda jp: cut(kt_all, jp)), each(lambda jp: cut(v_all, jp))
    p_last = each(lambda jp: e_incl[(jp[0] + 1) * cs - 1:(jp[0] + 1) * cs, jp[1] * pw:(jp[1] + 1) * pw])
    ar = each(lambda jp: jnp.concatenate([abar[jp], rbar[jp]], axis=0))
    ar_b = each(lambda jp: _bnt(ar[jp], block_diag(bt[jp])))
    ar_k = each(lambda jp: _bnt(ar[jp], block_diag(kt[jp])))
    a_ab = each(lambda jp: jnp.where(strict, ar_b[jp][:cs], 0.0))
    a_ak = each(lambda jp: jnp.where(strict, ar_k[jp][:cs], 0.0))
    a_rb = each(lambda jp: jnp.where(incl, ar_b[jp][cs:], 0.0))
    a_rk = each(lambda jp: jnp.where(incl, ar_k[jp][cs:], 0.0))
    v_bd = each(lambda jp: block_diag(v[jp]))
    ak_v = each(lambda jp: _bdot(a_ak[jp], v_bd[jp]))
    inv = each(lambda jp: eye + a_ab[jp])
    power = a_ab
    power_bd = each(lambda jp: block_diag(power[jp]))
    for _ in range(n_double):
        power = each(lambda jp: _bdot(power[jp], power_bd[jp]))
        power_bd = each(lambda jp: block_diag(power[jp]))
        inv = each(lambda jp: inv[jp] + _bdot(inv[jp], power_bd[jp]))
    a_hat = each(lambda jp: _bdot(inv[jp], block_diag(abar[jp])))
    u_hat = each(lambda jp: _bdot(inv[jp], block_diag(ak_v[jp])))
    r_hat = each(lambda jp: rbar[jp] + _bdot(a_rb[jp], block_diag(a_hat[jp])))
    y_hat = each(lambda jp: _bdot(a_rb[jp], block_diag(u_hat[jp])) + _bdot(a_rk[jp], v_bd[jp]))
    g_mat = each(lambda jp: eye * p_last[jp] + diag_blocks(_btn(a_hat[jp], bt[jp] * p_last[jp])))
    h_mat = each(lambda jp: diag_blocks(_btn(jnp.concatenate([u_hat[jp], v[jp]], axis=0),
                                             jnp.concatenate([bt[jp], kt[jp]], axis=0) * p_last[jp])))
    states = [s_ref[p] for p in range(n_pairs)]
    for j in range(nch):
        for p in range(n_pairs):
            y_ref[0, j * cs:(j + 1) * cs, p * pw:(p + 1) * pw] = (
                _bnt(r_hat[j, p], block_diag(states[p])) + y_hat[j, p])
        states = [_dot(states[p], block_diag(g_mat[j, p]), HIGHEST) + h_mat[j, p] for p in range(n_pairs)]
    for p in range(n_pairs):
        s_ref[p] = states[p]


def _rwkv_chunk(r, lw, k, v, kn, b_, cs=CHUNK, nch=8):
    b, s, w = r.shape
    spec = pl.BlockSpec((1, cs * nch, w), lambda bi, c: (bi, c, 0))
    return pl.pallas_call(
        functools.partial(_rwkv_chunk_kernel, cs=cs, nch=nch),
        grid=(b, s // (cs * nch)),
        in_specs=[spec] * 6,
        out_specs=spec,
        out_shape=jax.ShapeDtypeStruct((b, s, w), F32),
        scratch_shapes=[pltpu.VMEM((RWKV_HEADS // 2, HEAD_DIM, 2 * HEAD_DIM), F32)],
        compiler_params=_cparams(("parallel", "arbitrary")),
        name="rwkv_chunk",
    )(r, lw, k, v, kn, b_)


def _merge_kernel(x_ref, swa_ref, y_ref, g_ref, bonus_ref, fox_ref, dsa_ref, gng_ref, gnb_ref,
                  gbias_ref, wg_ref, wb_ref, wo_ref, lng_ref, lnb_ref, o_ref):
    y = y_ref[...]
    xb = x_ref[...].astype(BF16)
    head_mean = _head_block_ones(RWKV_W, 1.0 / HEAD_DIM)
    yc = y - _head_reduce(y, head_mean)
    yv = _head_reduce(yc * yc, head_mean)
    o_rwkv = (yc * lax.rsqrt(yv + RWKV_GN_EPS) * gng_ref[...] + gnb_ref[...] + bonus_ref[...]) * g_ref[...]
    branches = (swa_ref[...], o_rwkv, fox_ref[...], dsa_ref[...])
    merged = jnp.zeros(o_ref.shape, F32)
    off = 0
    for i, o in enumerate(branches):
        width = MIX_WIDTHS[i]
        proj = _dot(o.astype(BF16), wb_ref[off:off + width, :])
        gate = _sigmoid(_nt(xb, wg_ref[0, i * D_MODEL:(i + 1) * D_MODEL, :]) + gbias_ref[i:i + 1, :])
        merged = merged + gate * proj
        off += width
    y_out = _dot(merged.astype(BF16), wo_ref[...])
    o_ref[...] = _layer_norm(DN_ALPHA * x_ref[...] + y_out, lng_ref[...], lnb_ref[...])


def _merge(xf, o_swa, y, g, bonus, o_fox, o_dsa, gn_g, gn_b, gate_bias, w_gate_all, layer, w_branch, w_out,
           ln_g, ln_b, tm=512):
    n, d = xf.shape
    w_gate = w_gate_all
    row = lambda x: x.reshape(1, -1)
    tok = lambda wdt: pl.BlockSpec((tm, wdt), lambda i: (i, 0))
    full = lambda shape: pl.BlockSpec(shape, lambda i: (0,) * len(shape))
    return pl.pallas_call(
        _merge_kernel,
        grid=(n // tm,),
        in_specs=[tok(d), tok(512), tok(256), tok(256), tok(256), tok(256), tok(256),
                  full((1, 256)), full((1, 256)), full((N_BRANCHES, d)),
                  pl.BlockSpec((1, N_BRANCHES * d, d), lambda i: (layer, 0, 0)),
                  full((sum(MIX_WIDTHS), d)), full((d, d)), full((1, d)), full((1, d))],
        out_specs=tok(d),
        out_shape=jax.ShapeDtypeStruct((n, d), F32),
        compiler_params=_cparams(("parallel",)),
        name="merge",
    )(xf, o_swa, y, g, bonus, o_fox, o_dsa, row(gn_g), row(gn_b), gate_bias, w_gate, w_branch, w_out,
      row(ln_g), row(ln_b))


def _silu(x):
    return x * _sigmoid(x)


def _ffn_kernel(x_ref, w1_ref, w3_ref, w2_ref, lng_ref, lnb_ref, o_ref, *, tf):
    xb = x_ref[...].astype(BF16)
    y = None
    for f in range(0, w1_ref.shape[1], tf):
        hidden = _silu(_dot(xb, w1_ref[:, f:f + tf])) * _dot(xb, w3_ref[:, f:f + tf])
        part = _dot(hidden.astype(BF16), w2_ref[f:f + tf, :])
        y = part if y is None else y + part
    o_ref[...] = _layer_norm(DN_ALPHA * x_ref[...] + y, lng_ref[...], lnb_ref[...])


def _ffn(xf, w1, w3, w2, ln_g, ln_b, tm=512, tf=1408):
    n, d = xf.shape
    ff = w1.shape[1]
    row = lambda x: x.reshape(1, -1)
    tok = pl.BlockSpec((tm, d), lambda i: (i, 0))
    resident = lambda shape: pl.BlockSpec(shape, lambda i: (0, 0), pipeline_mode=pl.Buffered(1))
    return pl.pallas_call(
        functools.partial(_ffn_kernel, tf=tf),
        grid=(n // tm,),
        in_specs=[tok, resident((d, ff)), resident((d, ff)), resident((ff, d)),
                  resident((1, d)), resident((1, d))],
        out_specs=tok,
        out_shape=jax.ShapeDtypeStruct((n, d), F32),
        compiler_params=_cparams(("parallel",)),
        name="ffn",
    )(xf, w1, w3, w2, row(ln_g), row(ln_b))


def _moe_kernel(x_ref, rw_ref, rb_ref, w1_ref, w3_ref, w2_ref, lng_ref, lnb_ref, o_ref,
                comb_ref, rank_ref, cnt_ref, xb_ref, *, sb, sb_small):
    e = pl.program_id(1)
    tm = x_ref.shape[0]
    lane = lax.broadcasted_iota(jnp.int32, (tm, 128), 1).astype(F32)

    @pl.when(e == 0)
    def _():
        o_ref[...] = jnp.zeros_like(o_ref)
        xb_ref[...] = x_ref[...].astype(BF16)
        logits = _dot(x_ref[...], rw_ref[...], HIGHEST) + rb_ref[...]
        logits = jnp.where(lane < N_EXPERTS, logits, NEG_BIG)
        m1 = jnp.max(logits, axis=-1, keepdims=True)
        i1 = jnp.min(jnp.where(logits == m1, lane, 128.0), axis=-1, keepdims=True)
        rest = jnp.where(lane == i1, NEG_BIG, logits)
        m2 = jnp.max(rest, axis=-1, keepdims=True)
        i2 = jnp.min(jnp.where(rest == m2, lane, 128.0), axis=-1, keepdims=True)
        e2 = jnp.exp(m2 - m1)
        comb_ref[...] = jnp.where(lane == i1, 1.0 / (1.0 + e2), 0.0) + jnp.where(lane == i2, e2 / (1.0 + e2), 0.0)
        chosen = jnp.where(lane == i1, 1.0, 0.0) + jnp.where(lane == i2, 1.0, 0.0)
        r = lax.broadcasted_iota(jnp.int32, (tm, tm), 0)
        c = lax.broadcasted_iota(jnp.int32, (tm, tm), 1)
        earlier = jnp.where(c < r, 1.0, 0.0).astype(BF16)
        rank = jnp.where(chosen > 0.5, _dot(earlier, chosen.astype(BF16)), -1.0)
        lane8 = lax.broadcasted_iota(jnp.int32, (8, 128), 1)
        row8 = lax.broadcasted_iota(jnp.int32, (8, 128), 0)
        rank_ref[...] = _nt(jnp.where(lane8 == row8, 1.0, 0.0).astype(F32), rank, HIGHEST)
        cnt_ref[...] = jnp.sum(chosen, axis=0, keepdims=True)

    e_f = e.astype(F32)
    n_tok = jnp.sum(jnp.where(lane[0:1, :] == e_f, cnt_ref[...], 0.0)).astype(jnp.int32)
    rank_row = rank_ref[pl.ds(e, 1), :]
    weight = jnp.sum(jnp.where(lane == e_f, comb_ref[...], 0.0), axis=-1, keepdims=True)

    def block(first_rank, rows):
        slot = lax.broadcasted_iota(jnp.int32, (rows, tm), 0).astype(F32) + first_rank
        pick = jnp.where(rank_row == slot, 1.0, 0.0).astype(BF16)
        xs = _dot(pick, xb_ref[...]).astype(BF16)
        hidden = _silu(_dot(xs, w1_ref[0])) * _dot(xs, w3_ref[0])
        out = _dot(hidden.astype(BF16), w2_ref[0]).astype(BF16)
        o_ref[...] += _tn(pick, out) * weight

    @pl.when(n_tok <= sb_small)
    def _():
        block(0.0, sb_small)

    @pl.when(n_tok > sb_small)
    def _():
        def body(s, carry):
            block((s * sb).astype(F32), sb)
            return carry
        lax.fori_loop(0, lax.div(n_tok + sb - 1, sb), body, 0)

    @pl.when(e == pl.num_programs(1) - 1)
    def _():
        o_ref[...] = _layer_norm(DN_ALPHA * x_ref[...] + o_ref[...], lng_ref[...], lnb_ref[...])


def _moe(xf, router_w, router_b, w1, w3, w2, ln_g, ln_b, tm=1024, sb=320, sb_small=256):
    n, d = xf.shape
    ne, _, fe = w1.shape
    tm = min(tm, n)
    rw = jnp.pad(router_w, ((0, 0), (0, 128 - ne)))
    rb = jnp.pad(router_b, (0, 128 - ne)).reshape(1, 128)
    row = lambda x: x.reshape(1, -1)
    return pl.pallas_call(
        functools.partial(_moe_kernel, sb=sb, sb_small=sb_small),
        grid=(n // tm, ne),
        in_specs=[pl.BlockSpec((tm, d), lambda i, e: (i, 0)),
                  pl.BlockSpec((d, 128), lambda i, e: (0, 0)),
                  pl.BlockSpec((1, 128), lambda i, e: (0, 0)),
                  pl.BlockSpec((1, d, fe), lambda i, e: (e, 0, 0)),
                  pl.BlockSpec((1, d, fe), lambda i, e: (e, 0, 0)),
                  pl.BlockSpec((1, fe, d), lambda i, e: (e, 0, 0)),
                  pl.BlockSpec((1, d), lambda i, e: (0, 0)),
                  pl.BlockSpec((1, d), lambda i, e: (0, 0))],
        out_specs=pl.BlockSpec((tm, d), lambda i, e: (i, 0)),
        out_shape=jax.ShapeDtypeStruct((n, d), F32),
        scratch_shapes=[pltpu.VMEM((tm, 128), F32), pltpu.VMEM((8, tm), F32), pltpu.VMEM((1, 128), F32),
                        pltpu.VMEM((tm, d), BF16)],
        compiler_params=pltpu.CompilerParams(dimension_semantics=("parallel", "arbitrary"),
                                             vmem_limit_bytes=MOE_VMEM_LIMIT),
        name="moe",
    )(xf, rw, rb, w1, w3, w2, row(ln_g), row(ln_b))


def _mixer_layer(xf, bsz, seq, layer, w_in, w_gate_all, sinks, mu, w0, w_up, a0, a_up, g_up, k_k, k_a, r_k,
                 gn_g, gn_b, f_bias, gate_bias, w_branch, w_out, ln_g, ln_b):
    p2 = _matmul(xf, _layout_w_in(w_in[layer]).astype(BF16), tm=512)
    p3 = p2.reshape(bsz, seq, P_WIDTH)
    o_swa = _swa(p3, sinks)
    o_fox = _fox(p3, _fox_c(p3, f_bias))
    o_dsa = _dsa(p3)
    r, lw, k2, v, kn, b_, g, bonus = _rwkv_pre(p3, mu, w0, w_up, a0, a_up, g_up, k_k, k_a, r_k)
    y = _rwkv_chunk(r, lw, k2, v, kn, b_)
    n = bsz * seq
    flat = lambda t: t.reshape(n, t.shape[-1])
    return _merge(xf, flat(o_swa), flat(y), flat(g), flat(bonus), flat(o_fox), flat(o_dsa), gn_g, gn_b,
                  gate_bias, w_gate_all, layer, w_branch.astype(BF16), w_out.astype(BF16), ln_g, ln_b)


def kernel(x, w_in, swa_sinks, rwkv_mu, rwkv_w0, rwkv_w_up, rwkv_a0, rwkv_a_up, rwkv_g_up, rwkv_k_k, rwkv_k_a,
           rwkv_r_k, rwkv_gn_g, rwkv_gn_b, fox_f_bias, gate_bias, w_branch, w_out, ln_g, ln_b, ffn_w1, ffn_w3,
           ffn_w2, router_w, router_b, exp_w1, exp_w3, exp_w2):
    bsz, seq, d = x.shape
    xf = x.reshape(bsz * seq, d)
    w_gate_all = jnp.swapaxes(w_in[:, :, IN_OFF_GATE:], 1, 2).astype(BF16)
    for layer in range(DEPTH):
        xf = _mixer_layer(xf, bsz, seq, layer, w_in, w_gate_all, swa_sinks[layer], rwkv_mu[layer], rwkv_w0[layer],
                              rwkv_w_up[layer], rwkv_a0[layer], rwkv_a_up[layer], rwkv_g_up[layer],
                              rwkv_k_k[layer], rwkv_k_a[layer], rwkv_r_k[layer], rwkv_gn_g[layer],
                              rwkv_gn_b[layer], fox_f_bias[layer], gate_bias[layer], w_branch[layer],
                              w_out[layer], ln_g[layer, 0], ln_b[layer, 0])
        j = layer // 2
        if layer % 2 == 0:
            xf = _ffn(xf, ffn_w1[j].astype(BF16), ffn_w3[j].astype(BF16), ffn_w2[j].astype(BF16),
                      ln_g[layer, 1], ln_b[layer, 1])
        else:
            xf = _moe(xf, router_w[j], router_b[j], exp_w1[j].astype(BF16), exp_w3[j].astype(BF16),
                      exp_w2[j].astype(BF16), ln_g[layer, 1], ln_b[layer, 1])
    return xf.reshape(bsz, seq, d)
```

```python
import functools

import jax
import jax.numpy as jnp
from jax import lax
from jax.experimental import pallas as pl
from jax.experimental.pallas import tpu as pltpu

F32 = jnp.float32
BF16 = jnp.bfloat16
HIGHEST = lax.Precision.HIGHEST

D_MODEL = 1024
DEPTH = 2
CHUNK = 64
HEAD_DIM = 64
CHUNK_SHIFT = 6
HEAD_SHIFT = 6
SWA_HEADS = 8
SWA_KV_HEADS = 2
SWA_GROUP = SWA_HEADS // SWA_KV_HEADS
SWA_WINDOW = 128
SWA_WIN_CHUNKS = SWA_WINDOW // CHUNK
RWKV_HEADS = 4
RWKV_W = RWKV_HEADS * HEAD_DIM
RWKV_GN_EPS = 64e-5
FOX_HEADS = 4
DSA_HEADS = 4
IDX_HEADS = 4
IDX_DIM = 64
DSA_TOPK_MAX = 256
D_FF = 2816
N_EXPERTS = 8
D_FF_EXPERT = 1408
N_BRANCHES = 4
DN_ALPHA = (2 * DEPTH) ** 0.25
LN_EPS = 1e-5
ATTN_SCALE = HEAD_DIM ** -0.5
LOG2E = 1.4426950408889634
IDX_SCALE = IDX_DIM ** -0.5
IDX_W_SCALE = IDX_HEADS ** -0.5
MIX_WIDTHS = (SWA_HEADS * HEAD_DIM, RWKV_W, FOX_HEADS * HEAD_DIM, DSA_HEADS * HEAD_DIM)

COL_RWKV = 0
COL_SWA_Q = 1024
COL_FOX_KV = 1536
COL_DSA_QQ = 2048
COL_FOX_Q = 2560
COL_SWA_KV = 2816
COL_DSA_KV = 3072
COL_MISC = 3328
P_WIDTH = 3584

NEG_BIG = -1e30
INT_MIN = -(2 ** 31)
VMEM_LIMIT = 48 * 1024 * 1024
MOE_VMEM_LIMIT = 58 * 1024 * 1024


IN_OFF_SWA, IN_OFF_RWKV, IN_OFF_FOX, IN_OFF_DSA, IN_OFF_GATE = 0, 768, 1792, 2564, 3272
D_IN = IN_OFF_GATE + N_BRANCHES * D_MODEL


def _layout_w_in(w):
    d = w.shape[0]
    z = lambda n: jnp.zeros((d, n), w.dtype)
    swa, rwkv, fox, dsa = IN_OFF_SWA, IN_OFF_RWKV, IN_OFF_FOX, IN_OFF_DSA
    parts = [
        w[:, rwkv:rwkv + 1024],
        w[:, swa:swa + 512],
        w[:, fox + 256:fox + 768],
        w[:, dsa:dsa + 256], w[:, dsa + 384:dsa + 640],
        w[:, fox:fox + 256],
        w[:, swa + 512:swa + 768],
        w[:, dsa + 256:dsa + 384], w[:, dsa + 640:dsa + 704], z(64),
        w[:, fox + 768:fox + 772], w[:, dsa + 704:dsa + 708], z(120),
        z(128),
    ]
    return jnp.concatenate(parts, axis=1)


def _cparams(sem):
    return pltpu.CompilerParams(dimension_semantics=sem, vmem_limit_bytes=VMEM_LIMIT)


def _nt(a, b, precision=None):
    return lax.dot_general(a, b, (((1,), (1,)), ((), ())), precision=precision,
                           preferred_element_type=F32)


def _tn(a, b, precision=None):
    return lax.dot_general(a, b, (((0,), (0,)), ((), ())), precision=precision,
                           preferred_element_type=F32)


def _dot(a, b, precision=None):
    return jnp.dot(a, b, precision=precision, preferred_element_type=F32)


def _sigmoid(x):
    return 1.0 / (1.0 + jnp.exp(-x))


def _layer_norm(z, g, b):
    mu = jnp.mean(z, axis=-1, keepdims=True)
    zc = z - mu
    var = jnp.mean(zc * zc, axis=-1, keepdims=True)
    return zc * lax.rsqrt(var + LN_EPS) * g + b


def _skewed(n, first, second):
    out = []
    staged = first(0)
    for h in range(n):
        nxt = first(h + 1) if h + 1 < n else None
        out.append(second(h, staged))
        staged = nxt
    return out


def _paired_blocks(n, body, carry):
    carry = lax.fori_loop(0, lax.div(n, 2), lambda t, c: body((2 * t, 2 * t + 1), c), carry)
    return lax.cond(lax.rem(n, 2) == 1, lambda c: body((n - 1,), c), lambda c: c, carry)


def _head_block_ones(n, scale):
    r = lax.broadcasted_iota(jnp.int32, (n, n), 0) >> HEAD_SHIFT
    c = lax.broadcasted_iota(jnp.int32, (n, n), 1) >> HEAD_SHIFT
    return jnp.where(r == c, scale, 0.0).astype(BF16)


def _pieces(x):
    hi = x.astype(BF16)
    rest = x - hi.astype(F32)
    mid = rest.astype(BF16)
    return hi, mid, (rest - mid.astype(F32)).astype(BF16)


def _head_reduce(x, head_ones):
    hi, mid, low = _pieces(x)
    return _dot(hi, head_ones) + _dot(mid, head_ones) + _dot(low, head_ones)


def _ones_dot(ones, x):
    hi, mid, low = _pieces(x)
    return _dot(ones, hi) + _dot(ones, mid) + _dot(ones, low)


def _dot_hi_lo(a, b):
    a_hi = a.astype(BF16)
    a_lo = (a - a_hi.astype(F32)).astype(BF16)
    b_hi = b.astype(BF16)
    b_lo = (b - b_hi.astype(F32)).astype(BF16)
    return _dot(a_hi, b_hi) + _dot(a_hi, b_lo) + _dot(a_lo, b_hi)


def _mm_kernel(a_ref, b_ref, o_ref):
    o_ref[...] = _dot(a_ref[...].astype(BF16), b_ref[...])


def _matmul(a, b, tm):
    m, k = a.shape
    n = b.shape[1]
    tm = min(tm, m)
    return pl.pallas_call(
        _mm_kernel,
        grid=(m // tm,),
        in_specs=[pl.BlockSpec((tm, k), lambda i: (i, 0)),
                  pl.BlockSpec((k, n), lambda i: (0, 0), pipeline_mode=pl.Buffered(1))],
        out_specs=pl.BlockSpec((tm, n), lambda i: (i, 0)),
        out_shape=jax.ShapeDtypeStruct((m, n), F32),
        compiler_params=_cparams(("parallel",)),
        name="in_proj",
    )(a, b)


def _swa_kernel(sink_ref, q_ref, kv_ref, o_ref, *, tq):
    i = pl.program_id(1)
    win = tq + SWA_WINDOW
    s0 = pl.multiple_of(jnp.maximum(i * tq - SWA_WINDOW, 0), SWA_WINDOW)
    k_chunk = (s0 + lax.broadcasted_iota(jnp.int32, (win, tq), 0)) >> CHUNK_SHIFT
    q_chunk = (i * tq + lax.broadcasted_iota(jnp.int32, (win, tq), 1)) >> CHUNK_SHIFT
    bias = jnp.where(k_chunk <= q_chunk,
                     jnp.where(k_chunk >= q_chunk - SWA_WIN_CHUNKS, 0.0, NEG_BIG), NEG_BIG)
    ks = [kv_ref[0, pl.ds(s0, win), hk * HEAD_DIM:(hk + 1) * HEAD_DIM].astype(BF16)
          for hk in range(SWA_KV_HEADS)]
    kv_w = SWA_KV_HEADS * HEAD_DIM
    vs = [kv_ref[0, pl.ds(s0, win), kv_w + hk * HEAD_DIM:kv_w + (hk + 1) * HEAD_DIM].astype(BF16)
          for hk in range(SWA_KV_HEADS)]

    def logits(h):
        q = (q_ref[0, :, h * HEAD_DIM:(h + 1) * HEAD_DIM] * (ATTN_SCALE * LOG2E)).astype(BF16)
        return _nt(ks[h // SWA_GROUP], q) + bias

    def finish(h, s):
        sink = sink_ref[h] * LOG2E
        m = jnp.maximum(jnp.max(s, axis=0, keepdims=True), sink)
        e = jnp.exp2(s - m)
        denom = jnp.sum(e, axis=0, keepdims=True) + jnp.exp2(sink - m)
        return _tn(vs[h // SWA_GROUP], e.astype(BF16)) / denom

    o_ref[0] = jnp.concatenate(_skewed(SWA_HEADS, logits, finish), axis=0).T


def _swa(p3, sinks, tq=256):
    b, s, _ = p3.shape
    kv_w = 2 * SWA_KV_HEADS * HEAD_DIM
    return pl.pallas_call(
        functools.partial(_swa_kernel, tq=tq),
        grid=(b, s // tq),
        in_specs=[pl.BlockSpec(memory_space=pltpu.SMEM),
                  pl.BlockSpec((1, tq, 512), lambda bi, i: (bi, i, COL_SWA_Q // 512)),
                  pl.BlockSpec((1, s, kv_w), lambda bi, i: (bi, 0, COL_SWA_KV // kv_w))],
        out_specs=pl.BlockSpec((1, tq, 512), lambda bi, i: (bi, i, 0)),
        out_shape=jax.ShapeDtypeStruct((b, s, 512), F32),
        compiler_params=_cparams(("parallel", "parallel")),
        name="swa",
    )(sinks, p3, p3)


def _fox_c_kernel(f_ref, bias_ref, c_ref, *, blk):
    s = f_ref.shape[1]
    r = lax.broadcasted_iota(jnp.int32, (blk, blk), 0)
    c = lax.broadcasted_iota(jnp.int32, (blk, blk), 1)
    tri = jnp.where(c <= r, 1.0, 0.0).astype(BF16)
    carry = jnp.zeros((1, 128), F32)
    for j in range(s // blk):
        x = f_ref[0, j * blk:(j + 1) * blk, :] + bias_ref[...]
        log_f = jnp.minimum(x, 0.0) - jnp.log1p(jnp.exp(-jnp.abs(x)))
        cs = _ones_dot(tri, log_f) + carry
        carry = cs[blk - 1:blk, :]
        for h in range(FOX_HEADS):
            c_ref[0, h, j * blk:(j + 1) * blk, :] = jnp.broadcast_to(cs[:, h:h + 1] * LOG2E, (blk, 128))


def _fox_c(p3, f_bias):
    b, s, _ = p3.shape
    bias_row = jnp.pad(f_bias, (0, 128 - FOX_HEADS)).reshape(1, 128)
    return pl.pallas_call(
        functools.partial(_fox_c_kernel, blk=256),
        grid=(b,),
        in_specs=[pl.BlockSpec((1, s, 128), lambda bi: (bi, 0, COL_MISC // 128)),
                  pl.BlockSpec((1, 128), lambda bi: (0, 0))],
        out_specs=pl.BlockSpec((1, FOX_HEADS, s, 128), lambda bi: (bi, 0, 0, 0)),
        out_shape=jax.ShapeDtypeStruct((b, FOX_HEADS, s, 128), F32),
        compiler_params=_cparams(("parallel",)),
        name="fox_cumsum",
    )(p3, bias_row)


def _fox_kernel(q_ref, k_ref, c_ref, o_ref, lg_ref, acc_ref, *, tq):
    i = pl.program_id(1)
    tk = tq
    v_col = FOX_HEADS * HEAD_DIM
    key_i = lax.broadcasted_iota(jnp.int32, (tk, tq), 0)
    qry_i = lax.broadcasted_iota(jnp.int32, (tk, tq), 1)
    diag_bias = jnp.where(key_i <= qry_i, 0.0, NEG_BIG)
    q = [(q_ref[0, :, h * HEAD_DIM:(h + 1) * HEAD_DIM] * (ATTN_SCALE * LOG2E)).astype(BF16)
         for h in range(FOX_HEADS)]

    def logits_blocks(blocks, maxes, masked=False):
        starts = [pl.multiple_of(j * tk, tk) for j in blocks]
        dots = [[_nt(k_ref[0, pl.ds(ks, tk), h * HEAD_DIM:(h + 1) * HEAD_DIM].astype(BF16), q[h])
                 for h in range(FOX_HEADS)] for ks in starts]
        maxes = list(maxes)
        for b, ks in enumerate(starts):
            for h in range(FOX_HEADS):
                c_k = c_ref[0, h, pl.ds(ks, tk), :]
                s = dots[b][h] - jnp.concatenate([c_k] * (tq // 128), axis=1)
                if masked:
                    s = s + diag_bias
                lg_ref[h, pl.ds(ks, tk), :] = s
                maxes[h] = jnp.maximum(maxes[h], jnp.max(s, axis=0, keepdims=True))
        return tuple(maxes)

    maxes = tuple(jnp.full((1, tq), NEG_BIG, F32) for _ in range(FOX_HEADS))
    maxes = _paired_blocks(i, logits_blocks, maxes)
    maxes = logits_blocks((i,), maxes, masked=True)
    acc_ref[...] = jnp.zeros_like(acc_ref)

    def attend_blocks(blocks, sums):
        starts = [pl.multiple_of(j * tk, tk) for j in blocks]
        sums = list(sums)
        for h in range(FOX_HEADS):
            update = None
            for ks in starts:
                p = jnp.exp2(lg_ref[h, pl.ds(ks, tk), :] - maxes[h])
                v = k_ref[0, pl.ds(ks, tk), v_col + h * HEAD_DIM:v_col + (h + 1) * HEAD_DIM].astype(BF16)
                pv = _tn(v, p.astype(BF16))
                update = pv if update is None else update + pv
                sums[h] = sums[h] + jnp.sum(p, axis=0, keepdims=True)
            acc_ref[h] += update
        return tuple(sums)

    sums = _paired_blocks(i + 1, attend_blocks, tuple(jnp.zeros((1, tq), F32) for _ in range(FOX_HEADS)))
    out_t = jnp.concatenate([acc_ref[h] / sums[h] for h in range(FOX_HEADS)], axis=0)
    o_ref[0] = out_t.T


def _fox(p3, c, tq=256):
    b, s, _ = p3.shape
    w = FOX_HEADS * HEAD_DIM
    return pl.pallas_call(
        functools.partial(_fox_kernel, tq=tq),
        grid=(b, s // tq),
        in_specs=[pl.BlockSpec((1, tq, w), lambda bi, i: (bi, i, COL_FOX_Q // w)),
                  pl.BlockSpec((1, s, 2 * w), lambda bi, i: (bi, 0, COL_FOX_KV // (2 * w))),
                  pl.BlockSpec((1, FOX_HEADS, s, 128), lambda bi, i: (bi, 0, 0, 0))],
        out_specs=pl.BlockSpec((1, tq, w), lambda bi, i: (bi, i, 0)),
        out_shape=jax.ShapeDtypeStruct((b, s, w), F32),
        scratch_shapes=[pltpu.VMEM((FOX_HEADS, s, tq), F32), pltpu.VMEM((FOX_HEADS, HEAD_DIM, tq), F32)],
        compiler_params=_cparams(("parallel", "parallel")),
        name="fox",
    )(p3, p3, c)


def _key_to_score(key):
    return pltpu.bitcast(key ^ ((key >> 31) & 0x7FFFFFFF), F32)


def _bit_planes(score):
    bits = pltpu.bitcast(score, jnp.int32)
    image = bits ^ ((bits >> 31) | INT_MIN)
    rows = image.reshape(32, 8, image.shape[1])
    v = [rows[j] for j in range(32)]
    for d, mask in ((16, 0x0000FFFF), (8, 0x00FF00FF), (4, 0x0F0F0F0F), (2, 0x33333333), (1, 0x55555555)):
        for k in range(32):
            if k & d == 0:
                t = (v[k] ^ lax.shift_right_logical(v[k + d], d)) & mask
                v[k] = v[k] ^ t
                v[k + d] = v[k + d] ^ lax.shift_left(t, d)
    return v


def _dsa_kernel(qq_ref, misc_ref, kv_ref, o_ref, sc_ref, lg_ref, acc_ref, pl_ref, *, tq, kb, topk):
    i = pl.program_id(1)
    n_kb = lax.div((i + 1) * tq + kb - 1, kb)
    lane8 = lax.broadcasted_iota(jnp.int32, (8, 128), 1)
    row8 = lax.broadcasted_iota(jnp.int32, (8, 128), 0)
    pick = jnp.where((lane8 == row8 + 4) & (row8 < IDX_HEADS), 1.0, 0.0).astype(F32)
    w_t = _nt(pick, misc_ref[0], HIGHEST) * (IDX_W_SCALE * IDX_SCALE)
    q_chunk = (i * tq + lax.broadcasted_iota(jnp.int32, (1, tq), 1)) >> CHUNK_SHIFT
    qw = DSA_HEADS * HEAD_DIM
    q_idx = [qq_ref[0, :, qw + h * IDX_DIM:qw + (h + 1) * IDX_DIM].astype(BF16) for h in range(IDX_HEADS)]

    def score_blocks(blocks, carry):
        starts = [pl.multiple_of(j * kb, kb) for j in blocks]
        dots = []
        for ks in starts:
            k_idx = kv_ref[0, pl.ds(ks, kb), 2 * HEAD_DIM:2 * HEAD_DIM + IDX_DIM].astype(BF16)
            dots.append([_nt(k_idx, q_idx[h]) for h in range(IDX_HEADS)])
        for b, ks in enumerate(starts):
            score = jnp.maximum(dots[b][0], 0.0) * w_t[0:1, :]
            for h in range(1, IDX_HEADS):
                score = score + jnp.maximum(dots[b][h], 0.0) * w_t[h:h + 1, :]
            k_chunk = (ks + lax.broadcasted_iota(jnp.int32, (kb, tq), 0)) >> CHUNK_SHIFT
            score = jnp.where(k_chunk <= q_chunk, score, -jnp.inf)
            sc_ref[pl.ds(ks, kb), :] = score
            planes = _bit_planes(score)
            for t in range(32):
                pl_ref[blocks[b], t] = planes[t]
        return carry

    _paired_blocks(n_kb, score_blocks, 0)
    n_blocks = pl_ref.shape[0]

    def clear_planes(j, carry):
        pl_ref[j] = jnp.zeros(pl_ref.shape[1:], jnp.int32)
        return carry

    lax.fori_loop(n_kb, n_blocks, clear_planes, 0)

    def count(pred):
        def body(j, acc):
            blk = sc_ref[pl.ds(pl.multiple_of(j * kb, kb), kb), :]
            hit = jnp.where(pred(blk), 1, 0).astype(jnp.int32).reshape(kb // 8, 8, tq)
            parts = [hit[g] for g in range(kb // 8)]
            while len(parts) > 1:
                parts = [parts[g] + parts[g + 1] for g in range(0, len(parts), 2)]
            return acc + parts[0]
        acc = lax.fori_loop(0, n_kb, body, jnp.zeros((8, tq), jnp.int32))
        return jnp.sum(acc.astype(F32), axis=0, keepdims=True)

    def search_by_compares():
        zero = jnp.zeros((1, tq), F32)
        thr_key = jnp.where(count(lambda blk: blk >= zero) >= topk, 0, INT_MIN).astype(jnp.int32)

        def bit_step(bi, thr_key):
            cand = thr_key | lax.shift_left(jnp.int32(1), 30 - bi)
            cand_score = _key_to_score(cand)
            return jnp.where(count(lambda blk: blk >= cand_score) >= topk, cand, thr_key)

        thr_key = lax.fori_loop(0, 31, bit_step, thr_key)
        t = jnp.where(thr_key == INT_MIN, -jnp.inf, _key_to_score(thr_key))
        return t, count(lambda blk: blk > t)

    def search_by_planes():
        def bit_pass(t, carry):
            prefix, above, alive = carry
            ones = [alive[j] & pl_ref[j, t] for j in range(n_blocks)]
            parts = [lax.population_count(o) for o in ones]
            while len(parts) > 1:
                parts = [parts[g] + parts[g + 1] for g in range(0, len(parts), 2)]
            reach = above + jnp.sum(parts[0].astype(F32), axis=0, keepdims=True)
            take = reach >= topk
            prefix = jnp.where(take, prefix | lax.shift_left(jnp.int32(1), 31 - t), prefix)
            alive = tuple(jnp.where(take, ones[j], alive[j] ^ ones[j]) for j in range(n_blocks))
            return prefix, jnp.where(take, above, reach), alive

        init = (jnp.zeros((1, tq), jnp.int32), jnp.zeros((1, tq), F32),
                tuple(jnp.full((8, tq), -1, jnp.int32) for _ in range(n_blocks)))
        image, _, _ = lax.fori_loop(0, 32, bit_pass, init)
        return pltpu.bitcast(image ^ (~(image >> 31) | INT_MIN), F32)

    thr_planes = search_by_planes()
    n_above = count(lambda blk: blk > thr_planes)
    n_reach = count(lambda blk: blk >= thr_planes)
    confirmed = jnp.min(jnp.where(n_above < topk, jnp.where(n_reach >= topk, 1.0, 0.0), 0.0)) > 0.5
    thr, n_above = lax.cond(confirmed, lambda: (thr_planes, n_above), search_by_compares)
    need = topk - n_above

    r = lax.broadcasted_iota(jnp.int32, (kb, kb), 0)
    c = lax.broadcasted_iota(jnp.int32, (kb, kb), 1)
    earlier = jnp.where(c < r, 1.0, 0.0).astype(BF16)
    q_att = [(qq_ref[0, :, h * HEAD_DIM:(h + 1) * HEAD_DIM] * (ATTN_SCALE * LOG2E)).astype(BF16)
             for h in range(DSA_HEADS)]

    def logits_blocks(blocks, carry):
        ties_before, maxes = carry
        maxes = list(maxes)
        starts = [pl.multiple_of(j * kb, kb) for j in blocks]
        dots, ties = [], []
        for ks in starts:
            k = kv_ref[0, pl.ds(ks, kb), 0:HEAD_DIM].astype(BF16)
            dots.append([_nt(k, q_att[h]) for h in range(DSA_HEADS)])
            k_chunk = (ks + lax.broadcasted_iota(jnp.int32, (kb, tq), 0)) >> CHUNK_SHIFT
            ties.append(jnp.where(sc_ref[pl.ds(ks, kb), :] == thr, jnp.where(k_chunk <= q_chunk, 1.0, 0.0), 0.0))
        ranks = [_dot(earlier, tie.astype(BF16)) for tie in ties]
        for b, ks in enumerate(starts):
            tie_taken = jnp.where(ranks[b] + ties_before < need, ties[b], 0.0)
            bias = jnp.where(sc_ref[pl.ds(ks, kb), :] > thr, 0.0, jnp.where(tie_taken > 0.5, 0.0, NEG_BIG))
            ties_before = ties_before + jnp.sum(ties[b], axis=0, keepdims=True)
            for h in range(DSA_HEADS):
                s = dots[b][h] + bias
                lg_ref[h, pl.ds(ks, kb), :] = s
                maxes[h] = jnp.maximum(maxes[h], jnp.max(s, axis=0, keepdims=True))
        return ties_before, tuple(maxes)

    init = (jnp.zeros((1, tq), F32), tuple(jnp.full((1, tq), NEG_BIG, F32) for _ in range(DSA_HEADS)))
    _, maxes = _paired_blocks(n_kb, logits_blocks, init)

    acc_ref[...] = jnp.zeros_like(acc_ref)

    def attend_blocks(blocks, sums):
        starts = [pl.multiple_of(j * kb, kb) for j in blocks]
        vs = [kv_ref[0, pl.ds(ks, kb), HEAD_DIM:2 * HEAD_DIM].astype(BF16) for ks in starts]
        sums = list(sums)
        for h in range(DSA_HEADS):
            update = None
            for b, ks in enumerate(starts):
                p = jnp.exp2(lg_ref[h, pl.ds(ks, kb), :] - maxes[h])
                pv = _tn(vs[b], p.astype(BF16))
                update = pv if update is None else update + pv
                sums[h] = sums[h] + jnp.sum(p, axis=0, keepdims=True)
            acc_ref[h] += update
        return tuple(sums)

    sums = _paired_blocks(n_kb, attend_blocks, tuple(jnp.zeros((1, tq), F32) for _ in range(DSA_HEADS)))
    out_t = jnp.concatenate([acc_ref[h] / sums[h] for h in range(DSA_HEADS)], axis=0)
    o_ref[0] = out_t.T


def _dsa(p3, tq=256, kb=256):
    b, s, _ = p3.shape
    assert kb == 8 * 32, "a key block is the 32 row groups of one bit-plane word"
    topk = min(DSA_TOPK_MAX, s // 4)
    return pl.pallas_call(
        functools.partial(_dsa_kernel, tq=tq, kb=kb, topk=topk),
        grid=(b, s // tq),
        in_specs=[pl.BlockSpec((1, tq, 512), lambda bi, i: (bi, i, COL_DSA_QQ // 512)),
                  pl.BlockSpec((1, tq, 128), lambda bi, i: (bi, i, COL_MISC // 128)),
                  pl.BlockSpec((1, s, 256), lambda bi, i: (bi, 0, COL_DSA_KV // 256))],
        out_specs=pl.BlockSpec((1, tq, 256), lambda bi, i: (bi, i, 0)),
        out_shape=jax.ShapeDtypeStruct((b, s, 256), F32),
        scratch_shapes=[pltpu.VMEM((s, tq), F32), pltpu.VMEM((DSA_HEADS, s, tq), F32),
                        pltpu.VMEM((DSA_HEADS, HEAD_DIM, tq), F32), pltpu.VMEM((s // kb, 32, 8, tq), jnp.int32)],
        compiler_params=_cparams(("parallel", "parallel")),
        name="dsa",
    )(p3, p3, p3)


def _rwkv_pre_kernel(p_ref, mu_ref, w0_ref, wup_ref, a0_ref, aup_ref, gup_ref, kk_ref, ka_ref, rk_ref,
                     r_o, lw_o, k_o, v_o, kn_o, b_o, g_o, bonus_o, last_ref, *, tt):
    t = pl.program_id(1)

    @pl.when(t == 0)
    def _():
        last_ref[...] = jnp.zeros_like(last_ref)

    p = p_ref[0]
    row = lax.broadcasted_iota(jnp.int32, p.shape, 0)
    p_prev = jnp.where(row == 0, last_ref[...], pltpu.roll(p, 1, axis=0))
    last_ref[...] = p[tt - 1:tt, :]
    ps = p + mu_ref[...] * (p_prev - p)
    w = RWKV_W
    r, k, v = ps[:, 0:w], ps[:, w:2 * w], ps[:, 2 * w:3 * w]
    w_lo, a_lo, g_lo = ps[:, 3 * w:3 * w + 64], ps[:, 3 * w + 64:3 * w + 128], ps[:, 3 * w + 128:3 * w + 256]
    ww = w0_ref[...] + _dot_hi_lo(jnp.tanh(w_lo), wup_ref[...])
    softplus_neg = jnp.maximum(-ww, 0.0) + jnp.log1p(jnp.exp(-jnp.abs(ww)))
    log_w = -jnp.exp(-softplus_neg - 0.5)
    a = _sigmoid(a0_ref[...] + _dot_hi_lo(a_lo, aup_ref[...]))
    g = _dot_hi_lo(_sigmoid(g_lo), gup_ref[...])
    head_sum = _head_block_ones(w, 1.0)
    kn = k * kk_ref[...]
    kn = kn * lax.rsqrt(_head_reduce(kn * kn, head_sum) + 1e-12)
    k2 = k * (1.0 + (a - 1.0) * ka_ref[...])
    bonus = _head_reduce(r * k2 * rk_ref[...], head_sum) * v
    r_o[0] = r
    lw_o[0] = log_w
    k_o[0] = k2
    v_o[0] = v
    kn_o[0] = kn
    b_o[0] = kn * a
    g_o[0] = g
    bonus_o[0] = bonus


def _rwkv_pre(p3, mu, w0, w_up, a0, a_up, g_up, k_k, k_a, r_k, tt=512):
    b, s, _ = p3.shape
    w = RWKV_W
    row = lambda x: x.reshape(1, -1)
    full = lambda shape: pl.BlockSpec(shape, lambda bi, t: (0,) * len(shape))
    out = jax.ShapeDtypeStruct((b, s, w), F32)
    return pl.pallas_call(
        functools.partial(_rwkv_pre_kernel, tt=tt),
        grid=(b, s // tt),
        in_specs=[pl.BlockSpec((1, tt, 1024), lambda bi, t: (bi, t, COL_RWKV // 1024)),
                  full((1, 1024)), full((1, w)), full((64, w)), full((1, w)), full((64, w)),
                  full((128, w)), full((1, w)), full((1, w)), full((1, w))],
        out_specs=[pl.BlockSpec((1, tt, w), lambda bi, t: (bi, t, 0))] * 8,
        out_shape=[out] * 8,
        scratch_shapes=[pltpu.VMEM((1, 1024), F32)],
        compiler_params=_cparams(("parallel", "arbitrary")),
        name="rwkv_pre",
    )(p3, row(mu), row(w0), w_up, row(a0), a_up, g_up, row(k_k), row(k_a), row(r_k))


def _bdot(a, b):
    return _dot(a.astype(BF16), b.astype(BF16))


def _bnt(a, b):
    return _nt(a.astype(BF16), b.astype(BF16))


def _btn(a, b):
    return _tn(a.astype(BF16), b.astype(BF16))


def _rwkv_chunk_kernel(r_ref, lw_ref, k_ref, v_ref, kn_ref, b_ref, y_ref, s_ref, *, cs, nch):
    c = pl.program_id(1)

    @pl.when(c == 0)
    def _():
        s_ref[...] = jnp.zeros_like(s_ref)

    rows = cs * min(nch, 4)
    big_row = lax.broadcasted_iota(jnp.int32, (rows, rows), 0)
    big_col = lax.broadcasted_iota(jnp.int32, (rows, rows), 1)
    same_chunk = (big_row >> CHUNK_SHIFT) == (big_col >> CHUNK_SHIFT)
    tri = jnp.where(same_chunk, jnp.where(big_col <= big_row, 1.0, 0.0), 0.0).astype(BF16)
    pw = 2 * HEAD_DIM
    row = lax.broadcasted_iota(jnp.int32, (cs, pw), 0)
    lane = lax.broadcasted_iota(jnp.int32, (cs, pw), 1)
    col = lane & (HEAD_DIM - 1)
    first_head = lane < HEAD_DIM
    incl = col <= row
    strict = col < row
    eye = jnp.where(col == row, 1.0, 0.0).astype(F32)

    def block_diag(x):
        return jnp.concatenate([jnp.where(first_head, x, 0.0), jnp.where(first_head, 0.0, x)], axis=0)

    def diag_blocks(full):
        return jnp.where(first_head, full[:cs], full[cs:])

    lw = lw_ref[0]
    cum = jnp.concatenate([_ones_dot(tri, lw[g:g + rows]) for g in range(0, cs * nch, rows)],
                          axis=0)
    e_incl = jnp.exp(cum)
    e_neg = jnp.exp(-cum)
    abar_all = -kn_ref[0] * jnp.exp(cum - lw)
    rbar_all = r_ref[0] * e_incl
    bt_all = b_ref[0] * e_neg
    kt_all = k_ref[0] * e_neg
    v_all = v_ref[0]
    n_double = max(cs.bit_length() - 2, 0)
    n_pairs = RWKV_HEADS // 2
    probs = [(j, p) for j in range(nch) for p in range(n_pairs)]
    cut = lambda t, jp: t[jp[0] * cs:(jp[0] + 1) * cs, jp[1] * pw:(jp[1] + 1) * pw]
    each = lambda fn: {jp: fn(jp) for jp in probs}
    abar, rbar = each(lambda jp: cut(abar_all, jp)), each(lambda jp: cut(rbar_all, jp))
    bt, kt, v = each(lambda jp: cut(bt_all, jp)), each(lambda jp: cut(kt_all, jp)), each(lambda jp: cut(v_all, jp))
    p_last = each(lambda jp: e_incl[(jp[0] + 1) * cs - 1:(jp[0] + 1) * cs, jp[1] * pw:(jp[1] + 1) * pw])
    ar = each(lambda jp: jnp.concatenate([abar[jp], rbar[jp]], axis=0))
    ar_b = each(lambda jp: _bnt(ar[jp], block_diag(bt[jp])))
    ar_k = each(lambda jp: _bnt(ar[jp], block_diag(kt[jp])))
    a_ab = each(lambda jp: jnp.where(strict, ar_b[jp][:cs], 0.0))
    a_ak = each(lambda jp: jnp.where(strict, ar_k[jp][:cs], 0.0))
    a_rb = each(lambda jp: jnp.where(incl, ar_b[jp][cs:], 0.0))
    a_rk = each(lambda jp: jnp.where(incl, ar_k[jp][cs:], 0.0))
    v_bd = each(lambda jp: block_diag(v[jp]))
    ak_v = each(lambda jp: _bdot(a_ak[jp], v_bd[jp]))
    inv = each(lambda jp: eye + a_ab[jp])
    power = a_ab
    power_bd = each(lambda jp: block_diag(power[jp]))
    for _ in range(n_double):
        power = each(lambda jp: _bdot(power[jp], power_bd[jp]))
        power_bd = each(lambda jp: block_diag(power[jp]))
        inv = each(lambda jp: inv[jp] + _bdot(inv[jp], power_bd[jp]))
    a_hat = each(lambda jp: _bdot(inv[jp], block_diag(abar[jp])))
    u_hat = each(lambda jp: _bdot(inv[jp], block_diag(ak_v[jp])))
    r_hat = each(lambda jp: rbar[jp] + _bdot(a_rb[jp], block_diag(a_hat[jp])))
    y_hat = each(lambda jp: _bdot(a_rb[jp], block_diag(u_hat[jp])) + _bdot(a_rk[jp], v_bd[jp]))
    g_mat = each(lambda jp: eye * p_last[jp] + diag_blocks(_btn(a_hat[jp], bt[jp] * p_last[jp])))
    h_mat = each(lambda jp: diag_blocks(_btn(jnp.concatenate([u_hat[jp], v[jp]], axis=0),
                                             jnp.concatenate([bt[jp], kt[jp]], axis=0) * p_last[jp])))
    states = [s_ref[p] for p in range(n_pairs)]
    for j in range(nch):
        for p in range(n_pairs):
            y_ref[0, j * cs:(j + 1) * cs, p * pw:(p + 1) * pw] = (
                _bnt(r_hat[j, p], block_diag(states[p])) + y_hat[j, p])
        states = [_dot_hi_lo(states[p], block_diag(g_mat[j, p])) + h_mat[j, p] for p in range(n_pairs)]
    for p in range(n_pairs):
        s_ref[p] = states[p]


def _rwkv_chunk(r, lw, k, v, kn, b_, cs=CHUNK, nch=8):
    b, s, w = r.shape
    spec = pl.BlockSpec((1, cs * nch, w), lambda bi, c: (bi, c, 0))
    return pl.pallas_call(
        functools.partial(_rwkv_chunk_kernel, cs=cs, nch=nch),
        grid=(b, s // (cs * nch)),
        in_specs=[spec] * 6,
        out_specs=spec,
        out_shape=jax.ShapeDtypeStruct((b, s, w), F32),
        scratch_shapes=[pltpu.VMEM((RWKV_HEADS // 2, HEAD_DIM, 2 * HEAD_DIM), F32)],
        compiler_params=_cparams(("parallel", "arbitrary")),
        name="rwkv_chunk",
    )(r, lw, k, v, kn, b_)


def _merge_kernel(x_ref, swa_ref, y_ref, g_ref, bonus_ref, fox_ref, dsa_ref, gng_ref, gnb_ref,
                  gbias_ref, wg_ref, wb_ref, wo_ref, lng_ref, lnb_ref, o_ref):
    y = y_ref[...]
    xb = x_ref[...].astype(BF16)
    head_mean = _head_block_ones(RWKV_W, 1.0 / HEAD_DIM)
    yc = y - _head_reduce(y, head_mean)
    yv = _head_reduce(yc * yc, head_mean)
    o_rwkv = (yc * lax.rsqrt(yv + RWKV_GN_EPS) * gng_ref[...] + gnb_ref[...] + bonus_ref[...]) * g_ref[...]
    branches = (swa_ref[...], o_rwkv, fox_ref[...], dsa_ref[...])
    merged = jnp.zeros(o_ref.shape, F32)
    off = 0
    for i, o in enumerate(branches):
        width = MIX_WIDTHS[i]
        proj = _dot(o.astype(BF16), wb_ref[off:off + width, :])
        gate = _sigmoid(_nt(xb, wg_ref[0, i * D_MODEL:(i + 1) * D_MODEL, :]) + gbias_ref[i:i + 1, :])
        merged = merged + gate * proj
        off += width
    y_out = _dot(merged.astype(BF16), wo_ref[...])
    o_ref[...] = _layer_norm(DN_ALPHA * x_ref[...] + y_out, lng_ref[...], lnb_ref[...])


def _merge(xf, o_swa, y, g, bonus, o_fox, o_dsa, gn_g, gn_b, gate_bias, w_gate_all, layer, w_branch, w_out,
           ln_g, ln_b, tm=512):
    n, d = xf.shape
    w_gate = w_gate_all
    row = lambda x: x.reshape(1, -1)
    tok = lambda wdt: pl.BlockSpec((tm, wdt), lambda i: (i, 0))
    full = lambda shape: pl.BlockSpec(shape, lambda i: (0,) * len(shape))
    return pl.pallas_call(
        _merge_kernel,
        grid=(n // tm,),
        in_specs=[tok(d), tok(512), tok(256), tok(256), tok(256), tok(256), tok(256),
                  full((1, 256)), full((1, 256)), full((N_BRANCHES, d)),
                  pl.BlockSpec((1, N_BRANCHES * d, d), lambda i: (layer, 0, 0)),
                  full((sum(MIX_WIDTHS), d)), full((d, d)), full((1, d)), full((1, d))],
        out_specs=tok(d),
        out_shape=jax.ShapeDtypeStruct((n, d), F32),
        compiler_params=_cparams(("parallel",)),
        name="merge",
    )(xf, o_swa, y, g, bonus, o_fox, o_dsa, row(gn_g), row(gn_b), gate_bias, w_gate, w_branch, w_out,
      row(ln_g), row(ln_b))


def _silu(x):
    return x * _sigmoid(x)


def _ffn_kernel(x_ref, w1_ref, w3_ref, w2_ref, lng_ref, lnb_ref, o_ref, *, tf):
    xb = x_ref[...].astype(BF16)
    y = None
    for f in range(0, w1_ref.shape[1], tf):
        hidden = _silu(_dot(xb, w1_ref[:, f:f + tf])) * _dot(xb, w3_ref[:, f:f + tf])
        part = _dot(hidden.astype(BF16), w2_ref[f:f + tf, :])
        y = part if y is None else y + part
    o_ref[...] = _layer_norm(DN_ALPHA * x_ref[...] + y, lng_ref[...], lnb_ref[...])


def _ffn(xf, w1, w3, w2, ln_g, ln_b, tm=512, tf=1408):
    n, d = xf.shape
    ff = w1.shape[1]
    row = lambda x: x.reshape(1, -1)
    tok = pl.BlockSpec((tm, d), lambda i: (i, 0))
    resident = lambda shape: pl.BlockSpec(shape, lambda i: (0, 0), pipeline_mode=pl.Buffered(1))
    return pl.pallas_call(
        functools.partial(_ffn_kernel, tf=tf),
        grid=(n // tm,),
        in_specs=[tok, resident((d, ff)), resident((d, ff)), resident((ff, d)),
                  resident((1, d)), resident((1, d))],
        out_specs=tok,
        out_shape=jax.ShapeDtypeStruct((n, d), F32),
        compiler_params=_cparams(("parallel",)),
        name="ffn",
    )(xf, w1, w3, w2, row(ln_g), row(ln_b))


def _moe_kernel(x_ref, rw_ref, rb_ref, w1_ref, w3_ref, w2_ref, lng_ref, lnb_ref, o_ref,
                comb_ref, rank_ref, cnt_ref, xb_ref, *, sb, sb_small):
    e = pl.program_id(1)
    tm = x_ref.shape[0]
    lane = lax.broadcasted_iota(jnp.int32, (tm, 128), 1).astype(F32)

    @pl.when(e == 0)
    def _():
        o_ref[...] = jnp.zeros_like(o_ref)
        xb_ref[...] = x_ref[...].astype(BF16)
        logits = _dot(x_ref[...], rw_ref[...], HIGHEST) + rb_ref[...]
        logits = jnp.where(lane < N_EXPERTS, logits, NEG_BIG)
        m1 = jnp.max(logits, axis=-1, keepdims=True)
        i1 = jnp.min(jnp.where(logits == m1, lane, 128.0), axis=-1, keepdims=True)
        rest = jnp.where(lane == i1, NEG_BIG, logits)
        m2 = jnp.max(rest, axis=-1, keepdims=True)
        i2 = jnp.min(jnp.where(rest == m2, lane, 128.0), axis=-1, keepdims=True)
        e2 = jnp.exp(m2 - m1)
        comb_ref[...] = jnp.where(lane == i1, 1.0 / (1.0 + e2), 0.0) + jnp.where(lane == i2, e2 / (1.0 + e2), 0.0)
        chosen = jnp.where(lane == i1, 1.0, 0.0) + jnp.where(lane == i2, 1.0, 0.0)
        r = lax.broadcasted_iota(jnp.int32, (tm, tm), 0)
        c = lax.broadcasted_iota(jnp.int32, (tm, tm), 1)
        earlier = jnp.where(c < r, 1.0, 0.0).astype(BF16)
        rank = jnp.where(chosen > 0.5, _dot(earlier, chosen.astype(BF16)), -1.0)
        lane8 = lax.broadcasted_iota(jnp.int32, (8, 128), 1)
        row8 = lax.broadcasted_iota(jnp.int32, (8, 128), 0)
        rank_ref[...] = _nt(jnp.where(lane8 == row8, 1.0, 0.0).astype(F32), rank, HIGHEST)
        cnt_ref[...] = jnp.sum(chosen, axis=0, keepdims=True)

    e_f = e.astype(F32)
    n_tok = jnp.sum(jnp.where(lane[0:1, :] == e_f, cnt_ref[...], 0.0)).astype(jnp.int32)
    rank_row = rank_ref[pl.ds(e, 1), :]
    weight = jnp.sum(jnp.where(lane == e_f, comb_ref[...], 0.0), axis=-1, keepdims=True)

    def block(first_rank, rows):
        slot = lax.broadcasted_iota(jnp.int32, (rows, tm), 0).astype(F32) + first_rank
        pick = jnp.where(rank_row == slot, 1.0, 0.0).astype(BF16)
        xs = _dot(pick, xb_ref[...]).astype(BF16)
        hidden = _silu(_dot(xs, w1_ref[0])) * _dot(xs, w3_ref[0])
        out = _dot(hidden.astype(BF16), w2_ref[0]).astype(BF16)
        o_ref[...] += _tn(pick, out) * weight

    @pl.when(n_tok <= sb_small)
    def _():
        block(0.0, sb_small)

    @pl.when(n_tok > sb_small)
    def _():
        def body(s, carry):
            block((s * sb).astype(F32), sb)
            return carry
        lax.fori_loop(0, lax.div(n_tok + sb - 1, sb), body, 0)

    @pl.when(e == pl.num_programs(1) - 1)
    def _():
        o_ref[...] = _layer_norm(DN_ALPHA * x_ref[...] + o_ref[...], lng_ref[...], lnb_ref[...])


def _moe(xf, router_w, router_b, w1, w3, w2, ln_g, ln_b, tm=1024, sb=320, sb_small=256):
    n, d = xf.shape
    ne, _, fe = w1.shape
    tm = min(tm, n)
    rw = jnp.pad(router_w, ((0, 0), (0, 128 - ne)))
    rb = jnp.pad(router_b, (0, 128 - ne)).reshape(1, 128)
    row = lambda x: x.reshape(1, -1)
    return pl.pallas_call(
        functools.partial(_moe_kernel, sb=sb, sb_small=sb_small),
        grid=(n // tm, ne),
        in_specs=[pl.BlockSpec((tm, d), lambda i, e: (i, 0)),
                  pl.BlockSpec((d, 128), lambda i, e: (0, 0)),
                  pl.BlockSpec((1, 128), lambda i, e: (0, 0)),
                  pl.BlockSpec((1, d, fe), lambda i, e: (e, 0, 0)),
                  pl.BlockSpec((1, d, fe), lambda i, e: (e, 0, 0)),
                  pl.BlockSpec((1, fe, d), lambda i, e: (e, 0, 0)),
                  pl.BlockSpec((1, d), lambda i, e: (0, 0)),
                  pl.BlockSpec((1, d), lambda i, e: (0, 0))],
        out_specs=pl.BlockSpec((tm, d), lambda i, e: (i, 0)),
        out_shape=jax.ShapeDtypeStruct((n, d), F32),
        scratch_shapes=[pltpu.VMEM((tm, 128), F32), pltpu.VMEM((8, tm), F32), pltpu.VMEM((1, 128), F32),
                        pltpu.VMEM((tm, d), BF16)],
        compiler_params=pltpu.CompilerParams(dimension_semantics=("parallel", "arbitrary"),
                                             vmem_limit_bytes=MOE_VMEM_LIMIT),
        name="moe",
    )(xf, rw, rb, w1, w3, w2, row(ln_g), row(ln_b))


def _mixer_layer(xf, bsz, seq, layer, w_in, w_gate_all, sinks, mu, w0, w_up, a0, a_up, g_up, k_k, k_a, r_k,
                 gn_g, gn_b, f_bias, gate_bias, w_branch, w_out, ln_g, ln_b):
    p2 = _matmul(xf, _layout_w_in(w_in[layer]).astype(BF16), tm=512)
    p3 = p2.reshape(bsz, seq, P_WIDTH)
    o_swa = _swa(p3, sinks)
    o_fox = _fox(p3, _fox_c(p3, f_bias))
    o_dsa = _dsa(p3)
    r, lw, k2, v, kn, b_, g, bonus = _rwkv_pre(p3, mu, w0, w_up, a0, a_up, g_up, k_k, k_a, r_k)
    y = _rwkv_chunk(r, lw, k2, v, kn, b_)
    n = bsz * seq
    flat = lambda t: t.reshape(n, t.shape[-1])
    return _merge(xf, flat(o_swa), flat(y), flat(g), flat(bonus), flat(o_fox), flat(o_dsa), gn_g, gn_b,
                  gate_bias, w_gate_all, layer, w_branch.astype(BF16), w_out.astype(BF16), ln_g, ln_b)


def kernel(x, w_in, swa_sinks, rwkv_mu, rwkv_w0, rwkv_w_up, rwkv_a0, rwkv_a_up, rwkv_g_up, rwkv_k_k, rwkv_k_a,
           rwkv_r_k, rwkv_gn_g, rwkv_gn_b, fox_f_bias, gate_bias, w_branch, w_out, ln_g, ln_b, ffn_w1, ffn_w3,
           ffn_w2, router_w, router_b, exp_w1, exp_w3, exp_w2):
    bsz, seq, d = x.shape
    xf = x.reshape(bsz * seq, d)
    w_gate_all = jnp.swapaxes(w_in[:, :, IN_OFF_GATE:], 1, 2).astype(BF16)
    for layer in range(DEPTH):
        xf = _mixer_layer(xf, bsz, seq, layer, w_in, w_gate_all, swa_sinks[layer], rwkv_mu[layer], rwkv_w0[layer],
                              rwkv_w_up[layer], rwkv_a0[layer], rwkv_a_up[layer], rwkv_g_up[layer],
                              rwkv_k_k[layer], rwkv_k_a[layer], rwkv_r_k[layer], rwkv_gn_g[layer],
                              rwkv_gn_b[layer], fox_f_bias[layer], gate_bias[layer], w_branch[layer],
                              w_out[layer], ln_g[layer, 0], ln_b[layer, 0])
        j = layer // 2
        if layer % 2 == 0:
            xf = _ffn(xf, ffn_w1[j].astype(BF16), ffn_w3[j].astype(BF16), ffn_w2[j].astype(BF16),
                      ln_g[layer, 1], ln_b[layer, 1])
        else:
            xf = _moe(xf, router_w[j], router_b[j], exp_w1[j].astype(BF16), exp_w3[j].astype(BF16),
                      exp_w2[j].astype(BF16), ln_g[layer, 1], ln_b[layer, 1])
    return xf.reshape(bsz, seq, d)
```

```python
import functools

import jax
import jax.numpy as jnp
from jax import lax
from jax.experimental import pallas as pl
from jax.experimental.pallas import tpu as pltpu

F32 = jnp.float32
BF16 = jnp.bfloat16
HIGHEST = lax.Precision.HIGHEST

D_MODEL = 1024
DEPTH = 2
CHUNK = 64
HEAD_DIM = 64
CHUNK_SHIFT = 6
HEAD_SHIFT = 6
SWA_HEADS = 8
SWA_KV_HEADS = 2
SWA_GROUP = SWA_HEADS // SWA_KV_HEADS
SWA_WINDOW = 128
SWA_WIN_CHUNKS = SWA_WINDOW // CHUNK
RWKV_HEADS = 4
RWKV_W = RWKV_HEADS * HEAD_DIM
RWKV_GN_EPS = 64e-5
FOX_HEADS = 4
DSA_HEADS = 4
IDX_HEADS = 4
IDX_DIM = 64
DSA_TOPK_MAX = 256
D_FF = 2816
N_EXPERTS = 8
D_FF_EXPERT = 1408
N_BRANCHES = 4
DN_ALPHA = (2 * DEPTH) ** 0.25
LN_EPS = 1e-5
ATTN_SCALE = HEAD_DIM ** -0.5
LOG2E = 1.4426950408889634
IDX_SCALE = IDX_DIM ** -0.5
IDX_W_SCALE = IDX_HEADS ** -0.5
MIX_WIDTHS = (SWA_HEADS * HEAD_DIM, RWKV_W, FOX_HEADS * HEAD_DIM, DSA_HEADS * HEAD_DIM)

COL_RWKV = 0
COL_SWA_Q = 1024
COL_FOX_KV = 1536
COL_DSA_QQ = 2048
COL_FOX_Q = 2560
COL_SWA_KV = 2816
COL_DSA_KV = 3072
COL_MISC = 3328
P_WIDTH = 3584

NEG_BIG = -1e30
INT_MIN = -(2 ** 31)
VMEM_LIMIT = 48 * 1024 * 1024
MOE_VMEM_LIMIT = 58 * 1024 * 1024


IN_OFF_SWA, IN_OFF_RWKV, IN_OFF_FOX, IN_OFF_DSA, IN_OFF_GATE = 0, 768, 1792, 2564, 3272
D_IN = IN_OFF_GATE + N_BRANCHES * D_MODEL


def _layout_w_in(w):
    d = w.shape[0]
    z = lambda n: jnp.zeros((d, n), w.dtype)
    swa, rwkv, fox, dsa = IN_OFF_SWA, IN_OFF_RWKV, IN_OFF_FOX, IN_OFF_DSA
    parts = [
        w[:, rwkv:rwkv + 1024],
        w[:, swa:swa + 512],
        w[:, fox + 256:fox + 768],
        w[:, dsa:dsa + 256], w[:, dsa + 384:dsa + 640],
        w[:, fox:fox + 256],
        w[:, swa + 512:swa + 768],
        w[:, dsa + 256:dsa + 384], w[:, dsa + 640:dsa + 704], z(64),
        w[:, fox + 768:fox + 772], w[:, dsa + 704:dsa + 708], z(120),
        z(128),
    ]
    return jnp.concatenate(parts, axis=1)


def _cparams(sem):
    return pltpu.CompilerParams(dimension_semantics=sem, vmem_limit_bytes=VMEM_LIMIT)


def _nt(a, b, precision=None):
    return lax.dot_general(a, b, (((1,), (1,)), ((), ())), precision=precision,
                           preferred_element_type=F32)


def _tn(a, b, precision=None):
    return lax.dot_general(a, b, (((0,), (0,)), ((), ())), precision=precision,
                           preferred_element_type=F32)


def _dot(a, b, precision=None):
    return jnp.dot(a, b, precision=precision, preferred_element_type=F32)


def _sigmoid(x):
    return 1.0 / (1.0 + jnp.exp(-x))


def _layer_norm(z, g, b):
    mu = jnp.mean(z, axis=-1, keepdims=True)
    zc = z - mu
    var = jnp.mean(zc * zc, axis=-1, keepdims=True)
    return zc * lax.rsqrt(var + LN_EPS) * g + b


def _skewed(n, first, second):
    out = []
    staged = first(0)
    for h in range(n):
        nxt = first(h + 1) if h + 1 < n else None
        out.append(second(h, staged))
        staged = nxt
    return out


def _paired_blocks(n, body, carry):
    carry = lax.fori_loop(0, lax.div(n, 2), lambda t, c: body((2 * t, 2 * t + 1), c), carry)
    return lax.cond(lax.rem(n, 2) == 1, lambda c: body((n - 1,), c), lambda c: c, carry)


def _head_block_ones(n, scale):
    r = lax.broadcasted_iota(jnp.int32, (n, n), 0) >> HEAD_SHIFT
    c = lax.broadcasted_iota(jnp.int32, (n, n), 1) >> HEAD_SHIFT
    return jnp.where(r == c, scale, 0.0).astype(BF16)


def _pieces(x):
    hi = x.astype(BF16)
    rest = x - hi.astype(F32)
    mid = rest.astype(BF16)
    return hi, mid, (rest - mid.astype(F32)).astype(BF16)


def _head_reduce(x, head_ones):
    hi, mid, low = _pieces(x)
    return _dot(hi, head_ones) + _dot(mid, head_ones) + _dot(low, head_ones)


def _ones_dot(ones, x):
    hi, mid, low = _pieces(x)
    return _dot(ones, hi) + _dot(ones, mid) + _dot(ones, low)


def _dot_hi_lo(a, b):
    a_hi = a.astype(BF16)
    a_lo = (a - a_hi.astype(F32)).astype(BF16)
    b_hi = b.astype(BF16)
    b_lo = (b - b_hi.astype(F32)).astype(BF16)
    return _dot(a_hi, b_hi) + _dot(a_hi, b_lo) + _dot(a_lo, b_hi)


def _mm_kernel(a_ref, b_ref, o_ref):
    o_ref[...] = _dot(a_ref[...].astype(BF16), b_ref[...])


def _matmul(a, b, tm):
    m, k = a.shape
    n = b.shape[1]
    tm = min(tm, m)
    return pl.pallas_call(
        _mm_kernel,
        grid=(m // tm,),
        in_specs=[pl.BlockSpec((tm, k), lambda i: (i, 0)),
                  pl.BlockSpec((k, n), lambda i: (0, 0), pipeline_mode=pl.Buffered(1))],
        out_specs=pl.BlockSpec((tm, n), lambda i: (i, 0)),
        out_shape=jax.ShapeDtypeStruct((m, n), F32),
        compiler_params=_cparams(("parallel",)),
        name="in_proj",
    )(a, b)


def _swa_kernel(sink_ref, q_ref, kv_ref, o_ref, *, tq):
    i = pl.program_id(1)
    win = tq + SWA_WINDOW
    s0 = pl.multiple_of(jnp.maximum(i * tq - SWA_WINDOW, 0), SWA_WINDOW)
    k_chunk = (s0 + lax.broadcasted_iota(jnp.int32, (win, tq), 0)) >> CHUNK_SHIFT
    q_chunk = (i * tq + lax.broadcasted_iota(jnp.int32, (win, tq), 1)) >> CHUNK_SHIFT
    bias = jnp.where(k_chunk <= q_chunk,
                     jnp.where(k_chunk >= q_chunk - SWA_WIN_CHUNKS, 0.0, NEG_BIG), NEG_BIG)
    ks = [kv_ref[0, pl.ds(s0, win), hk * HEAD_DIM:(hk + 1) * HEAD_DIM].astype(BF16)
          for hk in range(SWA_KV_HEADS)]
    kv_w = SWA_KV_HEADS * HEAD_DIM
    vs = [kv_ref[0, pl.ds(s0, win), kv_w + hk * HEAD_DIM:kv_w + (hk + 1) * HEAD_DIM].astype(BF16)
          for hk in range(SWA_KV_HEADS)]

    def logits(h):
        q = (q_ref[0, :, h * HEAD_DIM:(h + 1) * HEAD_DIM] * (ATTN_SCALE * LOG2E)).astype(BF16)
        return _nt(ks[h // SWA_GROUP], q) + bias

    def finish(h, s):
        sink = sink_ref[h] * LOG2E
        m = jnp.maximum(jnp.max(s, axis=0, keepdims=True), sink)
        e = jnp.exp2(s - m)
        denom = jnp.sum(e, axis=0, keepdims=True) + jnp.exp2(sink - m)
        return _tn(vs[h // SWA_GROUP], e.astype(BF16)) / denom

    o_ref[0] = jnp.concatenate(_skewed(SWA_HEADS, logits, finish), axis=0).T


def _swa(p3, sinks, tq=512):
    b, s, _ = p3.shape
    tq = min(tq, s // 2)
    kv_w = 2 * SWA_KV_HEADS * HEAD_DIM
    return pl.pallas_call(
        functools.partial(_swa_kernel, tq=tq),
        grid=(b, s // tq),
        in_specs=[pl.BlockSpec(memory_space=pltpu.SMEM),
                  pl.BlockSpec((1, tq, 512), lambda bi, i: (bi, i, COL_SWA_Q // 512)),
                  pl.BlockSpec((1, s, kv_w), lambda bi, i: (bi, 0, COL_SWA_KV // kv_w))],
        out_specs=pl.BlockSpec((1, tq, 512), lambda bi, i: (bi, i, 0)),
        out_shape=jax.ShapeDtypeStruct((b, s, 512), F32),
        compiler_params=_cparams(("parallel", "parallel")),
        name="swa",
    )(sinks, p3, p3)


def _fox_c_kernel(f_ref, bias_ref, c_ref, *, blk):
    s = f_ref.shape[1]
    r = lax.broadcasted_iota(jnp.int32, (blk, blk), 0)
    c = lax.broadcasted_iota(jnp.int32, (blk, blk), 1)
    tri = jnp.where(c <= r, 1.0, 0.0).astype(BF16)
    carry = jnp.zeros((1, 128), F32)
    for j in range(s // blk):
        x = f_ref[0, j * blk:(j + 1) * blk, :] + bias_ref[...]
        log_f = jnp.minimum(x, 0.0) - jnp.log1p(jnp.exp(-jnp.abs(x)))
        cs = _ones_dot(tri, log_f) + carry
        carry = cs[blk - 1:blk, :]
        for h in range(FOX_HEADS):
            c_ref[0, h, j * blk:(j + 1) * blk, :] = jnp.broadcast_to(cs[:, h:h + 1] * LOG2E, (blk, 128))


def _fox_c(p3, f_bias):
    b, s, _ = p3.shape
    bias_row = jnp.pad(f_bias, (0, 128 - FOX_HEADS)).reshape(1, 128)
    return pl.pallas_call(
        functools.partial(_fox_c_kernel, blk=256),
        grid=(b,),
        in_specs=[pl.BlockSpec((1, s, 128), lambda bi: (bi, 0, COL_MISC // 128)),
                  pl.BlockSpec((1, 128), lambda bi: (0, 0))],
        out_specs=pl.BlockSpec((1, FOX_HEADS, s, 128), lambda bi: (bi, 0, 0, 0)),
        out_shape=jax.ShapeDtypeStruct((b, FOX_HEADS, s, 128), F32),
        compiler_params=_cparams(("parallel",)),
        name="fox_cumsum",
    )(p3, bias_row)


def _fox_kernel(q_ref, k_ref, c_ref, o_ref, lg_ref, acc_ref, *, tq):
    i = pl.program_id(1)
    tk = tq
    v_col = FOX_HEADS * HEAD_DIM
    key_i = lax.broadcasted_iota(jnp.int32, (tk, tq), 0)
    qry_i = lax.broadcasted_iota(jnp.int32, (tk, tq), 1)
    diag_bias = jnp.where(key_i <= qry_i, 0.0, NEG_BIG)
    q = [(q_ref[0, :, h * HEAD_DIM:(h + 1) * HEAD_DIM] * (ATTN_SCALE * LOG2E)).astype(BF16)
         for h in range(FOX_HEADS)]

    def logits_blocks(blocks, maxes, masked=False):
        starts = [pl.multiple_of(j * tk, tk) for j in blocks]
        dots = [[_nt(k_ref[0, pl.ds(ks, tk), h * HEAD_DIM:(h + 1) * HEAD_DIM].astype(BF16), q[h])
                 for h in range(FOX_HEADS)] for ks in starts]
        maxes = list(maxes)
        for b, ks in enumerate(starts):
            for h in range(FOX_HEADS):
                c_k = c_ref[0, h, pl.ds(ks, tk), :]
                s = dots[b][h] - jnp.concatenate([c_k] * (tq // 128), axis=1)
                if masked:
                    s = s + diag_bias
                lg_ref[h, pl.ds(ks, tk), :] = s
                maxes[h] = jnp.maximum(maxes[h], jnp.max(s, axis=0, keepdims=True))
        return tuple(maxes)

    maxes = tuple(jnp.full((1, tq), NEG_BIG, F32) for _ in range(FOX_HEADS))
    maxes = _paired_blocks(i, logits_blocks, maxes)
    maxes = logits_blocks((i,), maxes, masked=True)
    acc_ref[...] = jnp.zeros_like(acc_ref)

    def attend_blocks(blocks, sums):
        starts = [pl.multiple_of(j * tk, tk) for j in blocks]
        sums = list(sums)
        for h in range(FOX_HEADS):
            update = None
            for ks in starts:
                p = jnp.exp2(lg_ref[h, pl.ds(ks, tk), :] - maxes[h])
                v = k_ref[0, pl.ds(ks, tk), v_col + h * HEAD_DIM:v_col + (h + 1) * HEAD_DIM].astype(BF16)
                pv = _tn(v, p.astype(BF16))
                update = pv if update is None else update + pv
                sums[h] = sums[h] + jnp.sum(p, axis=0, keepdims=True)
            acc_ref[h] += update
        return tuple(sums)

    sums = _paired_blocks(i + 1, attend_blocks, tuple(jnp.zeros((1, tq), F32) for _ in range(FOX_HEADS)))
    out_t = jnp.concatenate([acc_ref[h] / sums[h] for h in range(FOX_HEADS)], axis=0)
    o_ref[0] = out_t.T


def _fox(p3, c, tq=256):
    b, s, _ = p3.shape
    w = FOX_HEADS * HEAD_DIM
    return pl.pallas_call(
        functools.partial(_fox_kernel, tq=tq),
        grid=(b, s // tq),
        in_specs=[pl.BlockSpec((1, tq, w), lambda bi, i: (bi, i, COL_FOX_Q // w)),
                  pl.BlockSpec((1, s, 2 * w), lambda bi, i: (bi, 0, COL_FOX_KV // (2 * w))),
                  pl.BlockSpec((1, FOX_HEADS, s, 128), lambda bi, i: (bi, 0, 0, 0))],
        out_specs=pl.BlockSpec((1, tq, w), lambda bi, i: (bi, i, 0)),
        out_shape=jax.ShapeDtypeStruct((b, s, w), F32),
        scratch_shapes=[pltpu.VMEM((FOX_HEADS, s, tq), F32), pltpu.VMEM((FOX_HEADS, HEAD_DIM, tq), F32)],
        compiler_params=_cparams(("parallel", "parallel")),
        name="fox",
    )(p3, p3, c)


def _key_to_score(key):
    return pltpu.bitcast(key ^ ((key >> 31) & 0x7FFFFFFF), F32)


def _bit_planes(score):
    bits = pltpu.bitcast(score, jnp.int32)
    image = bits ^ ((bits >> 31) | INT_MIN)
    rows = image.reshape(32, 8, image.shape[1])
    v = [rows[j] for j in range(32)]
    for d, mask in ((16, 0x0000FFFF), (8, 0x00FF00FF), (4, 0x0F0F0F0F), (2, 0x33333333), (1, 0x55555555)):
        for k in range(32):
            if k & d == 0:
                t = (v[k] ^ lax.shift_right_logical(v[k + d], d)) & mask
                v[k] = v[k] ^ t
                v[k + d] = v[k + d] ^ lax.shift_left(t, d)
    return v


def _dsa_kernel(qq_ref, misc_ref, kv_ref, o_ref, sc_ref, lg_ref, acc_ref, pl_ref, *, tq, kb, topk):
    i = pl.program_id(1)
    n_kb = lax.div((i + 1) * tq + kb - 1, kb)
    lane8 = lax.broadcasted_iota(jnp.int32, (8, 128), 1)
    row8 = lax.broadcasted_iota(jnp.int32, (8, 128), 0)
    pick = jnp.where((lane8 == row8 + 4) & (row8 < IDX_HEADS), 1.0, 0.0).astype(F32)
    w_t = _nt(pick, misc_ref[0], HIGHEST) * (IDX_W_SCALE * IDX_SCALE)
    q_chunk = (i * tq + lax.broadcasted_iota(jnp.int32, (1, tq), 1)) >> CHUNK_SHIFT
    qw = DSA_HEADS * HEAD_DIM
    q_idx = [qq_ref[0, :, qw + h * IDX_DIM:qw + (h + 1) * IDX_DIM].astype(BF16) for h in range(IDX_HEADS)]

    def score_blocks(blocks, carry):
        starts = [pl.multiple_of(j * kb, kb) for j in blocks]
        dots = []
        for ks in starts:
            k_idx = kv_ref[0, pl.ds(ks, kb), 2 * HEAD_DIM:2 * HEAD_DIM + IDX_DIM].astype(BF16)
            dots.append([_nt(k_idx, q_idx[h]) for h in range(IDX_HEADS)])
        for b, ks in enumerate(starts):
            score = jnp.maximum(dots[b][0], 0.0) * w_t[0:1, :]
            for h in range(1, IDX_HEADS):
                score = score + jnp.maximum(dots[b][h], 0.0) * w_t[h:h + 1, :]
            k_chunk = (ks + lax.broadcasted_iota(jnp.int32, (kb, tq), 0)) >> CHUNK_SHIFT
            score = jnp.where(k_chunk <= q_chunk, score, -jnp.inf)
            sc_ref[pl.ds(ks, kb), :] = score
            planes = _bit_planes(score)
            for t in range(32):
                pl_ref[blocks[b], t] = planes[t]
        return carry

    _paired_blocks(n_kb, score_blocks, 0)
    n_blocks = pl_ref.shape[0]

    def clear_planes(j, carry):
        pl_ref[j] = jnp.zeros(pl_ref.shape[1:], jnp.int32)
        return carry

    lax.fori_loop(n_kb, n_blocks, clear_planes, 0)

    def count(pred):
        def body(j, acc):
            blk = sc_ref[pl.ds(pl.multiple_of(j * kb, kb), kb), :]
            hit = jnp.where(pred(blk), 1, 0).astype(jnp.int32).reshape(kb // 8, 8, tq)
            parts = [hit[g] for g in range(kb // 8)]
            while len(parts) > 1:
                parts = [parts[g] + parts[g + 1] for g in range(0, len(parts), 2)]
            return acc + parts[0]
        acc = lax.fori_loop(0, n_kb, body, jnp.zeros((8, tq), jnp.int32))
        return jnp.sum(acc.astype(F32), axis=0, keepdims=True)

    def search_by_compares():
        zero = jnp.zeros((1, tq), F32)
        thr_key = jnp.where(count(lambda blk: blk >= zero) >= topk, 0, INT_MIN).astype(jnp.int32)

        def bit_step(bi, thr_key):
            cand = thr_key | lax.shift_left(jnp.int32(1), 30 - bi)
            cand_score = _key_to_score(cand)
            return jnp.where(count(lambda blk: blk >= cand_score) >= topk, cand, thr_key)

        thr_key = lax.fori_loop(0, 31, bit_step, thr_key)
        t = jnp.where(thr_key == INT_MIN, -jnp.inf, _key_to_score(thr_key))
        return t, count(lambda blk: blk > t)

    def search_by_planes():
        def bit_pass(t, carry):
            prefix, above, alive = carry
            ones = [alive[j] & pl_ref[j, t] for j in range(n_blocks)]
            parts = [lax.population_count(o) for o in ones]
            while len(parts) > 1:
                parts = [parts[g] + parts[g + 1] for g in range(0, len(parts), 2)]
            reach = above + jnp.sum(parts[0].astype(F32), axis=0, keepdims=True)
            take = reach >= topk
            prefix = jnp.where(take, prefix | lax.shift_left(jnp.int32(1), 31 - t), prefix)
            alive = tuple(jnp.where(take, ones[j], alive[j] ^ ones[j]) for j in range(n_blocks))
            return prefix, jnp.where(take, above, reach), alive

        init = (jnp.zeros((1, tq), jnp.int32), jnp.zeros((1, tq), F32),
                tuple(jnp.full((8, tq), -1, jnp.int32) for _ in range(n_blocks)))
        image, _, _ = lax.fori_loop(0, 32, bit_pass, init)
        return pltpu.bitcast(image ^ (~(image >> 31) | INT_MIN), F32)

    thr_planes = search_by_planes()
    n_above = count(lambda blk: blk > thr_planes)
    n_reach = count(lambda blk: blk >= thr_planes)
    confirmed = jnp.min(jnp.where(n_above < topk, jnp.where(n_reach >= topk, 1.0, 0.0), 0.0)) > 0.5
    thr, n_above = lax.cond(confirmed, lambda: (thr_planes, n_above), search_by_compares)
    need = topk - n_above

    r = lax.broadcasted_iota(jnp.int32, (kb, kb), 0)
    c = lax.broadcasted_iota(jnp.int32, (kb, kb), 1)
    earlier = jnp.where(c < r, 1.0, 0.0).astype(BF16)
    q_att = [(qq_ref[0, :, h * HEAD_DIM:(h + 1) * HEAD_DIM] * (ATTN_SCALE * LOG2E)).astype(BF16)
             for h in range(DSA_HEADS)]

    def logits_blocks(blocks, carry):
        ties_before, maxes = carry
        maxes = list(maxes)
        starts = [pl.multiple_of(j * kb, kb) for j in blocks]
        dots, ties = [], []
        for ks in starts:
            k = kv_ref[0, pl.ds(ks, kb), 0:HEAD_DIM].astype(BF16)
            dots.append([_nt(k, q_att[h]) for h in range(DSA_HEADS)])
            k_chunk = (ks + lax.broadcasted_iota(jnp.int32, (kb, tq), 0)) >> CHUNK_SHIFT
            ties.append(jnp.where(sc_ref[pl.ds(ks, kb), :] == thr, jnp.where(k_chunk <= q_chunk, 1.0, 0.0), 0.0))
        ranks = [_dot(earlier, tie.astype(BF16)) for tie in ties]
        for b, ks in enumerate(starts):
            tie_taken = jnp.where(ranks[b] + ties_before < need, ties[b], 0.0)
            bias = jnp.where(sc_ref[pl.ds(ks, kb), :] > thr, 0.0, jnp.where(tie_taken > 0.5, 0.0, NEG_BIG))
            ties_before = ties_before + jnp.sum(ties[b], axis=0, keepdims=True)
            for h in range(DSA_HEADS):
                s = dots[b][h] + bias
                lg_ref[h, pl.ds(ks, kb), :] = s
                maxes[h] = jnp.maximum(maxes[h], jnp.max(s, axis=0, keepdims=True))
        return ties_before, tuple(maxes)

    init = (jnp.zeros((1, tq), F32), tuple(jnp.full((1, tq), NEG_BIG, F32) for _ in range(DSA_HEADS)))
    _, maxes = _paired_blocks(n_kb, logits_blocks, init)

    acc_ref[...] = jnp.zeros_like(acc_ref)

    def attend_blocks(blocks, sums):
        starts = [pl.multiple_of(j * kb, kb) for j in blocks]
        vs = [kv_ref[0, pl.ds(ks, kb), HEAD_DIM:2 * HEAD_DIM].astype(BF16) for ks in starts]
        sums = list(sums)
        for h in range(DSA_HEADS):
            update = None
            for b, ks in enumerate(starts):
                p = jnp.exp2(lg_ref[h, pl.ds(ks, kb), :] - maxes[h])
                pv = _tn(vs[b], p.astype(BF16))
                update = pv if update is None else update + pv
                sums[h] = sums[h] + jnp.sum(p, axis=0, keepdims=True)
            acc_ref[h] += update
        return tuple(sums)

    sums = _paired_blocks(n_kb, attend_blocks, tuple(jnp.zeros((1, tq), F32) for _ in range(DSA_HEADS)))
    out_t = jnp.concatenate([acc_ref[h] / sums[h] for h in range(DSA_HEADS)], axis=0)
    o_ref[0] = out_t.T


def _dsa(p3, tq=256, kb=256):
    b, s, _ = p3.shape
    assert kb == 8 * 32, "a key block is the 32 row groups of one bit-plane word"
    topk = min(DSA_TOPK_MAX, s // 4)
    return pl.pallas_call(
        functools.partial(_dsa_kernel, tq=tq, kb=kb, topk=topk),
        grid=(b, s // tq),
        in_specs=[pl.BlockSpec((1, tq, 512), lambda bi, i: (bi, i, COL_DSA_QQ // 512)),
                  pl.BlockSpec((1, tq, 128), lambda bi, i: (bi, i, COL_MISC // 128)),
                  pl.BlockSpec((1, s, 256), lambda bi, i: (bi, 0, COL_DSA_KV // 256))],
        out_specs=pl.BlockSpec((1, tq, 256), lambda bi, i: (bi, i, 0)),
        out_shape=jax.ShapeDtypeStruct((b, s, 256), F32),
        scratch_shapes=[pltpu.VMEM((s, tq), F32), pltpu.VMEM((DSA_HEADS, s, tq), F32),
                        pltpu.VMEM((DSA_HEADS, HEAD_DIM, tq), F32), pltpu.VMEM((s // kb, 32, 8, tq), jnp.int32)],
        compiler_params=_cparams(("parallel", "parallel")),
        name="dsa",
    )(p3, p3, p3)


def _rwkv_pre_kernel(p_ref, mu_ref, w0_ref, wup_ref, a0_ref, aup_ref, gup_ref, kk_ref, ka_ref, rk_ref,
                     r_o, lw_o, k_o, v_o, kn_o, b_o, g_o, bonus_o, last_ref, *, tt):
    t = pl.program_id(1)

    @pl.when(t == 0)
    def _():
        last_ref[...] = jnp.zeros_like(last_ref)

    p = p_ref[0]
    row = lax.broadcasted_iota(jnp.int32, p.shape, 0)
    p_prev = jnp.where(row == 0, last_ref[...], pltpu.roll(p, 1, axis=0))
    last_ref[...] = p[tt - 1:tt, :]
    ps = p + mu_ref[...] * (p_prev - p)
    w = RWKV_W
    r, k, v = ps[:, 0:w], ps[:, w:2 * w], ps[:, 2 * w:3 * w]
    w_lo, a_lo, g_lo = ps[:, 3 * w:3 * w + 64], ps[:, 3 * w + 64:3 * w + 128], ps[:, 3 * w + 128:3 * w + 256]
    ww = w0_ref[...] + _dot_hi_lo(jnp.tanh(w_lo), wup_ref[...])
    softplus_neg = jnp.maximum(-ww, 0.0) + jnp.log1p(jnp.exp(-jnp.abs(ww)))
    log_w = -jnp.exp(-softplus_neg - 0.5)
    a = _sigmoid(a0_ref[...] + _dot_hi_lo(a_lo, aup_ref[...]))
    g = _dot_hi_lo(_sigmoid(g_lo), gup_ref[...])
    head_sum = _head_block_ones(w, 1.0)
    kn = k * kk_ref[...]
    kn = kn * lax.rsqrt(_head_reduce(kn * kn, head_sum) + 1e-12)
    k2 = k * (1.0 + (a - 1.0) * ka_ref[...])
    bonus = _head_reduce(r * k2 * rk_ref[...], head_sum) * v
    r_o[0] = r
    lw_o[0] = log_w
    k_o[0] = k2
    v_o[0] = v
    kn_o[0] = kn
    b_o[0] = kn * a
    g_o[0] = g
    bonus_o[0] = bonus


def _rwkv_pre(p3, mu, w0, w_up, a0, a_up, g_up, k_k, k_a, r_k, tt=512):
    b, s, _ = p3.shape
    w = RWKV_W
    row = lambda x: x.reshape(1, -1)
    full = lambda shape: pl.BlockSpec(shape, lambda bi, t: (0,) * len(shape))
    out = jax.ShapeDtypeStruct((b, s, w), F32)
    return pl.pallas_call(
        functools.partial(_rwkv_pre_kernel, tt=tt),
        grid=(b, s // tt),
        in_specs=[pl.BlockSpec((1, tt, 1024), lambda bi, t: (bi, t, COL_RWKV // 1024)),
                  full((1, 1024)), full((1, w)), full((64, w)), full((1, w)), full((64, w)),
                  full((128, w)), full((1, w)), full((1, w)), full((1, w))],
        out_specs=[pl.BlockSpec((1, tt, w), lambda bi, t: (bi, t, 0))] * 8,
        out_shape=[out] * 8,
        scratch_shapes=[pltpu.VMEM((1, 1024), F32)],
        compiler_params=_cparams(("parallel", "arbitrary")),
        name="rwkv_pre",
    )(p3, row(mu), row(w0), w_up, row(a0), a_up, g_up, row(k_k), row(k_a), row(r_k))


def _bdot(a, b):
    return _dot(a.astype(BF16), b.astype(BF16))


def _bnt(a, b):
    return _nt(a.astype(BF16), b.astype(BF16))


def _btn(a, b):
    return _tn(a.astype(BF16), b.astype(BF16))


def _rwkv_chunk_kernel(r_ref, lw_ref, k_ref, v_ref, kn_ref, b_ref, y_ref, s_ref, *, cs, nch):
    c = pl.program_id(1)

    @pl.when(c == 0)
    def _():
        s_ref[...] = jnp.zeros_like(s_ref)

    rows = cs * min(nch, 4)
    big_row = lax.broadcasted_iota(jnp.int32, (rows, rows), 0)
    big_col = lax.broadcasted_iota(jnp.int32, (rows, rows), 1)
    same_chunk = (big_row >> CHUNK_SHIFT) == (big_col >> CHUNK_SHIFT)
    tri = jnp.where(same_chunk, jnp.where(big_col <= big_row, 1.0, 0.0), 0.0).astype(BF16)
    pw = 2 * HEAD_DIM
    row = lax.broadcasted_iota(jnp.int32, (cs, pw), 0)
    lane = lax.broadcasted_iota(jnp.int32, (cs, pw), 1)
    col = lane & (HEAD_DIM - 1)
    first_head = lane < HEAD_DIM
    incl = col <= row
    strict = col < row
    eye = jnp.where(col == row, 1.0, 0.0).astype(F32)

    def block_diag(x):
        return jnp.concatenate([jnp.where(first_head, x, 0.0), jnp.where(first_head, 0.0, x)], axis=0)

    def diag_blocks(full):
        return jnp.where(first_head, full[:cs], full[cs:])

    lw = lw_ref[0]
    cum = jnp.concatenate([_ones_dot(tri, lw[g:g + rows]) for g in range(0, cs * nch, rows)],
                          axis=0)
    e_incl = jnp.exp(cum)
    e_neg = jnp.exp(-cum)
    abar_all = -kn_ref[0] * jnp.exp(cum - lw)
    rbar_all = r_ref[0] * e_incl
    bt_all = b_ref[0] * e_neg
    kt_all = k_ref[0] * e_neg
    v_all = v_ref[0]
    n_double = max(cs.bit_length() - 2, 0)
    n_pairs = RWKV_HEADS // 2
    probs = [(j, p) for j in range(nch) for p in range(n_pairs)]
    cut = lambda t, jp: t[jp[0] * cs:(jp[0] + 1) * cs, jp[1] * pw:(jp[1] + 1) * pw]
    each = lambda fn: {jp: fn(jp) for jp in probs}
    abar, rbar = each(lambda jp: cut(abar_all, jp)), each(lambda jp: cut(rbar_all, jp))
    bt, kt, v = each(lambda jp: cut(bt_all, jp)), each(lambda jp: cut(kt_all, jp)), each(lambda jp: cut(v_all, jp))
    p_last = each(lambda jp: e_incl[(jp[0] + 1) * cs - 1:(jp[0] + 1) * cs, jp[1] * pw:(jp[1] + 1) * pw])
    ar = each(lambda jp: jnp.concatenate([abar[jp], rbar[jp]], axis=0))
    ar_b = each(lambda jp: _bnt(ar[jp], block_diag(bt[jp])))
    ar_k = each(lambda jp: _bnt(ar[jp], block_diag(kt[jp])))
    a_ab = each(lambda jp: jnp.where(strict, ar_b[jp][:cs], 0.0))
    a_ak = each(lambda jp: jnp.where(strict, ar_k[jp][:cs], 0.0))
    a_rb = each(lambda jp: jnp.where(incl, ar_b[jp][cs:], 0.0))
    a_rk = each(lambda jp: jnp.where(incl, ar_k[jp][cs:], 0.0))
    v_bd = each(lambda jp: block_diag(v[jp]))
    ak_v = each(lambda jp: _bdot(a_ak[jp], v_bd[jp]))
    inv = each(lambda jp: eye + a_ab[jp])
    power = a_ab
    power_bd = each(lambda jp: block_diag(power[jp]))
    for _ in range(n_double):
        power = each(lambda jp: _bdot(power[jp], power_bd[jp]))
        power_bd = each(lambda jp: block_diag(power[jp]))
        inv = each(lambda jp: inv[jp] + _bdot(inv[jp], power_bd[jp]))
    a_hat = each(lambda jp: _bdot(inv[jp], block_diag(abar[jp])))
    u_hat = each(lambda jp: _bdot(inv[jp], block_diag(ak_v[jp])))
    r_hat = each(lambda jp: rbar[jp] + _bdot(a_rb[jp], block_diag(a_hat[jp])))
    y_hat = each(lambda jp: _bdot(a_rb[jp], block_diag(u_hat[jp])) + _bdot(a_rk[jp], v_bd[jp]))
    g_mat = each(lambda jp: eye * p_last[jp] + diag_blocks(_btn(a_hat[jp], bt[jp] * p_last[jp])))
    h_mat = each(lambda jp: diag_blocks(_btn(jnp.concatenate([u_hat[jp], v[jp]], axis=0),
                                             jnp.concatenate([bt[jp], kt[jp]], axis=0) * p_last[jp])))
    states = [s_ref[p] for p in range(n_pairs)]
    for j in range(nch):
        for p in range(n_pairs):
            y_ref[0, j * cs:(j + 1) * cs, p * pw:(p + 1) * pw] = (
                _bnt(r_hat[j, p], block_diag(states[p])) + y_hat[j, p])
        states = [_dot_hi_lo(states[p], block_diag(g_mat[j, p])) + h_mat[j, p] for p in range(n_pairs)]
    for p in range(n_pairs):
        s_ref[p] = states[p]


def _rwkv_chunk(r, lw, k, v, kn, b_, cs=CHUNK, nch=8):
    b, s, w = r.shape
    spec = pl.BlockSpec((1, cs * nch, w), lambda bi, c: (bi, c, 0))
    return pl.pallas_call(
        functools.partial(_rwkv_chunk_kernel, cs=cs, nch=nch),
        grid=(b, s // (cs * nch)),
        in_specs=[spec] * 6,
        out_specs=spec,
        out_shape=jax.ShapeDtypeStruct((b, s, w), F32),
        scratch_shapes=[pltpu.VMEM((RWKV_HEADS // 2, HEAD_DIM, 2 * HEAD_DIM), F32)],
        compiler_params=_cparams(("parallel", "arbitrary")),
        name="rwkv_chunk",
    )(r, lw, k, v, kn, b_)


def _merge_kernel(x_ref, swa_ref, y_ref, g_ref, bonus_ref, fox_ref, dsa_ref, gng_ref, gnb_ref,
                  gbias_ref, wg_ref, wb_ref, wo_ref, lng_ref, lnb_ref, o_ref):
    y = y_ref[...]
    xb = x_ref[...].astype(BF16)
    head_mean = _head_block_ones(RWKV_W, 1.0 / HEAD_DIM)
    yc = y - _head_reduce(y, head_mean)
    yv = _head_reduce(yc * yc, head_mean)
    o_rwkv = (yc * lax.rsqrt(yv + RWKV_GN_EPS) * gng_ref[...] + gnb_ref[...] + bonus_ref[...]) * g_ref[...]
    branches = (swa_ref[...], o_rwkv, fox_ref[...], dsa_ref[...])
    merged = jnp.zeros(o_ref.shape, F32)
    off = 0
    for i, o in enumerate(branches):
        width = MIX_WIDTHS[i]
        proj = _dot(o.astype(BF16), wb_ref[off:off + width, :])
        gate = _sigmoid(_nt(xb, wg_ref[0, i * D_MODEL:(i + 1) * D_MODEL, :]) + gbias_ref[i:i + 1, :])
        merged = merged + gate * proj
        off += width
    y_out = _dot(merged.astype(BF16), wo_ref[...])
    o_ref[...] = _layer_norm(DN_ALPHA * x_ref[...] + y_out, lng_ref[...], lnb_ref[...])


def _merge(xf, o_swa, y, g, bonus, o_fox, o_dsa, gn_g, gn_b, gate_bias, w_gate_all, layer, w_branch, w_out,
           ln_g, ln_b, tm=512):
    n, d = xf.shape
    w_gate = w_gate_all
    row = lambda x: x.reshape(1, -1)
    tok = lambda wdt: pl.BlockSpec((tm, wdt), lambda i: (i, 0))
    full = lambda shape: pl.BlockSpec(shape, lambda i: (0,) * len(shape))
    return pl.pallas_call(
        _merge_kernel,
        grid=(n // tm,),
        in_specs=[tok(d), tok(512), tok(256), tok(256), tok(256), tok(256), tok(256),
                  full((1, 256)), full((1, 256)), full((N_BRANCHES, d)),
                  pl.BlockSpec((1, N_BRANCHES * d, d), lambda i: (layer, 0, 0)),
                  full((sum(MIX_WIDTHS), d)), full((d, d)), full((1, d)), full((1, d))],
        out_specs=tok(d),
        out_shape=jax.ShapeDtypeStruct((n, d), F32),
        compiler_params=_cparams(("parallel",)),
        name="merge",
    )(xf, o_swa, y, g, bonus, o_fox, o_dsa, row(gn_g), row(gn_b), gate_bias, w_gate, w_branch, w_out,
      row(ln_g), row(ln_b))


def _silu(x):
    return x * _sigmoid(x)


def _ffn_kernel(x_ref, w1_ref, w3_ref, w2_ref, lng_ref, lnb_ref, o_ref, *, tf):
    xb = x_ref[...].astype(BF16)
    y = None
    for f in range(0, w1_ref.shape[1], tf):
        hidden = _silu(_dot(xb, w1_ref[:, f:f + tf])) * _dot(xb, w3_ref[:, f:f + tf])
        part = _dot(hidden.astype(BF16), w2_ref[f:f + tf, :])
        y = part if y is None else y + part
    o_ref[...] = _layer_norm(DN_ALPHA * x_ref[...] + y, lng_ref[...], lnb_ref[...])


def _ffn(xf, w1, w3, w2, ln_g, ln_b, tm=512, tf=1408):
    n, d = xf.shape
    ff = w1.shape[1]
    row = lambda x: x.reshape(1, -1)
    tok = pl.BlockSpec((tm, d), lambda i: (i, 0))
    resident = lambda shape: pl.BlockSpec(shape, lambda i: (0, 0), pipeline_mode=pl.Buffered(1))
    return pl.pallas_call(
        functools.partial(_ffn_kernel, tf=tf),
        grid=(n // tm,),
        in_specs=[tok, resident((d, ff)), resident((d, ff)), resident((ff, d)),
                  resident((1, d)), resident((1, d))],
        out_specs=tok,
        out_shape=jax.ShapeDtypeStruct((n, d), F32),
        compiler_params=_cparams(("parallel",)),
        name="ffn",
    )(xf, w1, w3, w2, row(ln_g), row(ln_b))


def _moe_kernel(x_ref, rw_ref, rb_ref, w1_ref, w3_ref, w2_ref, lng_ref, lnb_ref, o_ref,
                comb_ref, rank_ref, cnt_ref, xb_ref, *, sb, sb_small):
    e = pl.program_id(1)
    tm = x_ref.shape[0]
    half = tm // 2
    lane = lax.broadcasted_iota(jnp.int32, (tm, 128), 1).astype(F32)

    @pl.when(e == 0)
    def _():
        o_ref[...] = jnp.zeros_like(o_ref)
        xb_ref[...] = x_ref[...].astype(BF16)
        logits = _dot(x_ref[...], rw_ref[...], HIGHEST) + rb_ref[...]
        logits = jnp.where(lane < N_EXPERTS, logits, NEG_BIG)
        m1 = jnp.max(logits, axis=-1, keepdims=True)
        i1 = jnp.min(jnp.where(logits == m1, lane, 128.0), axis=-1, keepdims=True)
        rest = jnp.where(lane == i1, NEG_BIG, logits)
        m2 = jnp.max(rest, axis=-1, keepdims=True)
        i2 = jnp.min(jnp.where(rest == m2, lane, 128.0), axis=-1, keepdims=True)
        e2 = jnp.exp(m2 - m1)
        comb_ref[...] = jnp.where(lane == i1, 1.0 / (1.0 + e2), 0.0) + jnp.where(lane == i2, e2 / (1.0 + e2), 0.0)
        chosen = jnp.where(lane == i1, 1.0, 0.0) + jnp.where(lane == i2, 1.0, 0.0)
        r = lax.broadcasted_iota(jnp.int32, (tm, tm), 0)
        c = lax.broadcasted_iota(jnp.int32, (tm, tm), 1)
        earlier = jnp.where(c < r, 1.0, 0.0).astype(BF16)
        in_first = lax.broadcasted_iota(jnp.int32, (tm, 128), 0) < half
        cnt_a = jnp.sum(chosen[:half], axis=0, keepdims=True)
        rank = _dot(earlier, chosen.astype(BF16))
        rank = jnp.where(chosen > 0.5, jnp.where(in_first, rank, rank - cnt_a), -1.0)
        lane8 = lax.broadcasted_iota(jnp.int32, (8, 128), 1)
        row8 = lax.broadcasted_iota(jnp.int32, (8, 128), 0)
        rank_ref[...] = _nt(jnp.where(lane8 == row8, 1.0, 0.0).astype(F32), rank, HIGHEST)
        cnt_ref[0:1, :] = cnt_a
        cnt_ref[1:2, :] = jnp.sum(chosen[half:], axis=0, keepdims=True)

    e_f = e.astype(F32)
    n_a = jnp.sum(jnp.where(lane[0:1, :] == e_f, cnt_ref[0:1, :], 0.0)).astype(jnp.int32)
    n_b = jnp.sum(jnp.where(lane[0:1, :] == e_f, cnt_ref[1:2, :], 0.0)).astype(jnp.int32)
    rank_row = rank_ref[pl.ds(e, 1), :]
    weight = jnp.sum(jnp.where(lane == e_f, comb_ref[...], 0.0), axis=-1, keepdims=True)

    def expert(xs):
        hidden = _silu(_dot(xs, w1_ref[0])) * _dot(xs, w3_ref[0])
        return _dot(hidden.astype(BF16), w2_ref[0]).astype(BF16)

    def half_blocks(cap):
        slot = lax.broadcasted_iota(jnp.int32, (cap, half), 0).astype(F32)
        picks = [jnp.where(rank_row[:, g * half:(g + 1) * half] == slot, 1.0, 0.0).astype(BF16) for g in range(2)]
        xs = jnp.concatenate([_dot(picks[g], xb_ref[g * half:(g + 1) * half, :]) for g in range(2)], axis=0)
        out = expert(xs.astype(BF16))
        for g in range(2):
            o_ref[g * half:(g + 1) * half, :] += (_tn(picks[g], out[g * cap:(g + 1) * cap])
                                                  * weight[g * half:(g + 1) * half])

    def whole_tile_blocks():
        second = lax.broadcasted_iota(jnp.int32, (1, tm), 1) >= half
        rank_all = jnp.where(second & (rank_row >= 0.0), rank_row + n_a.astype(F32), rank_row)

        def body(s, carry):
            slot = lax.broadcasted_iota(jnp.int32, (sb, tm), 0).astype(F32) + (s * sb).astype(F32)
            pick = jnp.where(rank_all == slot, 1.0, 0.0).astype(BF16)
            o_ref[...] += _tn(pick, expert(_dot(pick, xb_ref[...]).astype(BF16))) * weight
            return carry

        lax.fori_loop(0, lax.div(n_a + n_b + sb - 1, sb), body, 0)

    most = jnp.maximum(n_a, n_b)

    @pl.when(most <= sb_small // 2)
    def _():
        half_blocks(sb_small // 2)

    @pl.when((most > sb_small // 2) & (most <= sb // 2))
    def _():
        half_blocks(sb // 2)

    @pl.when(most > sb // 2)
    def _():
        whole_tile_blocks()

    @pl.when(e == pl.num_programs(1) - 1)
    def _():
        o_ref[...] = _layer_norm(DN_ALPHA * x_ref[...] + o_ref[...], lng_ref[...], lnb_ref[...])


def _moe(xf, router_w, router_b, w1, w3, w2, ln_g, ln_b, tm=1024, sb=320, sb_small=256):
    n, d = xf.shape
    ne, _, fe = w1.shape
    tm = min(tm, n)
    rw = jnp.pad(router_w, ((0, 0), (0, 128 - ne)))
    rb = jnp.pad(router_b, (0, 128 - ne)).reshape(1, 128)
    row = lambda x: x.reshape(1, -1)
    return pl.pallas_call(
        functools.partial(_moe_kernel, sb=sb, sb_small=sb_small),
        grid=(n // tm, ne),
        in_specs=[pl.BlockSpec((tm, d), lambda i, e: (i, 0)),
                  pl.BlockSpec((d, 128), lambda i, e: (0, 0)),
                  pl.BlockSpec((1, 128), lambda i, e: (0, 0)),
                  pl.BlockSpec((1, d, fe), lambda i, e: (e, 0, 0)),
                  pl.BlockSpec((1, d, fe), lambda i, e: (e, 0, 0)),
                  pl.BlockSpec((1, fe, d), lambda i, e: (e, 0, 0)),
                  pl.BlockSpec((1, d), lambda i, e: (0, 0)),
                  pl.BlockSpec((1, d), lambda i, e: (0, 0))],
        out_specs=pl.BlockSpec((tm, d), lambda i, e: (i, 0)),
        out_shape=jax.ShapeDtypeStruct((n, d), F32),
        scratch_shapes=[pltpu.VMEM((tm, 128), F32), pltpu.VMEM((8, tm), F32), pltpu.VMEM((2, 128), F32),
                        pltpu.VMEM((tm, d), BF16)],
        compiler_params=pltpu.CompilerParams(dimension_semantics=("parallel", "arbitrary"),
                                             vmem_limit_bytes=MOE_VMEM_LIMIT),
        name="moe",
    )(xf, rw, rb, w1, w3, w2, row(ln_g), row(ln_b))


def _mixer_layer(xf, bsz, seq, layer, w_in, w_gate_all, sinks, mu, w0, w_up, a0, a_up, g_up, k_k, k_a, r_k,
                 gn_g, gn_b, f_bias, gate_bias, w_branch, w_out, ln_g, ln_b):
    p2 = _matmul(xf, _layout_w_in(w_in[layer]).astype(BF16), tm=512)
    p3 = p2.reshape(bsz, seq, P_WIDTH)
    o_swa = _swa(p3, sinks)
    o_fox = _fox(p3, _fox_c(p3, f_bias))
    o_dsa = _dsa(p3)
    r, lw, k2, v, kn, b_, g, bonus = _rwkv_pre(p3, mu, w0, w_up, a0, a_up, g_up, k_k, k_a, r_k)
    y = _rwkv_chunk(r, lw, k2, v, kn, b_)
    n = bsz * seq
    flat = lambda t: t.reshape(n, t.shape[-1])
    return _merge(xf, flat(o_swa), flat(y), flat(g), flat(bonus), flat(o_fox), flat(o_dsa), gn_g, gn_b,
                  gate_bias, w_gate_all, layer, w_branch.astype(BF16), w_out.astype(BF16), ln_g, ln_b)


def kernel(x, w_in, swa_sinks, rwkv_mu, rwkv_w0, rwkv_w_up, rwkv_a0, rwkv_a_up, rwkv_g_up, rwkv_k_k, rwkv_k_a,
           rwkv_r_k, rwkv_gn_g, rwkv_gn_b, fox_f_bias, gate_bias, w_branch, w_out, ln_g, ln_b, ffn_w1, ffn_w3,
           ffn_w2, router_w, router_b, exp_w1, exp_w3, exp_w2):
    bsz, seq, d = x.shape
    xf = x.reshape(bsz * seq, d)
    w_gate_all = jnp.swapaxes(w_in[:, :, IN_OFF_GATE:], 1, 2).astype(BF16)
    for layer in range(DEPTH):
        xf = _mixer_layer(xf, bsz, seq, layer, w_in, w_gate_all, swa_sinks[layer], rwkv_mu[layer], rwkv_w0[layer],
                              rwkv_w_up[layer], rwkv_a0[layer], rwkv_a_up[layer], rwkv_g_up[layer],
                              rwkv_k_k[layer], rwkv_k_a[layer], rwkv_r_k[layer], rwkv_gn_g[layer],
                              rwkv_gn_b[layer], fox_f_bias[layer], gate_bias[layer], w_branch[layer],
                              w_out[layer], ln_g[layer, 0], ln_b[layer, 0])
        j = layer // 2
        if layer % 2 == 0:
            xf = _ffn(xf, ffn_w1[j].astype(BF16), ffn_w3[j].astype(BF16), ffn_w2[j].astype(BF16),
                      ln_g[layer, 1], ln_b[layer, 1])
        else:
            xf = _moe(xf, router_w[j], router_b[j], exp_w1[j].astype(BF16), exp_w3[j].astype(BF16),
                      exp_w2[j].astype(BF16), ln_g[layer, 1], ln_b[layer, 1])
    return xf.reshape(bsz, seq, d)
```

```python
import functools

import jax
import jax.numpy as jnp
from jax import lax
from jax.experimental import pallas as pl
from jax.experimental.pallas import tpu as pltpu

F32 = jnp.float32
BF16 = jnp.bfloat16
HIGHEST = lax.Precision.HIGHEST

D_MODEL = 1024
DEPTH = 2
CHUNK = 64
HEAD_DIM = 64
CHUNK_SHIFT = 6
HEAD_SHIFT = 6
SWA_HEADS = 8
SWA_KV_HEADS = 2
SWA_GROUP = SWA_HEADS // SWA_KV_HEADS
SWA_WINDOW = 128
SWA_WIN_CHUNKS = SWA_WINDOW // CHUNK
RWKV_HEADS = 4
RWKV_W = RWKV_HEADS * HEAD_DIM
RWKV_GN_EPS = 64e-5
FOX_HEADS = 4
DSA_HEADS = 4
IDX_HEADS = 4
IDX_DIM = 64
DSA_TOPK_MAX = 256
D_FF = 2816
N_EXPERTS = 8
D_FF_EXPERT = 1408
N_BRANCHES = 4
DN_ALPHA = (2 * DEPTH) ** 0.25
LN_EPS = 1e-5
ATTN_SCALE = HEAD_DIM ** -0.5
LOG2E = 1.4426950408889634
IDX_SCALE = IDX_DIM ** -0.5
IDX_W_SCALE = IDX_HEADS ** -0.5
MIX_WIDTHS = (SWA_HEADS * HEAD_DIM, RWKV_W, FOX_HEADS * HEAD_DIM, DSA_HEADS * HEAD_DIM)

COL_RWKV = 0
COL_SWA_Q = 1024
COL_FOX_KV = 1536
COL_DSA_QQ = 2048
COL_FOX_Q = 2560
COL_SWA_KV = 2816
COL_DSA_KV = 3072
COL_MISC = 3328
P_WIDTH = 3584

NEG_BIG = -1e30
INT_MIN = -(2 ** 31)
VMEM_LIMIT = 48 * 1024 * 1024
MOE_VMEM_LIMIT = 58 * 1024 * 1024


IN_OFF_SWA, IN_OFF_RWKV, IN_OFF_FOX, IN_OFF_DSA, IN_OFF_GATE = 0, 768, 1792, 2564, 3272
D_IN = IN_OFF_GATE + N_BRANCHES * D_MODEL


def _layout_w_in(w):
    d = w.shape[0]
    z = lambda n: jnp.zeros((d, n), w.dtype)
    swa, rwkv, fox, dsa = IN_OFF_SWA, IN_OFF_RWKV, IN_OFF_FOX, IN_OFF_DSA
    parts = [
        w[:, rwkv:rwkv + 1024],
        w[:, swa:swa + 512],
        w[:, fox + 256:fox + 768],
        w[:, dsa:dsa + 256], w[:, dsa + 384:dsa + 640],
        w[:, fox:fox + 256],
        w[:, swa + 512:swa + 768],
        w[:, dsa + 256:dsa + 384], w[:, dsa + 640:dsa + 704], z(64),
        w[:, fox + 768:fox + 772], w[:, dsa + 704:dsa + 708], z(120),
        z(128),
    ]
    return jnp.concatenate(parts, axis=1)


def _cparams(sem):
    return pltpu.CompilerParams(dimension_semantics=sem, vmem_limit_bytes=VMEM_LIMIT)


def _nt(a, b, precision=None):
    return lax.dot_general(a, b, (((1,), (1,)), ((), ())), precision=precision,
                           preferred_element_type=F32)


def _tn(a, b, precision=None):
    return lax.dot_general(a, b, (((0,), (0,)), ((), ())), precision=precision,
                           preferred_element_type=F32)


def _dot(a, b, precision=None):
    return jnp.dot(a, b, precision=precision, preferred_element_type=F32)


def _sigmoid(x):
    return 1.0 / (1.0 + jnp.exp(-x))


def _layer_norm(z, g, b):
    mu = jnp.mean(z, axis=-1, keepdims=True)
    zc = z - mu
    var = jnp.mean(zc * zc, axis=-1, keepdims=True)
    return zc * lax.rsqrt(var + LN_EPS) * g + b


def _skewed(n, first, second):
    out = []
    staged = first(0)
    for h in range(n):
        nxt = first(h + 1) if h + 1 < n else None
        out.append(second(h, staged))
        staged = nxt
    return out


def _paired_blocks(n, body, carry):
    carry = lax.fori_loop(0, lax.div(n, 2), lambda t, c: body((2 * t, 2 * t + 1), c), carry)
    return lax.cond(lax.rem(n, 2) == 1, lambda c: body((n - 1,), c), lambda c: c, carry)


def _head_block_ones(n, scale):
    r = lax.broadcasted_iota(jnp.int32, (n, n), 0) >> HEAD_SHIFT
    c = lax.broadcasted_iota(jnp.int32, (n, n), 1) >> HEAD_SHIFT
    return jnp.where(r == c, scale, 0.0).astype(BF16)


def _pieces(x):
    hi = x.astype(BF16)
    rest = x - hi.astype(F32)
    mid = rest.astype(BF16)
    return hi, mid, (rest - mid.astype(F32)).astype(BF16)


def _head_reduce(x, head_ones):
    hi, mid, low = _pieces(x)
    return _dot(hi, head_ones) + _dot(mid, head_ones) + _dot(low, head_ones)


def _ones_dot(ones, x):
    hi, mid, low = _pieces(x)
    return _dot(ones, hi) + _dot(ones, mid) + _dot(ones, low)


def _dot_hi_lo(a, b):
    a_hi = a.astype(BF16)
    a_lo = (a - a_hi.astype(F32)).astype(BF16)
    b_hi = b.astype(BF16)
    b_lo = (b - b_hi.astype(F32)).astype(BF16)
    return _dot(a_hi, b_hi) + _dot(a_hi, b_lo) + _dot(a_lo, b_hi)


def _mm_kernel(a_ref, b_ref, o_ref):
    o_ref[...] = _dot(a_ref[...].astype(BF16), b_ref[...])


def _matmul(a, b, tm):
    m, k = a.shape
    n = b.shape[1]
    tm = min(tm, m)
    return pl.pallas_call(
        _mm_kernel,
        grid=(m // tm,),
        in_specs=[pl.BlockSpec((tm, k), lambda i: (i, 0)),
                  pl.BlockSpec((k, n), lambda i: (0, 0), pipeline_mode=pl.Buffered(1))],
        out_specs=pl.BlockSpec((tm, n), lambda i: (i, 0)),
        out_shape=jax.ShapeDtypeStruct((m, n), F32),
        compiler_params=_cparams(("parallel",)),
        name="in_proj",
    )(a, b)


def _swa_kernel(sink_ref, q_ref, kv_ref, o_ref, *, tq):
    i = pl.program_id(1)
    win = tq + SWA_WINDOW
    s0 = pl.multiple_of(jnp.maximum(i * tq - SWA_WINDOW, 0), SWA_WINDOW)
    k_chunk = (s0 + lax.broadcasted_iota(jnp.int32, (win, tq), 0)) >> CHUNK_SHIFT
    q_chunk = (i * tq + lax.broadcasted_iota(jnp.int32, (win, tq), 1)) >> CHUNK_SHIFT
    bias = jnp.where(k_chunk <= q_chunk,
                     jnp.where(k_chunk >= q_chunk - SWA_WIN_CHUNKS, 0.0, NEG_BIG), NEG_BIG)
    ks = [kv_ref[0, pl.ds(s0, win), hk * HEAD_DIM:(hk + 1) * HEAD_DIM].astype(BF16)
          for hk in range(SWA_KV_HEADS)]
    kv_w = SWA_KV_HEADS * HEAD_DIM
    vs = [kv_ref[0, pl.ds(s0, win), kv_w + hk * HEAD_DIM:kv_w + (hk + 1) * HEAD_DIM].astype(BF16)
          for hk in range(SWA_KV_HEADS)]

    def logits(h):
        q = (q_ref[0, :, h * HEAD_DIM:(h + 1) * HEAD_DIM] * (ATTN_SCALE * LOG2E)).astype(BF16)
        return _nt(ks[h // SWA_GROUP], q) + bias

    def finish(h, s):
        sink = sink_ref[h] * LOG2E
        m = jnp.maximum(jnp.max(s, axis=0, keepdims=True), sink)
        e = jnp.exp2(s - m)
        denom = jnp.sum(e, axis=0, keepdims=True) + jnp.exp2(sink - m)
        return _tn(vs[h // SWA_GROUP], e.astype(BF16)) / denom

    o_ref[0] = jnp.concatenate(_skewed(SWA_HEADS, logits, finish), axis=0).T


def _swa(p3, sinks, tq=512):
    b, s, _ = p3.shape
    tq = min(tq, s // 2)
    kv_w = 2 * SWA_KV_HEADS * HEAD_DIM
    return pl.pallas_call(
        functools.partial(_swa_kernel, tq=tq),
        grid=(b, s // tq),
        in_specs=[pl.BlockSpec(memory_space=pltpu.SMEM),
                  pl.BlockSpec((1, tq, 512), lambda bi, i: (bi, i, COL_SWA_Q // 512)),
                  pl.BlockSpec((1, s, kv_w), lambda bi, i: (bi, 0, COL_SWA_KV // kv_w))],
        out_specs=pl.BlockSpec((1, tq, 512), lambda bi, i: (bi, i, 0)),
        out_shape=jax.ShapeDtypeStruct((b, s, 512), F32),
        compiler_params=_cparams(("parallel", "parallel")),
        name="swa",
    )(sinks, p3, p3)


def _fox_c_kernel(f_ref, bias_ref, c_ref, *, blk):
    s = f_ref.shape[1]
    r = lax.broadcasted_iota(jnp.int32, (blk, blk), 0)
    c = lax.broadcasted_iota(jnp.int32, (blk, blk), 1)
    tri = jnp.where(c <= r, 1.0, 0.0).astype(BF16)
    carry = jnp.zeros((1, 128), F32)
    for j in range(s // blk):
        x = f_ref[0, j * blk:(j + 1) * blk, :] + bias_ref[...]
        log_f = jnp.minimum(x, 0.0) - jnp.log1p(jnp.exp(-jnp.abs(x)))
        cs = _ones_dot(tri, log_f) + carry
        carry = cs[blk - 1:blk, :]
        for h in range(FOX_HEADS):
            c_ref[0, h, j * blk:(j + 1) * blk, :] = jnp.broadcast_to(cs[:, h:h + 1] * LOG2E, (blk, 128))


def _fox_c(p3, f_bias):
    b, s, _ = p3.shape
    bias_row = jnp.pad(f_bias, (0, 128 - FOX_HEADS)).reshape(1, 128)
    return pl.pallas_call(
        functools.partial(_fox_c_kernel, blk=256),
        grid=(b,),
        in_specs=[pl.BlockSpec((1, s, 128), lambda bi: (bi, 0, COL_MISC // 128)),
                  pl.BlockSpec((1, 128), lambda bi: (0, 0))],
        out_specs=pl.BlockSpec((1, FOX_HEADS, s, 128), lambda bi: (bi, 0, 0, 0)),
        out_shape=jax.ShapeDtypeStruct((b, FOX_HEADS, s, 128), F32),
        compiler_params=_cparams(("parallel",)),
        name="fox_cumsum",
    )(p3, bias_row)


def _fox_kernel(q_ref, k_ref, c_ref, o_ref, lg_ref, acc_ref, *, tq):
    i = pl.program_id(1)
    tk = tq
    v_col = FOX_HEADS * HEAD_DIM
    key_i = lax.broadcasted_iota(jnp.int32, (tk, tq), 0)
    qry_i = lax.broadcasted_iota(jnp.int32, (tk, tq), 1)
    diag_bias = jnp.where(key_i <= qry_i, 0.0, NEG_BIG)
    q = [(q_ref[0, :, h * HEAD_DIM:(h + 1) * HEAD_DIM] * (ATTN_SCALE * LOG2E)).astype(BF16)
         for h in range(FOX_HEADS)]

    def logits_blocks(blocks, maxes, masked=False):
        starts = [pl.multiple_of(j * tk, tk) for j in blocks]
        dots = [[_nt(k_ref[0, pl.ds(ks, tk), h * HEAD_DIM:(h + 1) * HEAD_DIM].astype(BF16), q[h])
                 for h in range(FOX_HEADS)] for ks in starts]
        maxes = list(maxes)
        for b, ks in enumerate(starts):
            for h in range(FOX_HEADS):
                c_k = c_ref[0, h, pl.ds(ks, tk), :]
                s = dots[b][h] - jnp.concatenate([c_k] * (tq // 128), axis=1)
                if masked:
                    s = s + diag_bias
                lg_ref[h, pl.ds(ks, tk), :] = s
                maxes[h] = jnp.maximum(maxes[h], jnp.max(s, axis=0, keepdims=True))
        return tuple(maxes)

    maxes = tuple(jnp.full((1, tq), NEG_BIG, F32) for _ in range(FOX_HEADS))
    maxes = _paired_blocks(i, logits_blocks, maxes)
    maxes = logits_blocks((i,), maxes, masked=True)
    acc_ref[...] = jnp.zeros_like(acc_ref)

    def attend_blocks(blocks, sums):
        starts = [pl.multiple_of(j * tk, tk) for j in blocks]
        sums = list(sums)
        for h in range(FOX_HEADS):
            update = None
            for ks in starts:
                p = jnp.exp2(lg_ref[h, pl.ds(ks, tk), :] - maxes[h])
                v = k_ref[0, pl.ds(ks, tk), v_col + h * HEAD_DIM:v_col + (h + 1) * HEAD_DIM].astype(BF16)
                pv = _tn(v, p.astype(BF16))
                update = pv if update is None else update + pv
                sums[h] = sums[h] + jnp.sum(p, axis=0, keepdims=True)
            acc_ref[h] += update
        return tuple(sums)

    sums = _paired_blocks(i + 1, attend_blocks, tuple(jnp.zeros((1, tq), F32) for _ in range(FOX_HEADS)))
    out_t = jnp.concatenate([acc_ref[h] / sums[h] for h in range(FOX_HEADS)], axis=0)
    o_ref[0] = out_t.T


def _fox(p3, c, tq=256):
    b, s, _ = p3.shape
    w = FOX_HEADS * HEAD_DIM
    return pl.pallas_call(
        functools.partial(_fox_kernel, tq=tq),
        grid=(b, s // tq),
        in_specs=[pl.BlockSpec((1, tq, w), lambda bi, i: (bi, i, COL_FOX_Q // w)),
                  pl.BlockSpec((1, s, 2 * w), lambda bi, i: (bi, 0, COL_FOX_KV // (2 * w))),
                  pl.BlockSpec((1, FOX_HEADS, s, 128), lambda bi, i: (bi, 0, 0, 0))],
        out_specs=pl.BlockSpec((1, tq, w), lambda bi, i: (bi, i, 0)),
        out_shape=jax.ShapeDtypeStruct((b, s, w), F32),
        scratch_shapes=[pltpu.VMEM((FOX_HEADS, s, tq), F32), pltpu.VMEM((FOX_HEADS, HEAD_DIM, tq), F32)],
        compiler_params=_cparams(("parallel", "parallel")),
        name="fox",
    )(p3, p3, c)


def _key_to_score(key):
    return pltpu.bitcast(key ^ ((key >> 31) & 0x7FFFFFFF), F32)


def _bit_planes(score):
    bits = pltpu.bitcast(score, jnp.int32)
    image = bits ^ ((bits >> 31) | INT_MIN)
    rows = image.reshape(32, 8, image.shape[1])
    v = [rows[j] for j in range(32)]
    for d, mask in ((16, 0x0000FFFF), (8, 0x00FF00FF), (4, 0x0F0F0F0F), (2, 0x33333333), (1, 0x55555555)):
        for k in range(32):
            if k & d == 0:
                t = (v[k] ^ lax.shift_right_logical(v[k + d], d)) & mask
                v[k] = v[k] ^ t
                v[k + d] = v[k + d] ^ lax.shift_left(t, d)
    return v


def _dsa_kernel(qq_ref, misc_ref, kv_ref, o_ref, sc_ref, lg_ref, acc_ref, pl_ref, *, tq, kb, topk):
    i = pl.program_id(1)
    n_kb = lax.div((i + 1) * tq + kb - 1, kb)
    lane8 = lax.broadcasted_iota(jnp.int32, (8, 128), 1)
    row8 = lax.broadcasted_iota(jnp.int32, (8, 128), 0)
    pick = jnp.where((lane8 == row8 + 4) & (row8 < IDX_HEADS), 1.0, 0.0).astype(F32)
    w_t = _nt(pick, misc_ref[0], HIGHEST) * (IDX_W_SCALE * IDX_SCALE)
    q_chunk = (i * tq + lax.broadcasted_iota(jnp.int32, (1, tq), 1)) >> CHUNK_SHIFT
    qw = DSA_HEADS * HEAD_DIM
    q_idx = [qq_ref[0, :, qw + h * IDX_DIM:qw + (h + 1) * IDX_DIM].astype(BF16) for h in range(IDX_HEADS)]

    def score_blocks(blocks, carry):
        starts = [pl.multiple_of(j * kb, kb) for j in blocks]
        dots = []
        for ks in starts:
            k_idx = kv_ref[0, pl.ds(ks, kb), 2 * HEAD_DIM:2 * HEAD_DIM + IDX_DIM].astype(BF16)
            dots.append([_nt(k_idx, q_idx[h]) for h in range(IDX_HEADS)])
        for b, ks in enumerate(starts):
            score = jnp.maximum(dots[b][0], 0.0) * w_t[0:1, :]
            for h in range(1, IDX_HEADS):
                score = score + jnp.maximum(dots[b][h], 0.0) * w_t[h:h + 1, :]
            k_chunk = (ks + lax.broadcasted_iota(jnp.int32, (kb, tq), 0)) >> CHUNK_SHIFT
            score = jnp.where(k_chunk <= q_chunk, score, -jnp.inf)
            sc_ref[pl.ds(ks, kb), :] = score
            planes = _bit_planes(score)
            for t in range(32):
                pl_ref[blocks[b], t] = planes[t]
        return carry

    _paired_blocks(n_kb, score_blocks, 0)
    n_blocks = pl_ref.shape[0]

    def clear_planes(j, carry):
        pl_ref[j] = jnp.zeros(pl_ref.shape[1:], jnp.int32)
        return carry

    lax.fori_loop(n_kb, n_blocks, clear_planes, 0)

    def count(pred):
        def body(j, acc):
            blk = sc_ref[pl.ds(pl.multiple_of(j * kb, kb), kb), :]
            hit = jnp.where(pred(blk), 1, 0).astype(jnp.int32).reshape(kb // 8, 8, tq)
            parts = [hit[g] for g in range(kb // 8)]
            while len(parts) > 1:
                parts = [parts[g] + parts[g + 1] for g in range(0, len(parts), 2)]
            return acc + parts[0]
        acc = lax.fori_loop(0, n_kb, body, jnp.zeros((8, tq), jnp.int32))
        return jnp.sum(acc.astype(F32), axis=0, keepdims=True)

    def search_by_compares():
        zero = jnp.zeros((1, tq), F32)
        thr_key = jnp.where(count(lambda blk: blk >= zero) >= topk, 0, INT_MIN).astype(jnp.int32)

        def bit_step(bi, thr_key):
            cand = thr_key | lax.shift_left(jnp.int32(1), 30 - bi)
            cand_score = _key_to_score(cand)
            return jnp.where(count(lambda blk: blk >= cand_score) >= topk, cand, thr_key)

        thr_key = lax.fori_loop(0, 31, bit_step, thr_key)
        t = jnp.where(thr_key == INT_MIN, -jnp.inf, _key_to_score(thr_key))
        return t, count(lambda blk: blk > t)

    def search_by_planes():
        def bit_pass(t, carry):
            prefix, above, alive = carry
            ones = [alive[j] & pl_ref[j, t] for j in range(n_blocks)]
            parts = [lax.population_count(o) for o in ones]
            while len(parts) > 1:
                parts = [parts[g] + parts[g + 1] for g in range(0, len(parts), 2)]
            reach = above + jnp.sum(parts[0].astype(F32), axis=0, keepdims=True)
            take = reach >= topk
            prefix = jnp.where(take, prefix | lax.shift_left(jnp.int32(1), 31 - t), prefix)
            alive = tuple(jnp.where(take, ones[j], alive[j] ^ ones[j]) for j in range(n_blocks))
            return prefix, jnp.where(take, above, reach), alive

        init = (jnp.zeros((1, tq), jnp.int32), jnp.zeros((1, tq), F32),
                tuple(jnp.full((8, tq), -1, jnp.int32) for _ in range(n_blocks)))
        image, _, _ = lax.fori_loop(0, 32, bit_pass, init)
        return pltpu.bitcast(image ^ (~(image >> 31) | INT_MIN), F32)

    thr_planes = search_by_planes()
    n_above = count(lambda blk: blk > thr_planes)
    n_reach = count(lambda blk: blk >= thr_planes)
    confirmed = jnp.min(jnp.where(n_above < topk, jnp.where(n_reach >= topk, 1.0, 0.0), 0.0)) > 0.5
    thr, n_above = lax.cond(confirmed, lambda: (thr_planes, n_above), search_by_compares)
    need = topk - n_above

    r = lax.broadcasted_iota(jnp.int32, (kb, kb), 0)
    c = lax.broadcasted_iota(jnp.int32, (kb, kb), 1)
    earlier = jnp.where(c < r, 1.0, 0.0).astype(BF16)
    q_att = [(qq_ref[0, :, h * HEAD_DIM:(h + 1) * HEAD_DIM] * (ATTN_SCALE * LOG2E)).astype(BF16)
             for h in range(DSA_HEADS)]

    def logits_blocks(blocks, carry):
        ties_before, maxes = carry
        maxes = list(maxes)
        starts = [pl.multiple_of(j * kb, kb) for j in blocks]
        dots, ties = [], []
        for ks in starts:
            k = kv_ref[0, pl.ds(ks, kb), 0:HEAD_DIM].astype(BF16)
            dots.append([_nt(k, q_att[h]) for h in range(DSA_HEADS)])
            k_chunk = (ks + lax.broadcasted_iota(jnp.int32, (kb, tq), 0)) >> CHUNK_SHIFT
            ties.append(jnp.where(sc_ref[pl.ds(ks, kb), :] == thr, jnp.where(k_chunk <= q_chunk, 1.0, 0.0), 0.0))
        ranks = [_dot(earlier, tie.astype(BF16)) for tie in ties]
        for b, ks in enumerate(starts):
            tie_taken = jnp.where(ranks[b] + ties_before < need, ties[b], 0.0)
            bias = jnp.where(sc_ref[pl.ds(ks, kb), :] > thr, 0.0, jnp.where(tie_taken > 0.5, 0.0, NEG_BIG))
            ties_before = ties_before + jnp.sum(ties[b], axis=0, keepdims=True)
            for h in range(DSA_HEADS):
                s = dots[b][h] + bias
                lg_ref[h, pl.ds(ks, kb), :] = s
                maxes[h] = jnp.maximum(maxes[h], jnp.max(s, axis=0, keepdims=True))
        return ties_before, tuple(maxes)

    init = (jnp.zeros((1, tq), F32), tuple(jnp.full((1, tq), NEG_BIG, F32) for _ in range(DSA_HEADS)))
    _, maxes = _paired_blocks(n_kb, logits_blocks, init)

    acc_ref[...] = jnp.zeros_like(acc_ref)

    def attend_blocks(blocks, sums):
        starts = [pl.multiple_of(j * kb, kb) for j in blocks]
        vs = [kv_ref[0, pl.ds(ks, kb), HEAD_DIM:2 * HEAD_DIM].astype(BF16) for ks in starts]
        sums = list(sums)
        for h in range(DSA_HEADS):
            update = None
            for b, ks in enumerate(starts):
                p = jnp.exp2(lg_ref[h, pl.ds(ks, kb), :] - maxes[h])
                pv = _tn(vs[b], p.astype(BF16))
                update = pv if update is None else update + pv
                sums[h] = sums[h] + jnp.sum(p, axis=0, keepdims=True)
            acc_ref[h] += update
        return tuple(sums)

    sums = _paired_blocks(n_kb, attend_blocks, tuple(jnp.zeros((1, tq), F32) for _ in range(DSA_HEADS)))
    out_t = jnp.concatenate([acc_ref[h] / sums[h] for h in range(DSA_HEADS)], axis=0)
    o_ref[0] = out_t.T


def _dsa(p3, tq=256, kb=256):
    b, s, _ = p3.shape
    assert kb == 8 * 32, "a key block is the 32 row groups of one bit-plane word"
    topk = min(DSA_TOPK_MAX, s // 4)
    return pl.pallas_call(
        functools.partial(_dsa_kernel, tq=tq, kb=kb, topk=topk),
        grid=(b, s // tq),
        in_specs=[pl.BlockSpec((1, tq, 512), lambda bi, i: (bi, i, COL_DSA_QQ // 512)),
                  pl.BlockSpec((1, tq, 128), lambda bi, i: (bi, i, COL_MISC // 128)),
                  pl.BlockSpec((1, s, 256), lambda bi, i: (bi, 0, COL_DSA_KV // 256))],
        out_specs=pl.BlockSpec((1, tq, 256), lambda bi, i: (bi, i, 0)),
        out_shape=jax.ShapeDtypeStruct((b, s, 256), F32),
        scratch_shapes=[pltpu.VMEM((s, tq), F32), pltpu.VMEM((DSA_HEADS, s, tq), F32),
                        pltpu.VMEM((DSA_HEADS, HEAD_DIM, tq), F32), pltpu.VMEM((s // kb, 32, 8, tq), jnp.int32)],
        compiler_params=_cparams(("parallel", "parallel")),
        name="dsa",
    )(p3, p3, p3)


def _rwkv_pre_kernel(p_ref, mu_ref, w0_ref, wup_ref, a0_ref, aup_ref, gup_ref, kk_ref, ka_ref, rk_ref,
                     r_o, lw_o, k_o, v_o, kn_o, b_o, g_o, bonus_o, last_ref, *, tt):
    t = pl.program_id(1)

    @pl.when(t == 0)
    def _():
        last_ref[...] = jnp.zeros_like(last_ref)

    p = p_ref[0]
    row = lax.broadcasted_iota(jnp.int32, p.shape, 0)
    p_prev = jnp.where(row == 0, last_ref[...], pltpu.roll(p, 1, axis=0))
    last_ref[...] = p[tt - 1:tt, :]
    ps = p + mu_ref[...] * (p_prev - p)
    w = RWKV_W
    r, k, v = ps[:, 0:w], ps[:, w:2 * w], ps[:, 2 * w:3 * w]
    w_lo, a_lo, g_lo = ps[:, 3 * w:3 * w + 64], ps[:, 3 * w + 64:3 * w + 128], ps[:, 3 * w + 128:3 * w + 256]
    ww = w0_ref[...] + _dot_hi_lo(jnp.tanh(w_lo), wup_ref[...])
    softplus_neg = jnp.maximum(-ww, 0.0) + jnp.log1p(jnp.exp(-jnp.abs(ww)))
    log_w = -jnp.exp(-softplus_neg - 0.5)
    a = _sigmoid(a0_ref[...] + _dot_hi_lo(a_lo, aup_ref[...]))
    g = _dot_hi_lo(_sigmoid(g_lo), gup_ref[...])
    head_sum = _head_block_ones(w, 1.0)
    kn = k * kk_ref[...]
    kn = kn * lax.rsqrt(_head_reduce(kn * kn, head_sum) + 1e-12)
    k2 = k * (1.0 + (a - 1.0) * ka_ref[...])
    bonus = _head_reduce(r * k2 * rk_ref[...], head_sum) * v
    r_o[0] = r
    lw_o[0] = log_w
    k_o[0] = k2
    v_o[0] = v
    kn_o[0] = kn
    b_o[0] = kn * a
    g_o[0] = g
    bonus_o[0] = bonus


def _rwkv_pre(p3, mu, w0, w_up, a0, a_up, g_up, k_k, k_a, r_k, tt=512):
    b, s, _ = p3.shape
    w = RWKV_W
    row = lambda x: x.reshape(1, -1)
    full = lambda shape: pl.BlockSpec(shape, lambda bi, t: (0,) * len(shape))
    out = jax.ShapeDtypeStruct((b, s, w), F32)
    return pl.pallas_call(
        functools.partial(_rwkv_pre_kernel, tt=tt),
        grid=(b, s // tt),
        in_specs=[pl.BlockSpec((1, tt, 1024), lambda bi, t: (bi, t, COL_RWKV // 1024)),
                  full((1, 1024)), full((1, w)), full((64, w)), full((1, w)), full((64, w)),
                  full((128, w)), full((1, w)), full((1, w)), full((1, w))],
        out_specs=[pl.BlockSpec((1, tt, w), lambda bi, t: (bi, t, 0))] * 8,
        out_shape=[out] * 8,
        scratch_shapes=[pltpu.VMEM((1, 1024), F32)],
        compiler_params=_cparams(("parallel", "arbitrary")),
        name="rwkv_pre",
    )(p3, row(mu), row(w0), w_up, row(a0), a_up, g_up, row(k_k), row(k_a), row(r_k))


def _bdot(a, b):
    return _dot(a.astype(BF16), b.astype(BF16))


def _bnt(a, b):
    return _nt(a.astype(BF16), b.astype(BF16))


def _btn(a, b):
    return _tn(a.astype(BF16), b.astype(BF16))


def _rwkv_chunk_kernel(r_ref, lw_ref, k_ref, v_ref, kn_ref, b_ref, y_ref, s_ref, *, cs, nch):
    c = pl.program_id(1)

    @pl.when(c == 0)
    def _():
        s_ref[...] = jnp.zeros_like(s_ref)

    rows = cs * min(nch, 4)
    big_row = lax.broadcasted_iota(jnp.int32, (rows, rows), 0)
    big_col = lax.broadcasted_iota(jnp.int32, (rows, rows), 1)
    same_chunk = (big_row >> CHUNK_SHIFT) == (big_col >> CHUNK_SHIFT)
    tri = jnp.where(same_chunk, jnp.where(big_col <= big_row, 1.0, 0.0), 0.0).astype(BF16)
    pw = 2 * HEAD_DIM
    row = lax.broadcasted_iota(jnp.int32, (cs, pw), 0)
    lane = lax.broadcasted_iota(jnp.int32, (cs, pw), 1)
    col = lane & (HEAD_DIM - 1)
    first_head = lane < HEAD_DIM
    incl = col <= row
    strict = col < row
    eye = jnp.where(col == row, 1.0, 0.0).astype(F32)

    def block_diag(x):
        return jnp.concatenate([jnp.where(first_head, x, 0.0), jnp.where(first_head, 0.0, x)], axis=0)

    def diag_blocks(full):
        return jnp.where(first_head, full[:cs], full[cs:])

    lw = lw_ref[0]
    cum = jnp.concatenate([_ones_dot(tri, lw[g:g + rows]) for g in range(0, cs * nch, rows)],
                          axis=0)
    e_incl = jnp.exp(cum)
    e_neg = jnp.exp(-cum)
    abar_all = -kn_ref[0] * jnp.exp(cum - lw)
    rbar_all = r_ref[0] * e_incl
    bt_all = b_ref[0] * e_neg
    kt_all = k_ref[0] * e_neg
    v_all = v_ref[0]
    n_pairs = RWKV_HEADS // 2
    probs = [(j, p) for j in range(nch) for p in range(n_pairs)]
    cut = lambda t, jp: t[jp[0] * cs:(jp[0] + 1) * cs, jp[1] * pw:(jp[1] + 1) * pw]
    each = lambda fn: {jp: fn(jp) for jp in probs}
    abar, rbar = each(lambda jp: cut(abar_all, jp)), each(lambda jp: cut(rbar_all, jp))
    bt, kt, v = each(lambda jp: cut(bt_all, jp)), each(lambda jp: cut(kt_all, jp)), each(lambda jp: cut(v_all, jp))
    p_last = each(lambda jp: e_incl[(jp[0] + 1) * cs - 1:(jp[0] + 1) * cs, jp[1] * pw:(jp[1] + 1) * pw])
    ar = each(lambda jp: jnp.concatenate([abar[jp], rbar[jp]], axis=0))
    ar_b = each(lambda jp: _bnt(ar[jp], block_diag(bt[jp])))
    ar_k = each(lambda jp: _bnt(ar[jp], block_diag(kt[jp])))
    a_ab = each(lambda jp: jnp.where(strict, ar_b[jp][:cs], 0.0))
    a_ak = each(lambda jp: jnp.where(strict, ar_k[jp][:cs], 0.0))
    a_rb = each(lambda jp: jnp.where(incl, ar_b[jp][cs:], 0.0))
    a_rk = each(lambda jp: jnp.where(incl, ar_k[jp][cs:], 0.0))
    v_bd = each(lambda jp: block_diag(v[jp]))
    ak_v = each(lambda jp: _bdot(a_ak[jp], v_bd[jp]))
    inv = each(lambda jp: eye)
    for level in range(cs.bit_length() - 1):
        same_2b = (row >> (level + 1)) == (col >> (level + 1))
        lower_left = same_2b & (((row >> level) & 1) == 1) & (((col >> level) & 1) == 0)
        c_blocks = each(lambda jp: jnp.where(lower_left, a_ab[jp], 0.0))
        half_done = each(lambda jp: _bdot(inv[jp], block_diag(c_blocks[jp])))
        inv = each(lambda jp: inv[jp] + _bdot(half_done[jp], block_diag(inv[jp])))
    a_hat = each(lambda jp: _bdot(inv[jp], block_diag(abar[jp])))
    u_hat = each(lambda jp: _bdot(inv[jp], block_diag(ak_v[jp])))
    r_hat = each(lambda jp: rbar[jp] + _bdot(a_rb[jp], block_diag(a_hat[jp])))
    y_hat = each(lambda jp: _bdot(a_rb[jp], block_diag(u_hat[jp])) + _bdot(a_rk[jp], v_bd[jp]))
    g_mat = each(lambda jp: eye * p_last[jp] + diag_blocks(_btn(a_hat[jp], bt[jp] * p_last[jp])))
    h_mat = each(lambda jp: diag_blocks(_btn(jnp.concatenate([u_hat[jp], v[jp]], axis=0),
                                             jnp.concatenate([bt[jp], kt[jp]], axis=0) * p_last[jp])))
    states = [s_ref[p] for p in range(n_pairs)]
    for j in range(nch):
        for p in range(n_pairs):
            y_ref[0, j * cs:(j + 1) * cs, p * pw:(p + 1) * pw] = (
                _bnt(r_hat[j, p], block_diag(states[p])) + y_hat[j, p])
        states = [_dot_hi_lo(states[p], block_diag(g_mat[j, p])) + h_mat[j, p] for p in range(n_pairs)]
    for p in range(n_pairs):
        s_ref[p] = states[p]


def _rwkv_chunk(r, lw, k, v, kn, b_, cs=CHUNK, nch=8):
    b, s, w = r.shape
    spec = pl.BlockSpec((1, cs * nch, w), lambda bi, c: (bi, c, 0))
    return pl.pallas_call(
        functools.partial(_rwkv_chunk_kernel, cs=cs, nch=nch),
        grid=(b, s // (cs * nch)),
        in_specs=[spec] * 6,
        out_specs=spec,
        out_shape=jax.ShapeDtypeStruct((b, s, w), F32),
        scratch_shapes=[pltpu.VMEM((RWKV_HEADS // 2, HEAD_DIM, 2 * HEAD_DIM), F32)],
        compiler_params=_cparams(("parallel", "arbitrary")),
        name="rwkv_chunk",
    )(r, lw, k, v, kn, b_)


def _merge_kernel(x_ref, swa_ref, y_ref, g_ref, bonus_ref, fox_ref, dsa_ref, gng_ref, gnb_ref,
                  gbias_ref, wg_ref, wb_ref, wo_ref, lng_ref, lnb_ref, o_ref):
    y = y_ref[...]
    xb = x_ref[...].astype(BF16)
    head_mean = _head_block_ones(RWKV_W, 1.0 / HEAD_DIM)
    yc = y - _head_reduce(y, head_mean)
    yv = _head_reduce(yc * yc, head_mean)
    o_rwkv = (yc * lax.rsqrt(yv + RWKV_GN_EPS) * gng_ref[...] + gnb_ref[...] + bonus_ref[...]) * g_ref[...]
    branches = (swa_ref[...], o_rwkv, fox_ref[...], dsa_ref[...])
    merged = jnp.zeros(o_ref.shape, F32)
    off = 0
    for i, o in enumerate(branches):
        width = MIX_WIDTHS[i]
        proj = _dot(o.astype(BF16), wb_ref[off:off + width, :])
        gate = _sigmoid(_nt(xb, wg_ref[0, i * D_MODEL:(i + 1) * D_MODEL, :]) + gbias_ref[i:i + 1, :])
        merged = merged + gate * proj
        off += width
    y_out = _dot(merged.astype(BF16), wo_ref[...])
    o_ref[...] = _layer_norm(DN_ALPHA * x_ref[...] + y_out, lng_ref[...], lnb_ref[...])


def _merge(xf, o_swa, y, g, bonus, o_fox, o_dsa, gn_g, gn_b, gate_bias, w_gate_all, layer, w_branch, w_out,
           ln_g, ln_b, tm=512):
    n, d = xf.shape
    w_gate = w_gate_all
    row = lambda x: x.reshape(1, -1)
    tok = lambda wdt: pl.BlockSpec((tm, wdt), lambda i: (i, 0))
    full = lambda shape: pl.BlockSpec(shape, lambda i: (0,) * len(shape))
    return pl.pallas_call(
        _merge_kernel,
        grid=(n // tm,),
        in_specs=[tok(d), tok(512), tok(256), tok(256), tok(256), tok(256), tok(256),
                  full((1, 256)), full((1, 256)), full((N_BRANCHES, d)),
                  pl.BlockSpec((1, N_BRANCHES * d, d), lambda i: (layer, 0, 0)),
                  full((sum(MIX_WIDTHS), d)), full((d, d)), full((1, d)), full((1, d))],
        out_specs=tok(d),
        out_shape=jax.ShapeDtypeStruct((n, d), F32),
        compiler_params=_cparams(("parallel",)),
        name="merge",
    )(xf, o_swa, y, g, bonus, o_fox, o_dsa, row(gn_g), row(gn_b), gate_bias, w_gate, w_branch, w_out,
      row(ln_g), row(ln_b))


def _silu(x):
    return x * _sigmoid(x)


def _ffn_kernel(x_ref, w1_ref, w3_ref, w2_ref, lng_ref, lnb_ref, o_ref, *, tf):
    xb = x_ref[...].astype(BF16)
    y = None
    for f in range(0, w1_ref.shape[1], tf):
        hidden = _silu(_dot(xb, w1_ref[:, f:f + tf])) * _dot(xb, w3_ref[:, f:f + tf])
        part = _dot(hidden.astype(BF16), w2_ref[f:f + tf, :])
        y = part if y is None else y + part
    o_ref[...] = _layer_norm(DN_ALPHA * x_ref[...] + y, lng_ref[...], lnb_ref[...])


def _ffn(xf, w1, w3, w2, ln_g, ln_b, tm=512, tf=1408):
    n, d = xf.shape
    ff = w1.shape[1]
    row = lambda x: x.reshape(1, -1)
    tok = pl.BlockSpec((tm, d), lambda i: (i, 0))
    resident = lambda shape: pl.BlockSpec(shape, lambda i: (0, 0), pipeline_mode=pl.Buffered(1))
    return pl.pallas_call(
        functools.partial(_ffn_kernel, tf=tf),
        grid=(n // tm,),
        in_specs=[tok, resident((d, ff)), resident((d, ff)), resident((ff, d)),
                  resident((1, d)), resident((1, d))],
        out_specs=tok,
        out_shape=jax.ShapeDtypeStruct((n, d), F32),
        compiler_params=_cparams(("parallel",)),
        name="ffn",
    )(xf, w1, w3, w2, row(ln_g), row(ln_b))


def _moe_kernel(x_ref, rw_ref, rb_ref, w1_ref, w3_ref, w2_ref, lng_ref, lnb_ref, o_ref,
                comb_ref, rank_ref, cnt_ref, xb_ref, *, sb, sb_small):
    e = pl.program_id(1)
    tm = x_ref.shape[0]
    half = tm // 2
    lane = lax.broadcasted_iota(jnp.int32, (tm, 128), 1).astype(F32)

    @pl.when(e == 0)
    def _():
        o_ref[...] = jnp.zeros_like(o_ref)
        xb_ref[...] = x_ref[...].astype(BF16)
        logits = _dot(x_ref[...], rw_ref[...], HIGHEST) + rb_ref[...]
        logits = jnp.where(lane < N_EXPERTS, logits, NEG_BIG)
        m1 = jnp.max(logits, axis=-1, keepdims=True)
        i1 = jnp.min(jnp.where(logits == m1, lane, 128.0), axis=-1, keepdims=True)
        rest = jnp.where(lane == i1, NEG_BIG, logits)
        m2 = jnp.max(rest, axis=-1, keepdims=True)
        i2 = jnp.min(jnp.where(rest == m2, lane, 128.0), axis=-1, keepdims=True)
        e2 = jnp.exp(m2 - m1)
        comb_ref[...] = jnp.where(lane == i1, 1.0 / (1.0 + e2), 0.0) + jnp.where(lane == i2, e2 / (1.0 + e2), 0.0)
        chosen = jnp.where(lane == i1, 1.0, 0.0) + jnp.where(lane == i2, 1.0, 0.0)
        r = lax.broadcasted_iota(jnp.int32, (tm, tm), 0)
        c = lax.broadcasted_iota(jnp.int32, (tm, tm), 1)
        earlier = jnp.where(c < r, 1.0, 0.0).astype(BF16)
        in_first = lax.broadcasted_iota(jnp.int32, (tm, 128), 0) < half
        cnt_a = jnp.sum(chosen[:half], axis=0, keepdims=True)
        rank = _dot(earlier, chosen.astype(BF16))
        rank = jnp.where(chosen > 0.5, jnp.where(in_first, rank, rank - cnt_a), -1.0)
        lane8 = lax.broadcasted_iota(jnp.int32, (8, 128), 1)
        row8 = lax.broadcasted_iota(jnp.int32, (8, 128), 0)
        rank_ref[...] = _nt(jnp.where(lane8 == row8, 1.0, 0.0).astype(F32), rank, HIGHEST)
        cnt_ref[0:1, :] = cnt_a
        cnt_ref[1:2, :] = jnp.sum(chosen[half:], axis=0, keepdims=True)

    e_f = e.astype(F32)
    n_a = jnp.sum(jnp.where(lane[0:1, :] == e_f, cnt_ref[0:1, :], 0.0)).astype(jnp.int32)
    n_b = jnp.sum(jnp.where(lane[0:1, :] == e_f, cnt_ref[1:2, :], 0.0)).astype(jnp.int32)
    rank_row = rank_ref[pl.ds(e, 1), :]
    weight = jnp.sum(jnp.where(lane == e_f, comb_ref[...], 0.0), axis=-1, keepdims=True)

    def expert(xs):
        hidden = _silu(_dot(xs, w1_ref[0])) * _dot(xs, w3_ref[0])
        return _dot(hidden.astype(BF16), w2_ref[0]).astype(BF16)

    def half_blocks(cap):
        slot = lax.broadcasted_iota(jnp.int32, (cap, half), 0).astype(F32)
        picks = [jnp.where(rank_row[:, g * half:(g + 1) * half] == slot, 1.0, 0.0).astype(BF16) for g in range(2)]
        xs = jnp.concatenate([_dot(picks[g], xb_ref[g * half:(g + 1) * half, :]) for g in range(2)], axis=0)
        out = expert(xs.astype(BF16))
        for g in range(2):
            o_ref[g * half:(g + 1) * half, :] += (_tn(picks[g], out[g * cap:(g + 1) * cap])
                                                  * weight[g * half:(g + 1) * half])

    def whole_tile_blocks():
        second = lax.broadcasted_iota(jnp.int32, (1, tm), 1) >= half
        rank_all = jnp.where(second & (rank_row >= 0.0), rank_row + n_a.astype(F32), rank_row)

        def body(s, carry):
            slot = lax.broadcasted_iota(jnp.int32, (sb, tm), 0).astype(F32) + (s * sb).astype(F32)
            pick = jnp.where(rank_all == slot, 1.0, 0.0).astype(BF16)
            o_ref[...] += _tn(pick, expert(_dot(pick, xb_ref[...]).astype(BF16))) * weight
            return carry

        lax.fori_loop(0, lax.div(n_a + n_b + sb - 1, sb), body, 0)

    most = jnp.maximum(n_a, n_b)

    @pl.when(most <= sb_small // 2)
    def _():
        half_blocks(sb_small // 2)

    @pl.when((most > sb_small // 2) & (most <= sb // 2))
    def _():
        half_blocks(sb // 2)

    @pl.when(most > sb // 2)
    def _():
        whole_tile_blocks()

    @pl.when(e == pl.num_programs(1) - 1)
    def _():
        o_ref[...] = _layer_norm(DN_ALPHA * x_ref[...] + o_ref[...], lng_ref[...], lnb_ref[...])


def _moe(xf, router_w, router_b, w1, w3, w2, ln_g, ln_b, tm=1024, sb=320, sb_small=256):
    n, d = xf.shape
    ne, _, fe = w1.shape
    tm = min(tm, n)
    rw = jnp.pad(router_w, ((0, 0), (0, 128 - ne)))
    rb = jnp.pad(router_b, (0, 128 - ne)).reshape(1, 128)
    row = lambda x: x.reshape(1, -1)
    return pl.pallas_call(
        functools.partial(_moe_kernel, sb=sb, sb_small=sb_small),
        grid=(n // tm, ne),
        in_specs=[pl.BlockSpec((tm, d), lambda i, e: (i, 0)),
                  pl.BlockSpec((d, 128), lambda i, e: (0, 0)),
                  pl.BlockSpec((1, 128), lambda i, e: (0, 0)),
                  pl.BlockSpec((1, d, fe), lambda i, e: (e, 0, 0)),
                  pl.BlockSpec((1, d, fe), lambda i, e: (e, 0, 0)),
                  pl.BlockSpec((1, fe, d), lambda i, e: (e, 0, 0)),
                  pl.BlockSpec((1, d), lambda i, e: (0, 0)),
                  pl.BlockSpec((1, d), lambda i, e: (0, 0))],
        out_specs=pl.BlockSpec((tm, d), lambda i, e: (i, 0)),
        out_shape=jax.ShapeDtypeStruct((n, d), F32),
        scratch_shapes=[pltpu.VMEM((tm, 128), F32), pltpu.VMEM((8, tm), F32), pltpu.VMEM((2, 128), F32),
                        pltpu.VMEM((tm, d), BF16)],
        compiler_params=pltpu.CompilerParams(dimension_semantics=("parallel", "arbitrary"),
                                             vmem_limit_bytes=MOE_VMEM_LIMIT),
        name="moe",
    )(xf, rw, rb, w1, w3, w2, row(ln_g), row(ln_b))


def _mixer_layer(xf, bsz, seq, layer, w_in, w_gate_all, sinks, mu, w0, w_up, a0, a_up, g_up, k_k, k_a, r_k,
                 gn_g, gn_b, f_bias, gate_bias, w_branch, w_out, ln_g, ln_b):
    p2 = _matmul(xf, _layout_w_in(w_in[layer]).astype(BF16), tm=512)
    p3 = p2.reshape(bsz, seq, P_WIDTH)
    o_swa = _swa(p3, sinks)
    o_fox = _fox(p3, _fox_c(p3, f_bias))
    o_dsa = _dsa(p3)
    r, lw, k2, v, kn, b_, g, bonus = _rwkv_pre(p3, mu, w0, w_up, a0, a_up, g_up, k_k, k_a, r_k)
    y = _rwkv_chunk(r, lw, k2, v, kn, b_)
    n = bsz * seq
    flat = lambda t: t.reshape(n, t.shape[-1])
    return _merge(xf, flat(o_swa), flat(y), flat(g), flat(bonus), flat(o_fox), flat(o_dsa), gn_g, gn_b,
                  gate_bias, w_gate_all, layer, w_branch.astype(BF16), w_out.astype(BF16), ln_g, ln_b)


def kernel(x, w_in, swa_sinks, rwkv_mu, rwkv_w0, rwkv_w_up, rwkv_a0, rwkv_a_up, rwkv_g_up, rwkv_k_k, rwkv_k_a,
           rwkv_r_k, rwkv_gn_g, rwkv_gn_b, fox_f_bias, gate_bias, w_branch, w_out, ln_g, ln_b, ffn_w1, ffn_w3,
           ffn_w2, router_w, router_b, exp_w1, exp_w3, exp_w2):
    bsz, seq, d = x.shape
    xf = x.reshape(bsz * seq, d)
    w_gate_all = jnp.swapaxes(w_in[:, :, IN_OFF_GATE:], 1, 2).astype(BF16)
    for layer in range(DEPTH):
        xf = _mixer_layer(xf, bsz, seq, layer, w_in, w_gate_all, swa_sinks[layer], rwkv_mu[layer], rwkv_w0[layer],
                              rwkv_w_up[layer], rwkv_a0[layer], rwkv_a_up[layer], rwkv_g_up[layer],
                              rwkv_k_k[layer], rwkv_k_a[layer], rwkv_r_k[layer], rwkv_gn_g[layer],
                              rwkv_gn_b[layer], fox_f_bias[layer], gate_bias[layer], w_branch[layer],
                              w_out[layer], ln_g[layer, 0], ln_b[layer, 0])
        j = layer // 2
        if layer % 2 == 0:
            xf = _ffn(xf, ffn_w1[j].astype(BF16), ffn_w3[j].astype(BF16), ffn_w2[j].astype(BF16),
                      ln_g[layer, 1], ln_b[layer, 1])
        else:
            xf = _moe(xf, router_w[j], router_b[j], exp_w1[j].astype(BF16), exp_w3[j].astype(BF16),
                      exp_w2[j].astype(BF16), ln_g[layer, 1], ln_b[layer, 1])
    return xf.reshape(bsz, seq, d)
```

```python
import functools

import jax
import jax.numpy as jnp
from jax import lax
from jax.experimental import pallas as pl
from jax.experimental.pallas import tpu as pltpu

F32 = jnp.float32
BF16 = jnp.bfloat16
HIGHEST = lax.Precision.HIGHEST

D_MODEL = 1024
DEPTH = 2
CHUNK = 64
HEAD_DIM = 64
CHUNK_SHIFT = 6
HEAD_SHIFT = 6
SWA_HEADS = 8
SWA_KV_HEADS = 2
SWA_GROUP = SWA_HEADS // SWA_KV_HEADS
SWA_WINDOW = 128
SWA_WIN_CHUNKS = SWA_WINDOW // CHUNK
RWKV_HEADS = 4
RWKV_W = RWKV_HEADS * HEAD_DIM
RWKV_GN_EPS = 64e-5
FOX_HEADS = 4
DSA_HEADS = 4
IDX_HEADS = 4
IDX_DIM = 64
DSA_TOPK_MAX = 256
D_FF = 2816
N_EXPERTS = 8
D_FF_EXPERT = 1408
N_BRANCHES = 4
DN_ALPHA = (2 * DEPTH) ** 0.25
LN_EPS = 1e-5
ATTN_SCALE = HEAD_DIM ** -0.5
LOG2E = 1.4426950408889634
IDX_SCALE = IDX_DIM ** -0.5
IDX_W_SCALE = IDX_HEADS ** -0.5
MIX_WIDTHS = (SWA_HEADS * HEAD_DIM, RWKV_W, FOX_HEADS * HEAD_DIM, DSA_HEADS * HEAD_DIM)

COL_RWKV = 0
COL_SWA_Q = 1024
COL_FOX_KV = 1536
COL_DSA_QQ = 2048
COL_FOX_Q = 2560
COL_SWA_KV = 2816
COL_DSA_KV = 3072
COL_MISC = 3328
P_WIDTH = 3584

NEG_BIG = -1e30
INT_MIN = -(2 ** 31)
VMEM_LIMIT = 48 * 1024 * 1024
MOE_VMEM_LIMIT = 58 * 1024 * 1024


IN_OFF_SWA, IN_OFF_RWKV, IN_OFF_FOX, IN_OFF_DSA, IN_OFF_GATE = 0, 768, 1792, 2564, 3272
D_IN = IN_OFF_GATE + N_BRANCHES * D_MODEL


def _layout_w_in(w):
    d = w.shape[0]
    z = lambda n: jnp.zeros((d, n), w.dtype)
    swa, rwkv, fox, dsa = IN_OFF_SWA, IN_OFF_RWKV, IN_OFF_FOX, IN_OFF_DSA
    parts = [
        w[:, rwkv:rwkv + 1024],
        w[:, swa:swa + 512],
        w[:, fox + 256:fox + 768],
        w[:, dsa:dsa + 256], w[:, dsa + 384:dsa + 640],
        w[:, fox:fox + 256],
        w[:, swa + 512:swa + 768],
        w[:, dsa + 256:dsa + 384], w[:, dsa + 640:dsa + 704], z(64),
        w[:, fox + 768:fox + 772], w[:, dsa + 704:dsa + 708], z(120),
        z(128),
    ]
    return jnp.concatenate(parts, axis=1)


def _cparams(sem):
    return pltpu.CompilerParams(dimension_semantics=sem, vmem_limit_bytes=VMEM_LIMIT)


def _nt(a, b, precision=None):
    return lax.dot_general(a, b, (((1,), (1,)), ((), ())), precision=precision,
                           preferred_element_type=F32)


def _tn(a, b, precision=None):
    return lax.dot_general(a, b, (((0,), (0,)), ((), ())), precision=precision,
                           preferred_element_type=F32)


def _dot(a, b, precision=None):
    return jnp.dot(a, b, precision=precision, preferred_element_type=F32)


def _sigmoid(x):
    return 1.0 / (1.0 + jnp.exp(-x))


def _layer_norm(z, g, b):
    mu = jnp.mean(z, axis=-1, keepdims=True)
    zc = z - mu
    var = jnp.mean(zc * zc, axis=-1, keepdims=True)
    return zc * lax.rsqrt(var + LN_EPS) * g + b


def _skewed(n, first, second):
    out = []
    staged = first(0)
    for h in range(n):
        nxt = first(h + 1) if h + 1 < n else None
        out.append(second(h, staged))
        staged = nxt
    return out


def _paired_blocks(n, body, carry):
    carry = lax.fori_loop(0, lax.div(n, 2), lambda t, c: body((2 * t, 2 * t + 1), c), carry)
    return lax.cond(lax.rem(n, 2) == 1, lambda c: body((n - 1,), c), lambda c: c, carry)


def _head_block_ones(n, scale):
    r = lax.broadcasted_iota(jnp.int32, (n, n), 0) >> HEAD_SHIFT
    c = lax.broadcasted_iota(jnp.int32, (n, n), 1) >> HEAD_SHIFT
    return jnp.where(r == c, scale, 0.0).astype(BF16)


def _pieces(x):
    hi = x.astype(BF16)
    rest = x - hi.astype(F32)
    mid = rest.astype(BF16)
    return hi, mid, (rest - mid.astype(F32)).astype(BF16)


def _head_reduce(x, head_ones):
    hi, mid, low = _pieces(x)
    return _dot(hi, head_ones) + _dot(mid, head_ones) + _dot(low, head_ones)


def _ones_dot(ones, x):
    hi, mid, low = _pieces(x)
    return _dot(ones, hi) + _dot(ones, mid) + _dot(ones, low)


def _dot_hi_lo(a, b):
    a_hi = a.astype(BF16)
    a_lo = (a - a_hi.astype(F32)).astype(BF16)
    b_hi = b.astype(BF16)
    b_lo = (b - b_hi.astype(F32)).astype(BF16)
    return _dot(a_hi, b_hi) + _dot(a_hi, b_lo) + _dot(a_lo, b_hi)


def _mm_kernel(a_ref, b_ref, o_ref):
    o_ref[...] = _dot(a_ref[...].astype(BF16), b_ref[...])


def _matmul(a, b, tm):
    m, k = a.shape
    n = b.shape[1]
    tm = min(tm, m)
    return pl.pallas_call(
        _mm_kernel,
        grid=(m // tm,),
        in_specs=[pl.BlockSpec((tm, k), lambda i: (i, 0)),
                  pl.BlockSpec((k, n), lambda i: (0, 0), pipeline_mode=pl.Buffered(1))],
        out_specs=pl.BlockSpec((tm, n), lambda i: (i, 0)),
        out_shape=jax.ShapeDtypeStruct((m, n), F32),
        compiler_params=_cparams(("parallel",)),
        name="in_proj",
    )(a, b)


def _swa_kernel(sink_ref, q_ref, kv_ref, o_ref, *, tq):
    i = pl.program_id(1)
    win = tq + SWA_WINDOW
    s0 = pl.multiple_of(jnp.maximum(i * tq - SWA_WINDOW, 0), SWA_WINDOW)
    k_chunk = (s0 + lax.broadcasted_iota(jnp.int32, (win, tq), 0)) >> CHUNK_SHIFT
    q_chunk = (i * tq + lax.broadcasted_iota(jnp.int32, (win, tq), 1)) >> CHUNK_SHIFT
    bias = jnp.where(k_chunk <= q_chunk,
                     jnp.where(k_chunk >= q_chunk - SWA_WIN_CHUNKS, 0.0, NEG_BIG), NEG_BIG)
    ks = [kv_ref[0, pl.ds(s0, win), hk * HEAD_DIM:(hk + 1) * HEAD_DIM].astype(BF16)
          for hk in range(SWA_KV_HEADS)]
    kv_w = SWA_KV_HEADS * HEAD_DIM
    vs = [kv_ref[0, pl.ds(s0, win), kv_w + hk * HEAD_DIM:kv_w + (hk + 1) * HEAD_DIM].astype(BF16)
          for hk in range(SWA_KV_HEADS)]

    def logits(h):
        q = (q_ref[0, :, h * HEAD_DIM:(h + 1) * HEAD_DIM] * (ATTN_SCALE * LOG2E)).astype(BF16)
        return _nt(ks[h // SWA_GROUP], q) + bias

    def finish(h, s):
        sink = sink_ref[h] * LOG2E
        m = jnp.maximum(jnp.max(s, axis=0, keepdims=True), sink)
        e = jnp.exp2(s - m)
        denom = jnp.sum(e, axis=0, keepdims=True) + jnp.exp2(sink - m)
        return _tn(vs[h // SWA_GROUP], e.astype(BF16)) / denom

    o_ref[0] = jnp.concatenate(_skewed(SWA_HEADS, logits, finish), axis=0).T


def _swa(p3, sinks, tq=512):
    b, s, _ = p3.shape
    tq = min(tq, s // 2)
    kv_w = 2 * SWA_KV_HEADS * HEAD_DIM
    return pl.pallas_call(
        functools.partial(_swa_kernel, tq=tq),
        grid=(b, s // tq),
        in_specs=[pl.BlockSpec(memory_space=pltpu.SMEM),
                  pl.BlockSpec((1, tq, 512), lambda bi, i: (bi, i, COL_SWA_Q // 512)),
                  pl.BlockSpec((1, s, kv_w), lambda bi, i: (bi, 0, COL_SWA_KV // kv_w))],
        out_specs=pl.BlockSpec((1, tq, 512), lambda bi, i: (bi, i, 0)),
        out_shape=jax.ShapeDtypeStruct((b, s, 512), F32),
        compiler_params=_cparams(("parallel", "parallel")),
        name="swa",
    )(sinks, p3, p3)


def _fox_c_kernel(f_ref, bias_ref, c_ref, *, blk):
    s = f_ref.shape[1]
    r = lax.broadcasted_iota(jnp.int32, (blk, blk), 0)
    c = lax.broadcasted_iota(jnp.int32, (blk, blk), 1)
    tri = jnp.where(c <= r, 1.0, 0.0).astype(BF16)
    carry = jnp.zeros((1, 128), F32)
    for j in range(s // blk):
        x = f_ref[0, j * blk:(j + 1) * blk, :] + bias_ref[...]
        log_f = jnp.minimum(x, 0.0) - jnp.log1p(jnp.exp(-jnp.abs(x)))
        cs = _ones_dot(tri, log_f) + carry
        carry = cs[blk - 1:blk, :]
        for h in range(FOX_HEADS):
            c_ref[0, h, j * blk:(j + 1) * blk, :] = jnp.broadcast_to(cs[:, h:h + 1] * LOG2E, (blk, 128))


def _fox_c(p3, f_bias):
    b, s, _ = p3.shape
    bias_row = jnp.pad(f_bias, (0, 128 - FOX_HEADS)).reshape(1, 128)
    return pl.pallas_call(
        functools.partial(_fox_c_kernel, blk=256),
        grid=(b,),
        in_specs=[pl.BlockSpec((1, s, 128), lambda bi: (bi, 0, COL_MISC // 128)),
                  pl.BlockSpec((1, 128), lambda bi: (0, 0))],
        out_specs=pl.BlockSpec((1, FOX_HEADS, s, 128), lambda bi: (bi, 0, 0, 0)),
        out_shape=jax.ShapeDtypeStruct((b, FOX_HEADS, s, 128), F32),
        compiler_params=_cparams(("parallel",)),
        name="fox_cumsum",
    )(p3, bias_row)


def _fox_kernel(q_ref, k_ref, c_ref, o_ref, lg_ref, acc_ref, *, tq):
    i = pl.program_id(1)
    tk = tq
    v_col = FOX_HEADS * HEAD_DIM
    key_i = lax.broadcasted_iota(jnp.int32, (tk, tq), 0)
    qry_i = lax.broadcasted_iota(jnp.int32, (tk, tq), 1)
    diag_bias = jnp.where(key_i <= qry_i, 0.0, NEG_BIG)
    q = [(q_ref[0, :, h * HEAD_DIM:(h + 1) * HEAD_DIM] * (ATTN_SCALE * LOG2E)).astype(BF16)
         for h in range(FOX_HEADS)]

    def logits_blocks(blocks, maxes, masked=False):
        starts = [pl.multiple_of(j * tk, tk) for j in blocks]
        dots = [[_nt(k_ref[0, pl.ds(ks, tk), h * HEAD_DIM:(h + 1) * HEAD_DIM].astype(BF16), q[h])
                 for h in range(FOX_HEADS)] for ks in starts]
        maxes = list(maxes)
        for b, ks in enumerate(starts):
            for h in range(FOX_HEADS):
                c_k = c_ref[0, h, pl.ds(ks, tk), :]
                s = dots[b][h] - jnp.concatenate([c_k] * (tq // 128), axis=1)
                if masked:
                    s = s + diag_bias
                lg_ref[h, pl.ds(ks, tk), :] = s
                maxes[h] = jnp.maximum(maxes[h], jnp.max(s, axis=0, keepdims=True))
        return tuple(maxes)

    maxes = tuple(jnp.full((1, tq), NEG_BIG, F32) for _ in range(FOX_HEADS))
    maxes = _paired_blocks(i, logits_blocks, maxes)
    maxes = logits_blocks((i,), maxes, masked=True)
    acc_ref[...] = jnp.zeros_like(acc_ref)

    def attend_blocks(blocks, sums):
        starts = [pl.multiple_of(j * tk, tk) for j in blocks]
        sums = list(sums)
        for h in range(FOX_HEADS):
            update = None
            for ks in starts:
                p = jnp.exp2(lg_ref[h, pl.ds(ks, tk), :] - maxes[h])
                v = k_ref[0, pl.ds(ks, tk), v_col + h * HEAD_DIM:v_col + (h + 1) * HEAD_DIM].astype(BF16)
                pv = _tn(v, p.astype(BF16))
                update = pv if update is None else update + pv
                sums[h] = sums[h] + jnp.sum(p, axis=0, keepdims=True)
            acc_ref[h] += update
        return tuple(sums)

    sums = _paired_blocks(i + 1, attend_blocks, tuple(jnp.zeros((1, tq), F32) for _ in range(FOX_HEADS)))
    out_t = jnp.concatenate([acc_ref[h] / sums[h] for h in range(FOX_HEADS)], axis=0)
    o_ref[0] = out_t.T


def _fox(p3, c, tq=256):
    b, s, _ = p3.shape
    w = FOX_HEADS * HEAD_DIM
    return pl.pallas_call(
        functools.partial(_fox_kernel, tq=tq),
        grid=(b, s // tq),
        in_specs=[pl.BlockSpec((1, tq, w), lambda bi, i: (bi, i, COL_FOX_Q // w)),
                  pl.BlockSpec((1, s, 2 * w), lambda bi, i: (bi, 0, COL_FOX_KV // (2 * w))),
                  pl.BlockSpec((1, FOX_HEADS, s, 128), lambda bi, i: (bi, 0, 0, 0))],
        out_specs=pl.BlockSpec((1, tq, w), lambda bi, i: (bi, i, 0)),
        out_shape=jax.ShapeDtypeStruct((b, s, w), F32),
        scratch_shapes=[pltpu.VMEM((FOX_HEADS, s, tq), F32), pltpu.VMEM((FOX_HEADS, HEAD_DIM, tq), F32)],
        compiler_params=_cparams(("parallel", "parallel")),
        name="fox",
    )(p3, p3, c)


def _key_to_score(key):
    return pltpu.bitcast(key ^ ((key >> 31) & 0x7FFFFFFF), F32)


def _bit_planes(score):
    bits = pltpu.bitcast(score, jnp.int32)
    image = bits ^ ((bits >> 31) | INT_MIN)
    rows = image.reshape(32, 8, image.shape[1])
    v = [rows[j] for j in range(32)]
    for d, mask in ((16, 0x0000FFFF), (8, 0x00FF00FF), (4, 0x0F0F0F0F), (2, 0x33333333), (1, 0x55555555)):
        for k in range(32):
            if k & d == 0:
                t = (v[k] ^ lax.shift_right_logical(v[k + d], d)) & mask
                v[k] = v[k] ^ t
                v[k + d] = v[k + d] ^ lax.shift_left(t, d)
    return v


def _dsa_kernel(qq_ref, misc_ref, kv_ref, o_ref, sc_ref, lg_ref, acc_ref, pl_ref, *, tq, kb, topk):
    i = pl.program_id(1)
    n_kb = lax.div((i + 1) * tq + kb - 1, kb)
    lane8 = lax.broadcasted_iota(jnp.int32, (8, 128), 1)
    row8 = lax.broadcasted_iota(jnp.int32, (8, 128), 0)
    pick = jnp.where((lane8 == row8 + 4) & (row8 < IDX_HEADS), 1.0, 0.0).astype(F32)
    w_t = _nt(pick, misc_ref[0], HIGHEST) * (IDX_W_SCALE * IDX_SCALE)
    q_chunk = (i * tq + lax.broadcasted_iota(jnp.int32, (1, tq), 1)) >> CHUNK_SHIFT
    qw = DSA_HEADS * HEAD_DIM
    q_idx = [qq_ref[0, :, qw + h * IDX_DIM:qw + (h + 1) * IDX_DIM].astype(BF16) for h in range(IDX_HEADS)]

    def score_blocks(blocks, carry):
        starts = [pl.multiple_of(j * kb, kb) for j in blocks]
        dots = []
        for ks in starts:
            k_idx = kv_ref[0, pl.ds(ks, kb), 2 * HEAD_DIM:2 * HEAD_DIM + IDX_DIM].astype(BF16)
            dots.append([_nt(k_idx, q_idx[h]) for h in range(IDX_HEADS)])
        for b, ks in enumerate(starts):
            score = jnp.maximum(dots[b][0], 0.0) * w_t[0:1, :]
            for h in range(1, IDX_HEADS):
                score = score + jnp.maximum(dots[b][h], 0.0) * w_t[h:h + 1, :]
            k_chunk = (ks + lax.broadcasted_iota(jnp.int32, (kb, tq), 0)) >> CHUNK_SHIFT
            score = jnp.where(k_chunk <= q_chunk, score, -jnp.inf)
            sc_ref[pl.ds(ks, kb), :] = score
            planes = _bit_planes(score)
            for t in range(32):
                pl_ref[blocks[b], t] = planes[t]
        return carry

    _paired_blocks(n_kb, score_blocks, 0)
    n_blocks = pl_ref.shape[0]

    def clear_planes(j, carry):
        pl_ref[j] = jnp.zeros(pl_ref.shape[1:], jnp.int32)
        return carry

    lax.fori_loop(n_kb, n_blocks, clear_planes, 0)

    def count(pred):
        def body(j, acc):
            blk = sc_ref[pl.ds(pl.multiple_of(j * kb, kb), kb), :]
            hit = jnp.where(pred(blk), 1, 0).astype(jnp.int32).reshape(kb // 8, 8, tq)
            parts = [hit[g] for g in range(kb // 8)]
            while len(parts) > 1:
                parts = [parts[g] + parts[g + 1] for g in range(0, len(parts), 2)]
            return acc + parts[0]
        acc = lax.fori_loop(0, n_kb, body, jnp.zeros((8, tq), jnp.int32))
        return jnp.sum(acc.astype(F32), axis=0, keepdims=True)

    def search_by_compares():
        zero = jnp.zeros((1, tq), F32)
        thr_key = jnp.where(count(lambda blk: blk >= zero) >= topk, 0, INT_MIN).astype(jnp.int32)

        def bit_step(bi, thr_key):
            cand = thr_key | lax.shift_left(jnp.int32(1), 30 - bi)
            cand_score = _key_to_score(cand)
            return jnp.where(count(lambda blk: blk >= cand_score) >= topk, cand, thr_key)

        thr_key = lax.fori_loop(0, 31, bit_step, thr_key)
        t = jnp.where(thr_key == INT_MIN, -jnp.inf, _key_to_score(thr_key))
        return t, count(lambda blk: blk > t)

    def search_by_planes():
        def bit_pass(t, carry):
            prefix, above, alive = carry
            ones = [alive[j] & pl_ref[j, t] for j in range(n_blocks)]
            parts = [lax.population_count(o) for o in ones]
            while len(parts) > 1:
                parts = [parts[g] + parts[g + 1] for g in range(0, len(parts), 2)]
            reach = above + jnp.sum(parts[0].astype(F32), axis=0, keepdims=True)
            take = reach >= topk
            prefix = jnp.where(take, prefix | lax.shift_left(jnp.int32(1), 31 - t), prefix)
            alive = tuple(jnp.where(take, ones[j], alive[j] ^ ones[j]) for j in range(n_blocks))
            return prefix, jnp.where(take, above, reach), alive

        init = (jnp.zeros((1, tq), jnp.int32), jnp.zeros((1, tq), F32),
                tuple(jnp.full((8, tq), -1, jnp.int32) for _ in range(n_blocks)))
        image, _, _ = lax.fori_loop(0, 32, bit_pass, init)
        return pltpu.bitcast(image ^ (~(image >> 31) | INT_MIN), F32)

    thr_planes = search_by_planes()
    n_above = count(lambda blk: blk > thr_planes)
    n_reach = count(lambda blk: blk >= thr_planes)
    confirmed = jnp.min(jnp.where(n_above < topk, jnp.where(n_reach >= topk, 1.0, 0.0), 0.0)) > 0.5
    thr, n_above = lax.cond(confirmed, lambda: (thr_planes, n_above), search_by_compares)
    need = topk - n_above

    r = lax.broadcasted_iota(jnp.int32, (kb, kb), 0)
    c = lax.broadcasted_iota(jnp.int32, (kb, kb), 1)
    earlier = jnp.where(c < r, 1.0, 0.0).astype(BF16)
    q_att = [(qq_ref[0, :, h * HEAD_DIM:(h + 1) * HEAD_DIM] * (ATTN_SCALE * LOG2E)).astype(BF16)
             for h in range(DSA_HEADS)]

    def logits_blocks(blocks, carry):
        ties_before, maxes = carry
        maxes = list(maxes)
        starts = [pl.multiple_of(j * kb, kb) for j in blocks]
        dots, ties = [], []
        for ks in starts:
            k = kv_ref[0, pl.ds(ks, kb), 0:HEAD_DIM].astype(BF16)
            dots.append([_nt(k, q_att[h]) for h in range(DSA_HEADS)])
            k_chunk = (ks + lax.broadcasted_iota(jnp.int32, (kb, tq), 0)) >> CHUNK_SHIFT
            ties.append(jnp.where(sc_ref[pl.ds(ks, kb), :] == thr, jnp.where(k_chunk <= q_chunk, 1.0, 0.0), 0.0))
        ranks = [_dot(earlier, tie.astype(BF16)) for tie in ties]
        for b, ks in enumerate(starts):
            tie_taken = jnp.where(ranks[b] + ties_before < need, ties[b], 0.0)
            bias = jnp.where(sc_ref[pl.ds(ks, kb), :] > thr, 0.0, jnp.where(tie_taken > 0.5, 0.0, NEG_BIG))
            ties_before = ties_before + jnp.sum(ties[b], axis=0, keepdims=True)
            for h in range(DSA_HEADS):
                s = dots[b][h] + bias
                lg_ref[h, pl.ds(ks, kb), :] = s
                maxes[h] = jnp.maximum(maxes[h], jnp.max(s, axis=0, keepdims=True))
        return ties_before, tuple(maxes)

    init = (jnp.zeros((1, tq), F32), tuple(jnp.full((1, tq), NEG_BIG, F32) for _ in range(DSA_HEADS)))
    _, maxes = _paired_blocks(n_kb, logits_blocks, init)

    acc_ref[...] = jnp.zeros_like(acc_ref)

    def attend_blocks(blocks, sums):
        starts = [pl.multiple_of(j * kb, kb) for j in blocks]
        vs = [kv_ref[0, pl.ds(ks, kb), HEAD_DIM:2 * HEAD_DIM].astype(BF16) for ks in starts]
        sums = list(sums)
        for h in range(DSA_HEADS):
            update = None
            for b, ks in enumerate(starts):
                p = jnp.exp2(lg_ref[h, pl.ds(ks, kb), :] - maxes[h])
                pv = _tn(vs[b], p.astype(BF16))
                update = pv if update is None else update + pv
                sums[h] = sums[h] + jnp.sum(p, axis=0, keepdims=True)
            acc_ref[h] += update
        return tuple(sums)

    sums = _paired_blocks(n_kb, attend_blocks, tuple(jnp.zeros((1, tq), F32) for _ in range(DSA_HEADS)))
    out_t = jnp.concatenate([acc_ref[h] / sums[h] for h in range(DSA_HEADS)], axis=0)
    o_ref[0] = out_t.T


def _dsa(p3, tq=256, kb=256):
    b, s, _ = p3.shape
    assert kb == 8 * 32, "a key block is the 32 row groups of one bit-plane word"
    topk = min(DSA_TOPK_MAX, s // 4)
    return pl.pallas_call(
        functools.partial(_dsa_kernel, tq=tq, kb=kb, topk=topk),
        grid=(b, s // tq),
        in_specs=[pl.BlockSpec((1, tq, 512), lambda bi, i: (bi, i, COL_DSA_QQ // 512)),
                  pl.BlockSpec((1, tq, 128), lambda bi, i: (bi, i, COL_MISC // 128)),
                  pl.BlockSpec((1, s, 256), lambda bi, i: (bi, 0, COL_DSA_KV // 256))],
        out_specs=pl.BlockSpec((1, tq, 256), lambda bi, i: (bi, i, 0)),
        out_shape=jax.ShapeDtypeStruct((b, s, 256), F32),
        scratch_shapes=[pltpu.VMEM((s, tq), F32), pltpu.VMEM((DSA_HEADS, s, tq), F32),
                        pltpu.VMEM((DSA_HEADS, HEAD_DIM, tq), F32), pltpu.VMEM((s // kb, 32, 8, tq), jnp.int32)],
        compiler_params=_cparams(("parallel", "parallel")),
        name="dsa",
    )(p3, p3, p3)


def _rwkv_pre_kernel(p_ref, mu_ref, w0_ref, wup_ref, a0_ref, aup_ref, gup_ref, kk_ref, ka_ref, rk_ref,
                     r_o, lw_o, k_o, v_o, kn_o, b_o, g_o, bonus_o, last_ref, *, tt):
    t = pl.program_id(1)

    @pl.when(t == 0)
    def _():
        last_ref[...] = jnp.zeros_like(last_ref)

    p = p_ref[0]
    row = lax.broadcasted_iota(jnp.int32, p.shape, 0)
    p_prev = jnp.where(row == 0, last_ref[...], pltpu.roll(p, 1, axis=0))
    last_ref[...] = p[tt - 1:tt, :]
    ps = p + mu_ref[...] * (p_prev - p)
    w = RWKV_W
    r, k, v = ps[:, 0:w], ps[:, w:2 * w], ps[:, 2 * w:3 * w]
    w_lo, a_lo, g_lo = ps[:, 3 * w:3 * w + 64], ps[:, 3 * w + 64:3 * w + 128], ps[:, 3 * w + 128:3 * w + 256]
    ww = w0_ref[...] + _dot_hi_lo(jnp.tanh(w_lo), wup_ref[...])
    softplus_neg = jnp.maximum(-ww, 0.0) + jnp.log1p(jnp.exp(-jnp.abs(ww)))
    log_w = -jnp.exp(-softplus_neg - 0.5)
    a = _sigmoid(a0_ref[...] + _dot_hi_lo(a_lo, aup_ref[...]))
    g = _dot_hi_lo(_sigmoid(g_lo), gup_ref[...])
    head_sum = _head_block_ones(w, 1.0)
    kn = k * kk_ref[...]
    kn = kn * lax.rsqrt(_head_reduce(kn * kn, head_sum) + 1e-12)
    k2 = k * (1.0 + (a - 1.0) * ka_ref[...])
    bonus = _head_reduce(r * k2 * rk_ref[...], head_sum) * v
    r_o[0] = r
    lw_o[0] = log_w
    k_o[0] = k2
    v_o[0] = v
    kn_o[0] = kn
    b_o[0] = kn * a
    g_o[0] = g
    bonus_o[0] = bonus


def _rwkv_pre(p3, mu, w0, w_up, a0, a_up, g_up, k_k, k_a, r_k, tt=512):
    b, s, _ = p3.shape
    w = RWKV_W
    row = lambda x: x.reshape(1, -1)
    full = lambda shape: pl.BlockSpec(shape, lambda bi, t: (0,) * len(shape))
    out = jax.ShapeDtypeStruct((b, s, w), F32)
    return pl.pallas_call(
        functools.partial(_rwkv_pre_kernel, tt=tt),
        grid=(b, s // tt),
        in_specs=[pl.BlockSpec((1, tt, 1024), lambda bi, t: (bi, t, COL_RWKV // 1024)),
                  full((1, 1024)), full((1, w)), full((64, w)), full((1, w)), full((64, w)),
                  full((128, w)), full((1, w)), full((1, w)), full((1, w))],
        out_specs=[pl.BlockSpec((1, tt, w), lambda bi, t: (bi, t, 0))] * 8,
        out_shape=[out] * 8,
        scratch_shapes=[pltpu.VMEM((1, 1024), F32)],
        compiler_params=_cparams(("parallel", "arbitrary")),
        name="rwkv_pre",
    )(p3, row(mu), row(w0), w_up, row(a0), a_up, g_up, row(k_k), row(k_a), row(r_k))


def _bdot(a, b):
    return _dot(a.astype(BF16), b.astype(BF16))


def _bnt(a, b):
    return _nt(a.astype(BF16), b.astype(BF16))


def _btn(a, b):
    return _tn(a.astype(BF16), b.astype(BF16))


def _rwkv_chunk_kernel(r_ref, lw_ref, k_ref, v_ref, kn_ref, b_ref, y_ref, s_ref, *, cs, nch):
    c = pl.program_id(1)

    @pl.when(c == 0)
    def _():
        s_ref[...] = jnp.zeros_like(s_ref)

    rows = cs * min(nch, 4)
    big_row = lax.broadcasted_iota(jnp.int32, (rows, rows), 0)
    big_col = lax.broadcasted_iota(jnp.int32, (rows, rows), 1)
    same_chunk = (big_row >> CHUNK_SHIFT) == (big_col >> CHUNK_SHIFT)
    tri = jnp.where(same_chunk, jnp.where(big_col <= big_row, 1.0, 0.0), 0.0).astype(BF16)
    pw = 2 * HEAD_DIM
    row = lax.broadcasted_iota(jnp.int32, (cs, pw), 0)
    lane = lax.broadcasted_iota(jnp.int32, (cs, pw), 1)
    col = lane & (HEAD_DIM - 1)
    first_head = lane < HEAD_DIM
    incl = col <= row
    strict = col < row
    eye = jnp.where(col == row, 1.0, 0.0).astype(F32)

    def block_diag(x):
        return jnp.concatenate([jnp.where(first_head, x, 0.0), jnp.where(first_head, 0.0, x)], axis=0)

    def diag_blocks(full):
        return jnp.where(first_head, full[:cs], full[cs:])

    lw = lw_ref[0]
    cum = jnp.concatenate([_ones_dot(tri, lw[g:g + rows]) for g in range(0, cs * nch, rows)],
                          axis=0)
    e_incl = jnp.exp(cum)
    e_neg = jnp.exp(-cum)
    abar_all = -kn_ref[0] * jnp.exp(cum - lw)
    rbar_all = r_ref[0] * e_incl
    bt_all = b_ref[0] * e_neg
    kt_all = k_ref[0] * e_neg
    v_all = v_ref[0]
    n_pairs = RWKV_HEADS // 2
    probs = [(j, p) for j in range(nch) for p in range(n_pairs)]
    cut = lambda t, jp: t[jp[0] * cs:(jp[0] + 1) * cs, jp[1] * pw:(jp[1] + 1) * pw]
    each = lambda fn: {jp: fn(jp) for jp in probs}
    abar, rbar = each(lambda jp: cut(abar_all, jp)), each(lambda jp: cut(rbar_all, jp))
    bt, kt, v = each(lambda jp: cut(bt_all, jp)), each(lambda jp: cut(kt_all, jp)), each(lambda jp: cut(v_all, jp))
    p_last = each(lambda jp: e_incl[(jp[0] + 1) * cs - 1:(jp[0] + 1) * cs, jp[1] * pw:(jp[1] + 1) * pw])
    ar = each(lambda jp: jnp.concatenate([abar[jp], rbar[jp]], axis=0))
    ar_b = each(lambda jp: _bnt(ar[jp], block_diag(bt[jp])))
    ar_k = each(lambda jp: _bnt(ar[jp], block_diag(kt[jp])))
    a_ab = each(lambda jp: jnp.where(strict, ar_b[jp][:cs], 0.0))
    a_ak = each(lambda jp: jnp.where(strict, ar_k[jp][:cs], 0.0))
    a_rb = each(lambda jp: jnp.where(incl, ar_b[jp][cs:], 0.0))
    a_rk = each(lambda jp: jnp.where(incl, ar_k[jp][cs:], 0.0))
    v_bd = each(lambda jp: block_diag(v[jp]))
    ak_v = each(lambda jp: _bdot(a_ak[jp], v_bd[jp]))
    inv = each(lambda jp: eye)
    for level in range(cs.bit_length() - 1):
        same_2b = (row >> (level + 1)) == (col >> (level + 1))
        lower_left = same_2b & (((row >> level) & 1) == 1) & (((col >> level) & 1) == 0)
        c_blocks = each(lambda jp: jnp.where(lower_left, a_ab[jp], 0.0))
        half_done = each(lambda jp: _bdot(inv[jp], block_diag(c_blocks[jp])))
        inv = each(lambda jp: inv[jp] + _bdot(half_done[jp], block_diag(inv[jp])))
    a_hat = each(lambda jp: _bdot(inv[jp], block_diag(abar[jp])))
    u_hat = each(lambda jp: _bdot(inv[jp], block_diag(ak_v[jp])))
    r_hat = each(lambda jp: rbar[jp] + _bdot(a_rb[jp], block_diag(a_hat[jp])))
    y_hat = each(lambda jp: _bdot(a_rb[jp], block_diag(u_hat[jp])) + _bdot(a_rk[jp], v_bd[jp]))
    g_mat = each(lambda jp: eye * p_last[jp] + diag_blocks(_btn(a_hat[jp], bt[jp] * p_last[jp])))
    h_mat = each(lambda jp: diag_blocks(_btn(jnp.concatenate([u_hat[jp], v[jp]], axis=0),
                                             jnp.concatenate([bt[jp], kt[jp]], axis=0) * p_last[jp])))
    states = [s_ref[p] for p in range(n_pairs)]
    for j in range(nch):
        for p in range(n_pairs):
            y_ref[0, j * cs:(j + 1) * cs, p * pw:(p + 1) * pw] = (
                _bnt(r_hat[j, p], block_diag(states[p])) + y_hat[j, p])
        states = [_dot_hi_lo(states[p], block_diag(g_mat[j, p])) + h_mat[j, p] for p in range(n_pairs)]
    for p in range(n_pairs):
        s_ref[p] = states[p]


def _rwkv_chunk(r, lw, k, v, kn, b_, cs=CHUNK, nch=8):
    b, s, w = r.shape
    spec = pl.BlockSpec((1, cs * nch, w), lambda bi, c: (bi, c, 0))
    return pl.pallas_call(
        functools.partial(_rwkv_chunk_kernel, cs=cs, nch=nch),
        grid=(b, s // (cs * nch)),
        in_specs=[spec] * 6,
        out_specs=spec,
        out_shape=jax.ShapeDtypeStruct((b, s, w), F32),
        scratch_shapes=[pltpu.VMEM((RWKV_HEADS // 2, HEAD_DIM, 2 * HEAD_DIM), F32)],
        compiler_params=_cparams(("parallel", "arbitrary")),
        name="rwkv_chunk",
    )(r, lw, k, v, kn, b_)


def _merge_kernel(x_ref, swa_ref, y_ref, g_ref, bonus_ref, fox_ref, dsa_ref, gng_ref, gnb_ref,
                  gbias_ref, wg_ref, wb_ref, wo_ref, lng_ref, lnb_ref, o_ref):
    y = y_ref[...]
    xb = x_ref[...].astype(BF16)
    head_mean = _head_block_ones(RWKV_W, 1.0 / HEAD_DIM)
    yc = y - _head_reduce(y, head_mean)
    yv = _head_reduce(yc * yc, head_mean)
    o_rwkv = (yc * lax.rsqrt(yv + RWKV_GN_EPS) * gng_ref[...] + gnb_ref[...] + bonus_ref[...]) * g_ref[...]
    branches = (swa_ref[...], o_rwkv, fox_ref[...], dsa_ref[...])
    merged = jnp.zeros(o_ref.shape, F32)
    off = 0
    for i, o in enumerate(branches):
        width = MIX_WIDTHS[i]
        proj = _dot(o.astype(BF16), wb_ref[off:off + width, :])
        gate = _sigmoid(_nt(xb, wg_ref[0, i * D_MODEL:(i + 1) * D_MODEL, :]) + gbias_ref[i:i + 1, :])
        merged = merged + gate * proj
        off += width
    y_out = _dot(merged.astype(BF16), wo_ref[...])
    o_ref[...] = _layer_norm(DN_ALPHA * x_ref[...] + y_out, lng_ref[...], lnb_ref[...])


def _merge(xf, o_swa, y, g, bonus, o_fox, o_dsa, gn_g, gn_b, gate_bias, w_gate_all, layer, w_branch, w_out,
           ln_g, ln_b, tm=512):
    n, d = xf.shape
    w_gate = w_gate_all
    row = lambda x: x.reshape(1, -1)
    tok = lambda wdt: pl.BlockSpec((tm, wdt), lambda i: (i, 0))
    full = lambda shape: pl.BlockSpec(shape, lambda i: (0,) * len(shape))
    return pl.pallas_call(
        _merge_kernel,
        grid=(n // tm,),
        in_specs=[tok(d), tok(512), tok(256), tok(256), tok(256), tok(256), tok(256),
                  full((1, 256)), full((1, 256)), full((N_BRANCHES, d)),
                  pl.BlockSpec((1, N_BRANCHES * d, d), lambda i: (layer, 0, 0)),
                  full((sum(MIX_WIDTHS), d)), full((d, d)), full((1, d)), full((1, d))],
        out_specs=tok(d),
        out_shape=jax.ShapeDtypeStruct((n, d), F32),
        compiler_params=_cparams(("parallel",)),
        name="merge",
    )(xf, o_swa, y, g, bonus, o_fox, o_dsa, row(gn_g), row(gn_b), gate_bias, w_gate, w_branch, w_out,
      row(ln_g), row(ln_b))


def _silu(x):
    return x * _sigmoid(x)


def _ffn_kernel(x_ref, w1_ref, w3_ref, w2_ref, lng_ref, lnb_ref, o_ref, *, tf):
    xb = x_ref[...].astype(BF16)
    y = None
    for f in range(0, w1_ref.shape[1], tf):
        hidden = _silu(_dot(xb, w1_ref[:, f:f + tf])) * _dot(xb, w3_ref[:, f:f + tf])
        part = _dot(hidden.astype(BF16), w2_ref[f:f + tf, :])
        y = part if y is None else y + part
    o_ref[...] = _layer_norm(DN_ALPHA * x_ref[...] + y, lng_ref[...], lnb_ref[...])


def _ffn(xf, w1, w3, w2, ln_g, ln_b, tm=512, tf=1408):
    n, d = xf.shape
    ff = w1.shape[1]
    row = lambda x: x.reshape(1, -1)
    tok = pl.BlockSpec((tm, d), lambda i: (i, 0))
    resident = lambda shape: pl.BlockSpec(shape, lambda i: (0, 0), pipeline_mode=pl.Buffered(1))
    return pl.pallas_call(
        functools.partial(_ffn_kernel, tf=tf),
        grid=(n // tm,),
        in_specs=[tok, resident((d, ff)), resident((d, ff)), resident((ff, d)),
                  resident((1, d)), resident((1, d))],
        out_specs=tok,
        out_shape=jax.ShapeDtypeStruct((n, d), F32),
        compiler_params=_cparams(("parallel",)),
        name="ffn",
    )(xf, w1, w3, w2, row(ln_g), row(ln_b))


def _moe_kernel(x_ref, rw_ref, rb_ref, w1_ref, w3_ref, w2_ref, lng_ref, lnb_ref, o_ref,
                comb_ref, rank_ref, cnt_ref, xb_ref, *, sb, sb_small):
    e = pl.program_id(1)
    tm = x_ref.shape[0]
    half = tm // 2
    lane = lax.broadcasted_iota(jnp.int32, (tm, 128), 1).astype(F32)

    @pl.when(e == 0)
    def _():
        o_ref[...] = jnp.zeros_like(o_ref)
        xb_ref[...] = x_ref[...].astype(BF16)
        logits = _dot(x_ref[...], rw_ref[...], HIGHEST) + rb_ref[...]
        logits = jnp.where(lane < N_EXPERTS, logits, NEG_BIG)
        m1 = jnp.max(logits, axis=-1, keepdims=True)
        i1 = jnp.min(jnp.where(logits == m1, lane, 128.0), axis=-1, keepdims=True)
        rest = jnp.where(lane == i1, NEG_BIG, logits)
        m2 = jnp.max(rest, axis=-1, keepdims=True)
        i2 = jnp.min(jnp.where(rest == m2, lane, 128.0), axis=-1, keepdims=True)
        e2 = jnp.exp(m2 - m1)
        comb_ref[...] = jnp.where(lane == i1, 1.0 / (1.0 + e2), 0.0) + jnp.where(lane == i2, e2 / (1.0 + e2), 0.0)
        chosen = jnp.where(lane == i1, 1.0, 0.0) + jnp.where(lane == i2, 1.0, 0.0)
        r = lax.broadcasted_iota(jnp.int32, (half, half), 0)
        c = lax.broadcasted_iota(jnp.int32, (half, half), 1)
        earlier = jnp.where(c < r, 1.0, 0.0).astype(BF16)
        halves = [chosen[:half], chosen[half:]]
        rank = jnp.concatenate([_dot(earlier, h.astype(BF16)) for h in halves], axis=0)
        rank = jnp.where(chosen > 0.5, rank, -1.0)
        lane8 = lax.broadcasted_iota(jnp.int32, (8, 128), 1)
        row8 = lax.broadcasted_iota(jnp.int32, (8, 128), 0)
        rank_ref[...] = _nt(jnp.where(lane8 == row8, 1.0, 0.0).astype(F32), rank, HIGHEST)
        for g in range(2):
            cnt_ref[g:g + 1, :] = jnp.sum(halves[g], axis=0, keepdims=True)

    e_f = e.astype(F32)
    n_a = jnp.sum(jnp.where(lane[0:1, :] == e_f, cnt_ref[0:1, :], 0.0)).astype(jnp.int32)
    n_b = jnp.sum(jnp.where(lane[0:1, :] == e_f, cnt_ref[1:2, :], 0.0)).astype(jnp.int32)
    rank_row = rank_ref[pl.ds(e, 1), :]
    weight = jnp.sum(jnp.where(lane == e_f, comb_ref[...], 0.0), axis=-1, keepdims=True)

    def expert(xs):
        hidden = _silu(_dot(xs, w1_ref[0])) * _dot(xs, w3_ref[0])
        return _dot(hidden.astype(BF16), w2_ref[0]).astype(BF16)

    def half_blocks(cap):
        slot = lax.broadcasted_iota(jnp.int32, (cap, half), 0).astype(F32)
        picks = [jnp.where(rank_row[:, g * half:(g + 1) * half] == slot, 1.0, 0.0).astype(BF16) for g in range(2)]
        xs = jnp.concatenate([_dot(picks[g], xb_ref[g * half:(g + 1) * half, :]) for g in range(2)], axis=0)
        out = expert(xs.astype(BF16))
        for g in range(2):
            o_ref[g * half:(g + 1) * half, :] += (_tn(picks[g], out[g * cap:(g + 1) * cap])
                                                  * weight[g * half:(g + 1) * half])

    def whole_tile_blocks():
        second = lax.broadcasted_iota(jnp.int32, (1, tm), 1) >= half
        rank_all = jnp.where(second & (rank_row >= 0.0), rank_row + n_a.astype(F32), rank_row)

        def body(s, carry):
            slot = lax.broadcasted_iota(jnp.int32, (sb, tm), 0).astype(F32) + (s * sb).astype(F32)
            pick = jnp.where(rank_all == slot, 1.0, 0.0).astype(BF16)
            o_ref[...] += _tn(pick, expert(_dot(pick, xb_ref[...]).astype(BF16))) * weight
            return carry

        lax.fori_loop(0, lax.div(n_a + n_b + sb - 1, sb), body, 0)

    most = jnp.maximum(n_a, n_b)

    @pl.when(most <= sb_small // 2)
    def _():
        half_blocks(sb_small // 2)

    @pl.when((most > sb_small // 2) & (most <= sb // 2))
    def _():
        half_blocks(sb // 2)

    @pl.when(most > sb // 2)
    def _():
        whole_tile_blocks()

    @pl.when(e == pl.num_programs(1) - 1)
    def _():
        o_ref[...] = _layer_norm(DN_ALPHA * x_ref[...] + o_ref[...], lng_ref[...], lnb_ref[...])


def _moe(xf, router_w, router_b, w1, w3, w2, ln_g, ln_b, tm=1024, sb=320, sb_small=256):
    n, d = xf.shape
    ne, _, fe = w1.shape
    tm = min(tm, n)
    rw = jnp.pad(router_w, ((0, 0), (0, 128 - ne)))
    rb = jnp.pad(router_b, (0, 128 - ne)).reshape(1, 128)
    row = lambda x: x.reshape(1, -1)
    return pl.pallas_call(
        functools.partial(_moe_kernel, sb=sb, sb_small=sb_small),
        grid=(n // tm, ne),
        in_specs=[pl.BlockSpec((tm, d), lambda i, e: (i, 0)),
                  pl.BlockSpec((d, 128), lambda i, e: (0, 0)),
                  pl.BlockSpec((1, 128), lambda i, e: (0, 0)),
                  pl.BlockSpec((1, d, fe), lambda i, e: (e, 0, 0)),
                  pl.BlockSpec((1, d, fe), lambda i, e: (e, 0, 0)),
                  pl.BlockSpec((1, fe, d), lambda i, e: (e, 0, 0)),
                  pl.BlockSpec((1, d), lambda i, e: (0, 0)),
                  pl.BlockSpec((1, d), lambda i, e: (0, 0))],
        out_specs=pl.BlockSpec((tm, d), lambda i, e: (i, 0)),
        out_shape=jax.ShapeDtypeStruct((n, d), F32),
        scratch_shapes=[pltpu.VMEM((tm, 128), F32), pltpu.VMEM((8, tm), F32), pltpu.VMEM((2, 128), F32),
                        pltpu.VMEM((tm, d), BF16)],
        compiler_params=pltpu.CompilerParams(dimension_semantics=("parallel", "arbitrary"),
                                             vmem_limit_bytes=MOE_VMEM_LIMIT),
        name="moe",
    )(xf, rw, rb, w1, w3, w2, row(ln_g), row(ln_b))


def _mixer_layer(xf, bsz, seq, layer, w_in, w_gate_all, sinks, mu, w0, w_up, a0, a_up, g_up, k_k, k_a, r_k,
                 gn_g, gn_b, f_bias, gate_bias, w_branch, w_out, ln_g, ln_b):
    p2 = _matmul(xf, _layout_w_in(w_in[layer]).astype(BF16), tm=512)
    p3 = p2.reshape(bsz, seq, P_WIDTH)
    o_swa = _swa(p3, sinks)
    o_fox = _fox(p3, _fox_c(p3, f_bias))
    o_dsa = _dsa(p3)
    r, lw, k2, v, kn, b_, g, bonus = _rwkv_pre(p3, mu, w0, w_up, a0, a_up, g_up, k_k, k_a, r_k)
    y = _rwkv_chunk(r, lw, k2, v, kn, b_)
    n = bsz * seq
    flat = lambda t: t.reshape(n, t.shape[-1])
    return _merge(xf, flat(o_swa), flat(y), flat(g), flat(bonus), flat(o_fox), flat(o_dsa), gn_g, gn_b,
                  gate_bias, w_gate_all, layer, w_branch.astype(BF16), w_out.astype(BF16), ln_g, ln_b)


def kernel(x, w_in, swa_sinks, rwkv_mu, rwkv_w0, rwkv_w_up, rwkv_a0, rwkv_a_up, rwkv_g_up, rwkv_k_k, rwkv_k_a,
           rwkv_r_k, rwkv_gn_g, rwkv_gn_b, fox_f_bias, gate_bias, w_branch, w_out, ln_g, ln_b, ffn_w1, ffn_w3,
           ffn_w2, router_w, router_b, exp_w1, exp_w3, exp_w2):
    bsz, seq, d = x.shape
    xf = x.reshape(bsz * seq, d)
    w_gate_all = jnp.swapaxes(w_in[:, :, IN_OFF_GATE:], 1, 2).astype(BF16)
    for layer in range(DEPTH):
        xf = _mixer_layer(xf, bsz, seq, layer, w_in, w_gate_all, swa_sinks[layer], rwkv_mu[layer], rwkv_w0[layer],
                              rwkv_w_up[layer], rwkv_a0[layer], rwkv_a_up[layer], rwkv_g_up[layer],
                              rwkv_k_k[layer], rwkv_k_a[layer], rwkv_r_k[layer], rwkv_gn_g[layer],
                              rwkv_gn_b[layer], fox_f_bias[layer], gate_bias[layer], w_branch[layer],
                              w_out[layer], ln_g[layer, 0], ln_b[layer, 0])
        j = layer // 2
        if layer % 2 == 0:
            xf = _ffn(xf, ffn_w1[j].astype(BF16), ffn_w3[j].astype(BF16), ffn_w2[j].astype(BF16),
                      ln_g[layer, 1], ln_b[layer, 1])
        else:
            xf = _moe(xf, router_w[j], router_b[j], exp_w1[j].astype(BF16), exp_w3[j].astype(BF16),
                      exp_w2[j].astype(BF16), ln_g[layer, 1], ln_b[layer, 1])
    return xf.reshape(bsz, seq, d)
```

```python
import functools

import jax
import jax.numpy as jnp
from jax import lax
from jax.experimental import pallas as pl
from jax.experimental.pallas import tpu as pltpu

F32 = jnp.float32
BF16 = jnp.bfloat16
HIGHEST = lax.Precision.HIGHEST

D_MODEL = 1024
DEPTH = 2
CHUNK = 64
HEAD_DIM = 64
CHUNK_SHIFT = 6
HEAD_SHIFT = 6
SWA_HEADS = 8
SWA_KV_HEADS = 2
SWA_GROUP = SWA_HEADS // SWA_KV_HEADS
SWA_WINDOW = 128
SWA_WIN_CHUNKS = SWA_WINDOW // CHUNK
RWKV_HEADS = 4
RWKV_W = RWKV_HEADS * HEAD_DIM
RWKV_GN_EPS = 64e-5
FOX_HEADS = 4
DSA_HEADS = 4
IDX_HEADS = 4
IDX_DIM = 64
DSA_TOPK_MAX = 256
D_FF = 2816
N_EXPERTS = 8
D_FF_EXPERT = 1408
N_BRANCHES = 4
DN_ALPHA = (2 * DEPTH) ** 0.25
LN_EPS = 1e-5
ATTN_SCALE = HEAD_DIM ** -0.5
LOG2E = 1.4426950408889634
IDX_SCALE = IDX_DIM ** -0.5
IDX_W_SCALE = IDX_HEADS ** -0.5
MIX_WIDTHS = (SWA_HEADS * HEAD_DIM, RWKV_W, FOX_HEADS * HEAD_DIM, DSA_HEADS * HEAD_DIM)

COL_RWKV = 0
COL_SWA_Q = 1024
COL_FOX_KV = 1536
COL_DSA_QQ = 2048
COL_FOX_Q = 2560
COL_SWA_KV = 2816
COL_DSA_KV = 3072
COL_MISC = 3328
P_WIDTH = 3584

NEG_BIG = -1e30
INT_MIN = -(2 ** 31)
VMEM_LIMIT = 48 * 1024 * 1024
MOE_VMEM_LIMIT = 58 * 1024 * 1024


IN_OFF_SWA, IN_OFF_RWKV, IN_OFF_FOX, IN_OFF_DSA, IN_OFF_GATE = 0, 768, 1792, 2564, 3272
D_IN = IN_OFF_GATE + N_BRANCHES * D_MODEL


def _layout_w_in(w):
    d = w.shape[0]
    z = lambda n: jnp.zeros((d, n), w.dtype)
    swa, rwkv, fox, dsa = IN_OFF_SWA, IN_OFF_RWKV, IN_OFF_FOX, IN_OFF_DSA
    parts = [
        w[:, rwkv:rwkv + 1024],
        w[:, swa:swa + 512],
        w[:, fox + 256:fox + 768],
        w[:, dsa:dsa + 256], w[:, dsa + 384:dsa + 640],
        w[:, fox:fox + 256],
        w[:, swa + 512:swa + 768],
        w[:, dsa + 256:dsa + 384], w[:, dsa + 640:dsa + 704], z(64),
        w[:, fox + 768:fox + 772], w[:, dsa + 704:dsa + 708], z(120),
        z(128),
    ]
    return jnp.concatenate(parts, axis=1)


def _cparams(sem):
    return pltpu.CompilerParams(dimension_semantics=sem, vmem_limit_bytes=VMEM_LIMIT)


def _nt(a, b, precision=None):
    return lax.dot_general(a, b, (((1,), (1,)), ((), ())), precision=precision,
                           preferred_element_type=F32)


def _tn(a, b, precision=None):
    return lax.dot_general(a, b, (((0,), (0,)), ((), ())), precision=precision,
                           preferred_element_type=F32)


def _dot(a, b, precision=None):
    return jnp.dot(a, b, precision=precision, preferred_element_type=F32)


def _sigmoid(x):
    return 1.0 / (1.0 + jnp.exp(-x))


def _layer_norm(z, g, b):
    mu = jnp.mean(z, axis=-1, keepdims=True)
    zc = z - mu
    var = jnp.mean(zc * zc, axis=-1, keepdims=True)
    return zc * lax.rsqrt(var + LN_EPS) * g + b


def _skewed(n, first, second):
    out = []
    staged = first(0)
    for h in range(n):
        nxt = first(h + 1) if h + 1 < n else None
        out.append(second(h, staged))
        staged = nxt
    return out


def _paired_blocks(n, body, carry):
    carry = lax.fori_loop(0, lax.div(n, 2), lambda t, c: body((2 * t, 2 * t + 1), c), carry)
    return lax.cond(lax.rem(n, 2) == 1, lambda c: body((n - 1,), c), lambda c: c, carry)


def _head_block_ones(n, scale):
    r = lax.broadcasted_iota(jnp.int32, (n, n), 0) >> HEAD_SHIFT
    c = lax.broadcasted_iota(jnp.int32, (n, n), 1) >> HEAD_SHIFT
    return jnp.where(r == c, scale, 0.0).astype(BF16)


def _pieces(x):
    hi = x.astype(BF16)
    rest = x - hi.astype(F32)
    mid = rest.astype(BF16)
    return hi, mid, (rest - mid.astype(F32)).astype(BF16)


def _head_reduce(x, head_ones):
    hi, mid, low = _pieces(x)
    return _dot(hi, head_ones) + _dot(mid, head_ones) + _dot(low, head_ones)


def _ones_dot(ones, x):
    hi, mid, low = _pieces(x)
    return _dot(ones, hi) + _dot(ones, mid) + _dot(ones, low)


def _dot_hi_lo(a, b):
    a_hi = a.astype(BF16)
    a_lo = (a - a_hi.astype(F32)).astype(BF16)
    b_hi = b.astype(BF16)
    b_lo = (b - b_hi.astype(F32)).astype(BF16)
    return _dot(a_hi, b_hi) + _dot(a_hi, b_lo) + _dot(a_lo, b_hi)


def _mm_kernel(a_ref, b_ref, o_ref):
    o_ref[...] = _dot(a_ref[...].astype(BF16), b_ref[...])


def _matmul(a, b, tm):
    m, k = a.shape
    n = b.shape[1]
    tm = min(tm, m)
    return pl.pallas_call(
        _mm_kernel,
        grid=(m // tm,),
        in_specs=[pl.BlockSpec((tm, k), lambda i: (i, 0)),
                  pl.BlockSpec((k, n), lambda i: (0, 0), pipeline_mode=pl.Buffered(1))],
        out_specs=pl.BlockSpec((tm, n), lambda i: (i, 0)),
        out_shape=jax.ShapeDtypeStruct((m, n), F32),
        compiler_params=_cparams(("parallel",)),
        name="in_proj",
    )(a, b)


def _swa_kernel(sink_ref, q_ref, kv_ref, o_ref, *, tq):
    i = pl.program_id(1)
    win = tq + SWA_WINDOW
    s0 = pl.multiple_of(jnp.maximum(i * tq - SWA_WINDOW, 0), SWA_WINDOW)
    k_chunk = (s0 + lax.broadcasted_iota(jnp.int32, (win, tq), 0)) >> CHUNK_SHIFT
    q_chunk = (i * tq + lax.broadcasted_iota(jnp.int32, (win, tq), 1)) >> CHUNK_SHIFT
    bias = jnp.where(k_chunk <= q_chunk,
                     jnp.where(k_chunk >= q_chunk - SWA_WIN_CHUNKS, 0.0, NEG_BIG), NEG_BIG)
    ks = [kv_ref[0, pl.ds(s0, win), hk * HEAD_DIM:(hk + 1) * HEAD_DIM].astype(BF16)
          for hk in range(SWA_KV_HEADS)]
    kv_w = SWA_KV_HEADS * HEAD_DIM
    vs = [kv_ref[0, pl.ds(s0, win), kv_w + hk * HEAD_DIM:kv_w + (hk + 1) * HEAD_DIM].astype(BF16)
          for hk in range(SWA_KV_HEADS)]

    def logits(h):
        q = (q_ref[0, :, h * HEAD_DIM:(h + 1) * HEAD_DIM] * (ATTN_SCALE * LOG2E)).astype(BF16)
        return _nt(ks[h // SWA_GROUP], q) + bias

    def finish(h, s):
        sink = sink_ref[h] * LOG2E
        m = jnp.maximum(jnp.max(s, axis=0, keepdims=True), sink)
        e = jnp.exp2(s - m)
        denom = jnp.sum(e, axis=0, keepdims=True) + jnp.exp2(sink - m)
        return _tn(vs[h // SWA_GROUP], e.astype(BF16)) / denom

    o_ref[0] = jnp.concatenate(_skewed(SWA_HEADS, logits, finish), axis=0).T


def _swa(p3, sinks, tq=512):
    b, s, _ = p3.shape
    tq = min(tq, s // 2)
    kv_w = 2 * SWA_KV_HEADS * HEAD_DIM
    return pl.pallas_call(
        functools.partial(_swa_kernel, tq=tq),
        grid=(b, s // tq),
        in_specs=[pl.BlockSpec(memory_space=pltpu.SMEM),
                  pl.BlockSpec((1, tq, 512), lambda bi, i: (bi, i, COL_SWA_Q // 512)),
                  pl.BlockSpec((1, s, kv_w), lambda bi, i: (bi, 0, COL_SWA_KV // kv_w))],
        out_specs=pl.BlockSpec((1, tq, 512), lambda bi, i: (bi, i, 0)),
        out_shape=jax.ShapeDtypeStruct((b, s, 512), F32),
        compiler_params=_cparams(("parallel", "parallel")),
        name="swa",
    )(sinks, p3, p3)


def _fox_c_kernel(f_ref, bias_ref, c_ref, *, blk):
    s = f_ref.shape[1]
    r = lax.broadcasted_iota(jnp.int32, (blk, blk), 0)
    c = lax.broadcasted_iota(jnp.int32, (blk, blk), 1)
    tri = jnp.where(c <= r, 1.0, 0.0).astype(BF16)
    carry = jnp.zeros((1, 128), F32)
    for j in range(s // blk):
        x = f_ref[0, j * blk:(j + 1) * blk, :] + bias_ref[...]
        log_f = jnp.minimum(x, 0.0) - jnp.log1p(jnp.exp(-jnp.abs(x)))
        cs = _ones_dot(tri, log_f) + carry
        carry = cs[blk - 1:blk, :]
        for h in range(FOX_HEADS):
            c_ref[0, h, j * blk:(j + 1) * blk, :] = jnp.broadcast_to(cs[:, h:h + 1] * LOG2E, (blk, 128))


def _fox_c(p3, f_bias):
    b, s, _ = p3.shape
    bias_row = jnp.pad(f_bias, (0, 128 - FOX_HEADS)).reshape(1, 128)
    return pl.pallas_call(
        functools.partial(_fox_c_kernel, blk=256),
        grid=(b,),
        in_specs=[pl.BlockSpec((1, s, 128), lambda bi: (bi, 0, COL_MISC // 128)),
                  pl.BlockSpec((1, 128), lambda bi: (0, 0))],
        out_specs=pl.BlockSpec((1, FOX_HEADS, s, 128), lambda bi: (bi, 0, 0, 0)),
        out_shape=jax.ShapeDtypeStruct((b, FOX_HEADS, s, 128), F32),
        compiler_params=_cparams(("parallel",)),
        name="fox_cumsum",
    )(p3, bias_row)


def _fox_kernel(q_ref, k_ref, c_ref, o_ref, lg_ref, acc_ref, *, tq):
    i = pl.program_id(1)
    tk = tq
    v_col = FOX_HEADS * HEAD_DIM
    key_i = lax.broadcasted_iota(jnp.int32, (tk, tq), 0)
    qry_i = lax.broadcasted_iota(jnp.int32, (tk, tq), 1)
    diag_bias = jnp.where(key_i <= qry_i, 0.0, NEG_BIG)
    q = [(q_ref[0, :, h * HEAD_DIM:(h + 1) * HEAD_DIM] * (ATTN_SCALE * LOG2E)).astype(BF16)
         for h in range(FOX_HEADS)]

    def logits_blocks(blocks, maxes, masked=False):
        starts = [pl.multiple_of(j * tk, tk) for j in blocks]
        dots = [[_nt(k_ref[0, pl.ds(ks, tk), h * HEAD_DIM:(h + 1) * HEAD_DIM].astype(BF16), q[h])
                 for h in range(FOX_HEADS)] for ks in starts]
        maxes = list(maxes)
        for b, ks in enumerate(starts):
            for h in range(FOX_HEADS):
                c_k = c_ref[0, h, pl.ds(ks, tk), :]
                s = dots[b][h] - jnp.concatenate([c_k] * (tq // 128), axis=1)
                if masked:
                    s = s + diag_bias
                lg_ref[h, pl.ds(ks, tk), :] = s
                maxes[h] = jnp.maximum(maxes[h], jnp.max(s, axis=0, keepdims=True))
        return tuple(maxes)

    maxes = tuple(jnp.full((1, tq), NEG_BIG, F32) for _ in range(FOX_HEADS))
    maxes = _paired_blocks(i, logits_blocks, maxes)
    maxes = logits_blocks((i,), maxes, masked=True)
    acc_ref[...] = jnp.zeros_like(acc_ref)

    def attend_blocks(blocks, sums):
        starts = [pl.multiple_of(j * tk, tk) for j in blocks]
        sums = list(sums)
        for h in range(FOX_HEADS):
            update = None
            for ks in starts:
                p = jnp.exp2(lg_ref[h, pl.ds(ks, tk), :] - maxes[h])
                v = k_ref[0, pl.ds(ks, tk), v_col + h * HEAD_DIM:v_col + (h + 1) * HEAD_DIM].astype(BF16)
                pv = _tn(v, p.astype(BF16))
                update = pv if update is None else update + pv
                sums[h] = sums[h] + jnp.sum(p, axis=0, keepdims=True)
            acc_ref[h] += update
        return tuple(sums)

    sums = _paired_blocks(i + 1, attend_blocks, tuple(jnp.zeros((1, tq), F32) for _ in range(FOX_HEADS)))
    out_t = jnp.concatenate([acc_ref[h] / sums[h] for h in range(FOX_HEADS)], axis=0)
    o_ref[0] = out_t.T


def _fox(p3, c, tq=512):
    b, s, _ = p3.shape
    w = FOX_HEADS * HEAD_DIM
    return pl.pallas_call(
        functools.partial(_fox_kernel, tq=tq),
        grid=(b, s // tq),
        in_specs=[pl.BlockSpec((1, tq, w), lambda bi, i: (bi, i, COL_FOX_Q // w)),
                  pl.BlockSpec((1, s, 2 * w), lambda bi, i: (bi, 0, COL_FOX_KV // (2 * w))),
                  pl.BlockSpec((1, FOX_HEADS, s, 128), lambda bi, i: (bi, 0, 0, 0))],
        out_specs=pl.BlockSpec((1, tq, w), lambda bi, i: (bi, i, 0)),
        out_shape=jax.ShapeDtypeStruct((b, s, w), F32),
        scratch_shapes=[pltpu.VMEM((FOX_HEADS, s, tq), F32), pltpu.VMEM((FOX_HEADS, HEAD_DIM, tq), F32)],
        compiler_params=_cparams(("parallel", "parallel")),
        name="fox",
    )(p3, p3, c)


def _key_to_score(key):
    return pltpu.bitcast(key ^ ((key >> 31) & 0x7FFFFFFF), F32)


def _bit_planes(score):
    bits = pltpu.bitcast(score, jnp.int32)
    image = bits ^ ((bits >> 31) | INT_MIN)
    rows = image.reshape(32, 8, image.shape[1])
    v = [rows[j] for j in range(32)]
    for d, mask in ((16, 0x0000FFFF), (8, 0x00FF00FF), (4, 0x0F0F0F0F), (2, 0x33333333), (1, 0x55555555)):
        for k in range(32):
            if k & d == 0:
                t = (v[k] ^ lax.shift_right_logical(v[k + d], d)) & mask
                v[k] = v[k] ^ t
                v[k + d] = v[k + d] ^ lax.shift_left(t, d)
    return v


def _dsa_kernel(qq_ref, misc_ref, kv_ref, o_ref, sc_ref, lg_ref, acc_ref, pl_ref, *, tq, kb, topk):
    i = pl.program_id(1)
    n_kb = lax.div((i + 1) * tq + kb - 1, kb)
    lane8 = lax.broadcasted_iota(jnp.int32, (8, 128), 1)
    row8 = lax.broadcasted_iota(jnp.int32, (8, 128), 0)
    pick = jnp.where((lane8 == row8 + 4) & (row8 < IDX_HEADS), 1.0, 0.0).astype(F32)
    w_t = _nt(pick, misc_ref[0], HIGHEST) * (IDX_W_SCALE * IDX_SCALE)
    q_chunk = (i * tq + lax.broadcasted_iota(jnp.int32, (1, tq), 1)) >> CHUNK_SHIFT
    qw = DSA_HEADS * HEAD_DIM
    q_idx = [qq_ref[0, :, qw + h * IDX_DIM:qw + (h + 1) * IDX_DIM].astype(BF16) for h in range(IDX_HEADS)]

    def score_blocks(blocks, carry):
        starts = [pl.multiple_of(j * kb, kb) for j in blocks]
        dots = []
        for ks in starts:
            k_idx = kv_ref[0, pl.ds(ks, kb), 2 * HEAD_DIM:2 * HEAD_DIM + IDX_DIM].astype(BF16)
            dots.append([_nt(k_idx, q_idx[h]) for h in range(IDX_HEADS)])
        for b, ks in enumerate(starts):
            score = jnp.maximum(dots[b][0], 0.0) * w_t[0:1, :]
            for h in range(1, IDX_HEADS):
                score = score + jnp.maximum(dots[b][h], 0.0) * w_t[h:h + 1, :]
            k_chunk = (ks + lax.broadcasted_iota(jnp.int32, (kb, tq), 0)) >> CHUNK_SHIFT
            score = jnp.where(k_chunk <= q_chunk, score, -jnp.inf)
            sc_ref[pl.ds(ks, kb), :] = score
            planes = _bit_planes(score)
            for t in range(32):
                pl_ref[blocks[b], t] = planes[t]
        return carry

    _paired_blocks(n_kb, score_blocks, 0)
    n_blocks = pl_ref.shape[0]

    def clear_planes(j, carry):
        pl_ref[j] = jnp.zeros(pl_ref.shape[1:], jnp.int32)
        return carry

    lax.fori_loop(n_kb, n_blocks, clear_planes, 0)

    def count(pred):
        def body(j, acc):
            blk = sc_ref[pl.ds(pl.multiple_of(j * kb, kb), kb), :]
            hit = jnp.where(pred(blk), 1, 0).astype(jnp.int32).reshape(kb // 8, 8, tq)
            parts = [hit[g] for g in range(kb // 8)]
            while len(parts) > 1:
                parts = [parts[g] + parts[g + 1] for g in range(0, len(parts), 2)]
            return acc + parts[0]
        acc = lax.fori_loop(0, n_kb, body, jnp.zeros((8, tq), jnp.int32))
        return jnp.sum(acc.astype(F32), axis=0, keepdims=True)

    def search_by_compares():
        zero = jnp.zeros((1, tq), F32)
        thr_key = jnp.where(count(lambda blk: blk >= zero) >= topk, 0, INT_MIN).astype(jnp.int32)

        def bit_step(bi, thr_key):
            cand = thr_key | lax.shift_left(jnp.int32(1), 30 - bi)
            cand_score = _key_to_score(cand)
            return jnp.where(count(lambda blk: blk >= cand_score) >= topk, cand, thr_key)

        thr_key = lax.fori_loop(0, 31, bit_step, thr_key)
        t = jnp.where(thr_key == INT_MIN, -jnp.inf, _key_to_score(thr_key))
        return t, count(lambda blk: blk > t)

    def search_by_planes():
        def bit_pass(t, carry):
            prefix, above, alive = carry
            ones = [alive[j] & pl_ref[j, t] for j in range(n_blocks)]
            parts = [lax.population_count(o) for o in ones]
            while len(parts) > 1:
                parts = [parts[g] + parts[g + 1] for g in range(0, len(parts), 2)]
            reach = above + jnp.sum(parts[0].astype(F32), axis=0, keepdims=True)
            take = reach >= topk
            prefix = jnp.where(take, prefix | lax.shift_left(jnp.int32(1), 31 - t), prefix)
            alive = tuple(jnp.where(take, ones[j], alive[j] ^ ones[j]) for j in range(n_blocks))
            return prefix, jnp.where(take, above, reach), alive

        init = (jnp.zeros((1, tq), jnp.int32), jnp.zeros((1, tq), F32),
                tuple(jnp.full((8, tq), -1, jnp.int32) for _ in range(n_blocks)))
        image, _, _ = lax.fori_loop(0, 32, bit_pass, init)
        return pltpu.bitcast(image ^ (~(image >> 31) | INT_MIN), F32)

    thr_planes = search_by_planes()
    n_above = count(lambda blk: blk > thr_planes)
    n_reach = count(lambda blk: blk >= thr_planes)
    confirmed = jnp.min(jnp.where(n_above < topk, jnp.where(n_reach >= topk, 1.0, 0.0), 0.0)) > 0.5
    thr, n_above = lax.cond(confirmed, lambda: (thr_planes, n_above), search_by_compares)
    need = topk - n_above

    r = lax.broadcasted_iota(jnp.int32, (kb, kb), 0)
    c = lax.broadcasted_iota(jnp.int32, (kb, kb), 1)
    earlier = jnp.where(c < r, 1.0, 0.0).astype(BF16)
    q_att = [(qq_ref[0, :, h * HEAD_DIM:(h + 1) * HEAD_DIM] * (ATTN_SCALE * LOG2E)).astype(BF16)
             for h in range(DSA_HEADS)]

    def logits_blocks(blocks, carry):
        ties_before, maxes = carry
        maxes = list(maxes)
        starts = [pl.multiple_of(j * kb, kb) for j in blocks]
        dots, ties = [], []
        for ks in starts:
            k = kv_ref[0, pl.ds(ks, kb), 0:HEAD_DIM].astype(BF16)
            dots.append([_nt(k, q_att[h]) for h in range(DSA_HEADS)])
            k_chunk = (ks + lax.broadcasted_iota(jnp.int32, (kb, tq), 0)) >> CHUNK_SHIFT
            ties.append(jnp.where(sc_ref[pl.ds(ks, kb), :] == thr, jnp.where(k_chunk <= q_chunk, 1.0, 0.0), 0.0))
        ranks = [_dot(earlier, tie.astype(BF16)) for tie in ties]
        for b, ks in enumerate(starts):
            tie_taken = jnp.where(ranks[b] + ties_before < need, ties[b], 0.0)
            bias = jnp.where(sc_ref[pl.ds(ks, kb), :] > thr, 0.0, jnp.where(tie_taken > 0.5, 0.0, NEG_BIG))
            ties_before = ties_before + jnp.sum(ties[b], axis=0, keepdims=True)
            for h in range(DSA_HEADS):
                s = dots[b][h] + bias
                lg_ref[h, pl.ds(ks, kb), :] = s
                maxes[h] = jnp.maximum(maxes[h], jnp.max(s, axis=0, keepdims=True))
        return ties_before, tuple(maxes)

    init = (jnp.zeros((1, tq), F32), tuple(jnp.full((1, tq), NEG_BIG, F32) for _ in range(DSA_HEADS)))
    _, maxes = _paired_blocks(n_kb, logits_blocks, init)

    acc_ref[...] = jnp.zeros_like(acc_ref)

    def attend_blocks(blocks, sums):
        starts = [pl.multiple_of(j * kb, kb) for j in blocks]
        vs = [kv_ref[0, pl.ds(ks, kb), HEAD_DIM:2 * HEAD_DIM].astype(BF16) for ks in starts]
        sums = list(sums)
        for h in range(DSA_HEADS):
            update = None
            for b, ks in enumerate(starts):
                p = jnp.exp2(lg_ref[h, pl.ds(ks, kb), :] - maxes[h])
                pv = _tn(vs[b], p.astype(BF16))
                update = pv if update is None else update + pv
                sums[h] = sums[h] + jnp.sum(p, axis=0, keepdims=True)
            acc_ref[h] += update
        return tuple(sums)

    sums = _paired_blocks(n_kb, attend_blocks, tuple(jnp.zeros((1, tq), F32) for _ in range(DSA_HEADS)))
    out_t = jnp.concatenate([acc_ref[h] / sums[h] for h in range(DSA_HEADS)], axis=0)
    o_ref[0] = out_t.T


def _dsa(p3, tq=256, kb=256):
    b, s, _ = p3.shape
    assert kb == 8 * 32, "a key block is the 32 row groups of one bit-plane word"
    topk = min(DSA_TOPK_MAX, s // 4)
    return pl.pallas_call(
        functools.partial(_dsa_kernel, tq=tq, kb=kb, topk=topk),
        grid=(b, s // tq),
        in_specs=[pl.BlockSpec((1, tq, 512), lambda bi, i: (bi, i, COL_DSA_QQ // 512)),
                  pl.BlockSpec((1, tq, 128), lambda bi, i: (bi, i, COL_MISC // 128)),
                  pl.BlockSpec((1, s, 256), lambda bi, i: (bi, 0, COL_DSA_KV // 256))],
        out_specs=pl.BlockSpec((1, tq, 256), lambda bi, i: (bi, i, 0)),
        out_shape=jax.ShapeDtypeStruct((b, s, 256), F32),
        scratch_shapes=[pltpu.VMEM((s, tq), F32), pltpu.VMEM((DSA_HEADS, s, tq), F32),
                        pltpu.VMEM((DSA_HEADS, HEAD_DIM, tq), F32), pltpu.VMEM((s // kb, 32, 8, tq), jnp.int32)],
        compiler_params=_cparams(("parallel", "parallel")),
        name="dsa",
    )(p3, p3, p3)


def _rwkv_pre_kernel(p_ref, mu_ref, w0_ref, wup_ref, a0_ref, aup_ref, gup_ref, kk_ref, ka_ref, rk_ref,
                     r_o, lw_o, k_o, v_o, kn_o, b_o, g_o, bonus_o, last_ref, *, tt):
    t = pl.program_id(1)

    @pl.when(t == 0)
    def _():
        last_ref[...] = jnp.zeros_like(last_ref)

    p = p_ref[0]
    row = lax.broadcasted_iota(jnp.int32, p.shape, 0)
    p_prev = jnp.where(row == 0, last_ref[...], pltpu.roll(p, 1, axis=0))
    last_ref[...] = p[tt - 1:tt, :]
    ps = p + mu_ref[...] * (p_prev - p)
    w = RWKV_W
    r, k, v = ps[:, 0:w], ps[:, w:2 * w], ps[:, 2 * w:3 * w]
    w_lo, a_lo, g_lo = ps[:, 3 * w:3 * w + 64], ps[:, 3 * w + 64:3 * w + 128], ps[:, 3 * w + 128:3 * w + 256]
    ww = w0_ref[...] + _dot_hi_lo(jnp.tanh(w_lo), wup_ref[...])
    softplus_neg = jnp.maximum(-ww, 0.0) + jnp.log1p(jnp.exp(-jnp.abs(ww)))
    log_w = -jnp.exp(-softplus_neg - 0.5)
    a = _sigmoid(a0_ref[...] + _dot_hi_lo(a_lo, aup_ref[...]))
    g = _dot_hi_lo(_sigmoid(g_lo), gup_ref[...])
    head_sum = _head_block_ones(w, 1.0)
    kn = k * kk_ref[...]
    kn = kn * lax.rsqrt(_head_reduce(kn * kn, head_sum) + 1e-12)
    k2 = k * (1.0 + (a - 1.0) * ka_ref[...])
    bonus = _head_reduce(r * k2 * rk_ref[...], head_sum) * v
    r_o[0] = r
    lw_o[0] = log_w
    k_o[0] = k2
    v_o[0] = v
    kn_o[0] = kn
    b_o[0] = kn * a
    g_o[0] = g
    bonus_o[0] = bonus


def _rwkv_pre(p3, mu, w0, w_up, a0, a_up, g_up, k_k, k_a, r_k, tt=512):
    b, s, _ = p3.shape
    w = RWKV_W
    row = lambda x: x.reshape(1, -1)
    full = lambda shape: pl.BlockSpec(shape, lambda bi, t: (0,) * len(shape))
    out = jax.ShapeDtypeStruct((b, s, w), F32)
    return pl.pallas_call(
        functools.partial(_rwkv_pre_kernel, tt=tt),
        grid=(b, s // tt),
        in_specs=[pl.BlockSpec((1, tt, 1024), lambda bi, t: (bi, t, COL_RWKV // 1024)),
                  full((1, 1024)), full((1, w)), full((64, w)), full((1, w)), full((64, w)),
                  full((128, w)), full((1, w)), full((1, w)), full((1, w))],
        out_specs=[pl.BlockSpec((1, tt, w), lambda bi, t: (bi, t, 0))] * 8,
        out_shape=[out] * 8,
        scratch_shapes=[pltpu.VMEM((1, 1024), F32)],
        compiler_params=_cparams(("parallel", "arbitrary")),
        name="rwkv_pre",
    )(p3, row(mu), row(w0), w_up, row(a0), a_up, g_up, row(k_k), row(k_a), row(r_k))


def _bdot(a, b):
    return _dot(a.astype(BF16), b.astype(BF16))


def _bnt(a, b):
    return _nt(a.astype(BF16), b.astype(BF16))


def _btn(a, b):
    return _tn(a.astype(BF16), b.astype(BF16))


def _rwkv_chunk_kernel(r_ref, lw_ref, k_ref, v_ref, kn_ref, b_ref, y_ref, s_ref, *, cs, nch):
    c = pl.program_id(1)

    @pl.when(c == 0)
    def _():
        s_ref[...] = jnp.zeros_like(s_ref)

    rows = cs * min(nch, 4)
    big_row = lax.broadcasted_iota(jnp.int32, (rows, rows), 0)
    big_col = lax.broadcasted_iota(jnp.int32, (rows, rows), 1)
    same_chunk = (big_row >> CHUNK_SHIFT) == (big_col >> CHUNK_SHIFT)
    tri = jnp.where(same_chunk, jnp.where(big_col <= big_row, 1.0, 0.0), 0.0).astype(BF16)
    pw = 2 * HEAD_DIM
    row = lax.broadcasted_iota(jnp.int32, (cs, pw), 0)
    lane = lax.broadcasted_iota(jnp.int32, (cs, pw), 1)
    col = lane & (HEAD_DIM - 1)
    first_head = lane < HEAD_DIM
    incl = col <= row
    strict = col < row
    eye = jnp.where(col == row, 1.0, 0.0).astype(F32)

    def block_diag(x):
        return jnp.concatenate([jnp.where(first_head, x, 0.0), jnp.where(first_head, 0.0, x)], axis=0)

    def diag_blocks(full):
        return jnp.where(first_head, full[:cs], full[cs:])

    lw = lw_ref[0]
    cum = jnp.concatenate([_ones_dot(tri, lw[g:g + rows]) for g in range(0, cs * nch, rows)],
                          axis=0)
    e_incl = jnp.exp(cum)
    e_neg = jnp.exp(-cum)
    abar_all = -kn_ref[0] * jnp.exp(cum - lw)
    rbar_all = r_ref[0] * e_incl
    bt_all = b_ref[0] * e_neg
    kt_all = k_ref[0] * e_neg
    v_all = v_ref[0]
    n_pairs = RWKV_HEADS // 2
    probs = [(j, p) for j in range(nch) for p in range(n_pairs)]
    cut = lambda t, jp: t[jp[0] * cs:(jp[0] + 1) * cs, jp[1] * pw:(jp[1] + 1) * pw]
    each = lambda fn: {jp: fn(jp) for jp in probs}
    abar, rbar = each(lambda jp: cut(abar_all, jp)), each(lambda jp: cut(rbar_all, jp))
    bt, kt, v = each(lambda jp: cut(bt_all, jp)), each(lambda jp: cut(kt_all, jp)), each(lambda jp: cut(v_all, jp))
    p_last = each(lambda jp: e_incl[(jp[0] + 1) * cs - 1:(jp[0] + 1) * cs, jp[1] * pw:(jp[1] + 1) * pw])
    ar = each(lambda jp: jnp.concatenate([abar[jp], rbar[jp]], axis=0))
    ar_b = each(lambda jp: _bnt(ar[jp], block_diag(bt[jp])))
    ar_k = each(lambda jp: _bnt(ar[jp], block_diag(kt[jp])))
    a_ab = each(lambda jp: jnp.where(strict, ar_b[jp][:cs], 0.0))
    a_ak = each(lambda jp: jnp.where(strict, ar_k[jp][:cs], 0.0))
    a_rb = each(lambda jp: jnp.where(incl, ar_b[jp][cs:], 0.0))
    a_rk = each(lambda jp: jnp.where(incl, ar_k[jp][cs:], 0.0))
    v_bd = each(lambda jp: block_diag(v[jp]))
    ak_v = each(lambda jp: _bdot(a_ak[jp], v_bd[jp]))
    inv = each(lambda jp: eye)
    for level in range(cs.bit_length() - 1):
        same_2b = (row >> (level + 1)) == (col >> (level + 1))
        lower_left = same_2b & (((row >> level) & 1) == 1) & (((col >> level) & 1) == 0)
        c_blocks = each(lambda jp: jnp.where(lower_left, a_ab[jp], 0.0))
        half_done = each(lambda jp: _bdot(inv[jp], block_diag(c_blocks[jp])))
        inv = each(lambda jp: inv[jp] + _bdot(half_done[jp], block_diag(inv[jp])))
    a_hat = each(lambda jp: _bdot(inv[jp], block_diag(abar[jp])))
    u_hat = each(lambda jp: _bdot(inv[jp], block_diag(ak_v[jp])))
    r_hat = each(lambda jp: rbar[jp] + _bdot(a_rb[jp], block_diag(a_hat[jp])))
    y_hat = each(lambda jp: _bdot(a_rb[jp], block_diag(u_hat[jp])) + _bdot(a_rk[jp], v_bd[jp]))
    g_mat = each(lambda jp: eye * p_last[jp] + diag_blocks(_btn(a_hat[jp], bt[jp] * p_last[jp])))
    h_mat = each(lambda jp: diag_blocks(_btn(jnp.concatenate([u_hat[jp], v[jp]], axis=0),
                                             jnp.concatenate([bt[jp], kt[jp]], axis=0) * p_last[jp])))
    states = [s_ref[p] for p in range(n_pairs)]
    for j in range(nch):
        for p in range(n_pairs):
            y_ref[0, j * cs:(j + 1) * cs, p * pw:(p + 1) * pw] = (
                _bnt(r_hat[j, p], block_diag(states[p])) + y_hat[j, p])
        states = [_dot_hi_lo(states[p], block_diag(g_mat[j, p])) + h_mat[j, p] for p in range(n_pairs)]
    for p in range(n_pairs):
        s_ref[p] = states[p]


def _rwkv_chunk(r, lw, k, v, kn, b_, cs=CHUNK, nch=8):
    b, s, w = r.shape
    spec = pl.BlockSpec((1, cs * nch, w), lambda bi, c: (bi, c, 0))
    return pl.pallas_call(
        functools.partial(_rwkv_chunk_kernel, cs=cs, nch=nch),
        grid=(b, s // (cs * nch)),
        in_specs=[spec] * 6,
        out_specs=spec,
        out_shape=jax.ShapeDtypeStruct((b, s, w), F32),
        scratch_shapes=[pltpu.VMEM((RWKV_HEADS // 2, HEAD_DIM, 2 * HEAD_DIM), F32)],
        compiler_params=_cparams(("parallel", "arbitrary")),
        name="rwkv_chunk",
    )(r, lw, k, v, kn, b_)


def _merge_kernel(x_ref, swa_ref, y_ref, g_ref, bonus_ref, fox_ref, dsa_ref, gng_ref, gnb_ref,
                  gbias_ref, wg_ref, wb_ref, wo_ref, lng_ref, lnb_ref, o_ref):
    y = y_ref[...]
    xb = x_ref[...].astype(BF16)
    head_mean = _head_block_ones(RWKV_W, 1.0 / HEAD_DIM)
    yc = y - _head_reduce(y, head_mean)
    yv = _head_reduce(yc * yc, head_mean)
    o_rwkv = (yc * lax.rsqrt(yv + RWKV_GN_EPS) * gng_ref[...] + gnb_ref[...] + bonus_ref[...]) * g_ref[...]
    branches = (swa_ref[...], o_rwkv, fox_ref[...], dsa_ref[...])
    merged = jnp.zeros(o_ref.shape, F32)
    off = 0
    for i, o in enumerate(branches):
        width = MIX_WIDTHS[i]
        proj = _dot(o.astype(BF16), wb_ref[off:off + width, :])
        gate = _sigmoid(_nt(xb, wg_ref[0, i * D_MODEL:(i + 1) * D_MODEL, :]) + gbias_ref[i:i + 1, :])
        merged = merged + gate * proj
        off += width
    y_out = _dot(merged.astype(BF16), wo_ref[...])
    o_ref[...] = _layer_norm(DN_ALPHA * x_ref[...] + y_out, lng_ref[...], lnb_ref[...])


def _merge(xf, o_swa, y, g, bonus, o_fox, o_dsa, gn_g, gn_b, gate_bias, w_gate_all, layer, w_branch, w_out,
           ln_g, ln_b, tm=512):
    n, d = xf.shape
    w_gate = w_gate_all
    row = lambda x: x.reshape(1, -1)
    tok = lambda wdt: pl.BlockSpec((tm, wdt), lambda i: (i, 0))
    full = lambda shape: pl.BlockSpec(shape, lambda i: (0,) * len(shape))
    return pl.pallas_call(
        _merge_kernel,
        grid=(n // tm,),
        in_specs=[tok(d), tok(512), tok(256), tok(256), tok(256), tok(256), tok(256),
                  full((1, 256)), full((1, 256)), full((N_BRANCHES, d)),
                  pl.BlockSpec((1, N_BRANCHES * d, d), lambda i: (layer, 0, 0)),
                  full((sum(MIX_WIDTHS), d)), full((d, d)), full((1, d)), full((1, d))],
        out_specs=tok(d),
        out_shape=jax.ShapeDtypeStruct((n, d), F32),
        compiler_params=_cparams(("parallel",)),
        name="merge",
    )(xf, o_swa, y, g, bonus, o_fox, o_dsa, row(gn_g), row(gn_b), gate_bias, w_gate, w_branch, w_out,
      row(ln_g), row(ln_b))


def _silu(x):
    return x * _sigmoid(x)


def _ffn_kernel(x_ref, w1_ref, w3_ref, w2_ref, lng_ref, lnb_ref, o_ref, *, tf):
    xb = x_ref[...].astype(BF16)
    y = None
    for f in range(0, w1_ref.shape[1], tf):
        hidden = _silu(_dot(xb, w1_ref[:, f:f + tf])) * _dot(xb, w3_ref[:, f:f + tf])
        part = _dot(hidden.astype(BF16), w2_ref[f:f + tf, :])
        y = part if y is None else y + part
    o_ref[...] = _layer_norm(DN_ALPHA * x_ref[...] + y, lng_ref[...], lnb_ref[...])


def _ffn(xf, w1, w3, w2, ln_g, ln_b, tm=512, tf=1408):
    n, d = xf.shape
    ff = w1.shape[1]
    row = lambda x: x.reshape(1, -1)
    tok = pl.BlockSpec((tm, d), lambda i: (i, 0))
    resident = lambda shape: pl.BlockSpec(shape, lambda i: (0, 0), pipeline_mode=pl.Buffered(1))
    return pl.pallas_call(
        functools.partial(_ffn_kernel, tf=tf),
        grid=(n // tm,),
        in_specs=[tok, resident((d, ff)), resident((d, ff)), resident((ff, d)),
                  resident((1, d)), resident((1, d))],
        out_specs=tok,
        out_shape=jax.ShapeDtypeStruct((n, d), F32),
        compiler_params=_cparams(("parallel",)),
        name="ffn",
    )(xf, w1, w3, w2, row(ln_g), row(ln_b))


def _moe_kernel(x_ref, rw_ref, rb_ref, w1_ref, w3_ref, w2_ref, lng_ref, lnb_ref, o_ref,
                comb_ref, rank_ref, cnt_ref, xb_ref, *, sb, sb_small):
    e = pl.program_id(1)
    tm = x_ref.shape[0]
    half = tm // 2
    lane = lax.broadcasted_iota(jnp.int32, (tm, 128), 1).astype(F32)

    @pl.when(e == 0)
    def _():
        o_ref[...] = jnp.zeros_like(o_ref)
        xb_ref[...] = x_ref[...].astype(BF16)
        logits = _dot(x_ref[...], rw_ref[...], HIGHEST) + rb_ref[...]
        logits = jnp.where(lane < N_EXPERTS, logits, NEG_BIG)
        m1 = jnp.max(logits, axis=-1, keepdims=True)
        i1 = jnp.min(jnp.where(logits == m1, lane, 128.0), axis=-1, keepdims=True)
        rest = jnp.where(lane == i1, NEG_BIG, logits)
        m2 = jnp.max(rest, axis=-1, keepdims=True)
        i2 = jnp.min(jnp.where(rest == m2, lane, 128.0), axis=-1, keepdims=True)
        e2 = jnp.exp(m2 - m1)
        comb_ref[...] = jnp.where(lane == i1, 1.0 / (1.0 + e2), 0.0) + jnp.where(lane == i2, e2 / (1.0 + e2), 0.0)
        chosen = jnp.where(lane == i1, 1.0, 0.0) + jnp.where(lane == i2, 1.0, 0.0)
        r = lax.broadcasted_iota(jnp.int32, (tm, tm), 0)
        c = lax.broadcasted_iota(jnp.int32, (tm, tm), 1)
        earlier = jnp.where(c < r, 1.0, 0.0).astype(BF16)
        in_first = lax.broadcasted_iota(jnp.int32, (tm, 128), 0) < half
        cnt_a = jnp.sum(chosen[:half], axis=0, keepdims=True)
        rank = _dot(earlier, chosen.astype(BF16))
        rank = jnp.where(chosen > 0.5, jnp.where(in_first, rank, rank - cnt_a), -1.0)
        lane8 = lax.broadcasted_iota(jnp.int32, (8, 128), 1)
        row8 = lax.broadcasted_iota(jnp.int32, (8, 128), 0)
        rank_ref[...] = _nt(jnp.where(lane8 == row8, 1.0, 0.0).astype(F32), rank, HIGHEST)
        cnt_ref[0:1, :] = cnt_a
        cnt_ref[1:2, :] = jnp.sum(chosen[half:], axis=0, keepdims=True)

    e_f = e.astype(F32)
    n_a = jnp.sum(jnp.where(lane[0:1, :] == e_f, cnt_ref[0:1, :], 0.0)).astype(jnp.int32)
    n_b = jnp.sum(jnp.where(lane[0:1, :] == e_f, cnt_ref[1:2, :], 0.0)).astype(jnp.int32)
    rank_row = rank_ref[pl.ds(e, 1), :]
    weight = jnp.sum(jnp.where(lane == e_f, comb_ref[...], 0.0), axis=-1, keepdims=True)

    def expert(xs):
        hidden = _silu(_dot(xs, w1_ref[0])) * _dot(xs, w3_ref[0])
        return _dot(hidden.astype(BF16), w2_ref[0]).astype(BF16)

    def half_blocks(cap):
        slot = lax.broadcasted_iota(jnp.int32, (cap, half), 0).astype(F32)
        picks = [jnp.where(rank_row[:, g * half:(g + 1) * half] == slot, 1.0, 0.0).astype(BF16) for g in range(2)]
        xs = jnp.concatenate([_dot(picks[g], xb_ref[g * half:(g + 1) * half, :]) for g in range(2)], axis=0)
        out = expert(xs.astype(BF16))
        for g in range(2):
            o_ref[g * half:(g + 1) * half, :] += (_tn(picks[g], out[g * cap:(g + 1) * cap])
                                                  * weight[g * half:(g + 1) * half])

    def whole_tile_blocks():
        second = lax.broadcasted_iota(jnp.int32, (1, tm), 1) >= half
        rank_all = jnp.where(second & (rank_row >= 0.0), rank_row + n_a.astype(F32), rank_row)

        def body(s, carry):
            slot = lax.broadcasted_iota(jnp.int32, (sb, tm), 0).astype(F32) + (s * sb).astype(F32)
            pick = jnp.where(rank_all == slot, 1.0, 0.0).astype(BF16)
            o_ref[...] += _tn(pick, expert(_dot(pick, xb_ref[...]).astype(BF16))) * weight
            return carry

        lax.fori_loop(0, lax.div(n_a + n_b + sb - 1, sb), body, 0)

    most = jnp.maximum(n_a, n_b)

    @pl.when(most <= sb_small // 2)
    def _():
        half_blocks(sb_small // 2)

    @pl.when((most > sb_small // 2) & (most <= sb // 2))
    def _():
        half_blocks(sb // 2)

    @pl.when(most > sb // 2)
    def _():
        whole_tile_blocks()

    @pl.when(e == pl.num_programs(1) - 1)
    def _():
        o_ref[...] = _layer_norm(DN_ALPHA * x_ref[...] + o_ref[...], lng_ref[...], lnb_ref[...])


def _moe(xf, router_w, router_b, w1, w3, w2, ln_g, ln_b, tm=1024, sb=320, sb_small=256):
    n, d = xf.shape
    ne, _, fe = w1.shape
    tm = min(tm, n)
    rw = jnp.pad(router_w, ((0, 0), (0, 128 - ne)))
    rb = jnp.pad(router_b, (0, 128 - ne)).reshape(1, 128)
    row = lambda x: x.reshape(1, -1)
    return pl.pallas_call(
        functools.partial(_moe_kernel, sb=sb, sb_small=sb_small),
        grid=(n // tm, ne),
        in_specs=[pl.BlockSpec((tm, d), lambda i, e: (i, 0)),
                  pl.BlockSpec((d, 128), lambda i, e: (0, 0)),
                  pl.BlockSpec((1, 128), lambda i, e: (0, 0)),
                  pl.BlockSpec((1, d, fe), lambda i, e: (e, 0, 0)),
                  pl.BlockSpec((1, d, fe), lambda i, e: (e, 0, 0)),
                  pl.BlockSpec((1, fe, d), lambda i, e: (e, 0, 0)),
                  pl.BlockSpec((1, d), lambda i, e: (0, 0)),
                  pl.BlockSpec((1, d), lambda i, e: (0, 0))],
        out_specs=pl.BlockSpec((tm, d), lambda i, e: (i, 0)),
        out_shape=jax.ShapeDtypeStruct((n, d), F32),
        scratch_shapes=[pltpu.VMEM((tm, 128), F32), pltpu.VMEM((8, tm), F32), pltpu.VMEM((2, 128), F32),
                        pltpu.VMEM((tm, d), BF16)],
        compiler_params=pltpu.CompilerParams(dimension_semantics=("parallel", "arbitrary"),
                                             vmem_limit_bytes=MOE_VMEM_LIMIT),
        name="moe",
    )(xf, rw, rb, w1, w3, w2, row(ln_g), row(ln_b))


def _mixer_layer(xf, bsz, seq, layer, w_in, w_gate_all, sinks, mu, w0, w_up, a0, a_up, g_up, k_k, k_a, r_k,
                 gn_g, gn_b, f_bias, gate_bias, w_branch, w_out, ln_g, ln_b):
    p2 = _matmul(xf, _layout_w_in(w_in[layer]).astype(BF16), tm=512)
    p3 = p2.reshape(bsz, seq, P_WIDTH)
    o_swa = _swa(p3, sinks)
    o_fox = _fox(p3, _fox_c(p3, f_bias))
    o_dsa = _dsa(p3)
    r, lw, k2, v, kn, b_, g, bonus = _rwkv_pre(p3, mu, w0, w_up, a0, a_up, g_up, k_k, k_a, r_k)
    y = _rwkv_chunk(r, lw, k2, v, kn, b_)
    n = bsz * seq
    flat = lambda t: t.reshape(n, t.shape[-1])
    return _merge(xf, flat(o_swa), flat(y), flat(g), flat(bonus), flat(o_fox), flat(o_dsa), gn_g, gn_b,
                  gate_bias, w_gate_all, layer, w_branch.astype(BF16), w_out.astype(BF16), ln_g, ln_b)


def kernel(x, w_in, swa_sinks, rwkv_mu, rwkv_w0, rwkv_w_up, rwkv_a0, rwkv_a_up, rwkv_g_up, rwkv_k_k, rwkv_k_a,
           rwkv_r_k, rwkv_gn_g, rwkv_gn_b, fox_f_bias, gate_bias, w_branch, w_out, ln_g, ln_b, ffn_w1, ffn_w3,
           ffn_w2, router_w, router_b, exp_w1, exp_w3, exp_w2):
    bsz, seq, d = x.shape
    xf = x.reshape(bsz * seq, d)
    w_gate_all = jnp.swapaxes(w_in[:, :, IN_OFF_GATE:], 1, 2).astype(BF16)
    for layer in range(DEPTH):
        xf = _mixer_layer(xf, bsz, seq, layer, w_in, w_gate_all, swa_sinks[layer], rwkv_mu[layer], rwkv_w0[layer],
                              rwkv_w_up[layer], rwkv_a0[layer], rwkv_a_up[layer], rwkv_g_up[layer],
                              rwkv_k_k[layer], rwkv_k_a[layer], rwkv_r_k[layer], rwkv_gn_g[layer],
                              rwkv_gn_b[layer], fox_f_bias[layer], gate_bias[layer], w_branch[layer],
                              w_out[layer], ln_g[layer, 0], ln_b[layer, 0])
        j = layer // 2
        if layer % 2 == 0:
            xf = _ffn(xf, ffn_w1[j].astype(BF16), ffn_w3[j].astype(BF16), ffn_w2[j].astype(BF16),
                      ln_g[layer, 1], ln_b[layer, 1])
        else:
            xf = _moe(xf, router_w[j], router_b[j], exp_w1[j].astype(BF16), exp_w3[j].astype(BF16),
                      exp_w2[j].astype(BF16), ln_g[layer, 1], ln_b[layer, 1])
    return xf.reshape(bsz, seq, d)
```
